```python
import math
import jax, jax.numpy as jnp
from jax import lax
import numpy as np

D_MODEL = 2048
BATCH = 2
SEQ = 8192
DEPTH = 1

HEAD_DIM = 128
ROT_DIM = HEAD_DIM // 4
ROPE_THETA = 500000.0
NEG_INF = -1e30
BIG = 1e30
EPS = 1e-5

DIFF_HEADS = 4
DIFF_VDIM = 2 * HEAD_DIM
DIFF_Q_BLOCK = 128

NSA_HEADS = 8
NSA_KV_GROUPS = 2
NSA_HPG = NSA_HEADS // NSA_KV_GROUPS
NSA_Q_BLOCK = 64
CMP_BLOCK = 32
CMP_STRIDE = 16
CMP_HIDDEN = 256
SLC_BLOCK = 64
SLC_TOPK = 16
WINDOW = 512

D_MIX = DIFF_HEADS * DIFF_VDIM + NSA_HEADS * HEAD_DIM
DIFF_QK_COLS = DIFF_HEADS * 2 * HEAD_DIM
DIFF_V_COLS = DIFF_HEADS * DIFF_VDIM
NSA_Q_COLS = NSA_HEADS * HEAD_DIM
NSA_KV_COLS = NSA_KV_GROUPS * HEAD_DIM
NSA_GATE_COLS = NSA_HEADS * 3
IN_SPLITS = (DIFF_QK_COLS, DIFF_QK_COLS, DIFF_V_COLS, NSA_Q_COLS,
             NSA_KV_COLS, NSA_KV_COLS, NSA_KV_COLS, NSA_KV_COLS, NSA_KV_COLS, NSA_KV_COLS,
             NSA_GATE_COLS)
IN_COLS = sum(IN_SPLITS)

N_GROUPS = 4
EXPERTS_PER_GROUP = 8
N_EXPERTS = N_GROUPS * EXPERTS_PER_GROUP
EXPERT_HIDDEN = 512
TOPK_IN_GROUP = 2
MOE_TOKEN_CHUNK = 1024

DN_ALPHA = (2.0 * DEPTH) ** 0.25
DN_BETA = (8.0 * DEPTH) ** -0.25

kernel_name = "hybrid_diffattn_nsa_hiermoe_deepnorm"


def _layernorm(x, g, b):
    xf = x.astype(jnp.float32)
    mu = jnp.mean(xf, axis=-1, keepdims=True)
    var = jnp.mean(jnp.square(xf - mu), axis=-1, keepdims=True)
    return ((xf - mu) * lax.rsqrt(var + EPS) * g.astype(jnp.float32) + b.astype(jnp.float32)).astype(x.dtype)


def _rope_cos_sin(seq_len):
    inv_freq = ROPE_THETA ** (-jnp.arange(0, ROT_DIM, 2, dtype=jnp.float32) / ROT_DIM)
    ang = jnp.arange(seq_len, dtype=jnp.float32)[:, None] * inv_freq[None, :]
    return jnp.cos(ang), jnp.sin(ang)


def _partial_rope(t, cos, sin):
    half = ROT_DIM // 2
    c = cos[None, :, None, :]
    s = sin[None, :, None, :]
    t1 = t[..., :half].astype(jnp.float32)
    t2 = t[..., half:ROT_DIM].astype(jnp.float32)
    rot = jnp.concatenate([t1 * c - t2 * s, t2 * c + t1 * s], axis=-1).astype(t.dtype)
    return jnp.concatenate([rot, t[..., ROT_DIM:]], axis=-1)


def _diff_attention(q, k, v, lam, subln_g, lambda_init):
    B, T, H, _, Dh = q.shape
    Dv = v.shape[-1]
    nb = T // DIFF_Q_BLOCK
    scale = Dh ** -0.5
    q_blocks = jnp.moveaxis(q.reshape(B, nb, DIFF_Q_BLOCK, H, 2, Dh), 1, 0)
    kpos = jnp.arange(T)

    def block(args):
        i, qi = args
        qpos = i * DIFF_Q_BLOCK + jnp.arange(DIFF_Q_BLOCK)
        s = jnp.einsum('bqhcd,bkhcd->bhcqk', qi, k, preferred_element_type=jnp.float32) * scale
        mask = kpos[None, :] <= qpos[:, None]
        p = jax.nn.softmax(jnp.where(mask, s, NEG_INF), axis=-1)
        a = p[:, :, 0] - lam * p[:, :, 1]
        return jnp.einsum('bhqk,bkhe->bqhe', a.astype(v.dtype), v)

    o = lax.map(block, (jnp.arange(nb), q_blocks))
    o = jnp.moveaxis(o, 0, 1).reshape(B, T, H, Dv)
    of = o.astype(jnp.float32)
    of = of * lax.rsqrt(jnp.mean(jnp.square(of), axis=-1, keepdims=True) + EPS) * subln_g.astype(jnp.float32)
    of = of * (1.0 - lambda_init)
    return of.astype(v.dtype).reshape(B, T, H * Dv)


def _compress(t, pos, w1, b1, w2, b2):
    B, T, G, Dh = t.shape
    n_cmp = (T - CMP_BLOCK) // CMP_STRIDE + 1
    idx = jnp.arange(n_cmp)[:, None] * CMP_STRIDE + jnp.arange(CMP_BLOCK)[None, :]
    blk = t[:, idx] + pos[None, None, :, None, :]
    blk = jnp.transpose(blk, (0, 1, 3, 2, 4)).reshape(B, n_cmp, G, CMP_BLOCK * Dh)
    h = jax.nn.gelu(jnp.einsum('bcgi,ih->bcgh', blk, w1) + b1)
    return jnp.einsum('bcgh,hd->bcgd', h, w2) + b2


def _nsa_attention(q, q_rot, k_cmp, v_cmp, k_slc, v_slc, k_win, v_win, gates,
                   cmp_pos, cmp_w1, cmp_b1, cmp_w2, cmp_b2):
    B, T, H, Dh = q.shape
    G = NSA_KV_GROUPS
    QB = NSA_Q_BLOCK
    scale = Dh ** -0.5
    kc = _compress(k_cmp, cmp_pos[0], cmp_w1[0], cmp_b1[0], cmp_w2[0], cmp_b2[0])
    vc = _compress(v_cmp, cmp_pos[1], cmp_w1[1], cmp_b1[1], cmp_w2[1], cmp_b2[1])
    n_cmp = kc.shape[1]
    n_slc = T // SLC_BLOCK
    top_k = min(SLC_TOPK, n_slc)
    ci = np.arange(n_cmp)[:, None] * CMP_STRIDE
    sj = np.arange(n_slc)[None, :] * SLC_BLOCK
    overlap = jnp.asarray((ci < sj + SLC_BLOCK) & (ci + CMP_BLOCK > sj), dtype=jnp.float32)
    cmp_end = jnp.arange(n_cmp) * CMP_STRIDE + CMP_BLOCK - 1
    blk_start = jnp.arange(n_slc) * SLC_BLOCK
    blk_id = jnp.arange(n_slc)
    ks_blk = jnp.transpose(k_slc.reshape(B, n_slc, SLC_BLOCK, G, Dh), (0, 3, 1, 2, 4))
    vs_blk = jnp.transpose(v_slc.reshape(B, n_slc, SLC_BLOCK, G, Dh), (0, 3, 1, 2, 4))
    kw_pad = jnp.pad(k_win, ((0, 0), (WINDOW, 0), (0, 0), (0, 0)))
    vw_pad = jnp.pad(v_win, ((0, 0), (WINDOW, 0), (0, 0), (0, 0)))
    b_idx = jnp.arange(B)[:, None, None, None]
    g_idx = jnp.arange(G)[None, :, None, None]
    nb = T // QB

    def to_blocks(t):
        return jnp.moveaxis(t.reshape(B, nb, QB, *t.shape[2:]), 1, 0)

    def block(args):
        i, qb, qrb, gb = args
        qpos = i * QB + jnp.arange(QB)
        qg = qb.reshape(B, QB, G, NSA_HPG, Dh)
        qrg = qrb.reshape(B, QB, G, NSA_HPG, Dh)
        s_c = jnp.einsum('bqghd,bcgd->bghqc', qg, kc, preferred_element_type=jnp.float32) * scale
        valid_c = cmp_end[None, :] <= qpos[:, None]
        p_c = jax.nn.softmax(jnp.where(valid_c, s_c, NEG_INF), axis=-1) * valid_c
        o_c = jnp.einsum('bghqc,bcgd->bqghd', p_c.astype(vc.dtype), vc)
        imp = jnp.einsum('bghqc,cs->bgqs', p_c, overlap)
        cur = qpos // SLC_BLOCK
        valid_s = blk_start[None, :] <= qpos[:, None]
        forced = (blk_id[None, :] == 0) | (blk_id[None, :] == cur[:, None]) | (blk_id[None, :] == cur[:, None] - 1)
        score = jnp.where(valid_s & forced, BIG, jnp.where(valid_s, imp, NEG_INF))
        _, sel = lax.top_k(score, top_k)
        ks = ks_blk[b_idx, g_idx, sel].reshape(B, G, QB, top_k * SLC_BLOCK, Dh)
        vs = vs_blk[b_idx, g_idx, sel].reshape(B, G, QB, top_k * SLC_BLOCK, Dh)
        tok = (sel[..., None] * SLC_BLOCK + jnp.arange(SLC_BLOCK)).reshape(B, G, QB, top_k * SLC_BLOCK)
        s_s = jnp.einsum('bqghd,bgqnd->bghqn', qrg, ks, preferred_element_type=jnp.float32) * scale
        mask_s = (tok <= qpos[None, None, :, None])[:, :, None]
        p_s = jax.nn.softmax(jnp.where(mask_s, s_s, NEG_INF), axis=-1)
        o_s = jnp.einsum('bghqn,bgqnd->bqghd', p_s.astype(vs.dtype), vs)
        start = i * QB
        kw = lax.dynamic_slice_in_dim(kw_pad, start, WINDOW + QB, axis=1)
        vw = lax.dynamic_slice_in_dim(vw_pad, start, WINDOW + QB, axis=1)
        kpos = start - WINDOW + jnp.arange(WINDOW + QB)
        dpos = qpos[:, None] - kpos[None, :]
        mask_w = (dpos >= 0) & (dpos < WINDOW) & (kpos[None, :] >= 0)
        s_w = jnp.einsum('bqghd,bkgd->bghqk', qrg, kw, preferred_element_type=jnp.float32) * scale
        p_w = jax.nn.softmax(jnp.where(mask_w, s_w, NEG_INF), axis=-1)
        o_w = jnp.einsum('bghqk,bkgd->bqghd', p_w.astype(vw.dtype), vw)
        gg = gb.reshape(B, QB, G, NSA_HPG, 3)
        o = gg[..., 0:1] * o_c + gg[..., 1:2] * o_s + gg[..., 2:3] * o_w
        return o.reshape(B, QB, H * Dh)

    o = lax.map(block, (jnp.arange(nb), to_blocks(q), to_blocks(q_rot), to_blocks(gates)))
    return jnp.moveaxis(o, 0, 1).reshape(B, T, H * Dh)


def _mixer(x, w_in, w_out, diff_lambda, diff_subln_g, cmp_pos, cmp_w1, cmp_b1, cmp_w2, cmp_b2,
           cos, sin, lambda_init):
    B, T, _ = x.shape
    proj = jnp.einsum('btd,dc->btc', x, w_in)
    offsets = [int(o) for o in np.cumsum(IN_SPLITS)[:-1]]
    dq, dk, dv, nq, kc, vc, ks, vs, kw, vw, ng = jnp.split(proj, offsets, axis=-1)
    dq = _partial_rope(dq.reshape(B, T, DIFF_HEADS * 2, HEAD_DIM), cos, sin).reshape(B, T, DIFF_HEADS, 2, HEAD_DIM)
    dk = _partial_rope(dk.reshape(B, T, DIFF_HEADS * 2, HEAD_DIM), cos, sin).reshape(B, T, DIFF_HEADS, 2, HEAD_DIM)
    dv = dv.reshape(B, T, DIFF_HEADS, DIFF_VDIM)
    lf = diff_lambda.astype(jnp.float32)
    lam = jnp.exp(jnp.sum(lf[0] * lf[1])) - jnp.exp(jnp.sum(lf[2] * lf[3])) + lambda_init
    o_diff = _diff_attention(dq, dk, dv, lam, diff_subln_g, lambda_init)
    nq = nq.reshape(B, T, NSA_HEADS, HEAD_DIM)
    nq_rot = _partial_rope(nq, cos, sin)
    kvr = lambda t: t.reshape(B, T, NSA_KV_GROUPS, HEAD_DIM)
    ks = _partial_rope(kvr(ks), cos, sin)
    kw = _partial_rope(kvr(kw), cos, sin)
    gates = jax.nn.sigmoid(ng.astype(jnp.float32)).astype(x.dtype).reshape(B, T, NSA_HEADS, 3)
    o_nsa = _nsa_attention(nq, nq_rot, kvr(kc), kvr(vc), ks, kvr(vs), kw, kvr(vw), gates,
                           cmp_pos, cmp_w1, cmp_b1, cmp_w2, cmp_b2)
    o = jnp.concatenate([o_diff, o_nsa], axis=-1)
    return jnp.einsum('btc,cd->btd', o, w_out)


def _hier_moe(x, router_group, router_expert, w_gate, w_up, w_down):
    B, T, D = x.shape
    xt = x.reshape(B * T, D)
    N = xt.shape[0]
    g_logits = jnp.einsum('nd,dg->ng', xt, router_group).astype(jnp.float32)
    g_prob = jax.nn.softmax(g_logits, axis=-1)
    g_w, g_sel = lax.top_k(g_prob, 1)
    e_logits = jnp.einsum('nd,de->ne', xt, router_expert).astype(jnp.float32).reshape(N, N_GROUPS, EXPERTS_PER_GROUP)
    e_logits = jnp.take_along_axis(e_logits, g_sel[:, :, None], axis=1)[:, 0]
    e_prob = jax.nn.softmax(e_logits, axis=-1)
    e_w, e_sel = lax.top_k(e_prob, TOPK_IN_GROUP)
    e_w = e_w / jnp.sum(e_w, axis=-1, keepdims=True)
    weight = g_w * e_w
    global_idx = g_sel * EXPERTS_PER_GROUP + e_sel
    dense_w = jnp.sum(jax.nn.one_hot(global_idx, N_EXPERTS, dtype=jnp.float32) * weight[..., None], axis=1)
    chunk = math.gcd(N, MOE_TOKEN_CHUNK)
    xc = xt.reshape(N // chunk, chunk, D)
    wc = dense_w.reshape(N // chunk, chunk, N_EXPERTS)

    def run(args):
        xb, wb = args
        h = jax.nn.silu(jnp.einsum('nd,edf->nef', xb, w_gate)) * jnp.einsum('nd,edf->nef', xb, w_up)
        h = h * wb[..., None].astype(h.dtype)
        return jnp.einsum('nef,efd->nd', h, w_down)

    y = lax.map(run, (xc, wc))
    return y.reshape(B, T, D)


def setup_inputs(seed: int = 0) -> dict:
    key = jax.random.key(seed)
    ks = jax.random.split(key, 19)
    L = DEPTH

    def nrm(k, shape, scale):
        return jax.random.normal(k, shape, jnp.float32) * scale

    return {
        "x": nrm(ks[0], (BATCH, SEQ, D_MODEL), 1.0),
        "w_in": nrm(ks[1], (L, D_MODEL, IN_COLS), D_MODEL ** -0.5),
        "diff_lambda": nrm(ks[2], (L, 4, HEAD_DIM), 0.1),
        "diff_subln_g": 1.0 + nrm(ks[3], (L, DIFF_VDIM), 0.02),
        "cmp_pos": nrm(ks[4], (L, 2, CMP_BLOCK, HEAD_DIM), 0.1),
        "cmp_w1": nrm(ks[5], (L, 2, CMP_BLOCK * HEAD_DIM, CMP_HIDDEN), (CMP_BLOCK * HEAD_DIM) ** -0.5),
        "cmp_b1": nrm(ks[6], (L, 2, CMP_HIDDEN), 0.01),
        "cmp_w2": nrm(ks[7], (L, 2, CMP_HIDDEN, HEAD_DIM), CMP_HIDDEN ** -0.5),
        "cmp_b2": nrm(ks[8], (L, 2, HEAD_DIM), 0.01),
        "w_out": nrm(ks[9], (L, D_MIX, D_MODEL), DN_BETA * D_MIX ** -0.5),
        "ln1_g": 1.0 + nrm(ks[10], (L, D_MODEL), 0.02),
        "ln1_b": nrm(ks[11], (L, D_MODEL), 0.02),
        "router_group": nrm(ks[12], (L, D_MODEL, N_GROUPS), D_MODEL ** -0.5),
        "router_expert": nrm(ks[13], (L, D_MODEL, N_EXPERTS), D_MODEL ** -0.5),
        "expert_w_gate": nrm(ks[14], (L, N_EXPERTS, D_MODEL, EXPERT_HIDDEN), D_MODEL ** -0.5),
        "expert_w_up": nrm(ks[15], (L, N_EXPERTS, D_MODEL, EXPERT_HIDDEN), D_MODEL ** -0.5),
        "expert_w_down": nrm(ks[16], (L, N_EXPERTS, EXPERT_HIDDEN, D_MODEL), DN_BETA * EXPERT_HIDDEN ** -0.5),
        "ln2_g": 1.0 + nrm(ks[17], (L, D_MODEL), 0.02),
        "ln2_b": nrm(ks[18], (L, D_MODEL), 0.02),
    }


def reference(x, w_in, diff_lambda, diff_subln_g, cmp_pos, cmp_w1, cmp_b1, cmp_w2, cmp_b2, w_out,
              ln1_g, ln1_b, router_group, router_expert, expert_w_gate, expert_w_up, expert_w_down,
              ln2_g, ln2_b):
    cos, sin = _rope_cos_sin(x.shape[1])
    for l in range(DEPTH):
        lambda_init = 0.8 - 0.6 * math.exp(-0.3 * l)
        h = _mixer(x, w_in[l], w_out[l], diff_lambda[l], diff_subln_g[l], cmp_pos[l], cmp_w1[l],
                   cmp_b1[l], cmp_w2[l], cmp_b2[l], cos, sin, lambda_init)
        x = _layernorm(DN_ALPHA * x + h, ln1_g[l], ln1_b[l])
        h = _hier_moe(x, router_group[l], router_expert[l], expert_w_gate[l], expert_w_up[l], expert_w_down[l])
        x = _layernorm(DN_ALPHA * x + h, ln2_g[l], ln2_b[l])
    return x
```

```python
import functools
import math

import numpy as np
import jax
import jax.numpy as jnp
from jax import lax
from jax.experimental import pallas as pl
from jax.experimental.pallas import tpu as pltpu

F32 = jnp.float32
BF16 = jnp.bfloat16

D_MODEL = 2048
HEAD_DIM = 128
ROT_DIM = HEAD_DIM // 4
ROPE_THETA = 500000.0
NEG_INF = -1e30
BIG = 1e30
EPS = 1e-5

DIFF_HEADS = 4
DIFF_VDIM = 2 * HEAD_DIM

NSA_HEADS = 8
NSA_KV_GROUPS = 2
NSA_HPG = NSA_HEADS // NSA_KV_GROUPS
CMP_BLOCK = 32
CMP_STRIDE = 16
CMP_HIDDEN = 256
SLC_BLOCK = 64
SLC_TOPK = 16
WINDOW = 512

N_GROUPS = 4
EXPERTS_PER_GROUP = 8
N_EXPERTS = N_GROUPS * EXPERTS_PER_GROUP
EXPERT_HIDDEN = 512

DEPTH = 1
DN_ALPHA = (2.0 * DEPTH) ** 0.25

LANES = 128
VMEM_LIMIT = 56 * 1024 * 1024

U_NQ, U_NQR, U_DQ, U_DK, U_DV = 0, 8, 16, 24, 32
U_KC, U_VC, U_KS, U_VS, U_KW, U_VW, U_GATE = 40, 42, 44, 46, 48, 50, 52
N_UNITS = 54
ROPE_BLOCKS = tuple(range(4, 16)) + (22, 24)


def _dot(a, b):
    return jnp.dot(a, b, preferred_element_type=F32)


def _dot_nt(a, b):
    return lax.dot_general(a, b, (((1,), (1,)), ((), ())), preferred_element_type=F32)


def _params(sem, vmem=VMEM_LIMIT):
    return pltpu.CompilerParams(dimension_semantics=sem, vmem_limit_bytes=vmem)


def _proj_kernel(flags_ref, x_ref, w_ref, cos_ref, sin_ref, o_ref):
    j = pl.program_id(1)
    acc = _dot(x_ref[...], w_ref[...])

    @pl.when(flags_ref[j] == 0)
    def _():
        o_ref[...] = acc.astype(o_ref.dtype)

    @pl.when(flags_ref[j] != 0)
    def _():
        c = cos_ref[...]
        s = sin_ref[...]
        lane = lax.broadcasted_iota(jnp.int32, c.shape, 1)
        half = ROT_DIM // 2
        for hh in range(2):
            a = acc[:, hh * LANES:(hh + 1) * LANES]
            partner = jnp.where(lane < half, pltpu.roll(a, LANES - half, 1), pltpu.roll(a, half, 1))
            o_ref[:, hh * LANES:(hh + 1) * LANES] = (a * c + partner * s).astype(o_ref.dtype)


def _proj_call(xb, w_aug, flags, cos_t, sin_t, seq, tm):
    n, d = xb.shape
    nj = w_aug.shape[1] // 256
    tpb = seq // tm
    grid_spec = pltpu.PrefetchScalarGridSpec(
        num_scalar_prefetch=1,
        grid=(n // tm, nj),
        in_specs=[
            pl.BlockSpec((tm, d), lambda i, j, f: (i, 0)),
            pl.BlockSpec((d, 256), lambda i, j, f: (0, j)),
            pl.BlockSpec((tm, LANES), lambda i, j, f: (i % tpb, 0)),
            pl.BlockSpec((tm, LANES), lambda i, j, f: (i % tpb, 0)),
        ],
        out_specs=pl.BlockSpec((tm, 256), lambda i, j, f: (i, j)),
    )
    return pl.pallas_call(
        _proj_kernel,
        out_shape=jax.ShapeDtypeStruct((n, w_aug.shape[1]), BF16),
        grid_spec=grid_spec,
        compiler_params=_params(("parallel", "arbitrary")),
        name="proj",
    )(flags, xb, w_aug, cos_t, sin_t)


def _compress_kernel(r_ref, w1_ref, pos_ref, b1_ref, w2_ref, b2_ref, o_ref):
    nt = r_ref.shape[0]
    ab = _dot(r_ref[...], w1_ref[...])
    pos = pos_ref[...]
    pa = jnp.broadcast_to(pos[0:1], (8, pos.shape[1])).astype(BF16)
    pb = jnp.broadcast_to(pos[1:2], (8, pos.shape[1])).astype(BF16)
    const = _dot(pa, w1_ref[:, :CMP_HIDDEN])[0:1] + _dot(pb, w1_ref[:, CMP_HIDDEN:])[0:1]
    h = ab[:, :CMP_HIDDEN] + pltpu.roll(ab[:, CMP_HIDDEN:], nt - 1, 0) + const + b1_ref[...]
    h = jax.nn.gelu(h)
    o_ref[...] = (_dot(h.astype(BF16), w2_ref[...]) + b2_ref[...]).astype(o_ref.dtype)


def _compress_call(r, w1cat, pos2, b1, w2, b2):
    b, four, nt, k = r.shape
    return pl.pallas_call(
        _compress_kernel,
        out_shape=jax.ShapeDtypeStruct((b, four, nt, HEAD_DIM), BF16),
        grid=(b, four),
        in_specs=[
            pl.BlockSpec((None, None, nt, k), lambda i, c: (i, c, 0, 0)),
            pl.BlockSpec((None, k, 2 * CMP_HIDDEN), lambda i, c: (c // 2, 0, 0)),
            pl.BlockSpec((None, 2, k), lambda i, c: (c // 2, 0, 0)),
            pl.BlockSpec((None, 1, CMP_HIDDEN), lambda i, c: (c // 2, 0, 0)),
            pl.BlockSpec((None, CMP_HIDDEN, HEAD_DIM), lambda i, c: (c // 2, 0, 0)),
            pl.BlockSpec((None, 1, HEAD_DIM), lambda i, c: (c // 2, 0, 0)),
        ],
        out_specs=pl.BlockSpec((None, None, nt, HEAD_DIM), lambda i, c: (i, c, 0, 0)),
        compiler_params=_params(("parallel", "parallel")),
        name="compress",
    )(r, w1cat, pos2, b1, w2, b2)


def _diff_kernel(dl_ref, q_ref, k_ref, v_ref, g_ref, o_ref, m_sc, l_sc, acc_sc, *, tq, lambda_init):
    qi = pl.program_id(2)
    scale = HEAD_DIM ** -0.5
    m_sc[...] = jnp.full(m_sc.shape, NEG_INF, F32)
    l_sc[...] = jnp.zeros(l_sc.shape, F32)
    acc_sc[...] = jnp.zeros(acc_sc.shape, F32)
    q = q_ref[...]

    def step(j, masked):
        off = pl.multiple_of(j * tq, tq)
        kt = k_ref[pl.ds(off, tq), :]
        vt = v_ref[pl.ds(off, tq), :]
        for c in range(2):
            s = _dot_nt(q[:, c * HEAD_DIM:(c + 1) * HEAD_DIM], kt[:, c * HEAD_DIM:(c + 1) * HEAD_DIM]) * scale
            if masked:
                row = lax.broadcasted_iota(jnp.int32, s.shape, 0)
                col = lax.broadcasted_iota(jnp.int32, s.shape, 1)
                s = jnp.where(col <= row, s, NEG_INF)
            m_old = m_sc[c]
            m_new = jnp.maximum(m_old, jnp.max(s, axis=1, keepdims=True))
            p = jnp.exp(s - m_new)
            alpha = jnp.exp(m_old - m_new)
            l_sc[c] = alpha * l_sc[c] + jnp.sum(p, axis=1, keepdims=True)
            acc_sc[c] = alpha * acc_sc[c] + _dot(p.astype(BF16), vt)
            m_sc[c] = m_new

    def body(j, carry):
        step(j, False)
        return carry

    lax.fori_loop(0, qi, body, 0)
    step(qi, True)

    dl = dl_ref[...]
    lam = (jnp.exp(jnp.sum(dl[0:1] * dl[1:2], axis=1, keepdims=True))
           - jnp.exp(jnp.sum(dl[2:3] * dl[3:4], axis=1, keepdims=True)) + lambda_init)
    o = acc_sc[0] / l_sc[0] - lam * (acc_sc[1] / l_sc[1])
    o = o * lax.rsqrt(jnp.mean(o * o, axis=1, keepdims=True) + EPS) * g_ref[...]
    o_ref[...] = (o * (1.0 - lambda_init)).astype(o_ref.dtype)


def _diff_call(proj, dl, g, batch, seq, tq, lambda_init):
    nq = seq // tq
    kern = functools.partial(_diff_kernel, tq=tq, lambda_init=lambda_init)
    return pl.pallas_call(
        kern,
        out_shape=jax.ShapeDtypeStruct((batch * seq, DIFF_HEADS * DIFF_VDIM), BF16),
        grid=(batch, DIFF_HEADS, nq),
        in_specs=[
            pl.BlockSpec((4, HEAD_DIM), lambda b, h, i: (0, 0)),
            pl.BlockSpec((tq, 256), lambda b, h, i: (b * nq + i, U_DQ // 2 + h)),
            pl.BlockSpec((seq, 256), lambda b, h, i: (b, U_DK // 2 + h)),
            pl.BlockSpec((seq, 256), lambda b, h, i: (b, U_DV // 2 + h)),
            pl.BlockSpec((1, DIFF_VDIM), lambda b, h, i: (0, 0)),
        ],
        out_specs=pl.BlockSpec((tq, DIFF_VDIM), lambda b, h, i: (b * nq + i, h)),
        scratch_shapes=[
            pltpu.VMEM((2, tq, 1), F32),
            pltpu.VMEM((2, tq, 1), F32),
            pltpu.VMEM((2, tq, DIFF_VDIM), F32),
        ],
        compiler_params=_params(("parallel", "parallel", "arbitrary")),
        name="diff_attn",
    )(dl, proj, proj, proj, g)


def _stack_heads(x, tq):
    return jnp.concatenate([x[:, h * HEAD_DIM:(h + 1) * HEAD_DIM] for h in range(NSA_HPG)], axis=0)


def _nsa_kernel(q_ref, qr_ref, gate_ref, kc_ref, vc_ref, ks_ref, vs_ref, kw_ref, vw_ref, ovt_ref, e_ref,
                o_ref, m_sc, l_sc, acc_sc, *, tq, tk, seq, top_k):
    qi = pl.program_id(2)
    q0 = qi * tq
    scale = HEAD_DIM ** -0.5
    rows = NSA_HPG * tq
    nt = kc_ref.shape[0]
    ns = ovt_ref.shape[0]

    qs = _stack_heads(q_ref[...], tq)
    qr = _stack_heads(qr_ref[...], tq)

    s_c = _dot_nt(qs, kc_ref[...]) * scale
    rq = lax.broadcasted_iota(jnp.int32, (rows, nt), 0) & (tq - 1)
    cc = lax.broadcasted_iota(jnp.int32, (rows, nt), 1)
    valid_c = cc * CMP_STRIDE + (CMP_BLOCK - 1) <= q0 + rq
    s_c = jnp.where(valid_c, s_c, NEG_INF)
    m_c = jnp.max(s_c, axis=1, keepdims=True)
    e_c = jnp.where(valid_c, jnp.exp(s_c - m_c), 0.0)
    l_c = jnp.sum(e_c, axis=1, keepdims=True)
    p_c = e_c / jnp.where(l_c > 0.0, l_c, 1.0)
    o_c = _dot(p_c.astype(BF16), vc_ref[...])

    p_sum = p_c[0:tq]
    for h in range(1, NSA_HPG):
        p_sum = p_sum + p_c[h * tq:(h + 1) * tq]
    p_hi = p_sum.astype(BF16)
    p_lo = (p_sum - p_hi.astype(F32)).astype(BF16)
    imp = _dot_nt(ovt_ref[...], p_hi) + _dot_nt(ovt_ref[...], p_lo)

    blk = lax.broadcasted_iota(jnp.int32, (ns, tq), 0)
    qpos_l = q0 + lax.broadcasted_iota(jnp.int32, (ns, tq), 1)
    cur = lax.shift_right_logical(qpos_l, int(math.log2(SLC_BLOCK)))
    valid_s = blk <= cur
    forced = (blk == 0) | (blk == cur) | (blk == cur - 1)
    work = jnp.where(valid_s & forced, BIG, jnp.where(valid_s, imp, NEG_INF))
    sel = jnp.zeros((ns, tq), F32)
    blk_f = blk.astype(F32)
    for _ in range(top_k):
        mx = jnp.max(work, axis=0, keepdims=True)
        idx = jnp.min(jnp.where(work == mx, blk_f, float(ns)), axis=0, keepdims=True)
        pick = blk_f == idx
        sel = jnp.where(pick, 1.0, sel)
        work = jnp.where(pick, -jnp.inf, work)
    sel_q = sel.T.astype(BF16)

    m_sc[...] = jnp.full(m_sc.shape, NEG_INF, F32)
    l_sc[...] = jnp.zeros(l_sc.shape, F32)
    acc_sc[...] = jnp.zeros(acc_sc.shape, F32)

    def sel_step(j, causal):
        off = pl.multiple_of(j * tk, tk)
        kt = ks_ref[pl.ds(off, tk), :]
        vt = vs_ref[pl.ds(off, tk), :]
        keepf = _dot(sel_q, e_ref[:, pl.ds(off, tk)])
        if causal:
            rr = lax.broadcasted_iota(jnp.int32, (tq, tk), 0)
            kk = lax.broadcasted_iota(jnp.int32, (tq, tk), 1)
            keepf = jnp.where(off + kk <= q0 + rr, keepf, 0.0)
        keep = jnp.concatenate([keepf] * NSA_HPG, axis=0) > 0.5
        s = jnp.where(keep, _dot_nt(qr, kt) * scale, NEG_INF)
        m_old = m_sc[...]
        m_new = jnp.maximum(m_old, jnp.max(s, axis=1, keepdims=True))
        p = jnp.exp(s - m_new)
        alpha = jnp.exp(m_old - m_new)
        l_sc[...] = alpha * l_sc[...] + jnp.sum(p, axis=1, keepdims=True)
        acc_sc[...] = alpha * acc_sc[...] + _dot(p.astype(BF16), vt)
        m_sc[...] = m_new

    def sel_body(j, carry):
        sel_step(j, False)
        return carry

    j_last = q0 // tk
    lax.fori_loop(0, j_last, sel_body, 0)
    sel_step(j_last, True)
    o_s = acc_sc[...] / l_sc[...]

    slab = min(WINDOW + tq, seq)
    start = pl.multiple_of(jnp.maximum(q0 - WINDOW, 0), tq)
    kw = kw_ref[pl.ds(start, slab), :]
    vw = vw_ref[pl.ds(start, slab), :]
    s_w = _dot_nt(qr, kw) * scale
    rq_w = lax.broadcasted_iota(jnp.int32, (rows, slab), 0) & (tq - 1)
    kpos = start + lax.broadcasted_iota(jnp.int32, (rows, slab), 1)
    dpos = q0 + rq_w - kpos
    s_w = jnp.where((dpos >= 0) & (dpos < WINDOW), s_w, NEG_INF)
    p_w = jnp.exp(s_w - jnp.max(s_w, axis=1, keepdims=True))
    o_w = _dot(p_w.astype(BF16), vw) / jnp.sum(p_w, axis=1, keepdims=True)

    gates = jax.nn.sigmoid(gate_ref[...].astype(F32))
    for h in range(NSA_HPG):
        r0, r1 = h * tq, (h + 1) * tq
        o = (gates[:, 3 * h:3 * h + 1] * o_c[r0:r1] + gates[:, 3 * h + 1:3 * h + 2] * o_s[r0:r1]
             + gates[:, 3 * h + 2:3 * h + 3] * o_w[r0:r1])
        o_ref[:, h * HEAD_DIM:(h + 1) * HEAD_DIM] = o.astype(o_ref.dtype)


def _nsa_call(proj, kvc, ovt, emat, batch, seq, tq, tk):
    nq = seq // tq
    nt = kvc.shape[2]
    ns = ovt.shape[0]
    width = NSA_HPG * HEAD_DIM
    kern = functools.partial(_nsa_kernel, tq=tq, tk=tk, seq=seq, top_k=min(SLC_TOPK, ns))

    def col(unit):
        return pl.BlockSpec((seq, HEAD_DIM), lambda b, g, i: (b, unit + g))

    return pl.pallas_call(
        kern,
        out_shape=jax.ShapeDtypeStruct((batch * seq, NSA_HEADS * HEAD_DIM), BF16),
        grid=(batch, NSA_KV_GROUPS, nq),
        in_specs=[
            pl.BlockSpec((tq, width), lambda b, g, i: (b * nq + i, U_NQ // 4 + g)),
            pl.BlockSpec((tq, width), lambda b, g, i: (b * nq + i, U_NQR // 4 + g)),
            pl.BlockSpec((tq, LANES), lambda b, g, i: (b * nq + i, U_GATE + g)),
            pl.BlockSpec((None, None, nt, HEAD_DIM), lambda b, g, i: (b, g, 0, 0)),
            pl.BlockSpec((None, None, nt, HEAD_DIM), lambda b, g, i: (b, 2 + g, 0, 0)),
            col(U_KS), col(U_VS), col(U_KW), col(U_VW),
            pl.BlockSpec((ns, nt), lambda b, g, i: (0, 0)),
            pl.BlockSpec((ns, seq), lambda b, g, i: (0, 0)),
        ],
        out_specs=pl.BlockSpec((tq, width), lambda b, g, i: (b * nq + i, g)),
        scratch_shapes=[
            pltpu.VMEM((NSA_HPG * tq, 1), F32),
            pltpu.VMEM((NSA_HPG * tq, 1), F32),
            pltpu.VMEM((NSA_HPG * tq, HEAD_DIM), F32),
        ],
        compiler_params=_params(("parallel", "parallel", "arbitrary")),
        name="nsa_attn",
    )(proj, proj, proj, kvc, kvc, proj, proj, proj, proj, ovt, emat)


def _layernorm(y, g, b):
    mu = jnp.mean(y, axis=1, keepdims=True)
    yc = y - mu
    var = jnp.mean(yc * yc, axis=1, keepdims=True)
    return yc * lax.rsqrt(var + EPS) * g + b


def _outproj_kernel(od_ref, on_ref, w_ref, x_ref, g_ref, b_ref, wr_ref, x1_ref, route_ref):
    half = od_ref.shape[1]
    h = _dot(od_ref[...], w_ref[:half, :]) + _dot(on_ref[...], w_ref[half:, :])
    x1 = _layernorm(DN_ALPHA * x_ref[...] + h, g_ref[...], b_ref[...])
    x1_ref[...] = x1

    wr = wr_ref[...]
    w_hi = wr.astype(BF16)
    w_lo = (wr - w_hi.astype(F32)).astype(BF16)
    x_hi = x1.astype(BF16)
    x_lo = (x1 - x_hi.astype(F32)).astype(BF16)
    logits = _dot(x_hi, w_hi) + (_dot(x_lo, w_hi) + _dot(x_hi, w_lo))

    lane = lax.broadcasted_iota(jnp.int32, logits.shape, 1).astype(F32)
    ninf = -jnp.inf
    gl = jnp.where(lane < N_GROUPS, logits, ninf)
    gmax = jnp.max(gl, axis=1, keepdims=True)
    g_w = 1.0 / jnp.sum(jnp.exp(gl - gmax), axis=1, keepdims=True)
    g_sel = jnp.min(jnp.where(gl == gmax, lane, float(LANES)), axis=1, keepdims=True)
    lo = N_GROUPS + EXPERTS_PER_GROUP * g_sel
    el = jnp.where((lane >= lo) & (lane < lo + EXPERTS_PER_GROUP), logits, ninf)
    e1 = jnp.max(el, axis=1, keepdims=True)
    i1 = jnp.min(jnp.where(el == e1, lane, float(LANES)), axis=1, keepdims=True)
    el2 = jnp.where(lane == i1, ninf, el)
    e2 = jnp.max(el2, axis=1, keepdims=True)
    i2 = jnp.min(jnp.where(el2 == e2, lane, float(LANES)), axis=1, keepdims=True)
    r = jnp.exp(e2 - e1)
    w1 = g_w / (1.0 + r)
    w2 = g_w * r / (1.0 + r)
    route = jnp.where(lane == 0, i1 - N_GROUPS,
                      jnp.where(lane == 1, i2 - N_GROUPS,
                                jnp.where(lane == 2, w1, jnp.where(lane == 3, w2, 0.0))))
    route_ref[...] = route


def _outproj_call(o_diff, o_nsa, w_out, x2, g, b, wr, tm):
    n, d = x2.shape
    half = o_diff.shape[1]
    return pl.pallas_call(
        _outproj_kernel,
        out_shape=(jax.ShapeDtypeStruct((n, d), F32), jax.ShapeDtypeStruct((n, LANES), F32)),
        grid=(n // tm,),
        in_specs=[
            pl.BlockSpec((tm, half), lambda i: (i, 0)),
            pl.BlockSpec((tm, half), lambda i: (i, 0)),
            pl.BlockSpec((2 * half, d), lambda i: (0, 0)),
            pl.BlockSpec((tm, d), lambda i: (i, 0)),
            pl.BlockSpec((1, d), lambda i: (0, 0)),
            pl.BlockSpec((1, d), lambda i: (0, 0)),
            pl.BlockSpec((d, LANES), lambda i: (0, 0)),
        ],
        out_specs=(pl.BlockSpec((tm, d), lambda i: (i, 0)), pl.BlockSpec((tm, LANES), lambda i: (i, 0))),
        compiler_params=_params(("parallel",)),
        name="outproj_ln1_router",
    )(o_diff, o_nsa, w_out, x2, g, b, wr)


def _row_copy(src_hbm, dst, sem, src_row, dst_row):
    return pltpu.make_async_copy(src_hbm.at[pl.ds(src_row, 1)], dst.at[pl.ds(dst_row, 1)], sem)


def _gather_kernel(rows_ref, x_hbm, o_ref, sem):
    tm = o_ref.shape[0]

    def issue(r, carry):
        _row_copy(x_hbm, o_ref, sem, rows_ref[0, 0, r], r).start()
        return carry

    def drain(r, carry):
        _row_copy(x_hbm, o_ref, sem, 0, r).wait()
        return carry

    lax.fori_loop(0, tm, issue, 0)
    lax.fori_loop(0, tm, drain, 0)


def _gather_call(rows, x1, tm):
    n_tiles = rows.shape[0]
    d = x1.shape[1]
    return pl.pallas_call(
        _gather_kernel,
        out_shape=jax.ShapeDtypeStruct((n_tiles * tm, d), x1.dtype),
        grid=(n_tiles,),
        in_specs=[
            pl.BlockSpec((1, 1, tm), lambda i: (i, 0, 0), memory_space=pltpu.SMEM),
            pl.BlockSpec(memory_space=pl.ANY),
        ],
        out_specs=pl.BlockSpec((tm, d), lambda i: (i, 0)),
        scratch_shapes=[pltpu.SemaphoreType.DMA(())],
        compiler_params=_params(("arbitrary",)),
        name="moe_gather",
    )(rows, x1)


def _moe_kernel(te_ref, nu_ref, x_ref, wg_ref, wu_ref, wd_ref, o_ref, wg_sc, wu_sc, wd_sc):
    i = pl.program_id(0)
    prev = te_ref[jnp.maximum(i - 1, 0)]

    @pl.when((i == 0) | (te_ref[i] != prev))
    def _():
        wg_sc[...] = wg_ref[...].astype(BF16)
        wu_sc[...] = wu_ref[...].astype(BF16)
        wd_sc[...] = wd_ref[...].astype(BF16)

    @pl.when(i < nu_ref[0])
    def _():
        xb = x_ref[...].astype(BF16)
        gate = _dot(xb, wg_sc[...])
        up = _dot(xb, wu_sc[...])
        h = (gate * jax.nn.sigmoid(gate) * up).astype(BF16)
        o_ref[...] = _dot(h, wd_sc[...])

    @pl.when(i >= nu_ref[0])
    def _():
        o_ref[...] = jnp.zeros(o_ref.shape, o_ref.dtype)


def _moe_call(tile_expert, n_used, xs, w_gate, w_up, w_down, tm):
    s, d = xs.shape
    f = w_gate.shape[2]
    grid_spec = pltpu.PrefetchScalarGridSpec(
        num_scalar_prefetch=2,
        grid=(s // tm,),
        in_specs=[
            pl.BlockSpec((tm, d), lambda i, te, nu: (i, 0)),
            pl.BlockSpec((None, d, f), lambda i, te, nu: (te[i], 0, 0)),
            pl.BlockSpec((None, d, f), lambda i, te, nu: (te[i], 0, 0)),
            pl.BlockSpec((None, f, d), lambda i, te, nu: (te[i], 0, 0)),
        ],
        out_specs=pl.BlockSpec((tm, d), lambda i, te, nu: (i, 0)),
        scratch_shapes=[pltpu.VMEM((d, f), BF16), pltpu.VMEM((d, f), BF16), pltpu.VMEM((f, d), BF16)],
    )
    return pl.pallas_call(
        _moe_kernel,
        out_shape=jax.ShapeDtypeStruct((s, d), F32),
        grid_spec=grid_spec,
        compiler_params=_params(("arbitrary",)),
        name="moe_experts",
    )(tile_expert, n_used, xs, w_gate, w_up, w_down)


def _combine_kernel(slots_ref, y_hbm, x1_ref, route_ref, g_ref, b_ref, o_ref, buf, sem):
    tm = x1_ref.shape[0]

    def issue(r, carry):
        _row_copy(y_hbm, buf.at[0], sem, slots_ref[0, 0, 2 * r], r).start()
        _row_copy(y_hbm, buf.at[1], sem, slots_ref[0, 0, 2 * r + 1], r).start()
        return carry

    def drain(r, carry):
        _row_copy(y_hbm, buf.at[0], sem, 0, r).wait()
        _row_copy(y_hbm, buf.at[1], sem, 0, r).wait()
        return carry

    lax.fori_loop(0, tm, issue, 0)
    lax.fori_loop(0, tm, drain, 0)
    route = route_ref[...]
    y = route[:, 2:3] * buf[0] + route[:, 3:4] * buf[1]
    o_ref[...] = _layernorm(DN_ALPHA * x1_ref[...] + y, g_ref[...], b_ref[...])


def _combine_call(slots, y, x1, route, g, b, tm):
    n, d = x1.shape
    return pl.pallas_call(
        _combine_kernel,
        out_shape=jax.ShapeDtypeStruct((n, d), F32),
        grid=(n // tm,),
        in_specs=[
            pl.BlockSpec((1, 1, 2 * tm), lambda i: (i, 0, 0), memory_space=pltpu.SMEM),
            pl.BlockSpec(memory_space=pl.ANY),
            pl.BlockSpec((tm, d), lambda i: (i, 0)),
            pl.BlockSpec((tm, LANES), lambda i: (i, 0)),
            pl.BlockSpec((1, d), lambda i: (0, 0)),
            pl.BlockSpec((1, d), lambda i: (0, 0)),
        ],
        out_specs=pl.BlockSpec((tm, d), lambda i: (i, 0)),
        scratch_shapes=[pltpu.VMEM((2, tm, d), F32), pltpu.SemaphoreType.DMA(())],
        compiler_params=_params(("arbitrary",)),
        name="moe_combine_ln2",
    )(slots, y, x1, route, g, b)


def _moe_plan(e_idx, tm, n_tiles):
    n = e_idx.shape[0]
    flat = e_idx.reshape(-1)
    onehot = (flat[:, None] == jnp.arange(N_EXPERTS, dtype=jnp.int32)[None, :]).astype(jnp.int32)
    csum = jnp.cumsum(onehot, axis=0)
    rank = jnp.take_along_axis(csum, flat[:, None], axis=1)[:, 0] - 1
    counts = csum[-1]
    ptiles = (counts + tm - 1) // tm
    tile_end = jnp.cumsum(ptiles)
    slot = (tile_end - ptiles)[flat] * tm + rank
    rows = jnp.zeros((n_tiles * tm,), jnp.int32).at[slot].set(jnp.arange(2 * n, dtype=jnp.int32) // 2)
    tile_expert = jnp.minimum(
        jnp.searchsorted(tile_end, jnp.arange(n_tiles, dtype=jnp.int32), side="right"), N_EXPERTS - 1
    ).astype(jnp.int32)
    return slot.astype(jnp.int32), rows, tile_expert, tile_end[-1:].astype(jnp.int32)


def _rope_tables(seq):
    half = ROT_DIM // 2
    inv_freq = ROPE_THETA ** (-jnp.arange(0, ROT_DIM, 2, dtype=F32) / ROT_DIM)
    ang = jnp.arange(seq, dtype=F32)[:, None] * inv_freq[None, :]
    cos, sin = jnp.cos(ang), jnp.sin(ang)
    pad1 = jnp.ones((seq, HEAD_DIM - ROT_DIM), F32)
    pad0 = jnp.zeros((seq, HEAD_DIM - ROT_DIM), F32)
    return (jnp.concatenate([cos, cos, pad1], axis=1), jnp.concatenate([-sin, sin, pad0], axis=1))


def _layer(x, w_in, diff_lambda, diff_subln_g, cmp_pos, cmp_w1, cmp_b1, cmp_w2, cmp_b2, w_out,
           ln1_g, ln1_b, router_group, router_expert, w_gate, w_up, w_down, ln2_g, ln2_b, lambda_init):
    batch, seq, d = x.shape
    n = batch * seq
    x2 = x.reshape(n, d)

    seg = lambda a, b: w_in[:, a:b]
    pad = lambda w: jnp.pad(w, ((0, 0), (0, LANES - w.shape[1])))
    gate_w = seg(5632, 5656)
    w_aug = jnp.concatenate(
        [seg(3072, 4096), seg(3072, 4096), seg(0, 1024), seg(1024, 2048), seg(2048, 3072), seg(4096, 5632),
         pad(gate_w[:, :12]), pad(gate_w[:, 12:])], axis=1).astype(BF16)
    flags = jnp.zeros((N_UNITS // 2,), jnp.int32).at[jnp.array(ROPE_BLOCKS)].set(1)
    cos_t, sin_t = _rope_tables(seq)
    proj = _proj_call(x2.astype(BF16), w_aug, flags, cos_t, sin_t, seq, tm=min(1024, seq))

    nt = seq // CMP_STRIDE
    half_feat = CMP_STRIDE * HEAD_DIM
    r = proj[:, U_KC * LANES:U_KS * LANES].reshape(batch, nt, CMP_STRIDE, 4, HEAD_DIM)
    r = jnp.transpose(r, (0, 3, 1, 2, 4)).reshape(batch, 4, nt, half_feat)
    w1cat = jnp.concatenate([cmp_w1[:, :half_feat], cmp_w1[:, half_feat:]], axis=2).astype(BF16)
    kvc = _compress_call(r, w1cat, cmp_pos.reshape(2, 2, half_feat), cmp_b1[:, None, :],
                         cmp_w2.astype(BF16), cmp_b2[:, None, :])

    o_diff = _diff_call(proj, diff_lambda, diff_subln_g[None, :], batch, seq, min(512, seq), lambda_init)

    ns = seq // SLC_BLOCK
    ci = np.arange(nt)[None, :] * CMP_STRIDE
    sj = np.arange(ns)[:, None] * SLC_BLOCK
    ovt = jnp.asarray((ci < sj + SLC_BLOCK) & (ci + CMP_BLOCK > sj) & (np.arange(nt)[None, :] < nt - 1), dtype=BF16)
    emat = jnp.asarray(np.arange(seq)[None, :] // SLC_BLOCK == np.arange(ns)[:, None], dtype=BF16)
    o_nsa = _nsa_call(proj, kvc, ovt, emat, batch, seq, tq=128, tk=min(512, seq))

    wr = jnp.pad(jnp.concatenate([router_group, router_expert], axis=1),
                 ((0, 0), (0, LANES - N_GROUPS - N_EXPERTS)))
    x1, route = _outproj_call(o_diff, o_nsa, w_out.astype(BF16), x2, ln1_g[None, :], ln1_b[None, :], wr, tm=512)

    tm = 256
    n_tiles = 2 * n // tm + N_EXPERTS
    e_idx = route[:, 0:2].astype(jnp.int32)
    slot, rows, tile_expert, n_used = _moe_plan(e_idx, tm, n_tiles)
    xs = _gather_call(rows.reshape(n_tiles, 1, tm), x1, tm)
    ys = _moe_call(tile_expert, n_used, xs, w_gate, w_up, w_down, tm)
    out = _combine_call(slot.reshape(n // tm, 1, 2 * tm), ys, x1, route, ln2_g[None, :], ln2_b[None, :], tm)
    return out.reshape(batch, seq, d)


def kernel(x, w_in, diff_lambda, diff_subln_g, cmp_pos, cmp_w1, cmp_b1, cmp_w2, cmp_b2, w_out, ln1_g, ln1_b,
           router_group, router_expert, expert_w_gate, expert_w_up, expert_w_down, ln2_g, ln2_b):
    for l in range(DEPTH):
        lambda_init = 0.8 - 0.6 * math.exp(-0.3 * l)
        x = _layer(x, w_in[l], diff_lambda[l], diff_subln_g[l], cmp_pos[l], cmp_w1[l], cmp_b1[l], cmp_w2[l],
                   cmp_b2[l], w_out[l], ln1_g[l], ln1_b[l], router_group[l], router_expert[l],
                   expert_w_gate[l], expert_w_up[l], expert_w_down[l], ln2_g[l], ln2_b[l], lambda_init)
    return x
```

```python
import functools
import math

import numpy as np
import jax
import jax.numpy as jnp
from jax import lax
from jax.experimental import pallas as pl
from jax.experimental.pallas import tpu as pltpu

F32 = jnp.float32
BF16 = jnp.bfloat16

D_MODEL = 2048
HEAD_DIM = 128
ROT_DIM = HEAD_DIM // 4
ROPE_THETA = 500000.0
NEG_INF = -1e30
BIG = 1e30
EPS = 1e-5

DIFF_HEADS = 4
DIFF_VDIM = 2 * HEAD_DIM

NSA_HEADS = 8
NSA_KV_GROUPS = 2
NSA_HPG = NSA_HEADS // NSA_KV_GROUPS
CMP_BLOCK = 32
CMP_STRIDE = 16
CMP_HIDDEN = 256
SLC_BLOCK = 64
SLC_TOPK = 16
WINDOW = 512

N_GROUPS = 4
EXPERTS_PER_GROUP = 8
N_EXPERTS = N_GROUPS * EXPERTS_PER_GROUP
EXPERT_HIDDEN = 512

DEPTH = 1
DN_ALPHA = (2.0 * DEPTH) ** 0.25

LANES = 128
VMEM_LIMIT = 56 * 1024 * 1024

U_NQ, U_NQR, U_DQ, U_DK, U_DV = 0, 8, 16, 24, 32
U_KC, U_VC, U_KS, U_VS, U_KW, U_VW, U_GATE = 40, 42, 44, 46, 48, 50, 52
N_UNITS = 54
ROPE_BLOCKS = tuple(range(4, 16)) + (22, 24)


def _dot(a, b):
    return jnp.dot(a, b, preferred_element_type=F32)


def _dot_nt(a, b):
    return lax.dot_general(a, b, (((1,), (1,)), ((), ())), preferred_element_type=F32)


def _params(sem, vmem=VMEM_LIMIT):
    return pltpu.CompilerParams(dimension_semantics=sem, vmem_limit_bytes=vmem)


def _proj_kernel(flags_ref, x_ref, w_ref, cos_ref, sin_ref, o_ref):
    j = pl.program_id(1)
    acc = _dot(x_ref[...], w_ref[...])

    @pl.when(flags_ref[j] == 0)
    def _():
        o_ref[...] = acc.astype(o_ref.dtype)

    @pl.when(flags_ref[j] != 0)
    def _():
        c = cos_ref[...]
        s = sin_ref[...]
        lane = lax.broadcasted_iota(jnp.int32, c.shape, 1)
        half = ROT_DIM // 2
        for hh in range(2):
            a = acc[:, hh * LANES:(hh + 1) * LANES]
            partner = jnp.where(lane < half, pltpu.roll(a, LANES - half, 1), pltpu.roll(a, half, 1))
            o_ref[:, hh * LANES:(hh + 1) * LANES] = (a * c + partner * s).astype(o_ref.dtype)


def _proj_call(xb, w_aug, flags, cos_t, sin_t, seq, tm):
    n, d = xb.shape
    nj = w_aug.shape[1] // 256
    tpb = seq // tm
    grid_spec = pltpu.PrefetchScalarGridSpec(
        num_scalar_prefetch=1,
        grid=(n // tm, nj),
        in_specs=[
            pl.BlockSpec((tm, d), lambda i, j, f: (i, 0)),
            pl.BlockSpec((d, 256), lambda i, j, f: (0, j)),
            pl.BlockSpec((tm, LANES), lambda i, j, f: (i % tpb, 0)),
            pl.BlockSpec((tm, LANES), lambda i, j, f: (i % tpb, 0)),
        ],
        out_specs=pl.BlockSpec((tm, 256), lambda i, j, f: (i, j)),
    )
    return pl.pallas_call(
        _proj_kernel,
        out_shape=jax.ShapeDtypeStruct((n, w_aug.shape[1]), BF16),
        grid_spec=grid_spec,
        compiler_params=_params(("parallel", "arbitrary")),
        name="proj",
    )(flags, xb, w_aug, cos_t, sin_t)


def _compress_kernel(r_ref, w1_ref, pos_ref, b1_ref, w2_ref, b2_ref, o_ref):
    nt = r_ref.shape[0]
    ab = _dot(r_ref[...], w1_ref[...])
    pos = pos_ref[...]
    pa = jnp.broadcast_to(pos[0:1], (8, pos.shape[1])).astype(BF16)
    pb = jnp.broadcast_to(pos[1:2], (8, pos.shape[1])).astype(BF16)
    const = _dot(pa, w1_ref[:, :CMP_HIDDEN])[0:1] + _dot(pb, w1_ref[:, CMP_HIDDEN:])[0:1]
    h = ab[:, :CMP_HIDDEN] + pltpu.roll(ab[:, CMP_HIDDEN:], nt - 1, 0) + const + b1_ref[...]
    h = jax.nn.gelu(h)
    o_ref[...] = (_dot(h.astype(BF16), w2_ref[...]) + b2_ref[...]).astype(o_ref.dtype)


def _compress_call(r, w1cat, pos2, b1, w2, b2):
    b, four, nt, k = r.shape
    return pl.pallas_call(
        _compress_kernel,
        out_shape=jax.ShapeDtypeStruct((b, four, nt, HEAD_DIM), BF16),
        grid=(b, four),
        in_specs=[
            pl.BlockSpec((None, None, nt, k), lambda i, c: (i, c, 0, 0)),
            pl.BlockSpec((None, k, 2 * CMP_HIDDEN), lambda i, c: (c // 2, 0, 0)),
            pl.BlockSpec((None, 2, k), lambda i, c: (c // 2, 0, 0)),
            pl.BlockSpec((None, 1, CMP_HIDDEN), lambda i, c: (c // 2, 0, 0)),
            pl.BlockSpec((None, CMP_HIDDEN, HEAD_DIM), lambda i, c: (c // 2, 0, 0)),
            pl.BlockSpec((None, 1, HEAD_DIM), lambda i, c: (c // 2, 0, 0)),
        ],
        out_specs=pl.BlockSpec((None, None, nt, HEAD_DIM), lambda i, c: (i, c, 0, 0)),
        compiler_params=_params(("parallel", "parallel")),
        name="compress",
    )(r, w1cat, pos2, b1, w2, b2)


EXP2_SCALE = HEAD_DIM ** -0.5 * math.log2(math.e)


def _transpose_into(src_ref, dst_ref):
    def body(c, carry):
        off = pl.multiple_of(c * LANES, LANES)
        dst_ref[:, pl.ds(off, LANES)] = src_ref[pl.ds(off, LANES), :].astype(F32).T.astype(dst_ref.dtype)
        return carry

    lax.fori_loop(0, src_ref.shape[0] // LANES, body, 0)


def _diff_kernel(dl_ref, q_ref, k_ref, v_ref, g_ref, o_ref, vt_sc, acc_sc, *, tq, lambda_init):
    qi = pl.program_id(2)

    @pl.when(qi == 0)
    def _():
        for c in range(2):
            _transpose_into(v_ref.at[:, c * LANES:(c + 1) * LANES], vt_sc.at[c * LANES:(c + 1) * LANES, :])

    acc_sc[...] = jnp.zeros(acc_sc.shape, F32)
    q = q_ref[...]
    qloc = lax.broadcasted_iota(jnp.int32, (1, tq), 1)

    def step(j, masked, carry):
        off = pl.multiple_of(j * tq, tq)
        kt = k_ref[pl.ds(off, tq), :]
        vt = vt_sc[:, pl.ds(off, tq)]
        out = []
        for c in range(2):
            m_old, l_old = carry[2 * c], carry[2 * c + 1]
            s = _dot_nt(kt[:, c * HEAD_DIM:(c + 1) * HEAD_DIM], q[:, c * HEAD_DIM:(c + 1) * HEAD_DIM])
            if masked:
                kloc = lax.broadcasted_iota(jnp.int32, s.shape, 0)
                s = jnp.where(kloc <= qloc, s, NEG_INF)
            m_new = jnp.maximum(m_old, jnp.max(s, axis=0, keepdims=True))
            p = jnp.exp2((s - m_new) * EXP2_SCALE)
            alpha = jnp.exp2((m_old - m_new) * EXP2_SCALE)
            l_new = alpha * l_old + jnp.sum(p, axis=0, keepdims=True)
            acc_sc[c] = alpha * acc_sc[c] + _dot(vt, p.astype(BF16))
            out += [m_new, l_new]
        return tuple(out)

    init = (jnp.full((1, tq), NEG_INF, F32), jnp.zeros((1, tq), F32)) * 2
    carry = lax.fori_loop(0, qi, lambda j, c: step(j, False, c), init)
    _, l0, _, l1 = step(qi, True, carry)

    dl = dl_ref[...]
    lam = (jnp.exp(jnp.sum(dl[0:1] * dl[1:2], axis=1, keepdims=True))
           - jnp.exp(jnp.sum(dl[2:3] * dl[3:4], axis=1, keepdims=True)) + lambda_init)
    o = acc_sc[0] * (1.0 / l0) - lam * (acc_sc[1] * (1.0 / l1))
    o = o * (lax.rsqrt(jnp.mean(o * o, axis=0, keepdims=True) + EPS) * (1.0 - lambda_init))
    for c in range(DIFF_VDIM // LANES):
        for r in range(tq // LANES):
            blk = o[c * LANES:(c + 1) * LANES, r * LANES:(r + 1) * LANES].T
            o_ref[r * LANES:(r + 1) * LANES, c * LANES:(c + 1) * LANES] = (
                blk * g_ref[:, c * LANES:(c + 1) * LANES]).astype(o_ref.dtype)


def _diff_call(proj, dl, g, batch, seq, tq, lambda_init):
    nq = seq // tq
    kern = functools.partial(_diff_kernel, tq=tq, lambda_init=lambda_init)
    return pl.pallas_call(
        kern,
        out_shape=jax.ShapeDtypeStruct((batch * seq, DIFF_HEADS * DIFF_VDIM), BF16),
        grid=(batch, DIFF_HEADS, nq),
        in_specs=[
            pl.BlockSpec((4, HEAD_DIM), lambda b, h, i: (0, 0)),
            pl.BlockSpec((tq, 256), lambda b, h, i: (b * nq + i, U_DQ // 2 + h)),
            pl.BlockSpec((seq, 256), lambda b, h, i: (b, U_DK // 2 + h)),
            pl.BlockSpec((seq, 256), lambda b, h, i: (b, U_DV // 2 + h)),
            pl.BlockSpec((1, DIFF_VDIM), lambda b, h, i: (0, 0)),
        ],
        out_specs=pl.BlockSpec((tq, DIFF_VDIM), lambda b, h, i: (b * nq + i, h)),
        scratch_shapes=[
            pltpu.VMEM((DIFF_VDIM, seq), BF16),
            pltpu.VMEM((2, DIFF_VDIM, tq), F32),
        ],
        compiler_params=_params(("arbitrary", "arbitrary", "arbitrary")),
        name="diff_attn",
    )(dl, proj, proj, proj, g)


def _stack_heads(x):
    return jnp.concatenate([x[:, h * HEAD_DIM:(h + 1) * HEAD_DIM] for h in range(NSA_HPG)], axis=0)


def _nsa_kernel(q_ref, qr_ref, gate_ref, kc_ref, vc_ref, ks_ref, vs_ref, kw_ref, vw_ref, ovt_ref,
                o_ref, vct_sc, vst_sc, vwt_sc, bias_sc, acc_sc, *, tq, tk, seq, top_k):
    qi = pl.program_id(2)
    q0 = qi * tq
    rows = NSA_HPG * tq
    nt = kc_ref.shape[0]
    ns = ovt_ref.shape[0]

    @pl.when(qi == 0)
    def _():
        _transpose_into(vc_ref, vct_sc)
        _transpose_into(vs_ref, vst_sc)
        _transpose_into(vw_ref, vwt_sc)

    qs = _stack_heads(q_ref[...])
    qr = _stack_heads(qr_ref[...])
    qpos = q0 + (lax.broadcasted_iota(jnp.int32, (1, rows), 1) & (tq - 1))

    s_c = _dot_nt(kc_ref[...], qs)
    cend = lax.broadcasted_iota(jnp.int32, (nt, rows), 0) * CMP_STRIDE + (CMP_BLOCK - 1)
    valid_c = cend <= qpos
    s_c = jnp.where(valid_c, s_c, NEG_INF)
    m_c = jnp.max(s_c, axis=0, keepdims=True)
    e_c = jnp.where(valid_c, jnp.exp2((s_c - m_c) * EXP2_SCALE), 0.0)
    l_c = jnp.sum(e_c, axis=0, keepdims=True)
    p_c = e_c * (1.0 / jnp.where(l_c > 0.0, l_c, 1.0))
    o_c = _dot(vct_sc[...], p_c.astype(BF16))

    p_sum = p_c[:, 0:tq]
    for h in range(1, NSA_HPG):
        p_sum = p_sum + p_c[:, h * tq:(h + 1) * tq]
    p_hi = p_sum.astype(BF16)
    p_lo = (p_sum - p_hi.astype(F32)).astype(BF16)
    imp = _dot(ovt_ref[...], p_hi) + _dot(ovt_ref[...], p_lo)

    blk = lax.broadcasted_iota(jnp.int32, (ns, tq), 0)
    qpos_l = q0 + lax.broadcasted_iota(jnp.int32, (ns, tq), 1)
    cur = lax.shift_right_logical(qpos_l, int(math.log2(SLC_BLOCK)))
    valid_s = blk <= cur
    forced = (blk == 0) | (blk == cur) | (blk == cur - 1)
    work = jnp.where(valid_s & forced, BIG, jnp.where(valid_s, imp, NEG_INF))
    sel = jnp.zeros((ns, tq), F32)
    blk_f = blk.astype(F32)
    for _ in range(top_k):
        mx = jnp.max(work, axis=0, keepdims=True)
        idx = jnp.min(jnp.where(work == mx, blk_f, float(ns)), axis=0, keepdims=True)
        pick = blk_f == idx
        sel = jnp.where(pick, 1.0, sel)
        work = jnp.where(pick, -jnp.inf, work)
    bias_sc[...] = (sel - 1.0) * BIG

    acc_sc[...] = jnp.zeros(acc_sc.shape, F32)
    nb = tk // SLC_BLOCK

    def sel_step(j, causal, carry):
        m_old, l_old = carry
        off = pl.multiple_of(j * tk, tk)
        s = _dot_nt(ks_ref[pl.ds(off, tk), :], qr)
        if causal:
            kpos = off + lax.broadcasted_iota(jnp.int32, (tk, rows), 0)
            s = jnp.where(kpos <= qpos, s, NEG_INF)
        bias = bias_sc[pl.ds(pl.multiple_of(j * nb, nb), nb), :]
        bias = jnp.concatenate([bias] * NSA_HPG, axis=1)
        s3 = s.reshape(nb, SLC_BLOCK, rows) + bias[:, None, :]
        m_new = jnp.maximum(m_old, jnp.max(jnp.max(s3, axis=1), axis=0, keepdims=True))
        p3 = jnp.exp2((s3 - m_new) * EXP2_SCALE)
        alpha = jnp.exp2((m_old - m_new) * EXP2_SCALE)
        l_new = alpha * l_old + jnp.sum(jnp.sum(p3, axis=1), axis=0, keepdims=True)
        p = p3.reshape(tk, rows).astype(BF16)
        acc_sc[...] = alpha * acc_sc[...] + _dot(vst_sc[:, pl.ds(off, tk)], p)
        return m_new, l_new

    j_last = q0 // tk
    init = (jnp.full((1, rows), NEG_INF, F32), jnp.zeros((1, rows), F32))
    carry = lax.fori_loop(0, j_last, lambda j, c: sel_step(j, False, c), init)
    _, l_s = sel_step(j_last, True, carry)
    o_s = acc_sc[...] * (1.0 / l_s)

    slab = min(WINDOW + tq, seq)
    start = pl.multiple_of(jnp.maximum(q0 - WINDOW, 0), tq)
    s_w = _dot_nt(kw_ref[pl.ds(start, slab), :], qr)
    dpos = qpos - (start + lax.broadcasted_iota(jnp.int32, (slab, rows), 0))
    s_w = jnp.where((dpos >= 0) & (dpos < WINDOW), s_w, NEG_INF)
    p_w = jnp.exp2((s_w - jnp.max(s_w, axis=0, keepdims=True)) * EXP2_SCALE)
    o_w = _dot(vwt_sc[:, pl.ds(start, slab)], p_w.astype(BF16)) * (1.0 / jnp.sum(p_w, axis=0, keepdims=True))

    gates = jax.nn.sigmoid(gate_ref[...].astype(F32)).T
    for h in range(NSA_HPG):
        sl = slice(h * tq, (h + 1) * tq)
        o = (gates[3 * h:3 * h + 1] * o_c[:, sl] + gates[3 * h + 1:3 * h + 2] * o_s[:, sl]
             + gates[3 * h + 2:3 * h + 3] * o_w[:, sl])
        for r in range(tq // LANES):
            o_ref[r * LANES:(r + 1) * LANES, h * HEAD_DIM:(h + 1) * HEAD_DIM] = (
                o[:, r * LANES:(r + 1) * LANES].T.astype(o_ref.dtype))


def _nsa_call(proj, kvc, ovt, batch, seq, tq, tk):
    nq = seq // tq
    nt = kvc.shape[2]
    ns = ovt.shape[0]
    width = NSA_HPG * HEAD_DIM
    kern = functools.partial(_nsa_kernel, tq=tq, tk=tk, seq=seq, top_k=min(SLC_TOPK, ns))

    def col(unit):
        return pl.BlockSpec((seq, HEAD_DIM), lambda b, g, i: (b, unit + g))

    return pl.pallas_call(
        kern,
        out_shape=jax.ShapeDtypeStruct((batch * seq, NSA_HEADS * HEAD_DIM), BF16),
        grid=(batch, NSA_KV_GROUPS, nq),
        in_specs=[
            pl.BlockSpec((tq, width), lambda b, g, i: (b * nq + i, U_NQ // 4 + g)),
            pl.BlockSpec((tq, width), lambda b, g, i: (b * nq + i, U_NQR // 4 + g)),
            pl.BlockSpec((tq, LANES), lambda b, g, i: (b * nq + i, U_GATE + g)),
            pl.BlockSpec((None, None, nt, HEAD_DIM), lambda b, g, i: (b, g, 0, 0)),
            pl.BlockSpec((None, None, nt, HEAD_DIM), lambda b, g, i: (b, 2 + g, 0, 0)),
            col(U_KS), col(U_VS), col(U_KW), col(U_VW),
            pl.BlockSpec((ns, nt), lambda b, g, i: (0, 0)),
        ],
        out_specs=pl.BlockSpec((tq, width), lambda b, g, i: (b * nq + i, g)),
        scratch_shapes=[
            pltpu.VMEM((HEAD_DIM, nt), BF16),
            pltpu.VMEM((HEAD_DIM, seq), BF16),
            pltpu.VMEM((HEAD_DIM, seq), BF16),
            pltpu.VMEM((ns, tq), F32),
            pltpu.VMEM((HEAD_DIM, NSA_HPG * tq), F32),
        ],
        compiler_params=_params(("arbitrary", "arbitrary", "arbitrary")),
        name="nsa_attn",
    )(proj, proj, proj, kvc, kvc, proj, proj, proj, proj, ovt)


def _layernorm(y, g, b):
    mu = jnp.mean(y, axis=1, keepdims=True)
    yc = y - mu
    var = jnp.mean(yc * yc, axis=1, keepdims=True)
    return yc * lax.rsqrt(var + EPS) * g + b


def _outproj_kernel(od_ref, on_ref, w_ref, x_ref, g_ref, b_ref, wr_ref, x1_ref, route_ref):
    half = od_ref.shape[1]
    h = _dot(od_ref[...], w_ref[:half, :]) + _dot(on_ref[...], w_ref[half:, :])
    x1 = _layernorm(DN_ALPHA * x_ref[...] + h, g_ref[...], b_ref[...])
    x1_ref[...] = x1

    wr = wr_ref[...]
    w_hi = wr.astype(BF16)
    w_lo = (wr - w_hi.astype(F32)).astype(BF16)
    x_hi = x1.astype(BF16)
    x_lo = (x1 - x_hi.astype(F32)).astype(BF16)
    logits = _dot(x_hi, w_hi) + (_dot(x_lo, w_hi) + _dot(x_hi, w_lo))

    lane = lax.broadcasted_iota(jnp.int32, logits.shape, 1).astype(F32)
    ninf = -jnp.inf
    gl = jnp.where(lane < N_GROUPS, logits, ninf)
    gmax = jnp.max(gl, axis=1, keepdims=True)
    g_w = 1.0 / jnp.sum(jnp.exp(gl - gmax), axis=1, keepdims=True)
    g_sel = jnp.min(jnp.where(gl == gmax, lane, float(LANES)), axis=1, keepdims=True)
    lo = N_GROUPS + EXPERTS_PER_GROUP * g_sel
    el = jnp.where((lane >= lo) & (lane < lo + EXPERTS_PER_GROUP), logits, ninf)
    e1 = jnp.max(el, axis=1, keepdims=True)
    i1 = jnp.min(jnp.where(el == e1, lane, float(LANES)), axis=1, keepdims=True)
    el2 = jnp.where(lane == i1, ninf, el)
    e2 = jnp.max(el2, axis=1, keepdims=True)
    i2 = jnp.min(jnp.where(el2 == e2, lane, float(LANES)), axis=1, keepdims=True)
    r = jnp.exp(e2 - e1)
    w1 = g_w / (1.0 + r)
    w2 = g_w * r / (1.0 + r)
    route = jnp.where(lane == 0, i1 - N_GROUPS,
                      jnp.where(lane == 1, i2 - N_GROUPS,
                                jnp.where(lane == 2, w1, jnp.where(lane == 3, w2, 0.0))))
    route_ref[...] = route


def _outproj_call(o_diff, o_nsa, w_out, x2, g, b, wr, tm):
    n, d = x2.shape
    half = o_diff.shape[1]
    return pl.pallas_call(
        _outproj_kernel,
        out_shape=(jax.ShapeDtypeStruct((n, d), F32), jax.ShapeDtypeStruct((n, LANES), F32)),
        grid=(n // tm,),
        in_specs=[
            pl.BlockSpec((tm, half), lambda i: (i, 0)),
            pl.BlockSpec((tm, half), lambda i: (i, 0)),
            pl.BlockSpec((2 * half, d), lambda i: (0, 0)),
            pl.BlockSpec((tm, d), lambda i: (i, 0)),
            pl.BlockSpec((1, d), lambda i: (0, 0)),
            pl.BlockSpec((1, d), lambda i: (0, 0)),
            pl.BlockSpec((d, LANES), lambda i: (0, 0)),
        ],
        out_specs=(pl.BlockSpec((tm, d), lambda i: (i, 0)), pl.BlockSpec((tm, LANES), lambda i: (i, 0))),
        compiler_params=_params(("parallel",)),
        name="outproj_ln1_router",
    )(o_diff, o_nsa, w_out, x2, g, b, wr)


def _row_copy(src_hbm, dst, sem, src_row, dst_row):
    return pltpu.make_async_copy(src_hbm.at[pl.ds(src_row, 1)], dst.at[pl.ds(dst_row, 1)], sem)


def _gather_kernel(rows_ref, x_hbm, o_ref, sem):
    tm = o_ref.shape[0]

    def issue(r, carry):
        _row_copy(x_hbm, o_ref, sem, rows_ref[0, 0, r], r).start()
        return carry

    def drain(r, carry):
        _row_copy(x_hbm, o_ref, sem, 0, r).wait()
        return carry

    lax.fori_loop(0, tm, issue, 0)
    lax.fori_loop(0, tm, drain, 0)


def _gather_call(rows, x1, tm):
    n_tiles = rows.shape[0]
    d = x1.shape[1]
    return pl.pallas_call(
        _gather_kernel,
        out_shape=jax.ShapeDtypeStruct((n_tiles * tm, d), x1.dtype),
        grid=(n_tiles,),
        in_specs=[
            pl.BlockSpec((1, 1, tm), lambda i: (i, 0, 0), memory_space=pltpu.SMEM),
            pl.BlockSpec(memory_space=pl.ANY),
        ],
        out_specs=pl.BlockSpec((tm, d), lambda i: (i, 0)),
        scratch_shapes=[pltpu.SemaphoreType.DMA(())],
        compiler_params=_params(("arbitrary",)),
        name="moe_gather",
    )(rows, x1)


def _moe_kernel(te_ref, nu_ref, x_ref, wg_ref, wu_ref, wd_ref, o_ref, wg_sc, wu_sc, wd_sc):
    i = pl.program_id(0)
    prev = te_ref[jnp.maximum(i - 1, 0)]

    @pl.when((i == 0) | (te_ref[i] != prev))
    def _():
        wg_sc[...] = wg_ref[...].astype(BF16)
        wu_sc[...] = wu_ref[...].astype(BF16)
        wd_sc[...] = wd_ref[...].astype(BF16)

    @pl.when(i < nu_ref[0])
    def _():
        xb = x_ref[...].astype(BF16)
        gate = _dot(xb, wg_sc[...])
        up = _dot(xb, wu_sc[...])
        h = (gate * jax.nn.sigmoid(gate) * up).astype(BF16)
        o_ref[...] = _dot(h, wd_sc[...])

    @pl.when(i >= nu_ref[0])
    def _():
        o_ref[...] = jnp.zeros(o_ref.shape, o_ref.dtype)


def _moe_call(tile_expert, n_used, xs, w_gate, w_up, w_down, tm):
    s, d = xs.shape
    f = w_gate.shape[2]
    grid_spec = pltpu.PrefetchScalarGridSpec(
        num_scalar_prefetch=2,
        grid=(s // tm,),
        in_specs=[
            pl.BlockSpec((tm, d), lambda i, te, nu: (i, 0)),
            pl.BlockSpec((None, d, f), lambda i, te, nu: (te[i], 0, 0)),
            pl.BlockSpec((None, d, f), lambda i, te, nu: (te[i], 0, 0)),
            pl.BlockSpec((None, f, d), lambda i, te, nu: (te[i], 0, 0)),
        ],
        out_specs=pl.BlockSpec((tm, d), lambda i, te, nu: (i, 0)),
        scratch_shapes=[pltpu.VMEM((d, f), BF16), pltpu.VMEM((d, f), BF16), pltpu.VMEM((f, d), BF16)],
    )
    return pl.pallas_call(
        _moe_kernel,
        out_shape=jax.ShapeDtypeStruct((s, d), F32),
        grid_spec=grid_spec,
        compiler_params=_params(("arbitrary",)),
        name="moe_experts",
    )(tile_expert, n_used, xs, w_gate, w_up, w_down)


def _combine_kernel(slots_ref, y_hbm, x1_ref, route_ref, g_ref, b_ref, o_ref, buf, sem):
    tm = x1_ref.shape[0]

    def issue(r, carry):
        _row_copy(y_hbm, buf.at[0], sem, slots_ref[0, 0, 2 * r], r).start()
        _row_copy(y_hbm, buf.at[1], sem, slots_ref[0, 0, 2 * r + 1], r).start()
        return carry

    def drain(r, carry):
        _row_copy(y_hbm, buf.at[0], sem, 0, r).wait()
        _row_copy(y_hbm, buf.at[1], sem, 0, r).wait()
        return carry

    lax.fori_loop(0, tm, issue, 0)
    lax.fori_loop(0, tm, drain, 0)
    route = route_ref[...]
    y = route[:, 2:3] * buf[0] + route[:, 3:4] * buf[1]
    o_ref[...] = _layernorm(DN_ALPHA * x1_ref[...] + y, g_ref[...], b_ref[...])


def _combine_call(slots, y, x1, route, g, b, tm):
    n, d = x1.shape
    return pl.pallas_call(
        _combine_kernel,
        out_shape=jax.ShapeDtypeStruct((n, d), F32),
        grid=(n // tm,),
        in_specs=[
            pl.BlockSpec((1, 1, 2 * tm), lambda i: (i, 0, 0), memory_space=pltpu.SMEM),
            pl.BlockSpec(memory_space=pl.ANY),
            pl.BlockSpec((tm, d), lambda i: (i, 0)),
            pl.BlockSpec((tm, LANES), lambda i: (i, 0)),
            pl.BlockSpec((1, d), lambda i: (0, 0)),
            pl.BlockSpec((1, d), lambda i: (0, 0)),
        ],
        out_specs=pl.BlockSpec((tm, d), lambda i: (i, 0)),
        scratch_shapes=[pltpu.VMEM((2, tm, d), F32), pltpu.SemaphoreType.DMA(())],
        compiler_params=_params(("arbitrary",)),
        name="moe_combine_ln2",
    )(slots, y, x1, route, g, b)


def _moe_plan(e_idx, tm, n_tiles):
    n = e_idx.shape[0]
    flat = e_idx.reshape(-1)
    onehot = (flat[:, None] == jnp.arange(N_EXPERTS, dtype=jnp.int32)[None, :]).astype(jnp.int32)
    csum = jnp.cumsum(onehot, axis=0)
    rank = jnp.take_along_axis(csum, flat[:, None], axis=1)[:, 0] - 1
    counts = csum[-1]
    ptiles = (counts + tm - 1) // tm
    tile_end = jnp.cumsum(ptiles)
    slot = (tile_end - ptiles)[flat] * tm + rank
    rows = jnp.zeros((n_tiles * tm,), jnp.int32).at[slot].set(jnp.arange(2 * n, dtype=jnp.int32) // 2)
    tile_ids = jnp.arange(n_tiles, dtype=jnp.int32)
    tile_expert = jnp.minimum(jnp.sum((tile_end[None, :] <= tile_ids[:, None]).astype(jnp.int32), axis=1),
                              N_EXPERTS - 1)
    return slot.astype(jnp.int32), rows, tile_expert, tile_end[-1:].astype(jnp.int32)


def _rope_tables(seq):
    half = ROT_DIM // 2
    inv_freq = ROPE_THETA ** (-jnp.arange(0, ROT_DIM, 2, dtype=F32) / ROT_DIM)
    ang = jnp.arange(seq, dtype=F32)[:, None] * inv_freq[None, :]
    cos, sin = jnp.cos(ang), jnp.sin(ang)
    pad1 = jnp.ones((seq, HEAD_DIM - ROT_DIM), F32)
    pad0 = jnp.zeros((seq, HEAD_DIM - ROT_DIM), F32)
    return (jnp.concatenate([cos, cos, pad1], axis=1), jnp.concatenate([-sin, sin, pad0], axis=1))


def _layer(x, w_in, diff_lambda, diff_subln_g, cmp_pos, cmp_w1, cmp_b1, cmp_w2, cmp_b2, w_out,
           ln1_g, ln1_b, router_group, router_expert, w_gate, w_up, w_down, ln2_g, ln2_b, lambda_init):
    batch, seq, d = x.shape
    n = batch * seq
    x2 = x.reshape(n, d)

    seg = lambda a, b: w_in[:, a:b]
    pad = lambda w: jnp.pad(w, ((0, 0), (0, LANES - w.shape[1])))
    gate_w = seg(5632, 5656)
    w_aug = jnp.concatenate(
        [seg(3072, 4096), seg(3072, 4096), seg(0, 1024), seg(1024, 2048), seg(2048, 3072), seg(4096, 5632),
         pad(gate_w[:, :12]), pad(gate_w[:, 12:])], axis=1).astype(BF16)
    flags = jnp.zeros((N_UNITS // 2,), jnp.int32).at[jnp.array(ROPE_BLOCKS)].set(1)
    cos_t, sin_t = _rope_tables(seq)
    proj = _proj_call(x2.astype(BF16), w_aug, flags, cos_t, sin_t, seq, tm=min(1024, seq))

    nt = seq // CMP_STRIDE
    half_feat = CMP_STRIDE * HEAD_DIM
    r = proj[:, U_KC * LANES:U_KS * LANES].reshape(batch, nt, CMP_STRIDE, 4, HEAD_DIM)
    r = jnp.transpose(r, (0, 3, 1, 2, 4)).reshape(batch, 4, nt, half_feat)
    w1cat = jnp.concatenate([cmp_w1[:, :half_feat], cmp_w1[:, half_feat:]], axis=2).astype(BF16)
    kvc = _compress_call(r, w1cat, cmp_pos.reshape(2, 2, half_feat), cmp_b1[:, None, :],
                         cmp_w2.astype(BF16), cmp_b2[:, None, :])

    o_diff = _diff_call(proj, diff_lambda, diff_subln_g[None, :], batch, seq, min(512, seq), lambda_init)

    ns = seq // SLC_BLOCK
    ci = np.arange(nt)[None, :] * CMP_STRIDE
    sj = np.arange(ns)[:, None] * SLC_BLOCK
    ovt = jnp.asarray((ci < sj + SLC_BLOCK) & (ci + CMP_BLOCK > sj) & (np.arange(nt)[None, :] < nt - 1), dtype=BF16)
    o_nsa = _nsa_call(proj, kvc, ovt, batch, seq, tq=LANES, tk=min(512, seq))

    wr = jnp.pad(jnp.concatenate([router_group, router_expert], axis=1),
                 ((0, 0), (0, LANES - N_GROUPS - N_EXPERTS)))
    x1, route = _outproj_call(o_diff, o_nsa, w_out.astype(BF16), x2, ln1_g[None, :], ln1_b[None, :], wr, tm=512)

    tm = 256
    n_tiles = 2 * n // tm + N_EXPERTS
    e_idx = route[:, 0:2].astype(jnp.int32)
    slot, rows, tile_expert, n_used = _moe_plan(e_idx, tm, n_tiles)
    xs = _gather_call(rows.reshape(n_tiles, 1, tm), x1, tm)
    ys = _moe_call(tile_expert, n_used, xs, w_gate, w_up, w_down, tm)
    out = _combine_call(slot.reshape(n // tm, 1, 2 * tm), ys, x1, route, ln2_g[None, :], ln2_b[None, :], tm)
    return out.reshape(batch, seq, d)


def kernel(x, w_in, diff_lambda, diff_subln_g, cmp_pos, cmp_w1, cmp_b1, cmp_w2, cmp_b2, w_out, ln1_g, ln1_b,
           router_group, router_expert, expert_w_gate, expert_w_up, expert_w_down, ln2_g, ln2_b):
    for l in range(DEPTH):
        lambda_init = 0.8 - 0.6 * math.exp(-0.3 * l)
        x = _layer(x, w_in[l], diff_lambda[l], diff_subln_g[l], cmp_pos[l], cmp_w1[l], cmp_b1[l], cmp_w2[l],
                   cmp_b2[l], w_out[l], ln1_g[l], ln1_b[l], router_group[l], router_expert[l],
                   expert_w_gate[l], expert_w_up[l], expert_w_down[l], ln2_g[l], ln2_b[l], lambda_init)
    return x
```

```python
import functools
import math

import numpy as np
import jax
import jax.numpy as jnp
from jax import lax
from jax.experimental import pallas as pl
from jax.experimental.pallas import tpu as pltpu

F32 = jnp.float32
BF16 = jnp.bfloat16

D_MODEL = 2048
HEAD_DIM = 128
ROT_DIM = HEAD_DIM // 4
ROPE_THETA = 500000.0
NEG_INF = -1e30
BIG = 1e30
EPS = 1e-5

DIFF_HEADS = 4
DIFF_VDIM = 2 * HEAD_DIM

NSA_HEADS = 8
NSA_KV_GROUPS = 2
NSA_HPG = NSA_HEADS // NSA_KV_GROUPS
CMP_BLOCK = 32
CMP_STRIDE = 16
CMP_HIDDEN = 256
SLC_BLOCK = 64
SLC_TOPK = 16
WINDOW = 512

N_GROUPS = 4
EXPERTS_PER_GROUP = 8
N_EXPERTS = N_GROUPS * EXPERTS_PER_GROUP
EXPERT_HIDDEN = 512

DEPTH = 1
DN_ALPHA = (2.0 * DEPTH) ** 0.25

LANES = 128
VMEM_LIMIT = 56 * 1024 * 1024

U_NQ, U_NQR, U_DQ, U_DK, U_DV = 0, 8, 16, 24, 32
U_KC, U_VC, U_KS, U_VS, U_KW, U_VW, U_GATE = 40, 42, 44, 46, 48, 50, 52
N_UNITS = 54
ROPE_BLOCKS = tuple(range(4, 16)) + (22, 24)


def _dot(a, b):
    return jnp.dot(a, b, preferred_element_type=F32)


def _dot_nt(a, b):
    return lax.dot_general(a, b, (((1,), (1,)), ((), ())), preferred_element_type=F32)


def _params(sem, vmem=VMEM_LIMIT):
    return pltpu.CompilerParams(dimension_semantics=sem, vmem_limit_bytes=vmem)


def _proj_kernel(flags_ref, x_ref, w_ref, cos_ref, sin_ref, o_ref):
    j = pl.program_id(1)
    acc = _dot(x_ref[...], w_ref[...])

    @pl.when(flags_ref[j] == 0)
    def _():
        o_ref[...] = acc.astype(o_ref.dtype)

    @pl.when(flags_ref[j] != 0)
    def _():
        c = cos_ref[...]
        s = sin_ref[...]
        lane = lax.broadcasted_iota(jnp.int32, c.shape, 1)
        half = ROT_DIM // 2
        for hh in range(2):
            a = acc[:, hh * LANES:(hh + 1) * LANES]
            partner = jnp.where(lane < half, pltpu.roll(a, LANES - half, 1), pltpu.roll(a, half, 1))
            o_ref[:, hh * LANES:(hh + 1) * LANES] = (a * c + partner * s).astype(o_ref.dtype)


def _proj_call(xb, w_aug, flags, cos_t, sin_t, seq, tm):
    n, d = xb.shape
    nj = w_aug.shape[1] // 256
    tpb = seq // tm
    grid_spec = pltpu.PrefetchScalarGridSpec(
        num_scalar_prefetch=1,
        grid=(n // tm, nj),
        in_specs=[
            pl.BlockSpec((tm, d), lambda i, j, f: (i, 0)),
            pl.BlockSpec((d, 256), lambda i, j, f: (0, j)),
            pl.BlockSpec((tm, LANES), lambda i, j, f: (i % tpb, 0)),
            pl.BlockSpec((tm, LANES), lambda i, j, f: (i % tpb, 0)),
        ],
        out_specs=pl.BlockSpec((tm, 256), lambda i, j, f: (i, j)),
    )
    return pl.pallas_call(
        _proj_kernel,
        out_shape=jax.ShapeDtypeStruct((n, w_aug.shape[1]), BF16),
        grid_spec=grid_spec,
        compiler_params=_params(("parallel", "arbitrary")),
        name="proj",
    )(flags, xb, w_aug, cos_t, sin_t)


def _compress_kernel(r_ref, w1_ref, pos_ref, b1_ref, w2_ref, b2_ref, o_ref):
    nt = r_ref.shape[0]
    ab = _dot(r_ref[...], w1_ref[...])
    pos = pos_ref[...]
    pa = jnp.broadcast_to(pos[0:1], (8, pos.shape[1])).astype(BF16)
    pb = jnp.broadcast_to(pos[1:2], (8, pos.shape[1])).astype(BF16)
    const = _dot(pa, w1_ref[:, :CMP_HIDDEN])[0:1] + _dot(pb, w1_ref[:, CMP_HIDDEN:])[0:1]
    h = ab[:, :CMP_HIDDEN] + pltpu.roll(ab[:, CMP_HIDDEN:], nt - 1, 0) + const + b1_ref[...]
    h = jax.nn.gelu(h)
    o_ref[...] = (_dot(h.astype(BF16), w2_ref[...]) + b2_ref[...]).astype(o_ref.dtype)


def _compress_call(r, w1cat, pos2, b1, w2, b2):
    b, four, nt, k = r.shape
    return pl.pallas_call(
        _compress_kernel,
        out_shape=jax.ShapeDtypeStruct((b, four, nt, HEAD_DIM), BF16),
        grid=(b, four),
        in_specs=[
            pl.BlockSpec((None, None, nt, k), lambda i, c: (i, c, 0, 0)),
            pl.BlockSpec((None, k, 2 * CMP_HIDDEN), lambda i, c: (c // 2, 0, 0)),
            pl.BlockSpec((None, 2, k), lambda i, c: (c // 2, 0, 0)),
            pl.BlockSpec((None, 1, CMP_HIDDEN), lambda i, c: (c // 2, 0, 0)),
            pl.BlockSpec((None, CMP_HIDDEN, HEAD_DIM), lambda i, c: (c // 2, 0, 0)),
            pl.BlockSpec((None, 1, HEAD_DIM), lambda i, c: (c // 2, 0, 0)),
        ],
        out_specs=pl.BlockSpec((None, None, nt, HEAD_DIM), lambda i, c: (i, c, 0, 0)),
        compiler_params=_params(("parallel", "parallel")),
        name="compress",
    )(r, w1cat, pos2, b1, w2, b2)


EXP2_SCALE = HEAD_DIM ** -0.5 * math.log2(math.e)


def _transpose_into(src_ref, dst_ref):
    def body(c, carry):
        off = pl.multiple_of(c * LANES, LANES)
        dst_ref[:, pl.ds(off, LANES)] = src_ref[pl.ds(off, LANES), :].astype(F32).T.astype(dst_ref.dtype)
        return carry

    lax.fori_loop(0, src_ref.shape[0] // LANES, body, 0)


def _diff_kernel(dl_ref, q_ref, k_ref, v_ref, g_ref, o_ref, vt_sc, acc_sc, *, tq, lambda_init):
    qi = pl.program_id(2)

    @pl.when(qi == 0)
    def _():
        for c in range(2):
            _transpose_into(v_ref.at[:, c * LANES:(c + 1) * LANES], vt_sc.at[c * LANES:(c + 1) * LANES, :])

    acc_sc[...] = jnp.zeros(acc_sc.shape, F32)
    q = q_ref[...]
    qloc = lax.broadcasted_iota(jnp.int32, (1, tq), 1)

    def step(j, masked, carry):
        off = pl.multiple_of(j * tq, tq)
        kt = k_ref[pl.ds(off, tq), :]
        vt = vt_sc[:, pl.ds(off, tq)]
        out = []
        for c in range(2):
            m_old, l_old = carry[2 * c], carry[2 * c + 1]
            s = _dot_nt(kt[:, c * HEAD_DIM:(c + 1) * HEAD_DIM], q[:, c * HEAD_DIM:(c + 1) * HEAD_DIM])
            if masked:
                kloc = lax.broadcasted_iota(jnp.int32, s.shape, 0)
                s = jnp.where(kloc <= qloc, s, NEG_INF)
            m_new = jnp.maximum(m_old, jnp.max(s, axis=0, keepdims=True))
            p = jnp.exp2((s - m_new) * EXP2_SCALE)
            alpha = jnp.exp2((m_old - m_new) * EXP2_SCALE)
            l_new = alpha * l_old + jnp.sum(p, axis=0, keepdims=True)
            acc_sc[c] = alpha * acc_sc[c] + _dot(vt, p.astype(BF16))
            out += [m_new, l_new]
        return tuple(out)

    init = (jnp.full((1, tq), NEG_INF, F32), jnp.zeros((1, tq), F32)) * 2
    carry = lax.fori_loop(0, qi, lambda j, c: step(j, False, c), init)
    _, l0, _, l1 = step(qi, True, carry)

    dl = dl_ref[...]
    lam = (jnp.exp(jnp.sum(dl[0:1] * dl[1:2], axis=1, keepdims=True))
           - jnp.exp(jnp.sum(dl[2:3] * dl[3:4], axis=1, keepdims=True)) + lambda_init)
    o = acc_sc[0] * (1.0 / l0) - lam * (acc_sc[1] * (1.0 / l1))
    o = o * (lax.rsqrt(jnp.mean(o * o, axis=0, keepdims=True) + EPS) * (1.0 - lambda_init))
    for c in range(DIFF_VDIM // LANES):
        for r in range(tq // LANES):
            blk = o[c * LANES:(c + 1) * LANES, r * LANES:(r + 1) * LANES].T
            o_ref[r * LANES:(r + 1) * LANES, c * LANES:(c + 1) * LANES] = (
                blk * g_ref[:, c * LANES:(c + 1) * LANES]).astype(o_ref.dtype)


def _diff_call(proj, dl, g, batch, seq, tq, lambda_init):
    nq = seq // tq
    kern = functools.partial(_diff_kernel, tq=tq, lambda_init=lambda_init)
    return pl.pallas_call(
        kern,
        out_shape=jax.ShapeDtypeStruct((batch * seq, DIFF_HEADS * DIFF_VDIM), BF16),
        grid=(batch, DIFF_HEADS, nq),
        in_specs=[
            pl.BlockSpec((4, HEAD_DIM), lambda b, h, i: (0, 0)),
            pl.BlockSpec((tq, 256), lambda b, h, i: (b * nq + i, U_DQ // 2 + h)),
            pl.BlockSpec((seq, 256), lambda b, h, i: (b, U_DK // 2 + h)),
            pl.BlockSpec((seq, 256), lambda b, h, i: (b, U_DV // 2 + h)),
            pl.BlockSpec((1, DIFF_VDIM), lambda b, h, i: (0, 0)),
        ],
        out_specs=pl.BlockSpec((tq, DIFF_VDIM), lambda b, h, i: (b * nq + i, h)),
        scratch_shapes=[
            pltpu.VMEM((DIFF_VDIM, seq), BF16),
            pltpu.VMEM((2, DIFF_VDIM, tq), F32),
        ],
        compiler_params=_params(("arbitrary", "arbitrary", "arbitrary")),
        name="diff_attn",
    )(dl, proj, proj, proj, g)


def _stack_heads(x):
    return jnp.concatenate([x[:, h * HEAD_DIM:(h + 1) * HEAD_DIM] for h in range(NSA_HPG)], axis=0)


def _nsa_kernel(q_ref, qr_ref, gate_ref, kc_ref, vc_ref, ks_ref, vs_ref, kw_ref, vw_ref, ovt_ref,
                o_ref, vct_sc, vst_sc, vwt_sc, bias_sc, acc_sc, *, tq, tk, seq, top_k):
    qi = pl.program_id(2)
    q0 = qi * tq
    rows = NSA_HPG * tq
    nt = kc_ref.shape[0]
    ns = ovt_ref.shape[0]

    @pl.when(qi == 0)
    def _():
        _transpose_into(vc_ref, vct_sc)
        _transpose_into(vs_ref, vst_sc)
        _transpose_into(vw_ref, vwt_sc)

    qs = _stack_heads(q_ref[...])
    qr = _stack_heads(qr_ref[...])
    qpos = q0 + (lax.broadcasted_iota(jnp.int32, (1, rows), 1) & (tq - 1))

    s_c = _dot_nt(kc_ref[...], qs)
    cend = lax.broadcasted_iota(jnp.int32, (nt, rows), 0) * CMP_STRIDE + (CMP_BLOCK - 1)
    valid_c = cend <= qpos
    s_c = jnp.where(valid_c, s_c, NEG_INF)
    m_c = jnp.max(s_c, axis=0, keepdims=True)
    e_c = jnp.where(valid_c, jnp.exp2((s_c - m_c) * EXP2_SCALE), 0.0)
    l_c = jnp.sum(e_c, axis=0, keepdims=True)
    p_c = e_c * (1.0 / jnp.where(l_c > 0.0, l_c, 1.0))
    o_c = _dot(vct_sc[...], p_c.astype(BF16))

    p_sum = p_c[:, 0:tq]
    for h in range(1, NSA_HPG):
        p_sum = p_sum + p_c[:, h * tq:(h + 1) * tq]
    p_hi = p_sum.astype(BF16)
    p_lo = (p_sum - p_hi.astype(F32)).astype(BF16)
    imp = _dot(ovt_ref[...], p_hi) + _dot(ovt_ref[...], p_lo)

    blk = lax.broadcasted_iota(jnp.int32, (ns, tq), 0)
    qpos_l = q0 + lax.broadcasted_iota(jnp.int32, (ns, tq), 1)
    cur = lax.shift_right_logical(qpos_l, int(math.log2(SLC_BLOCK)))
    valid_s = blk <= cur
    forced = (blk == 0) | (blk == cur) | (blk == cur - 1)
    work = jnp.where(valid_s & forced, BIG, jnp.where(valid_s, imp, NEG_INF))
    sel = jnp.zeros((ns, tq), F32)
    blk_f = blk.astype(F32)
    for _ in range(top_k):
        mx = jnp.max(work, axis=0, keepdims=True)
        idx = jnp.min(jnp.where(work == mx, blk_f, float(ns)), axis=0, keepdims=True)
        pick = blk_f == idx
        sel = jnp.where(pick, 1.0, sel)
        work = jnp.where(pick, -jnp.inf, work)
    bias_sc[...] = (sel - 1.0) * BIG

    acc_sc[...] = jnp.zeros(acc_sc.shape, F32)
    nb = tk // SLC_BLOCK

    def sel_step(j, causal, carry):
        m_old, l_old = carry
        off = pl.multiple_of(j * tk, tk)
        s = _dot_nt(ks_ref[pl.ds(off, tk), :], qr)
        if causal:
            kpos = off + lax.broadcasted_iota(jnp.int32, (tk, rows), 0)
            s = jnp.where(kpos <= qpos, s, NEG_INF)
        bias = bias_sc[pl.ds(pl.multiple_of(j * nb, nb), nb), :]
        bias = jnp.concatenate([bias] * NSA_HPG, axis=1)
        s3 = s.reshape(nb, SLC_BLOCK, rows) + bias[:, None, :]
        m_new = jnp.maximum(m_old, jnp.max(jnp.max(s3, axis=1), axis=0, keepdims=True))
        p3 = jnp.exp2((s3 - m_new) * EXP2_SCALE)
        alpha = jnp.exp2((m_old - m_new) * EXP2_SCALE)
        l_new = alpha * l_old + jnp.sum(jnp.sum(p3, axis=1), axis=0, keepdims=True)
        p = p3.reshape(tk, rows).astype(BF16)
        acc_sc[...] = alpha * acc_sc[...] + _dot(vst_sc[:, pl.ds(off, tk)], p)
        return m_new, l_new

    j_last = q0 // tk
    init = (jnp.full((1, rows), NEG_INF, F32), jnp.zeros((1, rows), F32))
    carry = lax.fori_loop(0, j_last, lambda j, c: sel_step(j, False, c), init)
    _, l_s = sel_step(j_last, True, carry)
    o_s = acc_sc[...] * (1.0 / l_s)

    slab = min(WINDOW + tq, seq)
    start = pl.multiple_of(jnp.maximum(q0 - WINDOW, 0), tq)
    s_w = _dot_nt(kw_ref[pl.ds(start, slab), :], qr)
    dpos = qpos - (start + lax.broadcasted_iota(jnp.int32, (slab, rows), 0))
    s_w = jnp.where((dpos >= 0) & (dpos < WINDOW), s_w, NEG_INF)
    p_w = jnp.exp2((s_w - jnp.max(s_w, axis=0, keepdims=True)) * EXP2_SCALE)
    o_w = _dot(vwt_sc[:, pl.ds(start, slab)], p_w.astype(BF16)) * (1.0 / jnp.sum(p_w, axis=0, keepdims=True))

    gates = jax.nn.sigmoid(gate_ref[...].astype(F32)).T
    for h in range(NSA_HPG):
        sl = slice(h * tq, (h + 1) * tq)
        o = (gates[3 * h:3 * h + 1] * o_c[:, sl] + gates[3 * h + 1:3 * h + 2] * o_s[:, sl]
             + gates[3 * h + 2:3 * h + 3] * o_w[:, sl])
        for r in range(tq // LANES):
            o_ref[r * LANES:(r + 1) * LANES, h * HEAD_DIM:(h + 1) * HEAD_DIM] = (
                o[:, r * LANES:(r + 1) * LANES].T.astype(o_ref.dtype))


def _nsa_call(proj, kvc, ovt, batch, seq, tq, tk):
    nq = seq // tq
    nt = kvc.shape[2]
    ns = ovt.shape[0]
    width = NSA_HPG * HEAD_DIM
    kern = functools.partial(_nsa_kernel, tq=tq, tk=tk, seq=seq, top_k=min(SLC_TOPK, ns))

    def col(unit):
        return pl.BlockSpec((seq, HEAD_DIM), lambda b, g, i: (b, unit + g))

    return pl.pallas_call(
        kern,
        out_shape=jax.ShapeDtypeStruct((batch * seq, NSA_HEADS * HEAD_DIM), BF16),
        grid=(batch, NSA_KV_GROUPS, nq),
        in_specs=[
            pl.BlockSpec((tq, width), lambda b, g, i: (b * nq + i, U_NQ // 4 + g)),
            pl.BlockSpec((tq, width), lambda b, g, i: (b * nq + i, U_NQR // 4 + g)),
            pl.BlockSpec((tq, LANES), lambda b, g, i: (b * nq + i, U_GATE + g)),
            pl.BlockSpec((None, None, nt, HEAD_DIM), lambda b, g, i: (b, g, 0, 0)),
            pl.BlockSpec((None, None, nt, HEAD_DIM), lambda b, g, i: (b, 2 + g, 0, 0)),
            col(U_KS), col(U_VS), col(U_KW), col(U_VW),
            pl.BlockSpec((ns, nt), lambda b, g, i: (0, 0)),
        ],
        out_specs=pl.BlockSpec((tq, width), lambda b, g, i: (b * nq + i, g)),
        scratch_shapes=[
            pltpu.VMEM((HEAD_DIM, nt), BF16),
            pltpu.VMEM((HEAD_DIM, seq), BF16),
            pltpu.VMEM((HEAD_DIM, seq), BF16),
            pltpu.VMEM((ns, tq), F32),
            pltpu.VMEM((HEAD_DIM, NSA_HPG * tq), F32),
        ],
        compiler_params=_params(("arbitrary", "arbitrary", "arbitrary")),
        name="nsa_attn",
    )(proj, proj, proj, kvc, kvc, proj, proj, proj, proj, ovt)


def _layernorm(y, g, b):
    mu = jnp.mean(y, axis=1, keepdims=True)
    yc = y - mu
    var = jnp.mean(yc * yc, axis=1, keepdims=True)
    return yc * lax.rsqrt(var + EPS) * g + b


N_CHUNKS = D_MODEL // LANES
ROW_PITCH = N_CHUNKS + 1


def _store_chunk_rows(ref, val):
    tm = val.shape[0]
    for k in range(N_CHUNKS):
        ref[pl.ds(k, tm, stride=ROW_PITCH), :] = val[:, k * LANES:(k + 1) * LANES]
    for k in range(N_CHUNKS, ROW_PITCH):
        ref[pl.ds(k, tm, stride=ROW_PITCH), :] = jnp.zeros((tm, LANES), ref.dtype)


def _load_chunk_rows(ref, tm):
    return jnp.concatenate([ref[pl.ds(k, tm, stride=ROW_PITCH), :] for k in range(N_CHUNKS)], axis=1)


def _outproj_kernel(od_ref, on_ref, w_ref, x_ref, g_ref, b_ref, wr_ref, x1_ref, route_ref):
    half = od_ref.shape[1]
    h = _dot(od_ref[...], w_ref[:half, :]) + _dot(on_ref[...], w_ref[half:, :])
    x1 = _layernorm(DN_ALPHA * x_ref[...] + h, g_ref[...], b_ref[...])
    _store_chunk_rows(x1_ref, x1)

    wr = wr_ref[...]
    w_hi = wr.astype(BF16)
    w_lo = (wr - w_hi.astype(F32)).astype(BF16)
    x_hi = x1.astype(BF16)
    x_lo = (x1 - x_hi.astype(F32)).astype(BF16)
    logits = _dot(x_hi, w_hi) + (_dot(x_lo, w_hi) + _dot(x_hi, w_lo))

    lane = lax.broadcasted_iota(jnp.int32, logits.shape, 1).astype(F32)
    ninf = -jnp.inf
    gl = jnp.where(lane < N_GROUPS, logits, ninf)
    gmax = jnp.max(gl, axis=1, keepdims=True)
    g_w = 1.0 / jnp.sum(jnp.exp(gl - gmax), axis=1, keepdims=True)
    g_sel = jnp.min(jnp.where(gl == gmax, lane, float(LANES)), axis=1, keepdims=True)
    lo = N_GROUPS + EXPERTS_PER_GROUP * g_sel
    el = jnp.where((lane >= lo) & (lane < lo + EXPERTS_PER_GROUP), logits, ninf)
    e1 = jnp.max(el, axis=1, keepdims=True)
    i1 = jnp.min(jnp.where(el == e1, lane, float(LANES)), axis=1, keepdims=True)
    el2 = jnp.where(lane == i1, ninf, el)
    e2 = jnp.max(el2, axis=1, keepdims=True)
    i2 = jnp.min(jnp.where(el2 == e2, lane, float(LANES)), axis=1, keepdims=True)
    r = jnp.exp(e2 - e1)
    w1 = g_w / (1.0 + r)
    w2 = g_w * r / (1.0 + r)
    route = jnp.where(lane == 0, i1 - N_GROUPS,
                      jnp.where(lane == 1, i2 - N_GROUPS,
                                jnp.where(lane == 2, w1, jnp.where(lane == 3, w2, 0.0))))
    route_ref[...] = route


def _outproj_call(o_diff, o_nsa, w_out, x2, g, b, wr, tm):
    n, d = x2.shape
    half = o_diff.shape[1]
    return pl.pallas_call(
        _outproj_kernel,
        out_shape=(jax.ShapeDtypeStruct((n * ROW_PITCH, LANES), F32), jax.ShapeDtypeStruct((n, LANES), F32)),
        grid=(n // tm,),
        in_specs=[
            pl.BlockSpec((tm, half), lambda i: (i, 0)),
            pl.BlockSpec((tm, half), lambda i: (i, 0)),
            pl.BlockSpec((2 * half, d), lambda i: (0, 0)),
            pl.BlockSpec((tm, d), lambda i: (i, 0)),
            pl.BlockSpec((1, d), lambda i: (0, 0)),
            pl.BlockSpec((1, d), lambda i: (0, 0)),
            pl.BlockSpec((d, LANES), lambda i: (0, 0)),
        ],
        out_specs=(pl.BlockSpec((tm * ROW_PITCH, LANES), lambda i: (i, 0)),
                   pl.BlockSpec((tm, LANES), lambda i: (i, 0))),
        compiler_params=_params(("parallel",)),
        name="outproj_ln1_router",
    )(o_diff, o_nsa, w_out, x2, g, b, wr)


def _token_copy(src_hbm, dst, sem, src_tok, dst_tok):
    return pltpu.make_async_copy(src_hbm.at[pl.ds(src_tok * ROW_PITCH, N_CHUNKS)],
                                 dst.at[pl.ds(dst_tok * ROW_PITCH, N_CHUNKS)], sem)


def _wait_tokens(src_hbm, dst, sem, count):
    pltpu.make_async_copy(src_hbm.at[pl.ds(0, count * N_CHUNKS)], dst.at[pl.ds(0, count * N_CHUNKS)], sem).wait()


def _moe_kernel(te_ref, nu_ref, rows_ref, rows_next_ref, x_hbm, wg_ref, wu_ref, wd_ref, o_ref,
                xbuf, sem, wg_sc, wu_sc, wd_sc, *, tm):
    i = pl.program_id(0)
    n_used = nu_ref[0]
    slot = lax.rem(i, 2)

    def issue(ids_ref, s):
        def body(r, carry):
            _token_copy(x_hbm, xbuf.at[s], sem.at[s], ids_ref[0, 0, r], r).start()
            return carry

        lax.fori_loop(0, tm, body, 0, unroll=8)

    @pl.when(i == 0)
    def _():
        issue(rows_ref, 0)

    @pl.when(i + 1 < n_used)
    def _():
        issue(rows_next_ref, 1 - slot)

    prev = te_ref[jnp.maximum(i - 1, 0)]

    @pl.when((i == 0) | (te_ref[i] != prev))
    def _():
        wg_sc[...] = wg_ref[...].astype(BF16)
        wu_sc[...] = wu_ref[...].astype(BF16)
        wd_sc[...] = wd_ref[...].astype(BF16)

    @pl.when(i < n_used)
    def _():
        _wait_tokens(x_hbm, xbuf.at[slot], sem.at[slot], tm)
        xb = _load_chunk_rows(xbuf.at[slot], tm).astype(BF16)
        gate = _dot(xb, wg_sc[...])
        up = _dot(xb, wu_sc[...])
        h = (gate * jax.nn.sigmoid(gate) * up).astype(BF16)
        _store_chunk_rows(o_ref, _dot(h, wd_sc[...]))

    @pl.when(i >= n_used)
    def _():
        o_ref[...] = jnp.zeros(o_ref.shape, o_ref.dtype)


def _moe_call(tile_expert, n_used, rows, x1c, w_gate, w_up, w_down, tm):
    n_tiles = rows.shape[0]
    d, f = w_gate.shape[1], w_gate.shape[2]
    grid_spec = pltpu.PrefetchScalarGridSpec(
        num_scalar_prefetch=2,
        grid=(n_tiles,),
        in_specs=[
            pl.BlockSpec((1, 1, tm), lambda i, te, nu: (i, 0, 0), memory_space=pltpu.SMEM),
            pl.BlockSpec((1, 1, tm), lambda i, te, nu: (jnp.minimum(i + 1, n_tiles - 1), 0, 0),
                         memory_space=pltpu.SMEM),
            pl.BlockSpec(memory_space=pl.ANY),
            pl.BlockSpec((None, d, f), lambda i, te, nu: (te[i], 0, 0)),
            pl.BlockSpec((None, d, f), lambda i, te, nu: (te[i], 0, 0)),
            pl.BlockSpec((None, f, d), lambda i, te, nu: (te[i], 0, 0)),
        ],
        out_specs=pl.BlockSpec((tm * ROW_PITCH, LANES), lambda i, te, nu: (i, 0)),
        scratch_shapes=[
            pltpu.VMEM((2, tm * ROW_PITCH, LANES), F32),
            pltpu.SemaphoreType.DMA((2,)),
            pltpu.VMEM((d, f), BF16), pltpu.VMEM((d, f), BF16), pltpu.VMEM((f, d), BF16),
        ],
    )
    return pl.pallas_call(
        functools.partial(_moe_kernel, tm=tm),
        out_shape=jax.ShapeDtypeStruct((n_tiles * tm * ROW_PITCH, LANES), F32),
        grid_spec=grid_spec,
        compiler_params=_params(("arbitrary",)),
        name="moe_experts",
    )(tile_expert, n_used, rows, rows, x1c, w_gate, w_up, w_down)


def _combine_kernel(slots_ref, slots_next_ref, y_hbm, x1_ref, route_ref, g_ref, b_ref, o_ref, buf, sem, *, tm):
    i = pl.program_id(0)
    slot = lax.rem(i, 2)

    def issue(ids_ref, s):
        def body(r, carry):
            for k in range(2):
                _token_copy(y_hbm, buf.at[s, k], sem.at[s], ids_ref[0, 0, 2 * r + k], r).start()
            return carry

        lax.fori_loop(0, tm, body, 0, unroll=8)

    @pl.when(i == 0)
    def _():
        issue(slots_ref, 0)

    @pl.when(i + 1 < pl.num_programs(0))
    def _():
        issue(slots_next_ref, 1 - slot)

    for k in range(2):
        _wait_tokens(y_hbm, buf.at[slot, k], sem.at[slot], tm)
    route = route_ref[...]
    y = (route[:, 2:3] * _load_chunk_rows(buf.at[slot, 0], tm)
         + route[:, 3:4] * _load_chunk_rows(buf.at[slot, 1], tm))
    x1 = _load_chunk_rows(x1_ref, tm)
    o_ref[...] = _layernorm(DN_ALPHA * x1 + y, g_ref[...], b_ref[...])


def _combine_call(slots, yc, x1c, route, g, b, tm):
    n = route.shape[0]
    d = D_MODEL
    nt = n // tm
    return pl.pallas_call(
        functools.partial(_combine_kernel, tm=tm),
        out_shape=jax.ShapeDtypeStruct((n, d), F32),
        grid=(nt,),
        in_specs=[
            pl.BlockSpec((1, 1, 2 * tm), lambda i: (i, 0, 0), memory_space=pltpu.SMEM),
            pl.BlockSpec((1, 1, 2 * tm), lambda i: (jnp.minimum(i + 1, nt - 1), 0, 0), memory_space=pltpu.SMEM),
            pl.BlockSpec(memory_space=pl.ANY),
            pl.BlockSpec((tm * ROW_PITCH, LANES), lambda i: (i, 0)),
            pl.BlockSpec((tm, LANES), lambda i: (i, 0)),
            pl.BlockSpec((1, d), lambda i: (0, 0)),
            pl.BlockSpec((1, d), lambda i: (0, 0)),
        ],
        out_specs=pl.BlockSpec((tm, d), lambda i: (i, 0)),
        scratch_shapes=[pltpu.VMEM((2, 2, tm * ROW_PITCH, LANES), F32), pltpu.SemaphoreType.DMA((2,))],
        compiler_params=_params(("arbitrary",)),
        name="moe_combine_ln2",
    )(slots, slots, yc, x1c, route, g, b)


def _moe_plan(e_idx, tm, n_tiles):
    n = e_idx.shape[0]
    flat = e_idx.reshape(-1)
    onehot = (flat[:, None] == jnp.arange(N_EXPERTS, dtype=jnp.int32)[None, :]).astype(jnp.int32)
    csum = jnp.cumsum(onehot, axis=0)
    rank = jnp.take_along_axis(csum, flat[:, None], axis=1)[:, 0] - 1
    counts = csum[-1]
    ptiles = (counts + tm - 1) // tm
    tile_end = jnp.cumsum(ptiles)
    slot = (tile_end - ptiles)[flat] * tm + rank
    rows = jnp.zeros((n_tiles * tm,), jnp.int32).at[slot].set(jnp.arange(2 * n, dtype=jnp.int32) // 2)
    tile_ids = jnp.arange(n_tiles, dtype=jnp.int32)
    tile_expert = jnp.minimum(jnp.sum((tile_end[None, :] <= tile_ids[:, None]).astype(jnp.int32), axis=1),
                              N_EXPERTS - 1)
    return slot.astype(jnp.int32), rows, tile_expert, tile_end[-1:].astype(jnp.int32)


def _rope_tables(seq):
    half = ROT_DIM // 2
    inv_freq = ROPE_THETA ** (-jnp.arange(0, ROT_DIM, 2, dtype=F32) / ROT_DIM)
    ang = jnp.arange(seq, dtype=F32)[:, None] * inv_freq[None, :]
    cos, sin = jnp.cos(ang), jnp.sin(ang)
    pad1 = jnp.ones((seq, HEAD_DIM - ROT_DIM), F32)
    pad0 = jnp.zeros((seq, HEAD_DIM - ROT_DIM), F32)
    return (jnp.concatenate([cos, cos, pad1], axis=1), jnp.concatenate([-sin, sin, pad0], axis=1))


def _layer(x, w_in, diff_lambda, diff_subln_g, cmp_pos, cmp_w1, cmp_b1, cmp_w2, cmp_b2, w_out,
           ln1_g, ln1_b, router_group, router_expert, w_gate, w_up, w_down, ln2_g, ln2_b, lambda_init):
    batch, seq, d = x.shape
    n = batch * seq
    x2 = x.reshape(n, d)

    seg = lambda a, b: w_in[:, a:b]
    pad = lambda w: jnp.pad(w, ((0, 0), (0, LANES - w.shape[1])))
    gate_w = seg(5632, 5656)
    w_aug = jnp.concatenate(
        [seg(3072, 4096), seg(3072, 4096), seg(0, 1024), seg(1024, 2048), seg(2048, 3072), seg(4096, 5632),
         pad(gate_w[:, :12]), pad(gate_w[:, 12:])], axis=1).astype(BF16)
    flags = jnp.zeros((N_UNITS // 2,), jnp.int32).at[jnp.array(ROPE_BLOCKS)].set(1)
    cos_t, sin_t = _rope_tables(seq)
    proj = _proj_call(x2.astype(BF16), w_aug, flags, cos_t, sin_t, seq, tm=min(2048, seq))

    nt = seq // CMP_STRIDE
    half_feat = CMP_STRIDE * HEAD_DIM
    r = proj[:, U_KC * LANES:U_KS * LANES].reshape(batch, nt, CMP_STRIDE, 4, HEAD_DIM)
    r = jnp.transpose(r, (0, 3, 1, 2, 4)).reshape(batch, 4, nt, half_feat)
    w1cat = jnp.concatenate([cmp_w1[:, :half_feat], cmp_w1[:, half_feat:]], axis=2).astype(BF16)
    kvc = _compress_call(r, w1cat, cmp_pos.reshape(2, 2, half_feat), cmp_b1[:, None, :],
                         cmp_w2.astype(BF16), cmp_b2[:, None, :])

    o_diff = _diff_call(proj, diff_lambda, diff_subln_g[None, :], batch, seq, min(512, seq), lambda_init)

    ns = seq // SLC_BLOCK
    ci = np.arange(nt)[None, :] * CMP_STRIDE
    sj = np.arange(ns)[:, None] * SLC_BLOCK
    ovt = jnp.asarray((ci < sj + SLC_BLOCK) & (ci + CMP_BLOCK > sj) & (np.arange(nt)[None, :] < nt - 1), dtype=BF16)
    o_nsa = _nsa_call(proj, kvc, ovt, batch, seq, tq=LANES, tk=min(512, seq))

    wr = jnp.pad(jnp.concatenate([router_group, router_expert], axis=1),
                 ((0, 0), (0, LANES - N_GROUPS - N_EXPERTS)))
    x1, route = _outproj_call(o_diff, o_nsa, w_out.astype(BF16), x2, ln1_g[None, :], ln1_b[None, :], wr, tm=512)

    tm = 256
    n_tiles = 2 * n // tm + N_EXPERTS
    e_idx = route[:, 0:2].astype(jnp.int32)
    slot, rows, tile_expert, n_used = _moe_plan(e_idx, tm, n_tiles)
    ys = _moe_call(tile_expert, n_used, rows.reshape(n_tiles, 1, tm), x1, w_gate, w_up, w_down, tm)
    out = _combine_call(slot.reshape(n // tm, 1, 2 * tm), ys, x1, route, ln2_g[None, :], ln2_b[None, :], tm)
    return out.reshape(batch, seq, d)


def kernel(x, w_in, diff_lambda, diff_subln_g, cmp_pos, cmp_w1, cmp_b1, cmp_w2, cmp_b2, w_out, ln1_g, ln1_b,
           router_group, router_expert, expert_w_gate, expert_w_up, expert_w_down, ln2_g, ln2_b):
    for l in range(DEPTH):
        lambda_init = 0.8 - 0.6 * math.exp(-0.3 * l)
        x = _layer(x, w_in[l], diff_lambda[l], diff_subln_g[l], cmp_pos[l], cmp_w1[l], cmp_b1[l], cmp_w2[l],
                   cmp_b2[l], w_out[l], ln1_g[l], ln1_b[l], router_group[l], router_expert[l],
                   expert_w_gate[l], expert_w_up[l], expert_w_down[l], ln2_g[l], ln2_b[l], lambda_init)
    return x
```

```python
import functools
import math

import numpy as np
import jax
import jax.numpy as jnp
from jax import lax
from jax.experimental import pallas as pl
from jax.experimental.pallas import tpu as pltpu

F32 = jnp.float32
BF16 = jnp.bfloat16

D_MODEL = 2048
HEAD_DIM = 128
ROT_DIM = HEAD_DIM // 4
ROPE_THETA = 500000.0
NEG_INF = -1e30
BIG = 1e30
EPS = 1e-5

DIFF_HEADS = 4
DIFF_VDIM = 2 * HEAD_DIM

NSA_HEADS = 8
NSA_KV_GROUPS = 2
NSA_HPG = NSA_HEADS // NSA_KV_GROUPS
CMP_BLOCK = 32
CMP_STRIDE = 16
CMP_HIDDEN = 256
SLC_BLOCK = 64
SLC_TOPK = 16
WINDOW = 512

N_GROUPS = 4
EXPERTS_PER_GROUP = 8
N_EXPERTS = N_GROUPS * EXPERTS_PER_GROUP
EXPERT_HIDDEN = 512

DEPTH = 1
DN_ALPHA = (2.0 * DEPTH) ** 0.25

LANES = 128
VMEM_LIMIT = 56 * 1024 * 1024

U_NQ, U_NQR, U_DQ, U_DK, U_DV = 0, 8, 16, 24, 32
U_KC, U_VC, U_KS, U_VS, U_KW, U_VW, U_GATE = 40, 42, 44, 46, 48, 50, 52
N_UNITS = 54
ROPE_BLOCKS = tuple(range(4, 16)) + (22, 24)


def _dot(a, b):
    return jnp.dot(a, b, preferred_element_type=F32)


def _dot_nt(a, b):
    return lax.dot_general(a, b, (((1,), (1,)), ((), ())), preferred_element_type=F32)


def _params(sem, vmem=VMEM_LIMIT):
    return pltpu.CompilerParams(dimension_semantics=sem, vmem_limit_bytes=vmem)


def _proj_kernel(flags_ref, x_ref, w_ref, cos_ref, sin_ref, o_ref):
    j = pl.program_id(1)
    acc = _dot(x_ref[...], w_ref[...])

    @pl.when(flags_ref[j] == 0)
    def _():
        o_ref[...] = acc.astype(o_ref.dtype)

    @pl.when(flags_ref[j] != 0)
    def _():
        c = cos_ref[...]
        s = sin_ref[...]
        lane = lax.broadcasted_iota(jnp.int32, c.shape, 1)
        half = ROT_DIM // 2
        for hh in range(2):
            a = acc[:, hh * LANES:(hh + 1) * LANES]
            partner = jnp.where(lane < half, pltpu.roll(a, LANES - half, 1), pltpu.roll(a, half, 1))
            o_ref[:, hh * LANES:(hh + 1) * LANES] = (a * c + partner * s).astype(o_ref.dtype)


def _proj_call(xb, w_aug, flags, cos_t, sin_t, seq, tm):
    n, d = xb.shape
    nj = w_aug.shape[1] // 256
    tpb = seq // tm
    grid_spec = pltpu.PrefetchScalarGridSpec(
        num_scalar_prefetch=1,
        grid=(n // tm, nj),
        in_specs=[
            pl.BlockSpec((tm, d), lambda i, j, f: (i, 0)),
            pl.BlockSpec((d, 256), lambda i, j, f: (0, j)),
            pl.BlockSpec((tm, LANES), lambda i, j, f: (i % tpb, 0)),
            pl.BlockSpec((tm, LANES), lambda i, j, f: (i % tpb, 0)),
        ],
        out_specs=pl.BlockSpec((tm, 256), lambda i, j, f: (i, j)),
    )
    return pl.pallas_call(
        _proj_kernel,
        out_shape=jax.ShapeDtypeStruct((n, w_aug.shape[1]), BF16),
        grid_spec=grid_spec,
        compiler_params=_params(("parallel", "arbitrary")),
        name="proj",
    )(flags, xb, w_aug, cos_t, sin_t)


def _compress_kernel(r_ref, w1_ref, pos_ref, b1_ref, w2_ref, b2_ref, o_ref):
    nt = r_ref.shape[0]
    ab = _dot(r_ref[...], w1_ref[...])
    pos = pos_ref[...]
    pa = jnp.broadcast_to(pos[0:1], (8, pos.shape[1])).astype(BF16)
    pb = jnp.broadcast_to(pos[1:2], (8, pos.shape[1])).astype(BF16)
    const = _dot(pa, w1_ref[:, :CMP_HIDDEN])[0:1] + _dot(pb, w1_ref[:, CMP_HIDDEN:])[0:1]
    h = ab[:, :CMP_HIDDEN] + pltpu.roll(ab[:, CMP_HIDDEN:], nt - 1, 0) + const + b1_ref[...]
    h = jax.nn.gelu(h)
    o_ref[...] = (_dot(h.astype(BF16), w2_ref[...]) + b2_ref[...]).astype(o_ref.dtype)


def _compress_call(r, w1cat, pos2, b1, w2, b2):
    b, four, nt, k = r.shape
    return pl.pallas_call(
        _compress_kernel,
        out_shape=jax.ShapeDtypeStruct((b, four, nt, HEAD_DIM), BF16),
        grid=(b, four),
        in_specs=[
            pl.BlockSpec((None, None, nt, k), lambda i, c: (i, c, 0, 0)),
            pl.BlockSpec((None, k, 2 * CMP_HIDDEN), lambda i, c: (c // 2, 0, 0)),
            pl.BlockSpec((None, 2, k), lambda i, c: (c // 2, 0, 0)),
            pl.BlockSpec((None, 1, CMP_HIDDEN), lambda i, c: (c // 2, 0, 0)),
            pl.BlockSpec((None, CMP_HIDDEN, HEAD_DIM), lambda i, c: (c // 2, 0, 0)),
            pl.BlockSpec((None, 1, HEAD_DIM), lambda i, c: (c // 2, 0, 0)),
        ],
        out_specs=pl.BlockSpec((None, None, nt, HEAD_DIM), lambda i, c: (i, c, 0, 0)),
        compiler_params=_params(("parallel", "parallel")),
        name="compress",
    )(r, w1cat, pos2, b1, w2, b2)


EXP2_SCALE = HEAD_DIM ** -0.5 * math.log2(math.e)


def _transpose_into(src_ref, dst_ref):
    def body(c, carry):
        off = pl.multiple_of(c * LANES, LANES)
        dst_ref[:, pl.ds(off, LANES)] = src_ref[pl.ds(off, LANES), :].astype(F32).T.astype(dst_ref.dtype)
        return carry

    lax.fori_loop(0, src_ref.shape[0] // LANES, body, 0)


def _diff_kernel(dl_ref, q_ref, k_ref, v_ref, g_ref, o_ref, vt_sc, acc_sc, sa_sc, sb_sc, *, tq, lambda_init):
    qi = pl.program_id(2)

    @pl.when(qi == 0)
    def _():
        for c in range(2):
            _transpose_into(v_ref.at[:, c * LANES:(c + 1) * LANES], vt_sc.at[c * LANES:(c + 1) * LANES, :])

    acc_sc[...] = jnp.zeros(acc_sc.shape, F32)
    q = q_ref[...]
    qpos = qi * tq + lax.broadcasted_iota(jnp.int32, (1, tq), 1)

    def scores(j, dst):
        kt = k_ref[pl.ds(pl.multiple_of(j * tq, tq), tq), :]
        for c in range(2):
            dst[c] = _dot_nt(kt[:, c * HEAD_DIM:(c + 1) * HEAD_DIM], q[:, c * HEAD_DIM:(c + 1) * HEAD_DIM])

    def absorb(src, j, masked, carry):
        off = pl.multiple_of(j * tq, tq)
        vt = vt_sc[:, pl.ds(off, tq)]
        out = []
        for c in range(2):
            m_old, l_old = carry[2 * c], carry[2 * c + 1]
            s = src[c]
            if masked:
                kpos = off + lax.broadcasted_iota(jnp.int32, s.shape, 0)
                s = jnp.where(kpos <= qpos, s, NEG_INF)
            m_new = jnp.maximum(m_old, jnp.max(s, axis=0, keepdims=True))
            p = jnp.exp2((s - m_new) * EXP2_SCALE)
            alpha = jnp.exp2((m_old - m_new) * EXP2_SCALE)
            l_new = alpha * l_old + jnp.sum(p, axis=0, keepdims=True)
            acc_sc[c] = alpha * acc_sc[c] + _dot(vt, p.astype(BF16))
            out += [m_new, l_new]
        return tuple(out)

    def pair(p, carry):
        j = 2 * p
        scores(j + 1, sb_sc)
        carry = absorb(sa_sc, j, False, carry)
        scores(j + 2, sa_sc)
        return absorb(sb_sc, j + 1, False, carry)

    n_pairs = (qi + 2) // 2
    init = (jnp.full((1, tq), NEG_INF, F32), jnp.zeros((1, tq), F32)) * 2
    scores(0, sa_sc)
    carry = lax.fori_loop(0, n_pairs - 1, pair, init)
    j_tail = 2 * (n_pairs - 1)
    scores(j_tail + 1, sb_sc)
    carry = absorb(sa_sc, j_tail, True, carry)
    _, l0, _, l1 = absorb(sb_sc, j_tail + 1, True, carry)

    dl = dl_ref[...]
    lam = (jnp.exp(jnp.sum(dl[0:1] * dl[1:2], axis=1, keepdims=True))
           - jnp.exp(jnp.sum(dl[2:3] * dl[3:4], axis=1, keepdims=True)) + lambda_init)
    o = acc_sc[0] * (1.0 / l0) - lam * (acc_sc[1] * (1.0 / l1))
    o = o * (lax.rsqrt(jnp.mean(o * o, axis=0, keepdims=True) + EPS) * (1.0 - lambda_init))
    for c in range(DIFF_VDIM // LANES):
        for r in range(tq // LANES):
            blk = o[c * LANES:(c + 1) * LANES, r * LANES:(r + 1) * LANES].T
            o_ref[r * LANES:(r + 1) * LANES, c * LANES:(c + 1) * LANES] = (
                blk * g_ref[:, c * LANES:(c + 1) * LANES]).astype(o_ref.dtype)


def _diff_call(proj, dl, g, batch, seq, tq, lambda_init):
    nq = seq // tq
    assert nq % 2 == 0, "key tiles are processed in pairs"
    kern = functools.partial(_diff_kernel, tq=tq, lambda_init=lambda_init)
    return pl.pallas_call(
        kern,
        out_shape=jax.ShapeDtypeStruct((batch * seq, DIFF_HEADS * DIFF_VDIM), BF16),
        grid=(batch, DIFF_HEADS, nq),
        in_specs=[
            pl.BlockSpec((4, HEAD_DIM), lambda b, h, i: (0, 0)),
            pl.BlockSpec((tq, 256), lambda b, h, i: (b * nq + i, U_DQ // 2 + h)),
            pl.BlockSpec((seq, 256), lambda b, h, i: (b, U_DK // 2 + h)),
            pl.BlockSpec((seq, 256), lambda b, h, i: (b, U_DV // 2 + h)),
            pl.BlockSpec((1, DIFF_VDIM), lambda b, h, i: (0, 0)),
        ],
        out_specs=pl.BlockSpec((tq, DIFF_VDIM), lambda b, h, i: (b * nq + i, h)),
        scratch_shapes=[
            pltpu.VMEM((DIFF_VDIM, seq), BF16),
            pltpu.VMEM((2, DIFF_VDIM, tq), F32),
            pltpu.VMEM((2, tq, tq), F32),
            pltpu.VMEM((2, tq, tq), F32),
        ],
        compiler_params=_params(("arbitrary", "arbitrary", "arbitrary")),
        name="diff_attn",
    )(dl, proj, proj, proj, g)


def _stack_heads(x):
    return jnp.concatenate([x[:, h * HEAD_DIM:(h + 1) * HEAD_DIM] for h in range(NSA_HPG)], axis=0)


def _nsa_kernel(q_ref, qr_ref, gate_ref, kc_ref, vc_ref, ks_ref, vs_ref, kw_ref, vw_ref, ovt_ref,
                o_ref, vct_sc, vst_sc, vwt_sc, bias_sc, acc_sc, sa_sc, sb_sc, *, tq, tk, seq, top_k):
    qi = pl.program_id(2)
    q0 = qi * tq
    rows = NSA_HPG * tq
    nt = kc_ref.shape[0]
    ns = ovt_ref.shape[0]

    @pl.when(qi == 0)
    def _():
        _transpose_into(vc_ref, vct_sc)
        _transpose_into(vs_ref, vst_sc)
        _transpose_into(vw_ref, vwt_sc)

    qs = _stack_heads(q_ref[...])
    qr = _stack_heads(qr_ref[...])
    qpos = q0 + (lax.broadcasted_iota(jnp.int32, (1, rows), 1) & (tq - 1))

    s_c = _dot_nt(kc_ref[...], qs)
    cend = lax.broadcasted_iota(jnp.int32, (nt, rows), 0) * CMP_STRIDE + (CMP_BLOCK - 1)
    valid_c = cend <= qpos
    s_c = jnp.where(valid_c, s_c, NEG_INF)
    m_c = jnp.max(s_c, axis=0, keepdims=True)
    e_c = jnp.where(valid_c, jnp.exp2((s_c - m_c) * EXP2_SCALE), 0.0)
    l_c = jnp.sum(e_c, axis=0, keepdims=True)
    p_c = e_c * (1.0 / jnp.where(l_c > 0.0, l_c, 1.0))
    o_c = _dot(vct_sc[...], p_c.astype(BF16))

    p_sum = p_c[:, 0:tq]
    for h in range(1, NSA_HPG):
        p_sum = p_sum + p_c[:, h * tq:(h + 1) * tq]
    p_hi = p_sum.astype(BF16)
    p_lo = (p_sum - p_hi.astype(F32)).astype(BF16)
    imp = _dot(ovt_ref[...], p_hi) + _dot(ovt_ref[...], p_lo)

    blk = lax.broadcasted_iota(jnp.int32, (ns, tq), 0)
    qpos_l = q0 + lax.broadcasted_iota(jnp.int32, (ns, tq), 1)
    cur = lax.shift_right_logical(qpos_l, int(math.log2(SLC_BLOCK)))
    valid_s = blk <= cur
    forced = (blk == 0) | (blk == cur) | (blk == cur - 1)
    work = jnp.where(valid_s & forced, BIG, jnp.where(valid_s, imp, NEG_INF))
    sel = jnp.zeros((ns, tq), F32)
    blk_f = blk.astype(F32)
    for _ in range(top_k):
        mx = jnp.max(work, axis=0, keepdims=True)
        idx = jnp.min(jnp.where(work == mx, blk_f, float(ns)), axis=0, keepdims=True)
        pick = blk_f == idx
        sel = jnp.where(pick, 1.0, sel)
        work = jnp.where(pick, -jnp.inf, work)
    bias_sc[...] = (sel - 1.0) * BIG

    acc_sc[...] = jnp.zeros(acc_sc.shape, F32)
    nb = tk // SLC_BLOCK

    def sel_scores(j, dst):
        dst[...] = _dot_nt(ks_ref[pl.ds(pl.multiple_of(j * tk, tk), tk), :], qr)

    def sel_absorb(src, j, causal, carry):
        m_old, l_old = carry
        off = pl.multiple_of(j * tk, tk)
        s = src[...]
        if causal:
            kpos = off + lax.broadcasted_iota(jnp.int32, (tk, rows), 0)
            s = jnp.where(kpos <= qpos, s, NEG_INF)
        bias = bias_sc[pl.ds(pl.multiple_of(j * nb, nb), nb), :]
        bias = jnp.concatenate([bias] * NSA_HPG, axis=1)
        s3 = s.reshape(nb, SLC_BLOCK, rows) + bias[:, None, :]
        m_new = jnp.maximum(m_old, jnp.max(jnp.max(s3, axis=1), axis=0, keepdims=True))
        p3 = jnp.exp2((s3 - m_new) * EXP2_SCALE)
        alpha = jnp.exp2((m_old - m_new) * EXP2_SCALE)
        l_new = alpha * l_old + jnp.sum(jnp.sum(p3, axis=1), axis=0, keepdims=True)
        p = p3.reshape(tk, rows).astype(BF16)
        acc_sc[...] = alpha * acc_sc[...] + _dot(vst_sc[:, pl.ds(off, tk)], p)
        return m_new, l_new

    def sel_pair(p, carry):
        j = 2 * p
        sel_scores(j + 1, sb_sc)
        carry = sel_absorb(sa_sc, j, False, carry)
        sel_scores(j + 2, sa_sc)
        return sel_absorb(sb_sc, j + 1, False, carry)

    n_pairs = (q0 // tk + 2) // 2
    init = (jnp.full((1, rows), NEG_INF, F32), jnp.zeros((1, rows), F32))
    sel_scores(0, sa_sc)
    carry = lax.fori_loop(0, n_pairs - 1, sel_pair, init)
    j_tail = 2 * (n_pairs - 1)
    sel_scores(j_tail + 1, sb_sc)
    carry = sel_absorb(sa_sc, j_tail, True, carry)
    _, l_s = sel_absorb(sb_sc, j_tail + 1, True, carry)
    o_s = acc_sc[...] * (1.0 / l_s)

    slab = min(WINDOW + tq, seq)
    start = pl.multiple_of(jnp.maximum(q0 - WINDOW, 0), tq)
    s_w = _dot_nt(kw_ref[pl.ds(start, slab), :], qr)
    dpos = qpos - (start + lax.broadcasted_iota(jnp.int32, (slab, rows), 0))
    s_w = jnp.where((dpos >= 0) & (dpos < WINDOW), s_w, NEG_INF)
    p_w = jnp.exp2((s_w - jnp.max(s_w, axis=0, keepdims=True)) * EXP2_SCALE)
    o_w = _dot(vwt_sc[:, pl.ds(start, slab)], p_w.astype(BF16)) * (1.0 / jnp.sum(p_w, axis=0, keepdims=True))

    gates = jax.nn.sigmoid(gate_ref[...].astype(F32)).T
    for h in range(NSA_HPG):
        sl = slice(h * tq, (h + 1) * tq)
        o = (gates[3 * h:3 * h + 1] * o_c[:, sl] + gates[3 * h + 1:3 * h + 2] * o_s[:, sl]
             + gates[3 * h + 2:3 * h + 3] * o_w[:, sl])
        for r in range(tq // LANES):
            o_ref[r * LANES:(r + 1) * LANES, h * HEAD_DIM:(h + 1) * HEAD_DIM] = (
                o[:, r * LANES:(r + 1) * LANES].T.astype(o_ref.dtype))


def _nsa_call(proj, kvc, ovt, batch, seq, tq, tk):
    nq = seq // tq
    assert (seq // tk) % 2 == 0, "key tiles are processed in pairs"
    nt = kvc.shape[2]
    ns = ovt.shape[0]
    width = NSA_HPG * HEAD_DIM
    kern = functools.partial(_nsa_kernel, tq=tq, tk=tk, seq=seq, top_k=min(SLC_TOPK, ns))

    def col(unit):
        return pl.BlockSpec((seq, HEAD_DIM), lambda b, g, i: (b, unit + g))

    return pl.pallas_call(
        kern,
        out_shape=jax.ShapeDtypeStruct((batch * seq, NSA_HEADS * HEAD_DIM), BF16),
        grid=(batch, NSA_KV_GROUPS, nq),
        in_specs=[
            pl.BlockSpec((tq, width), lambda b, g, i: (b * nq + i, U_NQ // 4 + g)),
            pl.BlockSpec((tq, width), lambda b, g, i: (b * nq + i, U_NQR // 4 + g)),
            pl.BlockSpec((tq, LANES), lambda b, g, i: (b * nq + i, U_GATE + g)),
            pl.BlockSpec((None, None, nt, HEAD_DIM), lambda b, g, i: (b, g, 0, 0)),
            pl.BlockSpec((None, None, nt, HEAD_DIM), lambda b, g, i: (b, 2 + g, 0, 0)),
            col(U_KS), col(U_VS), col(U_KW), col(U_VW),
            pl.BlockSpec((ns, nt), lambda b, g, i: (0, 0)),
        ],
        out_specs=pl.BlockSpec((tq, width), lambda b, g, i: (b * nq + i, g)),
        scratch_shapes=[
            pltpu.VMEM((HEAD_DIM, nt), BF16),
            pltpu.VMEM((HEAD_DIM, seq), BF16),
            pltpu.VMEM((HEAD_DIM, seq), BF16),
            pltpu.VMEM((ns, tq), F32),
            pltpu.VMEM((HEAD_DIM, NSA_HPG * tq), F32),
            pltpu.VMEM((tk, NSA_HPG * tq), F32),
            pltpu.VMEM((tk, NSA_HPG * tq), F32),
        ],
        compiler_params=_params(("arbitrary", "arbitrary", "arbitrary")),
        name="nsa_attn",
    )(proj, proj, proj, kvc, kvc, proj, proj, proj, proj, ovt)


def _layernorm(y, g, b):
    mu = jnp.mean(y, axis=1, keepdims=True)
    yc = y - mu
    var = jnp.mean(yc * yc, axis=1, keepdims=True)
    return yc * lax.rsqrt(var + EPS) * g + b


N_CHUNKS = D_MODEL // LANES
ROW_PITCH = N_CHUNKS + 1


def _store_chunk_rows(ref, val):
    tm = val.shape[0]
    for k in range(N_CHUNKS):
        ref[pl.ds(k, tm, stride=ROW_PITCH), :] = val[:, k * LANES:(k + 1) * LANES]
    for k in range(N_CHUNKS, ROW_PITCH):
        ref[pl.ds(k, tm, stride=ROW_PITCH), :] = jnp.zeros((tm, LANES), ref.dtype)


def _load_chunk_rows(ref, tm):
    return jnp.concatenate([ref[pl.ds(k, tm, stride=ROW_PITCH), :] for k in range(N_CHUNKS)], axis=1)


def _outproj_kernel(od_ref, on_ref, w_ref, x_ref, g_ref, b_ref, wr_ref, x1_ref, route_ref):
    half = od_ref.shape[1]
    h = _dot(od_ref[...], w_ref[:half, :]) + _dot(on_ref[...], w_ref[half:, :])
    x1 = _layernorm(DN_ALPHA * x_ref[...] + h, g_ref[...], b_ref[...])
    _store_chunk_rows(x1_ref, x1)

    wr = wr_ref[...]
    w_hi = wr.astype(BF16)
    w_lo = (wr - w_hi.astype(F32)).astype(BF16)
    x_hi = x1.astype(BF16)
    x_lo = (x1 - x_hi.astype(F32)).astype(BF16)
    logits = _dot(x_hi, w_hi) + (_dot(x_lo, w_hi) + _dot(x_hi, w_lo))

    lane = lax.broadcasted_iota(jnp.int32, logits.shape, 1).astype(F32)
    ninf = -jnp.inf
    gl = jnp.where(lane < N_GROUPS, logits, ninf)
    gmax = jnp.max(gl, axis=1, keepdims=True)
    g_w = 1.0 / jnp.sum(jnp.exp(gl - gmax), axis=1, keepdims=True)
    g_sel = jnp.min(jnp.where(gl == gmax, lane, float(LANES)), axis=1, keepdims=True)
    lo = N_GROUPS + EXPERTS_PER_GROUP * g_sel
    el = jnp.where((lane >= lo) & (lane < lo + EXPERTS_PER_GROUP), logits, ninf)
    e1 = jnp.max(el, axis=1, keepdims=True)
    i1 = jnp.min(jnp.where(el == e1, lane, float(LANES)), axis=1, keepdims=True)
    el2 = jnp.where(lane == i1, ninf, el)
    e2 = jnp.max(el2, axis=1, keepdims=True)
    i2 = jnp.min(jnp.where(el2 == e2, lane, float(LANES)), axis=1, keepdims=True)
    r = jnp.exp(e2 - e1)
    w1 = g_w / (1.0 + r)
    w2 = g_w * r / (1.0 + r)
    route = jnp.where(lane == 0, i1 - N_GROUPS,
                      jnp.where(lane == 1, i2 - N_GROUPS,
                                jnp.where(lane == 2, w1, jnp.where(lane == 3, w2, 0.0))))
    route_ref[...] = route


def _outproj_call(o_diff, o_nsa, w_out, x2, g, b, wr, tm):
    n, d = x2.shape
    half = o_diff.shape[1]
    return pl.pallas_call(
        _outproj_kernel,
        out_shape=(jax.ShapeDtypeStruct((n * ROW_PITCH, LANES), F32), jax.ShapeDtypeStruct((n, LANES), F32)),
        grid=(n // tm,),
        in_specs=[
            pl.BlockSpec((tm, half), lambda i: (i, 0)),
            pl.BlockSpec((tm, half), lambda i: (i, 0)),
            pl.BlockSpec((2 * half, d), lambda i: (0, 0)),
            pl.BlockSpec((tm, d), lambda i: (i, 0)),
            pl.BlockSpec((1, d), lambda i: (0, 0)),
            pl.BlockSpec((1, d), lambda i: (0, 0)),
            pl.BlockSpec((d, LANES), lambda i: (0, 0)),
        ],
        out_specs=(pl.BlockSpec((tm * ROW_PITCH, LANES), lambda i: (i, 0)),
                   pl.BlockSpec((tm, LANES), lambda i: (i, 0))),
        compiler_params=_params(("parallel",)),
        name="outproj_ln1_router",
    )(o_diff, o_nsa, w_out, x2, g, b, wr)


def _token_copy(src_hbm, dst, sem, src_tok, dst_tok):
    return pltpu.make_async_copy(src_hbm.at[pl.ds(src_tok * ROW_PITCH, N_CHUNKS)],
                                 dst.at[pl.ds(dst_tok * ROW_PITCH, N_CHUNKS)], sem)


def _wait_tokens(src_hbm, dst, sem, count):
    pltpu.make_async_copy(src_hbm.at[pl.ds(0, count * N_CHUNKS)], dst.at[pl.ds(0, count * N_CHUNKS)], sem).wait()


def _moe_kernel(te_ref, nu_ref, rows_ref, rows_next_ref, x_hbm, wg_ref, wu_ref, wd_ref, o_ref,
                xbuf, sem, wg_sc, wu_sc, wd_sc, *, tm):
    i = pl.program_id(0)
    n_used = nu_ref[0]
    slot = lax.rem(i, 2)

    def issue(ids_ref, s):
        def body(r, carry):
            _token_copy(x_hbm, xbuf.at[s], sem.at[s], ids_ref[0, 0, r], r).start()
            return carry

        lax.fori_loop(0, tm, body, 0, unroll=8)

    @pl.when(i == 0)
    def _():
        issue(rows_ref, 0)

    @pl.when(i + 1 < n_used)
    def _():
        issue(rows_next_ref, 1 - slot)

    prev = te_ref[jnp.maximum(i - 1, 0)]

    @pl.when((i == 0) | (te_ref[i] != prev))
    def _():
        wg_sc[...] = wg_ref[...].astype(BF16)
        wu_sc[...] = wu_ref[...].astype(BF16)
        wd_sc[...] = wd_ref[...].astype(BF16)

    @pl.when(i < n_used)
    def _():
        _wait_tokens(x_hbm, xbuf.at[slot], sem.at[slot], tm)
        xb = _load_chunk_rows(xbuf.at[slot], tm).astype(BF16)
        gate = _dot(xb, wg_sc[...])
        up = _dot(xb, wu_sc[...])
        h = (gate * jax.nn.sigmoid(gate) * up).astype(BF16)
        _store_chunk_rows(o_ref, _dot(h, wd_sc[...]))

    @pl.when(i >= n_used)
    def _():
        o_ref[...] = jnp.zeros(o_ref.shape, o_ref.dtype)


def _moe_call(tile_expert, n_used, rows, x1c, w_gate, w_up, w_down, tm):
    n_tiles = rows.shape[0]
    d, f = w_gate.shape[1], w_gate.shape[2]
    grid_spec = pltpu.PrefetchScalarGridSpec(
        num_scalar_prefetch=2,
        grid=(n_tiles,),
        in_specs=[
            pl.BlockSpec((1, 1, tm), lambda i, te, nu: (i, 0, 0), memory_space=pltpu.SMEM),
            pl.BlockSpec((1, 1, tm), lambda i, te, nu: (jnp.minimum(i + 1, n_tiles - 1), 0, 0),
                         memory_space=pltpu.SMEM),
            pl.BlockSpec(memory_space=pl.ANY),
            pl.BlockSpec((None, d, f), lambda i, te, nu: (te[i], 0, 0)),
            pl.BlockSpec((None, d, f), lambda i, te, nu: (te[i], 0, 0)),
            pl.BlockSpec((None, f, d), lambda i, te, nu: (te[i], 0, 0)),
        ],
        out_specs=pl.BlockSpec((tm * ROW_PITCH, LANES), lambda i, te, nu: (i, 0)),
        scratch_shapes=[
            pltpu.VMEM((2, tm * ROW_PITCH, LANES), F32),
            pltpu.SemaphoreType.DMA((2,)),
            pltpu.VMEM((d, f), BF16), pltpu.VMEM((d, f), BF16), pltpu.VMEM((f, d), BF16),
        ],
    )
    return pl.pallas_call(
        functools.partial(_moe_kernel, tm=tm),
        out_shape=jax.ShapeDtypeStruct((n_tiles * tm * ROW_PITCH, LANES), F32),
        grid_spec=grid_spec,
        compiler_params=_params(("arbitrary",)),
        name="moe_experts",
    )(tile_expert, n_used, rows, rows, x1c, w_gate, w_up, w_down)


def _combine_kernel(slots_ref, slots_next_ref, y_hbm, x1_ref, route_ref, g_ref, b_ref, o_ref, buf, sem, *, tm):
    i = pl.program_id(0)
    slot = lax.rem(i, 2)

    def issue(ids_ref, s):
        def body(r, carry):
            for k in range(2):
                _token_copy(y_hbm, buf.at[s, k], sem.at[s], ids_ref[0, 0, 2 * r + k], r).start()
            return carry

        lax.fori_loop(0, tm, body, 0, unroll=8)

    @pl.when(i == 0)
    def _():
        issue(slots_ref, 0)

    @pl.when(i + 1 < pl.num_programs(0))
    def _():
        issue(slots_next_ref, 1 - slot)

    for k in range(2):
        _wait_tokens(y_hbm, buf.at[slot, k], sem.at[slot], tm)
    route = route_ref[...]
    y = (route[:, 2:3] * _load_chunk_rows(buf.at[slot, 0], tm)
         + route[:, 3:4] * _load_chunk_rows(buf.at[slot, 1], tm))
    x1 = _load_chunk_rows(x1_ref, tm)
    o_ref[...] = _layernorm(DN_ALPHA * x1 + y, g_ref[...], b_ref[...])


def _combine_call(slots, yc, x1c, route, g, b, tm):
    n = route.shape[0]
    d = D_MODEL
    nt = n // tm
    return pl.pallas_call(
        functools.partial(_combine_kernel, tm=tm),
        out_shape=jax.ShapeDtypeStruct((n, d), F32),
        grid=(nt,),
        in_specs=[
            pl.BlockSpec((1, 1, 2 * tm), lambda i: (i, 0, 0), memory_space=pltpu.SMEM),
            pl.BlockSpec((1, 1, 2 * tm), lambda i: (jnp.minimum(i + 1, nt - 1), 0, 0), memory_space=pltpu.SMEM),
            pl.BlockSpec(memory_space=pl.ANY),
            pl.BlockSpec((tm * ROW_PITCH, LANES), lambda i: (i, 0)),
            pl.BlockSpec((tm, LANES), lambda i: (i, 0)),
            pl.BlockSpec((1, d), lambda i: (0, 0)),
            pl.BlockSpec((1, d), lambda i: (0, 0)),
        ],
        out_specs=pl.BlockSpec((tm, d), lambda i: (i, 0)),
        scratch_shapes=[pltpu.VMEM((2, 2, tm * ROW_PITCH, LANES), F32), pltpu.SemaphoreType.DMA((2,))],
        compiler_params=_params(("arbitrary",)),
        name="moe_combine_ln2",
    )(slots, slots, yc, x1c, route, g, b)


def _moe_plan(e_idx, tm, n_tiles):
    n = e_idx.shape[0]
    flat = e_idx.reshape(-1)
    onehot = (flat[:, None] == jnp.arange(N_EXPERTS, dtype=jnp.int32)[None, :]).astype(jnp.int32)
    csum = jnp.cumsum(onehot, axis=0)
    rank = jnp.take_along_axis(csum, flat[:, None], axis=1)[:, 0] - 1
    counts = csum[-1]
    ptiles = (counts + tm - 1) // tm
    tile_end = jnp.cumsum(ptiles)
    slot = (tile_end - ptiles)[flat] * tm + rank
    rows = jnp.zeros((n_tiles * tm,), jnp.int32).at[slot].set(jnp.arange(2 * n, dtype=jnp.int32) // 2)
    tile_ids = jnp.arange(n_tiles, dtype=jnp.int32)
    tile_expert = jnp.minimum(jnp.sum((tile_end[None, :] <= tile_ids[:, None]).astype(jnp.int32), axis=1),
                              N_EXPERTS - 1)
    return slot.astype(jnp.int32), rows, tile_expert, tile_end[-1:].astype(jnp.int32)


def _rope_tables(seq):
    half = ROT_DIM // 2
    inv_freq = ROPE_THETA ** (-jnp.arange(0, ROT_DIM, 2, dtype=F32) / ROT_DIM)
    ang = jnp.arange(seq, dtype=F32)[:, None] * inv_freq[None, :]
    cos, sin = jnp.cos(ang), jnp.sin(ang)
    pad1 = jnp.ones((seq, HEAD_DIM - ROT_DIM), F32)
    pad0 = jnp.zeros((seq, HEAD_DIM - ROT_DIM), F32)
    return (jnp.concatenate([cos, cos, pad1], axis=1), jnp.concatenate([-sin, sin, pad0], axis=1))


def _layer(x, w_in, diff_lambda, diff_subln_g, cmp_pos, cmp_w1, cmp_b1, cmp_w2, cmp_b2, w_out,
           ln1_g, ln1_b, router_group, router_expert, w_gate, w_up, w_down, ln2_g, ln2_b, lambda_init):
    batch, seq, d = x.shape
    n = batch * seq
    x2 = x.reshape(n, d)

    seg = lambda a, b: w_in[:, a:b]
    pad = lambda w: jnp.pad(w, ((0, 0), (0, LANES - w.shape[1])))
    gate_w = seg(5632, 5656)
    w_aug = jnp.concatenate(
        [seg(3072, 4096), seg(3072, 4096), seg(0, 1024), seg(1024, 2048), seg(2048, 3072), seg(4096, 5632),
         pad(gate_w[:, :12]), pad(gate_w[:, 12:])], axis=1).astype(BF16)
    flags = jnp.zeros((N_UNITS // 2,), jnp.int32).at[jnp.array(ROPE_BLOCKS)].set(1)
    cos_t, sin_t = _rope_tables(seq)
    proj = _proj_call(x2.astype(BF16), w_aug, flags, cos_t, sin_t, seq, tm=min(2048, seq))

    nt = seq // CMP_STRIDE
    half_feat = CMP_STRIDE * HEAD_DIM
    r = proj[:, U_KC * LANES:U_KS * LANES].reshape(batch, nt, CMP_STRIDE, 4, HEAD_DIM)
    r = jnp.transpose(r, (0, 3, 1, 2, 4)).reshape(batch, 4, nt, half_feat)
    w1cat = jnp.concatenate([cmp_w1[:, :half_feat], cmp_w1[:, half_feat:]], axis=2).astype(BF16)
    kvc = _compress_call(r, w1cat, cmp_pos.reshape(2, 2, half_feat), cmp_b1[:, None, :],
                         cmp_w2.astype(BF16), cmp_b2[:, None, :])

    o_diff = _diff_call(proj, diff_lambda, diff_subln_g[None, :], batch, seq, min(512, seq), lambda_init)

    ns = seq // SLC_BLOCK
    ci = np.arange(nt)[None, :] * CMP_STRIDE
    sj = np.arange(ns)[:, None] * SLC_BLOCK
    ovt = jnp.asarray((ci < sj + SLC_BLOCK) & (ci + CMP_BLOCK > sj) & (np.arange(nt)[None, :] < nt - 1), dtype=BF16)
    o_nsa = _nsa_call(proj, kvc, ovt, batch, seq, tq=LANES, tk=min(512, seq))

    wr = jnp.pad(jnp.concatenate([router_group, router_expert], axis=1),
                 ((0, 0), (0, LANES - N_GROUPS - N_EXPERTS)))
    x1, route = _outproj_call(o_diff, o_nsa, w_out.astype(BF16), x2, ln1_g[None, :], ln1_b[None, :], wr, tm=512)

    tm = 256
    n_tiles = 2 * n // tm + N_EXPERTS
    e_idx = route[:, 0:2].astype(jnp.int32)
    slot, rows, tile_expert, n_used = _moe_plan(e_idx, tm, n_tiles)
    ys = _moe_call(tile_expert, n_used, rows.reshape(n_tiles, 1, tm), x1, w_gate, w_up, w_down, tm)
    out = _combine_call(slot.reshape(n // tm, 1, 2 * tm), ys, x1, route, ln2_g[None, :], ln2_b[None, :], tm)
    return out.reshape(batch, seq, d)


def kernel(x, w_in, diff_lambda, diff_subln_g, cmp_pos, cmp_w1, cmp_b1, cmp_w2, cmp_b2, w_out, ln1_g, ln1_b,
           router_group, router_expert, expert_w_gate, expert_w_up, expert_w_down, ln2_g, ln2_b):
    for l in range(DEPTH):
        lambda_init = 0.8 - 0.6 * math.exp(-0.3 * l)
        x = _layer(x, w_in[l], diff_lambda[l], diff_subln_g[l], cmp_pos[l], cmp_w1[l], cmp_b1[l], cmp_w2[l],
                   cmp_b2[l], w_out[l], ln1_g[l], ln1_b[l], router_group[l], router_expert[l],
                   expert_w_gate[l], expert_w_up[l], expert_w_down[l], ln2_g[l], ln2_b[l], lambda_init)
    return x
```

```python
import functools
import math

import numpy as np
import jax
import jax.numpy as jnp
from jax import lax
from jax.experimental import pallas as pl
from jax.experimental.pallas import tpu as pltpu

F32 = jnp.float32
BF16 = jnp.bfloat16

D_MODEL = 2048
HEAD_DIM = 128
ROT_DIM = HEAD_DIM // 4
ROPE_THETA = 500000.0
NEG_INF = -1e30
BIG = 1e30
EPS = 1e-5

DIFF_HEADS = 4
DIFF_VDIM = 2 * HEAD_DIM

NSA_HEADS = 8
NSA_KV_GROUPS = 2
NSA_HPG = NSA_HEADS // NSA_KV_GROUPS
CMP_BLOCK = 32
CMP_STRIDE = 16
CMP_HIDDEN = 256
SLC_BLOCK = 64
SLC_TOPK = 16
WINDOW = 512

N_GROUPS = 4
EXPERTS_PER_GROUP = 8
N_EXPERTS = N_GROUPS * EXPERTS_PER_GROUP
EXPERT_HIDDEN = 512

DEPTH = 1
DN_ALPHA = (2.0 * DEPTH) ** 0.25

LANES = 128
VMEM_LIMIT = 56 * 1024 * 1024

U_NQ, U_DQ, U_DK, U_DV = 0, 8, 16, 24
U_KC, U_VC, U_KS, U_VS, U_KW, U_VW, U_GATE = 32, 34, 36, 38, 40, 42, 44
N_UNITS = 46
ROPE_BLOCKS = tuple(range(U_DQ // 2, U_DV // 2)) + (U_KS // 2, U_KW // 2)


def _dot(a, b):
    return jnp.dot(a, b, preferred_element_type=F32)


def _dot_nt(a, b):
    return lax.dot_general(a, b, (((1,), (1,)), ((), ())), preferred_element_type=F32)


def _params(sem, vmem=VMEM_LIMIT):
    return pltpu.CompilerParams(dimension_semantics=sem, vmem_limit_bytes=vmem)


def _rope(a, c, s):
    lane = lax.broadcasted_iota(jnp.int32, a.shape, 1)
    half = ROT_DIM // 2
    partner = jnp.where(lane < half, pltpu.roll(a, LANES - half, 1), pltpu.roll(a, half, 1))
    return a * c + partner * s


def _proj_kernel(flags_ref, x_ref, w_ref, cos_ref, sin_ref, o_ref, xb_sc):
    j = pl.program_id(1)

    @pl.when(j == 0)
    def _():
        xb_sc[...] = x_ref[...].astype(BF16)

    acc = _dot(xb_sc[...], w_ref[...])

    @pl.when(flags_ref[j] == 0)
    def _():
        o_ref[...] = acc.astype(o_ref.dtype)

    @pl.when(flags_ref[j] != 0)
    def _():
        for hh in range(2):
            a = acc[:, hh * LANES:(hh + 1) * LANES]
            o_ref[:, hh * LANES:(hh + 1) * LANES] = _rope(a, cos_ref[...], sin_ref[...]).astype(o_ref.dtype)


def _proj_call(xb, w_aug, flags, cos_t, sin_t, seq, tm):
    n, d = xb.shape
    nj = w_aug.shape[1] // 256
    tpb = seq // tm
    grid_spec = pltpu.PrefetchScalarGridSpec(
        num_scalar_prefetch=1,
        grid=(n // tm, nj),
        in_specs=[
            pl.BlockSpec((tm, d), lambda i, j, f: (i, 0)),
            pl.BlockSpec((d, 256), lambda i, j, f: (0, j)),
            pl.BlockSpec((tm, LANES), lambda i, j, f: (i % tpb, 0)),
            pl.BlockSpec((tm, LANES), lambda i, j, f: (i % tpb, 0)),
        ],
        out_specs=pl.BlockSpec((tm, 256), lambda i, j, f: (i, j)),
        scratch_shapes=[pltpu.VMEM((tm, d), BF16)],
    )
    return pl.pallas_call(
        _proj_kernel,
        out_shape=jax.ShapeDtypeStruct((n, w_aug.shape[1]), BF16),
        grid_spec=grid_spec,
        compiler_params=_params(("parallel", "arbitrary")),
        name="proj",
    )(flags, xb, w_aug, cos_t, sin_t)


def _compress_kernel(r_ref, w1_ref, pos_ref, b1_ref, w2_ref, b2_ref, o_ref):
    nt = r_ref.shape[0]
    ab = _dot(r_ref[...], w1_ref[...])
    pos = pos_ref[...]
    pa = jnp.broadcast_to(pos[0:1], (8, pos.shape[1])).astype(BF16)
    pb = jnp.broadcast_to(pos[1:2], (8, pos.shape[1])).astype(BF16)
    const = _dot(pa, w1_ref[:, :CMP_HIDDEN])[0:1] + _dot(pb, w1_ref[:, CMP_HIDDEN:])[0:1]
    h = ab[:, :CMP_HIDDEN] + pltpu.roll(ab[:, CMP_HIDDEN:], nt - 1, 0) + const + b1_ref[...]
    h = jax.nn.gelu(h)
    o_ref[...] = (_dot(h.astype(BF16), w2_ref[...]) + b2_ref[...]).astype(o_ref.dtype)


def _compress_call(r, w1cat, pos2, b1, w2, b2):
    b, four, nt, k = r.shape
    return pl.pallas_call(
        _compress_kernel,
        out_shape=jax.ShapeDtypeStruct((b, four, nt, HEAD_DIM), BF16),
        grid=(b, four),
        in_specs=[
            pl.BlockSpec((None, None, nt, k), lambda i, c: (i, c, 0, 0)),
            pl.BlockSpec((None, k, 2 * CMP_HIDDEN), lambda i, c: (c // 2, 0, 0)),
            pl.BlockSpec((None, 2, k), lambda i, c: (c // 2, 0, 0)),
            pl.BlockSpec((None, 1, CMP_HIDDEN), lambda i, c: (c // 2, 0, 0)),
            pl.BlockSpec((None, CMP_HIDDEN, HEAD_DIM), lambda i, c: (c // 2, 0, 0)),
            pl.BlockSpec((None, 1, HEAD_DIM), lambda i, c: (c // 2, 0, 0)),
        ],
        out_specs=pl.BlockSpec((None, None, nt, HEAD_DIM), lambda i, c: (i, c, 0, 0)),
        compiler_params=_params(("parallel", "parallel")),
        name="compress",
    )(r, w1cat, pos2, b1, w2, b2)


EXP2_SCALE = HEAD_DIM ** -0.5 * math.log2(math.e)


def _transpose_into(src_ref, dst_ref):
    def body(c, carry):
        off = pl.multiple_of(c * LANES, LANES)
        dst_ref[:, pl.ds(off, LANES)] = src_ref[pl.ds(off, LANES), :].astype(F32).T.astype(dst_ref.dtype)
        return carry

    lax.fori_loop(0, src_ref.shape[0] // LANES, body, 0)


def _diff_kernel(dl_ref, q_ref, k_ref, v_ref, g_ref, o_ref, vt_sc, acc_sc, sa_sc, sb_sc, *, tq, lambda_init):
    qi = pl.program_id(2)

    @pl.when(qi == 0)
    def _():
        for c in range(2):
            _transpose_into(v_ref.at[:, c * LANES:(c + 1) * LANES], vt_sc.at[c * LANES:(c + 1) * LANES, :])

    acc_sc[...] = jnp.zeros(acc_sc.shape, F32)
    q = q_ref[...]
    qpos = qi * tq + lax.broadcasted_iota(jnp.int32, (1, tq), 1)

    def scores(j, dst):
        kt = k_ref[pl.ds(pl.multiple_of(j * tq, tq), tq), :]
        for c in range(2):
            dst[c] = _dot_nt(kt[:, c * HEAD_DIM:(c + 1) * HEAD_DIM], q[:, c * HEAD_DIM:(c + 1) * HEAD_DIM])

    def absorb(src, j, masked, carry):
        off = pl.multiple_of(j * tq, tq)
        vt = vt_sc[:, pl.ds(off, tq)]
        out = []
        for c in range(2):
            m_old, l_old = carry[2 * c], carry[2 * c + 1]
            s = src[c]
            if masked:
                kpos = off + lax.broadcasted_iota(jnp.int32, s.shape, 0)
                s = jnp.where(kpos <= qpos, s, NEG_INF)
            m_new = jnp.maximum(m_old, jnp.max(s, axis=0, keepdims=True))
            p = jnp.exp2((s - m_new) * EXP2_SCALE)
            alpha = jnp.exp2((m_old - m_new) * EXP2_SCALE)
            l_new = alpha * l_old + jnp.sum(p, axis=0, keepdims=True)
            acc_sc[c] = alpha * acc_sc[c] + _dot(vt, p.astype(BF16))
            out += [m_new, l_new]
        return tuple(out)

    def pair(p, carry):
        j = 2 * p
        scores(j + 1, sb_sc)
        carry = absorb(sa_sc, j, False, carry)
        scores(j + 2, sa_sc)
        return absorb(sb_sc, j + 1, False, carry)

    n_pairs = (qi + 2) // 2
    init = (jnp.full((1, tq), NEG_INF, F32), jnp.zeros((1, tq), F32)) * 2
    scores(0, sa_sc)
    carry = lax.fori_loop(0, n_pairs - 1, pair, init)
    j_tail = 2 * (n_pairs - 1)
    scores(j_tail + 1, sb_sc)
    carry = absorb(sa_sc, j_tail, True, carry)
    _, l0, _, l1 = absorb(sb_sc, j_tail + 1, True, carry)

    dl = dl_ref[...]
    lam = (jnp.exp(jnp.sum(dl[0:1] * dl[1:2], axis=1, keepdims=True))
           - jnp.exp(jnp.sum(dl[2:3] * dl[3:4], axis=1, keepdims=True)) + lambda_init)
    o = acc_sc[0] * (1.0 / l0) - lam * (acc_sc[1] * (1.0 / l1))
    o = o * (lax.rsqrt(jnp.mean(o * o, axis=0, keepdims=True) + EPS) * (1.0 - lambda_init))
    for c in range(DIFF_VDIM // LANES):
        for r in range(tq // LANES):
            blk = o[c * LANES:(c + 1) * LANES, r * LANES:(r + 1) * LANES].T
            o_ref[r * LANES:(r + 1) * LANES, c * LANES:(c + 1) * LANES] = (
                blk * g_ref[:, c * LANES:(c + 1) * LANES]).astype(o_ref.dtype)


def _diff_call(proj, dl, g, batch, seq, tq, lambda_init):
    nq = seq // tq
    assert nq % 2 == 0, "key tiles are processed in pairs"
    kern = functools.partial(_diff_kernel, tq=tq, lambda_init=lambda_init)
    return pl.pallas_call(
        kern,
        out_shape=jax.ShapeDtypeStruct((batch * seq, DIFF_HEADS * DIFF_VDIM), BF16),
        grid=(batch, DIFF_HEADS, nq),
        in_specs=[
            pl.BlockSpec((4, HEAD_DIM), lambda b, h, i: (0, 0)),
            pl.BlockSpec((tq, 256), lambda b, h, i: (b * nq + i, U_DQ // 2 + h)),
            pl.BlockSpec((seq, 256), lambda b, h, i: (b, U_DK // 2 + h)),
            pl.BlockSpec((seq, 256), lambda b, h, i: (b, U_DV // 2 + h)),
            pl.BlockSpec((1, DIFF_VDIM), lambda b, h, i: (0, 0)),
        ],
        out_specs=pl.BlockSpec((tq, DIFF_VDIM), lambda b, h, i: (b * nq + i, h)),
        scratch_shapes=[
            pltpu.VMEM((DIFF_VDIM, seq), BF16),
            pltpu.VMEM((2, DIFF_VDIM, tq), F32),
            pltpu.VMEM((2, tq, tq), F32),
            pltpu.VMEM((2, tq, tq), F32),
        ],
        compiler_params=_params(("arbitrary", "arbitrary", "arbitrary")),
        name="diff_attn",
    )(dl, proj, proj, proj, g)


def _stack_heads(x):
    return jnp.concatenate([x[:, h * HEAD_DIM:(h + 1) * HEAD_DIM] for h in range(NSA_HPG)], axis=0)


def _nsa_kernel(q_ref, cos_ref, sin_ref, gate_ref, kc_ref, vc_ref, ks_ref, vs_ref, kw_ref, vw_ref, ovt_ref,
                o_ref, vct_sc, vst_sc, vwt_sc, bias_sc, acc_sc, sa_sc, sb_sc, *, tq, tk, seq, top_k):
    qi = pl.program_id(2)
    q0 = qi * tq
    rows = NSA_HPG * tq
    nt = kc_ref.shape[0]
    ns = ovt_ref.shape[0]
    nb = tk // SLC_BLOCK

    @pl.when(qi == 0)
    def _():
        _transpose_into(vc_ref, vct_sc)
        _transpose_into(vs_ref, vst_sc)
        _transpose_into(vw_ref, vwt_sc)

    q = q_ref[...]
    qs = _stack_heads(q)
    qr = jnp.concatenate(
        [_rope(q[:, h * HEAD_DIM:(h + 1) * HEAD_DIM].astype(F32), cos_ref[...], sin_ref[...]).astype(BF16)
         for h in range(NSA_HPG)], axis=0)
    qpos = q0 + (lax.broadcasted_iota(jnp.int32, (1, rows), 1) & (tq - 1))

    def sel_scores(j, dst):
        dst[...] = _dot_nt(ks_ref[pl.ds(pl.multiple_of(j * tk, tk), tk), :], qr)

    sel_scores(0, sa_sc)

    slab = min(WINDOW + tq, seq)
    start = pl.multiple_of(jnp.maximum(q0 - WINDOW, 0), tq)
    s_w = _dot_nt(kw_ref[pl.ds(start, slab), :], qr)
    dpos = qpos - (start + lax.broadcasted_iota(jnp.int32, (slab, rows), 0))
    s_w = jnp.where((dpos >= 0) & (dpos < WINDOW), s_w, NEG_INF)
    p_w = jnp.exp2((s_w - jnp.max(s_w, axis=0, keepdims=True)) * EXP2_SCALE)
    o_w = _dot(vwt_sc[:, pl.ds(start, slab)], p_w.astype(BF16)) * (1.0 / jnp.sum(p_w, axis=0, keepdims=True))

    s_c = _dot_nt(kc_ref[...], qs)
    cend = lax.broadcasted_iota(jnp.int32, (nt, rows), 0) * CMP_STRIDE + (CMP_BLOCK - 1)
    valid_c = cend <= qpos
    s_c = jnp.where(valid_c, s_c, NEG_INF)
    m_c = jnp.max(s_c, axis=0, keepdims=True)
    e_c = jnp.where(valid_c, jnp.exp2((s_c - m_c) * EXP2_SCALE), 0.0)
    l_c = jnp.sum(e_c, axis=0, keepdims=True)
    p_c = e_c * (1.0 / jnp.where(l_c > 0.0, l_c, 1.0))
    o_c = _dot(vct_sc[...], p_c.astype(BF16))

    p_sum = p_c[:, 0:tq]
    for h in range(1, NSA_HPG):
        p_sum = p_sum + p_c[:, h * tq:(h + 1) * tq]
    p_hi = p_sum.astype(BF16)
    p_lo = (p_sum - p_hi.astype(F32)).astype(BF16)
    imp = _dot(ovt_ref[...], p_hi) + _dot(ovt_ref[...], p_lo)

    blk = lax.broadcasted_iota(jnp.int32, (ns, tq), 0)
    qpos_l = q0 + lax.broadcasted_iota(jnp.int32, (ns, tq), 1)
    cur = lax.shift_right_logical(qpos_l, int(math.log2(SLC_BLOCK)))
    valid_s = blk <= cur
    forced = (blk == 0) | (blk == cur) | (blk == cur - 1)
    work = jnp.where(forced, -jnp.inf, jnp.where(valid_s, imp, NEG_INF))
    sel = jnp.where(forced, 1.0, 0.0)
    blk_f = blk.astype(F32)
    for _ in range(top_k - 3):
        mx = jnp.max(work, axis=0, keepdims=True)
        idx = jnp.min(jnp.where(work == mx, blk_f, float(ns)), axis=0, keepdims=True)
        pick = blk_f == idx
        sel = jnp.where(pick, 1.0, sel)
        work = jnp.where(pick, -jnp.inf, work)
    bias_sc[...] = (sel - 1.0) * BIG

    acc_sc[...] = jnp.zeros(acc_sc.shape, F32)

    def sel_absorb(src, j, causal, carry):
        m_old, l_old = carry
        off = pl.multiple_of(j * tk, tk)
        s = src[...]
        if causal:
            kpos = off + lax.broadcasted_iota(jnp.int32, (tk, rows), 0)
            s = jnp.where(kpos <= qpos, s, NEG_INF)
        bias = bias_sc[pl.ds(pl.multiple_of(j * nb, nb), nb), :]
        bias = jnp.concatenate([bias] * NSA_HPG, axis=1)
        s3 = s.reshape(nb, SLC_BLOCK, rows) + bias[:, None, :]
        m_new = jnp.maximum(m_old, jnp.max(jnp.max(s3, axis=1), axis=0, keepdims=True))
        p3 = jnp.exp2((s3 - m_new) * EXP2_SCALE)
        alpha = jnp.exp2((m_old - m_new) * EXP2_SCALE)
        l_new = alpha * l_old + jnp.sum(jnp.sum(p3, axis=1), axis=0, keepdims=True)
        p = p3.reshape(tk, rows).astype(BF16)
        acc_sc[...] = alpha * acc_sc[...] + _dot(vst_sc[:, pl.ds(off, tk)], p)
        return m_new, l_new

    def sel_pair(p, carry):
        j = 2 * p
        sel_scores(j + 1, sb_sc)
        carry = sel_absorb(sa_sc, j, False, carry)
        sel_scores(j + 2, sa_sc)
        return sel_absorb(sb_sc, j + 1, False, carry)

    n_pairs = (q0 // tk + 2) // 2
    init = (jnp.full((1, rows), NEG_INF, F32), jnp.zeros((1, rows), F32))
    carry = lax.fori_loop(0, n_pairs - 1, sel_pair, init)
    j_tail = 2 * (n_pairs - 1)
    sel_scores(j_tail + 1, sb_sc)
    carry = sel_absorb(sa_sc, j_tail, True, carry)
    _, l_s = sel_absorb(sb_sc, j_tail + 1, True, carry)
    o_s = acc_sc[...] * (1.0 / l_s)

    gates = jax.nn.sigmoid(gate_ref[...].astype(F32)).T
    for h in range(NSA_HPG):
        sl = slice(h * tq, (h + 1) * tq)
        o = (gates[3 * h:3 * h + 1] * o_c[:, sl] + gates[3 * h + 1:3 * h + 2] * o_s[:, sl]
             + gates[3 * h + 2:3 * h + 3] * o_w[:, sl])
        for r in range(tq // LANES):
            o_ref[r * LANES:(r + 1) * LANES, h * HEAD_DIM:(h + 1) * HEAD_DIM] = (
                o[:, r * LANES:(r + 1) * LANES].T.astype(o_ref.dtype))


def _nsa_call(proj, cos_t, sin_t, kvc, ovt, batch, seq, tq, tk):
    nq = seq // tq
    assert (seq // tk) % 2 == 0, "key tiles are processed in pairs"
    nt = kvc.shape[2]
    ns = ovt.shape[0]
    width = NSA_HPG * HEAD_DIM
    kern = functools.partial(_nsa_kernel, tq=tq, tk=tk, seq=seq, top_k=min(SLC_TOPK, ns))

    def col(unit):
        return pl.BlockSpec((seq, HEAD_DIM), lambda b, g, i: (b, unit + g))

    return pl.pallas_call(
        kern,
        out_shape=jax.ShapeDtypeStruct((batch * seq, NSA_HEADS * HEAD_DIM), BF16),
        grid=(batch, NSA_KV_GROUPS, nq),
        in_specs=[
            pl.BlockSpec((tq, width), lambda b, g, i: (b * nq + i, U_NQ // 4 + g)),
            pl.BlockSpec((tq, LANES), lambda b, g, i: (i, 0)),
            pl.BlockSpec((tq, LANES), lambda b, g, i: (i, 0)),
            pl.BlockSpec((tq, LANES), lambda b, g, i: (b * nq + i, U_GATE + g)),
            pl.BlockSpec((None, None, nt, HEAD_DIM), lambda b, g, i: (b, g, 0, 0)),
            pl.BlockSpec((None, None, nt, HEAD_DIM), lambda b, g, i: (b, 2 + g, 0, 0)),
            col(U_KS), col(U_VS), col(U_KW), col(U_VW),
            pl.BlockSpec((ns, nt), lambda b, g, i: (0, 0)),
        ],
        out_specs=pl.BlockSpec((tq, width), lambda b, g, i: (b * nq + i, g)),
        scratch_shapes=[
            pltpu.VMEM((HEAD_DIM, nt), BF16),
            pltpu.VMEM((HEAD_DIM, seq), BF16),
            pltpu.VMEM((HEAD_DIM, seq), BF16),
            pltpu.VMEM((ns, tq), F32),
            pltpu.VMEM((HEAD_DIM, NSA_HPG * tq), F32),
            pltpu.VMEM((tk, NSA_HPG * tq), F32),
            pltpu.VMEM((tk, NSA_HPG * tq), F32),
        ],
        compiler_params=_params(("arbitrary", "arbitrary", "arbitrary")),
        name="nsa_attn",
    )(proj, cos_t, sin_t, proj, kvc, kvc, proj, proj, proj, proj, ovt)


def _layernorm(y, g, b):
    mu = jnp.mean(y, axis=1, keepdims=True)
    yc = y - mu
    var = jnp.mean(yc * yc, axis=1, keepdims=True)
    return yc * lax.rsqrt(var + EPS) * g + b


N_CHUNKS = D_MODEL // LANES
ROW_PITCH = N_CHUNKS + 1


def _store_chunk_rows(ref, val):
    tm = val.shape[0]
    for k in range(N_CHUNKS):
        ref[pl.ds(k, tm, stride=ROW_PITCH), :] = val[:, k * LANES:(k + 1) * LANES]
    for k in range(N_CHUNKS, ROW_PITCH):
        ref[pl.ds(k, tm, stride=ROW_PITCH), :] = jnp.zeros((tm, LANES), ref.dtype)


def _load_chunk_rows(ref, tm):
    return jnp.concatenate([ref[pl.ds(k, tm, stride=ROW_PITCH), :] for k in range(N_CHUNKS)], axis=1)


def _outproj_kernel(od_ref, on_ref, w_ref, x_ref, g_ref, b_ref, wr_ref, x1_ref, route_ref):
    half = od_ref.shape[1]
    h = _dot(od_ref[...], w_ref[:half, :]) + _dot(on_ref[...], w_ref[half:, :])
    x1 = _layernorm(DN_ALPHA * x_ref[...] + h, g_ref[...], b_ref[...])
    _store_chunk_rows(x1_ref, x1)

    wr = wr_ref[...]
    w_hi = wr.astype(BF16)
    w_lo = (wr - w_hi.astype(F32)).astype(BF16)
    x_hi = x1.astype(BF16)
    x_lo = (x1 - x_hi.astype(F32)).astype(BF16)
    logits = _dot(x_hi, w_hi) + (_dot(x_lo, w_hi) + _dot(x_hi, w_lo))

    lane = lax.broadcasted_iota(jnp.int32, logits.shape, 1).astype(F32)
    ninf = -jnp.inf
    gl = jnp.where(lane < N_GROUPS, logits, ninf)
    gmax = jnp.max(gl, axis=1, keepdims=True)
    g_w = 1.0 / jnp.sum(jnp.exp(gl - gmax), axis=1, keepdims=True)
    g_sel = jnp.min(jnp.where(gl == gmax, lane, float(LANES)), axis=1, keepdims=True)
    lo = N_GROUPS + EXPERTS_PER_GROUP * g_sel
    el = jnp.where((lane >= lo) & (lane < lo + EXPERTS_PER_GROUP), logits, ninf)
    e1 = jnp.max(el, axis=1, keepdims=True)
    i1 = jnp.min(jnp.where(el == e1, lane, float(LANES)), axis=1, keepdims=True)
    el2 = jnp.where(lane == i1, ninf, el)
    e2 = jnp.max(el2, axis=1, keepdims=True)
    i2 = jnp.min(jnp.where(el2 == e2, lane, float(LANES)), axis=1, keepdims=True)
    r = jnp.exp(e2 - e1)
    w1 = g_w / (1.0 + r)
    w2 = g_w * r / (1.0 + r)
    route = jnp.where(lane == 0, i1 - N_GROUPS,
                      jnp.where(lane == 1, i2 - N_GROUPS,
                                jnp.where(lane == 2, w1, jnp.where(lane == 3, w2, 0.0))))
    route_ref[...] = route


def _outproj_call(o_diff, o_nsa, w_out, x2, g, b, wr, tm):
    n, d = x2.shape
    half = o_diff.shape[1]
    return pl.pallas_call(
        _outproj_kernel,
        out_shape=(jax.ShapeDtypeStruct((n * ROW_PITCH, LANES), F32), jax.ShapeDtypeStruct((n, LANES), F32)),
        grid=(n // tm,),
        in_specs=[
            pl.BlockSpec((tm, half), lambda i: (i, 0)),
            pl.BlockSpec((tm, half), lambda i: (i, 0)),
            pl.BlockSpec((2 * half, d), lambda i: (0, 0)),
            pl.BlockSpec((tm, d), lambda i: (i, 0)),
            pl.BlockSpec((1, d), lambda i: (0, 0)),
            pl.BlockSpec((1, d), lambda i: (0, 0)),
            pl.BlockSpec((d, LANES), lambda i: (0, 0)),
        ],
        out_specs=(pl.BlockSpec((tm * ROW_PITCH, LANES), lambda i: (i, 0)),
                   pl.BlockSpec((tm, LANES), lambda i: (i, 0))),
        compiler_params=_params(("parallel",)),
        name="outproj_ln1_router",
    )(o_diff, o_nsa, w_out, x2, g, b, wr)


def _token_copy(src_hbm, dst, sem, src_tok, dst_tok):
    return pltpu.make_async_copy(src_hbm.at[pl.ds(src_tok * ROW_PITCH, N_CHUNKS)],
                                 dst.at[pl.ds(dst_tok * ROW_PITCH, N_CHUNKS)], sem)


def _wait_tokens(src_hbm, dst, sem, count):
    pltpu.make_async_copy(src_hbm.at[pl.ds(0, count * N_CHUNKS)], dst.at[pl.ds(0, count * N_CHUNKS)], sem).wait()


def _moe_kernel(te_ref, nu_ref, rows_ref, rows_next_ref, x_hbm, wg_ref, wu_ref, wd_ref, o_ref,
                xbuf, sem, wg_sc, wu_sc, wd_sc, *, tm):
    i = pl.program_id(0)
    n_used = nu_ref[0]
    slot = lax.rem(i, 2)

    @pl.when(i == 0)
    def _():
        def body(r, carry):
            _token_copy(x_hbm, xbuf.at[0], sem.at[0], rows_ref[0, 0, r], r).start()
            return carry

        lax.fori_loop(0, tm, body, 0, unroll=8)

    prev = te_ref[jnp.maximum(i - 1, 0)]

    @pl.when((i == 0) | (te_ref[i] != prev))
    def _():
        wg_sc[...] = wg_ref[...].astype(BF16)
        wu_sc[...] = wu_ref[...].astype(BF16)
        wd_sc[...] = wd_ref[...].astype(BF16)

    @pl.when(i < n_used)
    def _():
        def prefetch(part, parts=4):
            for r in range(part * tm // parts, (part + 1) * tm // parts):
                _token_copy(x_hbm, xbuf.at[1 - slot], sem.at[1 - slot], rows_next_ref[0, 0, r], r).start()

        _wait_tokens(x_hbm, xbuf.at[slot], sem.at[slot], tm)
        xb = _load_chunk_rows(xbuf.at[slot], tm).astype(BF16)
        prefetch(0)
        gate = _dot(xb, wg_sc[...])
        prefetch(1)
        up = _dot(xb, wu_sc[...])
        prefetch(2)
        h = (gate * jax.nn.sigmoid(gate) * up).astype(BF16)
        y = _dot(h, wd_sc[...])
        prefetch(3)
        _store_chunk_rows(o_ref, y)

    @pl.when(i == n_used)
    def _():
        _wait_tokens(x_hbm, xbuf.at[slot], sem.at[slot], tm)

    @pl.when(i >= n_used)
    def _():
        o_ref[...] = jnp.zeros(o_ref.shape, o_ref.dtype)


def _moe_call(tile_expert, n_used, rows, x1c, w_gate, w_up, w_down, tm):
    n_tiles = rows.shape[0]
    d, f = w_gate.shape[1], w_gate.shape[2]
    grid_spec = pltpu.PrefetchScalarGridSpec(
        num_scalar_prefetch=2,
        grid=(n_tiles,),
        in_specs=[
            pl.BlockSpec((1, 1, tm), lambda i, te, nu: (i, 0, 0), memory_space=pltpu.SMEM),
            pl.BlockSpec((1, 1, tm), lambda i, te, nu: (jnp.minimum(i + 1, n_tiles - 1), 0, 0),
                         memory_space=pltpu.SMEM),
            pl.BlockSpec(memory_space=pl.ANY),
            pl.BlockSpec((None, d, f), lambda i, te, nu: (te[i], 0, 0)),
            pl.BlockSpec((None, d, f), lambda i, te, nu: (te[i], 0, 0)),
            pl.BlockSpec((None, f, d), lambda i, te, nu: (te[i], 0, 0)),
        ],
        out_specs=pl.BlockSpec((tm * ROW_PITCH, LANES), lambda i, te, nu: (i, 0)),
        scratch_shapes=[
            pltpu.VMEM((2, tm * ROW_PITCH, LANES), F32),
            pltpu.SemaphoreType.DMA((2,)),
            pltpu.VMEM((d, f), BF16), pltpu.VMEM((d, f), BF16), pltpu.VMEM((f, d), BF16),
        ],
    )
    return pl.pallas_call(
        functools.partial(_moe_kernel, tm=tm),
        out_shape=jax.ShapeDtypeStruct((n_tiles * tm * ROW_PITCH, LANES), F32),
        grid_spec=grid_spec,
        compiler_params=_params(("arbitrary",)),
        name="moe_experts",
    )(tile_expert, n_used, rows, rows, x1c, w_gate, w_up, w_down)


def _combine_kernel(slots_ref, slots_next_ref, y_hbm, x1_ref, route_ref, g_ref, b_ref, o_ref, buf, sem, *, tm):
    i = pl.program_id(0)
    slot = lax.rem(i, 2)

    @pl.when(i == 0)
    def _():
        def body(r, carry):
            for k in range(2):
                _token_copy(y_hbm, buf.at[0, k], sem.at[0], slots_ref[0, 0, 2 * r + k], r).start()
            return carry

        lax.fori_loop(0, tm, body, 0, unroll=8)

    def tile(prefetch):
        if prefetch:
            for r in range(tm):
                for k in range(2):
                    _token_copy(y_hbm, buf.at[1 - slot, k], sem.at[1 - slot],
                                slots_next_ref[0, 0, 2 * r + k], r).start()
        for k in range(2):
            _wait_tokens(y_hbm, buf.at[slot, k], sem.at[slot], tm)
        route = route_ref[...]
        y = (route[:, 2:3] * _load_chunk_rows(buf.at[slot, 0], tm)
             + route[:, 3:4] * _load_chunk_rows(buf.at[slot, 1], tm))
        x1 = _load_chunk_rows(x1_ref, tm)
        o_ref[...] = _layernorm(DN_ALPHA * x1 + y, g_ref[...], b_ref[...])

    last = pl.num_programs(0) - 1
    pl.when(i < last)(lambda: tile(True))
    pl.when(i == last)(lambda: tile(False))


def _combine_call(slots, yc, x1c, route, g, b, tm):
    n = route.shape[0]
    d = D_MODEL
    nt = n // tm
    return pl.pallas_call(
        functools.partial(_combine_kernel, tm=tm),
        out_shape=jax.ShapeDtypeStruct((n, d), F32),
        grid=(nt,),
        in_specs=[
            pl.BlockSpec((1, 1, 2 * tm), lambda i: (i, 0, 0), memory_space=pltpu.SMEM),
            pl.BlockSpec((1, 1, 2 * tm), lambda i: (jnp.minimum(i + 1, nt - 1), 0, 0), memory_space=pltpu.SMEM),
            pl.BlockSpec(memory_space=pl.ANY),
            pl.BlockSpec((tm * ROW_PITCH, LANES), lambda i: (i, 0)),
            pl.BlockSpec((tm, LANES), lambda i: (i, 0)),
            pl.BlockSpec((1, d), lambda i: (0, 0)),
            pl.BlockSpec((1, d), lambda i: (0, 0)),
        ],
        out_specs=pl.BlockSpec((tm, d), lambda i: (i, 0)),
        scratch_shapes=[pltpu.VMEM((2, 2, tm * ROW_PITCH, LANES), F32), pltpu.SemaphoreType.DMA((2,))],
        compiler_params=_params(("arbitrary",)),
        name="moe_combine_ln2",
    )(slots, slots, yc, x1c, route, g, b)


def _moe_plan(e_idx, tm, n_tiles):
    n = e_idx.shape[0]
    flat = e_idx.reshape(-1)
    onehot = (flat[:, None] == jnp.arange(N_EXPERTS, dtype=jnp.int32)[None, :]).astype(jnp.int32)
    csum = jnp.cumsum(onehot, axis=0)
    rank = jnp.take_along_axis(csum, flat[:, None], axis=1)[:, 0] - 1
    counts = csum[-1]
    ptiles = (counts + tm - 1) // tm
    tile_end = jnp.cumsum(ptiles)
    slot = (tile_end - ptiles)[flat] * tm + rank
    rows = jnp.zeros((n_tiles * tm,), jnp.int32).at[slot].set(jnp.arange(2 * n, dtype=jnp.int32) // 2)
    tile_ids = jnp.arange(n_tiles, dtype=jnp.int32)
    tile_expert = jnp.minimum(jnp.sum((tile_end[None, :] <= tile_ids[:, None]).astype(jnp.int32), axis=1),
                              N_EXPERTS - 1)
    return slot.astype(jnp.int32), rows, tile_expert, tile_end[-1:].astype(jnp.int32)


def _rope_tables(seq):
    half = ROT_DIM // 2
    inv_freq = ROPE_THETA ** (-jnp.arange(0, ROT_DIM, 2, dtype=F32) / ROT_DIM)
    ang = jnp.arange(seq, dtype=F32)[:, None] * inv_freq[None, :]
    cos, sin = jnp.cos(ang), jnp.sin(ang)
    pad1 = jnp.ones((seq, HEAD_DIM - ROT_DIM), F32)
    pad0 = jnp.zeros((seq, HEAD_DIM - ROT_DIM), F32)
    return (jnp.concatenate([cos, cos, pad1], axis=1), jnp.concatenate([-sin, sin, pad0], axis=1))


def _layer(x, w_in, diff_lambda, diff_subln_g, cmp_pos, cmp_w1, cmp_b1, cmp_w2, cmp_b2, w_out,
           ln1_g, ln1_b, router_group, router_expert, w_gate, w_up, w_down, ln2_g, ln2_b, lambda_init):
    batch, seq, d = x.shape
    n = batch * seq
    x2 = x.reshape(n, d)

    seg = lambda a, b: w_in[:, a:b]
    pad = lambda w: jnp.pad(w, ((0, 0), (0, LANES - w.shape[1])))
    gate_w = seg(5632, 5656)
    w_aug = jnp.concatenate(
        [seg(3072, 4096), seg(0, 1024), seg(1024, 2048), seg(2048, 3072), seg(4096, 5632),
         pad(gate_w[:, :12]), pad(gate_w[:, 12:])], axis=1).astype(BF16)
    flags = jnp.zeros((N_UNITS // 2,), jnp.int32).at[jnp.array(ROPE_BLOCKS)].set(1)
    cos_t, sin_t = _rope_tables(seq)
    proj = _proj_call(x2, w_aug, flags, cos_t, sin_t, seq, tm=min(2048, seq))

    nt = seq // CMP_STRIDE
    half_feat = CMP_STRIDE * HEAD_DIM
    r = proj[:, U_KC * LANES:U_KS * LANES].reshape(batch, nt, CMP_STRIDE, 4, HEAD_DIM)
    r = jnp.transpose(r, (0, 3, 1, 2, 4)).reshape(batch, 4, nt, half_feat)
    w1cat = jnp.concatenate([cmp_w1[:, :half_feat], cmp_w1[:, half_feat:]], axis=2).astype(BF16)
    kvc = _compress_call(r, w1cat, cmp_pos.reshape(2, 2, half_feat), cmp_b1[:, None, :],
                         cmp_w2.astype(BF16), cmp_b2[:, None, :])

    o_diff = _diff_call(proj, diff_lambda, diff_subln_g[None, :], batch, seq, min(512, seq), lambda_init)

    ns = seq // SLC_BLOCK
    ci = np.arange(nt)[None, :] * CMP_STRIDE
    sj = np.arange(ns)[:, None] * SLC_BLOCK
    ovt = jnp.asarray((ci < sj + SLC_BLOCK) & (ci + CMP_BLOCK > sj) & (np.arange(nt)[None, :] < nt - 1), dtype=BF16)
    o_nsa = _nsa_call(proj, cos_t, sin_t, kvc, ovt, batch, seq, tq=LANES, tk=min(512, seq))

    wr = jnp.pad(jnp.concatenate([router_group, router_expert], axis=1),
                 ((0, 0), (0, LANES - N_GROUPS - N_EXPERTS)))
    x1, route = _outproj_call(o_diff, o_nsa, w_out.astype(BF16), x2, ln1_g[None, :], ln1_b[None, :], wr, tm=512)

    tm = 256
    n_tiles = 2 * n // tm + N_EXPERTS + 1
    e_idx = route[:, 0:2].astype(jnp.int32)
    slot, rows, tile_expert, n_used = _moe_plan(e_idx, tm, n_tiles)
    ys = _moe_call(tile_expert, n_used, rows.reshape(n_tiles, 1, tm), x1, w_gate, w_up, w_down, tm)
    out = _combine_call(slot.reshape(n // tm, 1, 2 * tm), ys, x1, route, ln2_g[None, :], ln2_b[None, :], tm)
    return out.reshape(batch, seq, d)


def kernel(x, w_in, diff_lambda, diff_subln_g, cmp_pos, cmp_w1, cmp_b1, cmp_w2, cmp_b2, w_out, ln1_g, ln1_b,
           router_group, router_expert, expert_w_gate, expert_w_up, expert_w_down, ln2_g, ln2_b):
    for l in range(DEPTH):
        lambda_init = 0.8 - 0.6 * math.exp(-0.3 * l)
        x = _layer(x, w_in[l], diff_lambda[l], diff_subln_g[l], cmp_pos[l], cmp_w1[l], cmp_b1[l], cmp_w2[l],
                   cmp_b2[l], w_out[l], ln1_g[l], ln1_b[l], router_group[l], router_expert[l],
                   expert_w_gate[l], expert_w_up[l], expert_w_down[l], ln2_g[l], ln2_b[l], lambda_init)
    return x
```

```python
import functools
import math

import numpy as np
import jax
import jax.numpy as jnp
from jax import lax
from jax.experimental import pallas as pl
from jax.experimental.pallas import tpu as pltpu

F32 = jnp.float32
BF16 = jnp.bfloat16

D_MODEL = 2048
HEAD_DIM = 128
ROT_DIM = HEAD_DIM // 4
ROPE_THETA = 500000.0
NEG_INF = -1e30
BIG = 1e30
EPS = 1e-5

DIFF_HEADS = 4
DIFF_VDIM = 2 * HEAD_DIM

NSA_HEADS = 8
NSA_KV_GROUPS = 2
NSA_HPG = NSA_HEADS // NSA_KV_GROUPS
CMP_BLOCK = 32
CMP_STRIDE = 16
CMP_HIDDEN = 256
SLC_BLOCK = 64
SLC_TOPK = 16
WINDOW = 512

N_GROUPS = 4
EXPERTS_PER_GROUP = 8
N_EXPERTS = N_GROUPS * EXPERTS_PER_GROUP
EXPERT_HIDDEN = 512

DEPTH = 1
DN_ALPHA = (2.0 * DEPTH) ** 0.25

LANES = 128
VMEM_LIMIT = 56 * 1024 * 1024

U_NQ, U_DQ, U_DK, U_DV = 0, 8, 16, 24
U_KC, U_VC, U_KS, U_VS, U_KW, U_VW, U_GATE = 32, 34, 36, 38, 40, 42, 44
N_UNITS = 46
ROPE_BLOCKS = tuple(range(U_DQ // 2, U_DV // 2)) + (U_KS // 2, U_KW // 2)


def _dot(a, b):
    return jnp.dot(a, b, preferred_element_type=F32)


def _dot_nt(a, b):
    return lax.dot_general(a, b, (((1,), (1,)), ((), ())), preferred_element_type=F32)


def _params(sem, vmem=VMEM_LIMIT):
    return pltpu.CompilerParams(dimension_semantics=sem, vmem_limit_bytes=vmem)


def _rope(a, c, s):
    lane = lax.broadcasted_iota(jnp.int32, a.shape, 1)
    half = ROT_DIM // 2
    partner = jnp.where(lane < half, pltpu.roll(a, LANES - half, 1), pltpu.roll(a, half, 1))
    return a * c + partner * s


def _proj_kernel(flags_ref, x_ref, w_ref, cos_ref, sin_ref, o_ref, xb_sc):
    j = pl.program_id(1)

    @pl.when(j == 0)
    def _():
        xb_sc[...] = x_ref[...].astype(BF16)

    acc = _dot(xb_sc[...], w_ref[...])

    @pl.when(flags_ref[j] == 0)
    def _():
        o_ref[...] = acc.astype(o_ref.dtype)

    @pl.when(flags_ref[j] != 0)
    def _():
        for hh in range(2):
            a = acc[:, hh * LANES:(hh + 1) * LANES]
            o_ref[:, hh * LANES:(hh + 1) * LANES] = _rope(a, cos_ref[...], sin_ref[...]).astype(o_ref.dtype)


def _proj_call(xb, w_aug, flags, cos_t, sin_t, seq, tm):
    n, d = xb.shape
    nj = w_aug.shape[1] // 256
    tpb = seq // tm
    grid_spec = pltpu.PrefetchScalarGridSpec(
        num_scalar_prefetch=1,
        grid=(n // tm, nj),
        in_specs=[
            pl.BlockSpec((tm, d), lambda i, j, f: (i, 0)),
            pl.BlockSpec((d, 256), lambda i, j, f: (0, j)),
            pl.BlockSpec((tm, LANES), lambda i, j, f: (i % tpb, 0)),
            pl.BlockSpec((tm, LANES), lambda i, j, f: (i % tpb, 0)),
        ],
        out_specs=pl.BlockSpec((tm, 256), lambda i, j, f: (i, j)),
        scratch_shapes=[pltpu.VMEM((tm, d), BF16)],
    )
    return pl.pallas_call(
        _proj_kernel,
        out_shape=jax.ShapeDtypeStruct((n, w_aug.shape[1]), BF16),
        grid_spec=grid_spec,
        compiler_params=_params(("parallel", "arbitrary")),
        name="proj",
    )(flags, xb, w_aug, cos_t, sin_t)


def _compress_kernel(r_ref, w1_ref, pos_ref, b1_ref, w2_ref, b2_ref, o_ref):
    nt = r_ref.shape[0]
    ab = _dot(r_ref[...], w1_ref[...])
    pos = pos_ref[...]
    pa = jnp.broadcast_to(pos[0:1], (8, pos.shape[1])).astype(BF16)
    pb = jnp.broadcast_to(pos[1:2], (8, pos.shape[1])).astype(BF16)
    const = _dot(pa, w1_ref[:, :CMP_HIDDEN])[0:1] + _dot(pb, w1_ref[:, CMP_HIDDEN:])[0:1]
    h = ab[:, :CMP_HIDDEN] + pltpu.roll(ab[:, CMP_HIDDEN:], nt - 1, 0) + const + b1_ref[...]
    h = jax.nn.gelu(h)
    o_ref[...] = (_dot(h.astype(BF16), w2_ref[...]) + b2_ref[...]).astype(o_ref.dtype)


def _compress_call(r, w1cat, pos2, b1, w2, b2):
    b, four, nt, k = r.shape
    return pl.pallas_call(
        _compress_kernel,
        out_shape=jax.ShapeDtypeStruct((b, four, nt, HEAD_DIM), BF16),
        grid=(b, four),
        in_specs=[
            pl.BlockSpec((None, None, nt, k), lambda i, c: (i, c, 0, 0)),
            pl.BlockSpec((None, k, 2 * CMP_HIDDEN), lambda i, c: (c // 2, 0, 0)),
            pl.BlockSpec((None, 2, k), lambda i, c: (c // 2, 0, 0)),
            pl.BlockSpec((None, 1, CMP_HIDDEN), lambda i, c: (c // 2, 0, 0)),
            pl.BlockSpec((None, CMP_HIDDEN, HEAD_DIM), lambda i, c: (c // 2, 0, 0)),
            pl.BlockSpec((None, 1, HEAD_DIM), lambda i, c: (c // 2, 0, 0)),
        ],
        out_specs=pl.BlockSpec((None, None, nt, HEAD_DIM), lambda i, c: (i, c, 0, 0)),
        compiler_params=_params(("parallel", "parallel")),
        name="compress",
    )(r, w1cat, pos2, b1, w2, b2)


EXP2_SCALE = HEAD_DIM ** -0.5 * math.log2(math.e)
ONES_ROWS = 16


def _transpose_into(src_ref, dst_ref):
    def body(c, carry):
        off = pl.multiple_of(c * LANES, LANES)
        dst_ref[:, pl.ds(off, LANES)] = src_ref[pl.ds(off, LANES), :].astype(F32).T.astype(dst_ref.dtype)
        return carry

    lax.fori_loop(0, src_ref.shape[0] // LANES, body, 0)


def _diff_kernel(dl_ref, q_ref, k_ref, v_ref, g_ref, o_ref, vt_sc, acc_sc, sa_sc, sb_sc, *, tq, lambda_init):
    qi = pl.program_id(2)

    @pl.when(qi == 0)
    def _():
        for c in range(2):
            _transpose_into(v_ref.at[:, c * LANES:(c + 1) * LANES], vt_sc.at[c * LANES:(c + 1) * LANES, :])

    acc_sc[...] = jnp.zeros(acc_sc.shape, F32)
    q = q_ref[...]
    qpos = qi * tq + lax.broadcasted_iota(jnp.int32, (1, tq), 1)

    def scores(j, dst):
        kt = k_ref[pl.ds(pl.multiple_of(j * tq, tq), tq), :]
        for c in range(2):
            dst[c] = _dot_nt(kt[:, c * HEAD_DIM:(c + 1) * HEAD_DIM], q[:, c * HEAD_DIM:(c + 1) * HEAD_DIM])

    def absorb(src, j, masked, carry):
        off = pl.multiple_of(j * tq, tq)
        vt = vt_sc[:, pl.ds(off, tq)]
        out = []
        for c in range(2):
            m_old, l_old = carry[2 * c], carry[2 * c + 1]
            s = src[c]
            if masked:
                kpos = off + lax.broadcasted_iota(jnp.int32, s.shape, 0)
                s = jnp.where(kpos <= qpos, s, NEG_INF)
            m_new = jnp.maximum(m_old, jnp.max(s, axis=0, keepdims=True))
            p = jnp.exp2((s - m_new) * EXP2_SCALE)
            alpha = jnp.exp2((m_old - m_new) * EXP2_SCALE)
            l_new = alpha * l_old + jnp.sum(p, axis=0, keepdims=True)
            acc_sc[c] = alpha * acc_sc[c] + _dot(vt, p.astype(BF16))
            out += [m_new, l_new]
        return tuple(out)

    def pair(p, carry):
        j = 2 * p
        scores(j + 1, sb_sc)
        carry = absorb(sa_sc, j, False, carry)
        scores(j + 2, sa_sc)
        return absorb(sb_sc, j + 1, False, carry)

    n_pairs = (qi + 2) // 2
    init = (jnp.full((1, tq), NEG_INF, F32), jnp.zeros((1, tq), F32)) * 2
    scores(0, sa_sc)
    carry = lax.fori_loop(0, n_pairs - 1, pair, init)
    j_tail = 2 * (n_pairs - 1)
    scores(j_tail + 1, sb_sc)
    carry = absorb(sa_sc, j_tail, True, carry)
    _, l0, _, l1 = absorb(sb_sc, j_tail + 1, True, carry)

    dl = dl_ref[...]
    lam = (jnp.exp(jnp.sum(dl[0:1] * dl[1:2], axis=1, keepdims=True))
           - jnp.exp(jnp.sum(dl[2:3] * dl[3:4], axis=1, keepdims=True)) + lambda_init)
    o = acc_sc[0] * (1.0 / l0) - lam * (acc_sc[1] * (1.0 / l1))
    o = o * (lax.rsqrt(jnp.mean(o * o, axis=0, keepdims=True) + EPS) * (1.0 - lambda_init))
    for c in range(DIFF_VDIM // LANES):
        for r in range(tq // LANES):
            blk = o[c * LANES:(c + 1) * LANES, r * LANES:(r + 1) * LANES].T
            o_ref[r * LANES:(r + 1) * LANES, c * LANES:(c + 1) * LANES] = (
                blk * g_ref[:, c * LANES:(c + 1) * LANES]).astype(o_ref.dtype)


def _diff_call(proj, dl, g, batch, seq, tq, lambda_init):
    nq = seq // tq
    assert nq % 2 == 0, "key tiles are processed in pairs"
    kern = functools.partial(_diff_kernel, tq=tq, lambda_init=lambda_init)
    return pl.pallas_call(
        kern,
        out_shape=jax.ShapeDtypeStruct((batch * seq, DIFF_HEADS * DIFF_VDIM), BF16),
        grid=(batch, DIFF_HEADS, nq),
        in_specs=[
            pl.BlockSpec((4, HEAD_DIM), lambda b, h, i: (0, 0)),
            pl.BlockSpec((tq, 256), lambda b, h, i: (b * nq + i, U_DQ // 2 + h)),
            pl.BlockSpec((seq, 256), lambda b, h, i: (b, U_DK // 2 + h)),
            pl.BlockSpec((seq, 256), lambda b, h, i: (b, U_DV // 2 + h)),
            pl.BlockSpec((1, DIFF_VDIM), lambda b, h, i: (0, 0)),
        ],
        out_specs=pl.BlockSpec((tq, DIFF_VDIM), lambda b, h, i: (b * nq + i, h)),
        scratch_shapes=[
            pltpu.VMEM((DIFF_VDIM, seq), BF16),
            pltpu.VMEM((2, DIFF_VDIM, tq), F32),
            pltpu.VMEM((2, tq, tq), F32),
            pltpu.VMEM((2, tq, tq), F32),
        ],
        compiler_params=_params(("arbitrary", "arbitrary", "arbitrary")),
        name="diff_attn",
    )(dl, proj, proj, proj, g)


def _stack_heads(x):
    return jnp.concatenate([x[:, h * HEAD_DIM:(h + 1) * HEAD_DIM] for h in range(NSA_HPG)], axis=0)


def _nsa_kernel(q_ref, cos_ref, sin_ref, gate_ref, kc_ref, vc_ref, ks_ref, vs_ref, kw_ref, vw_ref, ovt_ref,
                o_ref, vct_sc, vst_sc, vwt_sc, bias_sc, acc_sc, sa_sc, sb_sc, *, tq, tk, seq, top_k):
    qi = pl.program_id(2)
    q0 = qi * tq
    rows = NSA_HPG * tq
    nt = kc_ref.shape[0]
    ns = ovt_ref.shape[0]
    nb = tk // SLC_BLOCK

    @pl.when(qi == 0)
    def _():
        _transpose_into(vc_ref, vct_sc)
        for v_ref, vt_sc in ((vs_ref, vst_sc), (vw_ref, vwt_sc)):
            _transpose_into(v_ref, vt_sc.at[:HEAD_DIM, :])
            vt_sc[HEAD_DIM:, :] = jnp.ones((ONES_ROWS, seq), vt_sc.dtype)

    q = q_ref[...]
    qs = _stack_heads(q)
    qr = jnp.concatenate(
        [_rope(q[:, h * HEAD_DIM:(h + 1) * HEAD_DIM].astype(F32), cos_ref[...], sin_ref[...]).astype(BF16)
         for h in range(NSA_HPG)], axis=0)
    qpos = q0 + (lax.broadcasted_iota(jnp.int32, (1, rows), 1) & (tq - 1))

    def sel_scores(j, dst):
        dst[...] = _dot_nt(ks_ref[pl.ds(pl.multiple_of(j * tk, tk), tk), :], qr)

    sel_scores(0, sa_sc)

    slab = min(WINDOW + tq, seq)
    start = pl.multiple_of(jnp.maximum(q0 - WINDOW, 0), tq)
    s_w = _dot_nt(kw_ref[pl.ds(start, slab), :], qr)
    dpos = qpos - (start + lax.broadcasted_iota(jnp.int32, (slab, rows), 0))
    s_w = jnp.where((dpos >= 0) & (dpos < WINDOW), s_w, NEG_INF)
    p_w = jnp.exp2((s_w - jnp.max(s_w, axis=0, keepdims=True)) * EXP2_SCALE)
    o_w = _dot(vwt_sc[:, pl.ds(start, slab)], p_w.astype(BF16))
    o_w = o_w[:HEAD_DIM] * (1.0 / o_w[HEAD_DIM:HEAD_DIM + 1])

    s_c = _dot_nt(kc_ref[...], qs)
    cend = lax.broadcasted_iota(jnp.int32, (nt, rows), 0) * CMP_STRIDE + (CMP_BLOCK - 1)
    valid_c = cend <= qpos
    s_c = jnp.where(valid_c, s_c, NEG_INF)
    m_c = jnp.max(s_c, axis=0, keepdims=True)
    e_c = jnp.where(valid_c, jnp.exp2((s_c - m_c) * EXP2_SCALE), 0.0)
    l_c = jnp.sum(e_c, axis=0, keepdims=True)
    p_c = e_c * (1.0 / jnp.where(l_c > 0.0, l_c, 1.0))
    o_c = _dot(vct_sc[...], p_c.astype(BF16))

    p_sum = p_c[:, 0:tq]
    for h in range(1, NSA_HPG):
        p_sum = p_sum + p_c[:, h * tq:(h + 1) * tq]
    p_hi = p_sum.astype(BF16)
    p_lo = (p_sum - p_hi.astype(F32)).astype(BF16)
    imp = _dot(ovt_ref[...], p_hi) + _dot(ovt_ref[...], p_lo)

    blk = lax.broadcasted_iota(jnp.int32, (ns, tq), 0)
    qpos_l = q0 + lax.broadcasted_iota(jnp.int32, (ns, tq), 1)
    cur = lax.shift_right_logical(qpos_l, int(math.log2(SLC_BLOCK)))
    valid_s = blk <= cur
    forced = (blk == 0) | (blk == cur) | (blk == cur - 1)
    work = jnp.where(forced, -jnp.inf, jnp.where(valid_s, imp, NEG_INF))
    sel = jnp.where(forced, 1.0, 0.0)
    blk_f = blk.astype(F32)
    for _ in range(top_k - 3):
        mx = jnp.max(work, axis=0, keepdims=True)
        idx = jnp.min(jnp.where(work == mx, blk_f, float(ns)), axis=0, keepdims=True)
        pick = blk_f == idx
        sel = jnp.where(pick, 1.0, sel)
        work = jnp.where(pick, -jnp.inf, work)
    bias_sc[...] = (sel - 1.0) * BIG

    acc_sc[...] = jnp.zeros(acc_sc.shape, F32)

    def sel_absorb(src, j, causal, m_old):
        off = pl.multiple_of(j * tk, tk)
        s = src[...]
        if causal:
            kpos = off + lax.broadcasted_iota(jnp.int32, (tk, rows), 0)
            s = jnp.where(kpos <= qpos, s, NEG_INF)
        bias = bias_sc[pl.ds(pl.multiple_of(j * nb, nb), nb), :]
        bias = jnp.concatenate([bias] * NSA_HPG, axis=1)
        s3 = s.reshape(nb, SLC_BLOCK, rows) + bias[:, None, :]
        m_new = jnp.maximum(m_old, jnp.max(jnp.max(s3, axis=1), axis=0, keepdims=True))
        p3 = jnp.exp2((s3 - m_new) * EXP2_SCALE)
        alpha = jnp.exp2((m_old - m_new) * EXP2_SCALE)
        p = p3.reshape(tk, rows).astype(BF16)
        acc_sc[...] = alpha * acc_sc[...] + _dot(vst_sc[:, pl.ds(off, tk)], p)
        return m_new

    def sel_pair(p, carry):
        j = 2 * p
        sel_scores(j + 1, sb_sc)
        carry = sel_absorb(sa_sc, j, False, carry)
        sel_scores(j + 2, sa_sc)
        return sel_absorb(sb_sc, j + 1, False, carry)

    n_pairs = (q0 // tk + 2) // 2
    carry = lax.fori_loop(0, n_pairs - 1, sel_pair, jnp.full((1, rows), NEG_INF, F32))
    j_tail = 2 * (n_pairs - 1)
    sel_scores(j_tail + 1, sb_sc)
    carry = sel_absorb(sa_sc, j_tail, True, carry)
    sel_absorb(sb_sc, j_tail + 1, True, carry)
    o_s = acc_sc[:HEAD_DIM, :] * (1.0 / acc_sc[HEAD_DIM:HEAD_DIM + 1, :])

    gates = jax.nn.sigmoid(gate_ref[...].astype(F32))
    gates = jnp.concatenate([gates[r * LANES:(r + 1) * LANES].T for r in range(tq // LANES)],
                            axis=1)
    for h in range(NSA_HPG):
        sl = slice(h * tq, (h + 1) * tq)
        o = (gates[3 * h:3 * h + 1] * o_c[:, sl] + gates[3 * h + 1:3 * h + 2] * o_s[:, sl]
             + gates[3 * h + 2:3 * h + 3] * o_w[:, sl])
        for r in range(tq // LANES):
            o_ref[r * LANES:(r + 1) * LANES, h * HEAD_DIM:(h + 1) * HEAD_DIM] = (
                o[:, r * LANES:(r + 1) * LANES].T.astype(o_ref.dtype))


def _nsa_call(proj, cos_t, sin_t, kvc, ovt, batch, seq, tq, tk):
    nq = seq // tq
    assert (seq // tk) % 2 == 0, "key tiles are processed in pairs"
    nt = kvc.shape[2]
    ns = ovt.shape[0]
    width = NSA_HPG * HEAD_DIM
    kern = functools.partial(_nsa_kernel, tq=tq, tk=tk, seq=seq, top_k=min(SLC_TOPK, ns))

    def col(unit):
        return pl.BlockSpec((seq, HEAD_DIM), lambda b, g, i: (b, unit + g))

    return pl.pallas_call(
        kern,
        out_shape=jax.ShapeDtypeStruct((batch * seq, NSA_HEADS * HEAD_DIM), BF16),
        grid=(batch, NSA_KV_GROUPS, nq),
        in_specs=[
            pl.BlockSpec((tq, width), lambda b, g, i: (b * nq + i, U_NQ // 4 + g)),
            pl.BlockSpec((tq, LANES), lambda b, g, i: (i, 0)),
            pl.BlockSpec((tq, LANES), lambda b, g, i: (i, 0)),
            pl.BlockSpec((tq, LANES), lambda b, g, i: (b * nq + i, U_GATE + g)),
            pl.BlockSpec((None, None, nt, HEAD_DIM), lambda b, g, i: (b, g, 0, 0)),
            pl.BlockSpec((None, None, nt, HEAD_DIM), lambda b, g, i: (b, 2 + g, 0, 0)),
            col(U_KS), col(U_VS), col(U_KW), col(U_VW),
            pl.BlockSpec((ns, nt), lambda b, g, i: (0, 0)),
        ],
        out_specs=pl.BlockSpec((tq, width), lambda b, g, i: (b * nq + i, g)),
        scratch_shapes=[
            pltpu.VMEM((HEAD_DIM, nt), BF16),
            pltpu.VMEM((HEAD_DIM + ONES_ROWS, seq), BF16),
            pltpu.VMEM((HEAD_DIM + ONES_ROWS, seq), BF16),
            pltpu.VMEM((ns, tq), F32),
            pltpu.VMEM((HEAD_DIM + ONES_ROWS, NSA_HPG * tq), F32),
            pltpu.VMEM((tk, NSA_HPG * tq), F32),
            pltpu.VMEM((tk, NSA_HPG * tq), F32),
        ],
        compiler_params=_params(("arbitrary", "arbitrary", "arbitrary")),
        name="nsa_attn",
    )(proj, cos_t, sin_t, proj, kvc, kvc, proj, proj, proj, proj, ovt)


def _layernorm(y, g, b):
    mu = jnp.mean(y, axis=1, keepdims=True)
    yc = y - mu
    var = jnp.mean(yc * yc, axis=1, keepdims=True)
    return yc * lax.rsqrt(var + EPS) * g + b


N_CHUNKS = D_MODEL // LANES
ROW_PITCH = N_CHUNKS + 1


def _store_chunk_rows(ref, val):
    tm = val.shape[0]
    for k in range(N_CHUNKS):
        ref[pl.ds(k, tm, stride=ROW_PITCH), :] = val[:, k * LANES:(k + 1) * LANES]
    for k in range(N_CHUNKS, ROW_PITCH):
        ref[pl.ds(k, tm, stride=ROW_PITCH), :] = jnp.zeros((tm, LANES), ref.dtype)


def _load_chunk_rows(ref, tm):
    return jnp.concatenate([ref[pl.ds(k, tm, stride=ROW_PITCH), :] for k in range(N_CHUNKS)], axis=1)


def _outproj_kernel(od_ref, on_ref, w_ref, x_ref, g_ref, b_ref, wr_ref, x1_ref, route_ref):
    half = od_ref.shape[1]
    h = _dot(od_ref[...], w_ref[:half, :]) + _dot(on_ref[...], w_ref[half:, :])
    x1 = _layernorm(DN_ALPHA * x_ref[...] + h, g_ref[...], b_ref[...])
    _store_chunk_rows(x1_ref, x1)

    wr = wr_ref[...]
    w_hi = wr.astype(BF16)
    w_lo = (wr - w_hi.astype(F32)).astype(BF16)
    x_hi = x1.astype(BF16)
    x_lo = (x1 - x_hi.astype(F32)).astype(BF16)
    logits = _dot(x_hi, w_hi) + (_dot(x_lo, w_hi) + _dot(x_hi, w_lo))

    lane = lax.broadcasted_iota(jnp.int32, logits.shape, 1).astype(F32)
    ninf = -jnp.inf
    gl = jnp.where(lane < N_GROUPS, logits, ninf)
    gmax = jnp.max(gl, axis=1, keepdims=True)
    g_w = 1.0 / jnp.sum(jnp.exp(gl - gmax), axis=1, keepdims=True)
    g_sel = jnp.min(jnp.where(gl == gmax, lane, float(LANES)), axis=1, keepdims=True)
    lo = N_GROUPS + EXPERTS_PER_GROUP * g_sel
    el = jnp.where((lane >= lo) & (lane < lo + EXPERTS_PER_GROUP), logits, ninf)
    e1 = jnp.max(el, axis=1, keepdims=True)
    i1 = jnp.min(jnp.where(el == e1, lane, float(LANES)), axis=1, keepdims=True)
    el2 = jnp.where(lane == i1, ninf, el)
    e2 = jnp.max(el2, axis=1, keepdims=True)
    i2 = jnp.min(jnp.where(el2 == e2, lane, float(LANES)), axis=1, keepdims=True)
    r = jnp.exp(e2 - e1)
    w1 = g_w / (1.0 + r)
    w2 = g_w * r / (1.0 + r)
    route = jnp.where(lane == 0, i1 - N_GROUPS,
                      jnp.where(lane == 1, i2 - N_GROUPS,
                                jnp.where(lane == 2, w1, jnp.where(lane == 3, w2, 0.0))))
    route_ref[...] = route


def _outproj_call(o_diff, o_nsa, w_out, x2, g, b, wr, tm):
    n, d = x2.shape
    half = o_diff.shape[1]
    return pl.pallas_call(
        _outproj_kernel,
        out_shape=(jax.ShapeDtypeStruct((n * ROW_PITCH, LANES), F32), jax.ShapeDtypeStruct((n, LANES), F32)),
        grid=(n // tm,),
        in_specs=[
            pl.BlockSpec((tm, half), lambda i: (i, 0)),
            pl.BlockSpec((tm, half), lambda i: (i, 0)),
            pl.BlockSpec((2 * half, d), lambda i: (0, 0)),
            pl.BlockSpec((tm, d), lambda i: (i, 0)),
            pl.BlockSpec((1, d), lambda i: (0, 0)),
            pl.BlockSpec((1, d), lambda i: (0, 0)),
            pl.BlockSpec((d, LANES), lambda i: (0, 0)),
        ],
        out_specs=(pl.BlockSpec((tm * ROW_PITCH, LANES), lambda i: (i, 0)),
                   pl.BlockSpec((tm, LANES), lambda i: (i, 0))),
        compiler_params=_params(("parallel",)),
        name="outproj_ln1_router",
    )(o_diff, o_nsa, w_out, x2, g, b, wr)


def _token_copy(src_hbm, dst, sem, src_tok, dst_tok):
    return pltpu.make_async_copy(src_hbm.at[pl.ds(src_tok * ROW_PITCH, N_CHUNKS)],
                                 dst.at[pl.ds(dst_tok * ROW_PITCH, N_CHUNKS)], sem)


def _wait_tokens(src_hbm, dst, sem, count):
    pltpu.make_async_copy(src_hbm.at[pl.ds(0, count * N_CHUNKS)], dst.at[pl.ds(0, count * N_CHUNKS)], sem).wait()


def _moe_kernel(te_ref, nu_ref, rows_ref, rows_next_ref, x_hbm, wg_ref, wu_ref, wd_ref, o_ref,
                xbuf, sem, wg_sc, wu_sc, wd_sc, *, tm):
    i = pl.program_id(0)
    n_used = nu_ref[0]
    slot = lax.rem(i, 2)

    @pl.when(i == 0)
    def _():
        def body(r, carry):
            _token_copy(x_hbm, xbuf.at[0], sem.at[0], rows_ref[0, 0, r], r).start()
            return carry

        lax.fori_loop(0, tm, body, 0, unroll=8)

    prev = te_ref[jnp.maximum(i - 1, 0)]

    @pl.when((i == 0) | (te_ref[i] != prev))
    def _():
        wg_sc[...] = wg_ref[...].astype(BF16)
        wu_sc[...] = wu_ref[...].astype(BF16)
        wd_sc[...] = wd_ref[...].astype(BF16)

    @pl.when(i < n_used)
    def _():
        def prefetch(part, parts=4):
            for r in range(part * tm // parts, (part + 1) * tm // parts):
                _token_copy(x_hbm, xbuf.at[1 - slot], sem.at[1 - slot], rows_next_ref[0, 0, r], r).start(
                    priority=r % 2)

        _wait_tokens(x_hbm, xbuf.at[slot], sem.at[slot], tm)
        xb = _load_chunk_rows(xbuf.at[slot], tm).astype(BF16)
        prefetch(0)
        gate = _dot(xb, wg_sc[...])
        prefetch(1)
        up = _dot(xb, wu_sc[...])
        prefetch(2)
        h = (gate * jax.nn.sigmoid(gate) * up).astype(BF16)
        y = _dot(h, wd_sc[...])
        prefetch(3)
        _store_chunk_rows(o_ref, y)

    @pl.when(i == n_used)
    def _():
        _wait_tokens(x_hbm, xbuf.at[slot], sem.at[slot], tm)

    @pl.when(i >= n_used)
    def _():
        o_ref[...] = jnp.zeros(o_ref.shape, o_ref.dtype)


def _moe_call(tile_expert, n_used, rows, x1c, w_gate, w_up, w_down, tm):
    n_tiles = rows.shape[0]
    d, f = w_gate.shape[1], w_gate.shape[2]
    grid_spec = pltpu.PrefetchScalarGridSpec(
        num_scalar_prefetch=2,
        grid=(n_tiles,),
        in_specs=[
            pl.BlockSpec((1, 1, tm), lambda i, te, nu: (i, 0, 0), memory_space=pltpu.SMEM),
            pl.BlockSpec((1, 1, tm), lambda i, te, nu: (jnp.minimum(i + 1, n_tiles - 1), 0, 0),
                         memory_space=pltpu.SMEM),
            pl.BlockSpec(memory_space=pl.ANY),
            pl.BlockSpec((None, d, f), lambda i, te, nu: (te[i], 0, 0)),
            pl.BlockSpec((None, d, f), lambda i, te, nu: (te[i], 0, 0)),
            pl.BlockSpec((None, f, d), lambda i, te, nu: (te[i], 0, 0)),
        ],
        out_specs=pl.BlockSpec((tm * ROW_PITCH, LANES), lambda i, te, nu: (i, 0)),
        scratch_shapes=[
            pltpu.VMEM((2, tm * ROW_PITCH, LANES), F32),
            pltpu.SemaphoreType.DMA((2,)),
            pltpu.VMEM((d, f), BF16), pltpu.VMEM((d, f), BF16), pltpu.VMEM((f, d), BF16),
        ],
    )
    return pl.pallas_call(
        functools.partial(_moe_kernel, tm=tm),
        out_shape=jax.ShapeDtypeStruct((n_tiles * tm * ROW_PITCH, LANES), F32),
        grid_spec=grid_spec,
        compiler_params=_params(("arbitrary",)),
        name="moe_experts",
    )(tile_expert, n_used, rows, rows, x1c, w_gate, w_up, w_down)


def _combine_kernel(slots_ref, slots_next_ref, y_hbm, x1_ref, route_ref, g_ref, b_ref, o_ref, buf, sem, *, tm):
    i = pl.program_id(0)
    slot = lax.rem(i, 2)

    @pl.when(i == 0)
    def _():
        def body(r, carry):
            for k in range(2):
                _token_copy(y_hbm, buf.at[0, k], sem.at[0], slots_ref[0, 0, 2 * r + k], r).start()
            return carry

        lax.fori_loop(0, tm, body, 0, unroll=8)

    def tile(prefetch):
        if prefetch:
            for r in range(tm):
                for k in range(2):
                    _token_copy(y_hbm, buf.at[1 - slot, k], sem.at[1 - slot],
                                slots_next_ref[0, 0, 2 * r + k], r).start(priority=k)
        for k in range(2):
            _wait_tokens(y_hbm, buf.at[slot, k], sem.at[slot], tm)
        route = route_ref[...]
        y = (route[:, 2:3] * _load_chunk_rows(buf.at[slot, 0], tm)
             + route[:, 3:4] * _load_chunk_rows(buf.at[slot, 1], tm))
        x1 = _load_chunk_rows(x1_ref, tm)
        o_ref[...] = _layernorm(DN_ALPHA * x1 + y, g_ref[...], b_ref[...])

    last = pl.num_programs(0) - 1
    pl.when(i < last)(lambda: tile(True))
    pl.when(i == last)(lambda: tile(False))


def _combine_call(slots, yc, x1c, route, g, b, tm):
    n = route.shape[0]
    d = D_MODEL
    nt = n // tm
    return pl.pallas_call(
        functools.partial(_combine_kernel, tm=tm),
        out_shape=jax.ShapeDtypeStruct((n, d), F32),
        grid=(nt,),
        in_specs=[
            pl.BlockSpec((1, 1, 2 * tm), lambda i: (i, 0, 0), memory_space=pltpu.SMEM),
            pl.BlockSpec((1, 1, 2 * tm), lambda i: (jnp.minimum(i + 1, nt - 1), 0, 0), memory_space=pltpu.SMEM),
            pl.BlockSpec(memory_space=pl.ANY),
            pl.BlockSpec((tm * ROW_PITCH, LANES), lambda i: (i, 0)),
            pl.BlockSpec((tm, LANES), lambda i: (i, 0)),
            pl.BlockSpec((1, d), lambda i: (0, 0)),
            pl.BlockSpec((1, d), lambda i: (0, 0)),
        ],
        out_specs=pl.BlockSpec((tm, d), lambda i: (i, 0)),
        scratch_shapes=[pltpu.VMEM((2, 2, tm * ROW_PITCH, LANES), F32), pltpu.SemaphoreType.DMA((2,))],
        compiler_params=_params(("arbitrary",)),
        name="moe_combine_ln2",
    )(slots, slots, yc, x1c, route, g, b)


def _moe_plan(e_idx, tm, n_tiles):
    n = e_idx.shape[0]
    flat = e_idx.reshape(-1)
    onehot = (flat[:, None] == jnp.arange(N_EXPERTS, dtype=jnp.int32)[None, :]).astype(jnp.int32)
    csum = jnp.cumsum(onehot, axis=0)
    rank = jnp.take_along_axis(csum, flat[:, None], axis=1)[:, 0] - 1
    counts = csum[-1]
    ptiles = (counts + tm - 1) // tm
    tile_end = jnp.cumsum(ptiles)
    slot = (tile_end - ptiles)[flat] * tm + rank
    rows = jnp.zeros((n_tiles * tm,), jnp.int32).at[slot].set(jnp.arange(2 * n, dtype=jnp.int32) // 2)
    tile_ids = jnp.arange(n_tiles, dtype=jnp.int32)
    tile_expert = jnp.minimum(jnp.sum((tile_end[None, :] <= tile_ids[:, None]).astype(jnp.int32), axis=1),
                              N_EXPERTS - 1)
    return slot.astype(jnp.int32), rows, tile_expert, tile_end[-1:].astype(jnp.int32)


def _rope_tables(seq):
    half = ROT_DIM // 2
    inv_freq = ROPE_THETA ** (-jnp.arange(0, ROT_DIM, 2, dtype=F32) / ROT_DIM)
    ang = jnp.arange(seq, dtype=F32)[:, None] * inv_freq[None, :]
    cos, sin = jnp.cos(ang), jnp.sin(ang)
    pad1 = jnp.ones((seq, HEAD_DIM - ROT_DIM), F32)
    pad0 = jnp.zeros((seq, HEAD_DIM - ROT_DIM), F32)
    return (jnp.concatenate([cos, cos, pad1], axis=1), jnp.concatenate([-sin, sin, pad0], axis=1))


def _layer(x, w_in, diff_lambda, diff_subln_g, cmp_pos, cmp_w1, cmp_b1, cmp_w2, cmp_b2, w_out,
           ln1_g, ln1_b, router_group, router_expert, w_gate, w_up, w_down, ln2_g, ln2_b, lambda_init):
    batch, seq, d = x.shape
    n = batch * seq
    x2 = x.reshape(n, d)

    seg = lambda a, b: w_in[:, a:b]
    pad = lambda w: jnp.pad(w, ((0, 0), (0, LANES - w.shape[1])))
    gate_w = seg(5632, 5656)
    w_aug = jnp.concatenate(
        [seg(3072, 4096), seg(0, 1024), seg(1024, 2048), seg(2048, 3072), seg(4096, 5632),
         pad(gate_w[:, :12]), pad(gate_w[:, 12:])], axis=1).astype(BF16)
    flags = jnp.zeros((N_UNITS // 2,), jnp.int32).at[jnp.array(ROPE_BLOCKS)].set(1)
    cos_t, sin_t = _rope_tables(seq)
    proj = _proj_call(x2, w_aug, flags, cos_t, sin_t, seq, tm=min(2048, seq))

    nt = seq // CMP_STRIDE
    half_feat = CMP_STRIDE * HEAD_DIM
    r = proj[:, U_KC * LANES:U_KS * LANES].reshape(batch, nt, CMP_STRIDE, 4, HEAD_DIM)
    r = jnp.transpose(r, (0, 3, 1, 2, 4)).reshape(batch, 4, nt, half_feat)
    w1cat = jnp.concatenate([cmp_w1[:, :half_feat], cmp_w1[:, half_feat:]], axis=2).astype(BF16)
    kvc = _compress_call(r, w1cat, cmp_pos.reshape(2, 2, half_feat), cmp_b1[:, None, :],
                         cmp_w2.astype(BF16), cmp_b2[:, None, :])

    o_diff = _diff_call(proj, diff_lambda, diff_subln_g[None, :], batch, seq, min(512, seq), lambda_init)

    ns = seq // SLC_BLOCK
    ci = np.arange(nt)[None, :] * CMP_STRIDE
    sj = np.arange(ns)[:, None] * SLC_BLOCK
    ovt = jnp.asarray((ci < sj + SLC_BLOCK) & (ci + CMP_BLOCK > sj) & (np.arange(nt)[None, :] < nt - 1), dtype=BF16)
    o_nsa = _nsa_call(proj, cos_t, sin_t, kvc, ovt, batch, seq, tq=2 * LANES, tk=min(512, seq))

    wr = jnp.pad(jnp.concatenate([router_group, router_expert], axis=1),
                 ((0, 0), (0, LANES - N_GROUPS - N_EXPERTS)))
    x1, route = _outproj_call(o_diff, o_nsa, w_out.astype(BF16), x2, ln1_g[None, :], ln1_b[None, :], wr, tm=512)

    tm = 256
    n_tiles = 2 * n // tm + N_EXPERTS + 1
    e_idx = route[:, 0:2].astype(jnp.int32)
    slot, rows, tile_expert, n_used = _moe_plan(e_idx, tm, n_tiles)
    ys = _moe_call(tile_expert, n_used, rows.reshape(n_tiles, 1, tm), x1, w_gate, w_up, w_down, tm)
    out = _combine_call(slot.reshape(n // tm, 1, 2 * tm), ys, x1, route, ln2_g[None, :], ln2_b[None, :], tm)
    return out.reshape(batch, seq, d)


def kernel(x, w_in, diff_lambda, diff_subln_g, cmp_pos, cmp_w1, cmp_b1, cmp_w2, cmp_b2, w_out, ln1_g, ln1_b,
           router_group, router_expert, expert_w_gate, expert_w_up, expert_w_down, ln2_g, ln2_b):
    for l in range(DEPTH):
        lambda_init = 0.8 - 0.6 * math.exp(-0.3 * l)
        x = _layer(x, w_in[l], diff_lambda[l], diff_subln_g[l], cmp_pos[l], cmp_w1[l], cmp_b1[l], cmp_w2[l],
                   cmp_b2[l], w_out[l], ln1_g[l], ln1_b[l], router_group[l], router_expert[l],
                   expert_w_gate[l], expert_w_up[l], expert_w_down[l], ln2_g[l], ln2_b[l], lambda_init)
    return x
```

```python
import functools
import math

import numpy as np
import jax
import jax.numpy as jnp
from jax import lax
from jax.experimental import pallas as pl
from jax.experimental.pallas import tpu as pltpu

F32 = jnp.float32
BF16 = jnp.bfloat16

D_MODEL = 2048
HEAD_DIM = 128
ROT_DIM = HEAD_DIM // 4
ROPE_THETA = 500000.0
NEG_INF = -1e30
BIG = 1e30
EPS = 1e-5

DIFF_HEADS = 4
DIFF_VDIM = 2 * HEAD_DIM

NSA_HEADS = 8
NSA_KV_GROUPS = 2
NSA_HPG = NSA_HEADS // NSA_KV_GROUPS
CMP_BLOCK = 32
CMP_STRIDE = 16
CMP_HIDDEN = 256
SLC_BLOCK = 64
SLC_TOPK = 16
WINDOW = 512

N_GROUPS = 4
EXPERTS_PER_GROUP = 8
N_EXPERTS = N_GROUPS * EXPERTS_PER_GROUP
EXPERT_HIDDEN = 512

DEPTH = 1
DN_ALPHA = (2.0 * DEPTH) ** 0.25

LANES = 128
VMEM_LIMIT = 56 * 1024 * 1024

U_NQ, U_DQ, U_DK, U_DV = 0, 8, 16, 24
U_KC, U_VC, U_KS, U_VS, U_KW, U_VW, U_GATE = 32, 34, 36, 38, 40, 42, 44
N_UNITS = 46
ROPE_BLOCKS = tuple(range(U_DQ // 2, U_DV // 2)) + (U_KS // 2, U_KW // 2)


def _dot(a, b):
    return jnp.dot(a, b, preferred_element_type=F32)


def _dot_nt(a, b):
    return lax.dot_general(a, b, (((1,), (1,)), ((), ())), preferred_element_type=F32)


def _params(sem, vmem=VMEM_LIMIT):
    return pltpu.CompilerParams(dimension_semantics=sem, vmem_limit_bytes=vmem)


PROJ_ROW_CHUNK = 512


def _rope(a, c, s):
    lane = lax.broadcasted_iota(jnp.int32, a.shape, 1)
    half = ROT_DIM // 2
    partner = jnp.where(lane < half, pltpu.roll(a, LANES - half, 1), pltpu.roll(a, half, 1))
    return a * c + partner * s


def _proj_kernel(flags_ref, x_ref, w_ref, cos_ref, sin_ref, o_ref, xb_sc):
    j = pl.program_id(1)

    @pl.when(j == 0)
    def _():
        xb_sc[...] = x_ref[...].astype(BF16)

    def block(rotary):
        for r0 in range(0, xb_sc.shape[0], PROJ_ROW_CHUNK):
            rs = slice(r0, r0 + PROJ_ROW_CHUNK)
            acc = _dot(xb_sc[rs, :], w_ref[...])
            if rotary:
                for hh in range(2):
                    cs = slice(hh * LANES, (hh + 1) * LANES)
                    o_ref[rs, cs] = _rope(acc[:, cs], cos_ref[rs, :], sin_ref[rs, :]).astype(o_ref.dtype)
            else:
                o_ref[rs, :] = acc.astype(o_ref.dtype)

    pl.when(flags_ref[j] == 0)(lambda: block(False))
    pl.when(flags_ref[j] != 0)(lambda: block(True))


def _proj_call(xb, w_aug, flags, cos_t, sin_t, seq, tm):
    n, d = xb.shape
    nj = w_aug.shape[1] // 256
    tpb = seq // tm
    grid_spec = pltpu.PrefetchScalarGridSpec(
        num_scalar_prefetch=1,
        grid=(n // tm, nj),
        in_specs=[
            pl.BlockSpec((tm, d), lambda i, j, f: (i, 0)),
            pl.BlockSpec((d, 256), lambda i, j, f: (0, j)),
            pl.BlockSpec((tm, LANES), lambda i, j, f: (i % tpb, 0)),
            pl.BlockSpec((tm, LANES), lambda i, j, f: (i % tpb, 0)),
        ],
        out_specs=pl.BlockSpec((tm, 256), lambda i, j, f: (i, j)),
        scratch_shapes=[pltpu.VMEM((tm, d), BF16)],
    )
    return pl.pallas_call(
        _proj_kernel,
        out_shape=jax.ShapeDtypeStruct((n, w_aug.shape[1]), BF16),
        grid_spec=grid_spec,
        compiler_params=_params(("parallel", "arbitrary")),
        name="proj",
    )(flags, xb, w_aug, cos_t, sin_t)


def _compress_kernel(r_ref, w1_ref, pos_ref, b1_ref, w2_ref, b2_ref, o_ref):
    nt = r_ref.shape[0]
    ab = _dot(r_ref[...], w1_ref[...])
    pos = pos_ref[...]
    pa = jnp.broadcast_to(pos[0:1], (8, pos.shape[1])).astype(BF16)
    pb = jnp.broadcast_to(pos[1:2], (8, pos.shape[1])).astype(BF16)
    const = _dot(pa, w1_ref[:, :CMP_HIDDEN])[0:1] + _dot(pb, w1_ref[:, CMP_HIDDEN:])[0:1]
    h = ab[:, :CMP_HIDDEN] + pltpu.roll(ab[:, CMP_HIDDEN:], nt - 1, 0) + const + b1_ref[...]
    h = jax.nn.gelu(h)
    o_ref[...] = (_dot(h.astype(BF16), w2_ref[...]) + b2_ref[...]).astype(o_ref.dtype)


def _compress_call(r, w1cat, pos2, b1, w2, b2):
    b, four, nt, k = r.shape
    return pl.pallas_call(
        _compress_kernel,
        out_shape=jax.ShapeDtypeStruct((b, four, nt, HEAD_DIM), BF16),
        grid=(b, four),
        in_specs=[
            pl.BlockSpec((None, None, nt, k), lambda i, c: (i, c, 0, 0)),
            pl.BlockSpec((None, k, 2 * CMP_HIDDEN), lambda i, c: (c // 2, 0, 0)),
            pl.BlockSpec((None, 2, k), lambda i, c: (c // 2, 0, 0)),
            pl.BlockSpec((None, 1, CMP_HIDDEN), lambda i, c: (c // 2, 0, 0)),
            pl.BlockSpec((None, CMP_HIDDEN, HEAD_DIM), lambda i, c: (c // 2, 0, 0)),
            pl.BlockSpec((None, 1, HEAD_DIM), lambda i, c: (c // 2, 0, 0)),
        ],
        out_specs=pl.BlockSpec((None, None, nt, HEAD_DIM), lambda i, c: (i, c, 0, 0)),
        compiler_params=_params(("parallel", "parallel")),
        name="compress",
    )(r, w1cat, pos2, b1, w2, b2)


EXP2_SCALE = HEAD_DIM ** -0.5 * math.log2(math.e)
ONES_ROWS = 16


def _transpose_into(src_ref, dst_ref):
    def body(c, carry):
        off = pl.multiple_of(c * LANES, LANES)
        dst_ref[:, pl.ds(off, LANES)] = src_ref[pl.ds(off, LANES), :].astype(F32).T.astype(dst_ref.dtype)
        return carry

    lax.fori_loop(0, src_ref.shape[0] // LANES, body, 0)


def _diff_kernel(dl_ref, q_ref, k_ref, v_ref, g_ref, o_ref, vt_sc, acc_sc, sa_sc, sb_sc, *, tq, lambda_init):
    qi = pl.program_id(2)

    @pl.when(qi == 0)
    def _():
        for c in range(2):
            _transpose_into(v_ref.at[:, c * LANES:(c + 1) * LANES], vt_sc.at[c * LANES:(c + 1) * LANES, :])

    acc_sc[...] = jnp.zeros(acc_sc.shape, F32)
    q = q_ref[...]
    qpos = qi * tq + lax.broadcasted_iota(jnp.int32, (1, tq), 1)

    def scores(j, dst):
        kt = k_ref[pl.ds(pl.multiple_of(j * tq, tq), tq), :]
        for c in range(2):
            dst[c] = _dot_nt(kt[:, c * HEAD_DIM:(c + 1) * HEAD_DIM], q[:, c * HEAD_DIM:(c + 1) * HEAD_DIM])

    def absorb(src, j, masked, carry):
        off = pl.multiple_of(j * tq, tq)
        vt = vt_sc[:, pl.ds(off, tq)]
        out = []
        for c in range(2):
            m_old, l_old = carry[2 * c], carry[2 * c + 1]
            s = src[c]
            if masked:
                s = jnp.where(lax.broadcasted_iota(jnp.int32, s.shape, 0) <= qpos - off, s, NEG_INF)
            m_new = jnp.maximum(m_old, jnp.max(s, axis=0, keepdims=True))
            p = jnp.exp2((s - m_new) * EXP2_SCALE)
            alpha = jnp.exp2((m_old - m_new) * EXP2_SCALE)
            l_new = alpha * l_old + jnp.sum(p, axis=0, keepdims=True)
            acc_sc[c] = alpha * acc_sc[c] + _dot(vt, p.astype(BF16))
            out += [m_new, l_new]
        return tuple(out)

    def pair(p, carry):
        j = 2 * p
        scores(j + 1, sb_sc)
        carry = absorb(sa_sc, j, False, carry)
        scores(j + 2, sa_sc)
        return absorb(sb_sc, j + 1, False, carry)

    n_pairs = (qi + 2) // 2
    init = (jnp.full((1, tq), NEG_INF, F32), jnp.zeros((1, tq), F32)) * 2
    scores(0, sa_sc)
    carry = lax.fori_loop(0, n_pairs - 1, pair, init)
    j_tail = 2 * (n_pairs - 1)
    scores(j_tail + 1, sb_sc)
    carry = absorb(sa_sc, j_tail, True, carry)
    _, l0, _, l1 = absorb(sb_sc, j_tail + 1, True, carry)

    dl = dl_ref[...]
    lam = (jnp.exp(jnp.sum(dl[0:1] * dl[1:2], axis=1, keepdims=True))
           - jnp.exp(jnp.sum(dl[2:3] * dl[3:4], axis=1, keepdims=True)) + lambda_init)
    o = acc_sc[0] * (1.0 / l0) - lam * (acc_sc[1] * (1.0 / l1))
    o = o * (lax.rsqrt(jnp.mean(o * o, axis=0, keepdims=True) + EPS) * (1.0 - lambda_init))
    for c in range(DIFF_VDIM // LANES):
        for r in range(tq // LANES):
            blk = o[c * LANES:(c + 1) * LANES, r * LANES:(r + 1) * LANES].T
            o_ref[r * LANES:(r + 1) * LANES, c * LANES:(c + 1) * LANES] = (
                blk * g_ref[:, c * LANES:(c + 1) * LANES]).astype(o_ref.dtype)


def _diff_call(proj, dl, g, batch, seq, tq, lambda_init):
    nq = seq // tq
    assert nq % 2 == 0, "key tiles are processed in pairs"
    kern = functools.partial(_diff_kernel, tq=tq, lambda_init=lambda_init)
    return pl.pallas_call(
        kern,
        out_shape=jax.ShapeDtypeStruct((batch * seq, DIFF_HEADS * DIFF_VDIM), BF16),
        grid=(batch, DIFF_HEADS, nq),
        in_specs=[
            pl.BlockSpec((4, HEAD_DIM), lambda b, h, i: (0, 0)),
            pl.BlockSpec((tq, 256), lambda b, h, i: (b * nq + i, U_DQ // 2 + h)),
            pl.BlockSpec((seq, 256), lambda b, h, i: (b, U_DK // 2 + h)),
            pl.BlockSpec((seq, 256), lambda b, h, i: (b, U_DV // 2 + h)),
            pl.BlockSpec((1, DIFF_VDIM), lambda b, h, i: (0, 0)),
        ],
        out_specs=pl.BlockSpec((tq, DIFF_VDIM), lambda b, h, i: (b * nq + i, h)),
        scratch_shapes=[
            pltpu.VMEM((DIFF_VDIM, seq), BF16),
            pltpu.VMEM((2, DIFF_VDIM, tq), F32),
            pltpu.VMEM((2, tq, tq), F32),
            pltpu.VMEM((2, tq, tq), F32),
        ],
        compiler_params=_params(("arbitrary", "arbitrary", "arbitrary")),
        name="diff_attn",
    )(dl, proj, proj, proj, g)


def _stack_heads(x):
    return jnp.concatenate([x[:, h * HEAD_DIM:(h + 1) * HEAD_DIM] for h in range(NSA_HPG)], axis=0)


def _nsa_kernel(q_ref, cos_ref, sin_ref, gate_ref, kc_ref, vc_ref, ks_ref, vs_ref, kw_ref, vw_ref, ovt_ref,
                o_ref, vct_sc, vst_sc, vwt_sc, bias_sc, acc_sc, sa_sc, sb_sc, *, tq, tk, seq, top_k):
    qi = pl.program_id(2)
    q0 = qi * tq
    rows = NSA_HPG * tq
    nt = kc_ref.shape[0]
    ns = ovt_ref.shape[0]
    nb = tk // SLC_BLOCK

    @pl.when(qi == 0)
    def _():
        _transpose_into(vc_ref, vct_sc)
        for v_ref, vt_sc in ((vs_ref, vst_sc), (vw_ref, vwt_sc)):
            _transpose_into(v_ref, vt_sc.at[:HEAD_DIM, :])
            vt_sc[HEAD_DIM:, :] = jnp.ones((ONES_ROWS, seq), vt_sc.dtype)

    q = q_ref[...]
    qs = _stack_heads(q)
    qr = jnp.concatenate(
        [_rope(q[:, h * HEAD_DIM:(h + 1) * HEAD_DIM].astype(F32), cos_ref[...], sin_ref[...]).astype(BF16)
         for h in range(NSA_HPG)], axis=0)
    qpos = q0 + (lax.broadcasted_iota(jnp.int32, (1, rows), 1) & (tq - 1))

    def sel_scores(j, dst):
        dst[...] = _dot_nt(ks_ref[pl.ds(pl.multiple_of(j * tk, tk), tk), :], qr)

    sel_scores(0, sa_sc)

    slab = min(WINDOW + tq, seq)
    start = pl.multiple_of(jnp.maximum(q0 - WINDOW, 0), tq)
    s_w = _dot_nt(kw_ref[pl.ds(start, slab), :], qr)
    krow = lax.broadcasted_iota(jnp.int32, (slab, rows), 0)
    newest = qpos - start
    s_w = jnp.where((krow <= newest) & (krow > newest - WINDOW), s_w, NEG_INF)
    p_w = jnp.exp2((s_w - jnp.max(s_w, axis=0, keepdims=True)) * EXP2_SCALE)
    o_w = _dot(vwt_sc[:, pl.ds(start, slab)], p_w.astype(BF16))
    o_w = o_w[:HEAD_DIM] * (1.0 / o_w[HEAD_DIM:HEAD_DIM + 1])

    s_c = _dot_nt(kc_ref[...], qs)
    last_c = jnp.right_shift(qpos - (CMP_BLOCK - 1), int(math.log2(CMP_STRIDE)))
    valid_c = lax.broadcasted_iota(jnp.int32, (nt, rows), 0) <= last_c
    s_c = jnp.where(valid_c, s_c, NEG_INF)
    m_c = jnp.max(s_c, axis=0, keepdims=True)
    e_c = jnp.where(valid_c, jnp.exp2((s_c - m_c) * EXP2_SCALE), 0.0)
    l_c = jnp.sum(e_c, axis=0, keepdims=True)
    p_c = e_c * (1.0 / jnp.where(l_c > 0.0, l_c, 1.0))
    o_c = _dot(vct_sc[...], p_c.astype(BF16))

    p_sum = p_c[:, 0:tq]
    for h in range(1, NSA_HPG):
        p_sum = p_sum + p_c[:, h * tq:(h + 1) * tq]
    p_hi = p_sum.astype(BF16)
    p_lo = (p_sum - p_hi.astype(F32)).astype(BF16)
    imp = _dot(ovt_ref[...], p_hi) + _dot(ovt_ref[...], p_lo)

    blk = lax.broadcasted_iota(jnp.int32, (ns, tq), 0)
    qpos_l = q0 + lax.broadcasted_iota(jnp.int32, (ns, tq), 1)
    cur = lax.shift_right_logical(qpos_l, int(math.log2(SLC_BLOCK)))
    valid_s = blk <= cur
    forced = (blk == 0) | (blk == cur) | (blk == cur - 1)
    work = jnp.where(forced, -jnp.inf, jnp.where(valid_s, imp, NEG_INF))
    sel = jnp.where(forced, 1.0, 0.0)
    blk_f = blk.astype(F32)
    for _ in range(top_k - 3):
        mx = jnp.max(work, axis=0, keepdims=True)
        idx = jnp.min(jnp.where(work == mx, blk_f, float(ns)), axis=0, keepdims=True)
        pick = blk_f == idx
        sel = jnp.where(pick, 1.0, sel)
        work = jnp.where(pick, -jnp.inf, work)
    bias_sc[...] = (sel - 1.0) * BIG

    acc_sc[...] = jnp.zeros(acc_sc.shape, F32)

    def sel_absorb(src, j, causal, m_old):
        off = pl.multiple_of(j * tk, tk)
        s = src[...]
        if causal:
            s = jnp.where(lax.broadcasted_iota(jnp.int32, (tk, rows), 0) <= qpos - off, s, NEG_INF)
        bias = bias_sc[pl.ds(pl.multiple_of(j * nb, nb), nb), :]
        bias = jnp.concatenate([bias] * NSA_HPG, axis=1)
        s3 = s.reshape(nb, SLC_BLOCK, rows) + bias[:, None, :]
        m_new = jnp.maximum(m_old, jnp.max(jnp.max(s3, axis=0), axis=0, keepdims=True))
        p3 = jnp.exp2((s3 - m_new) * EXP2_SCALE)
        alpha = jnp.exp2((m_old - m_new) * EXP2_SCALE)
        p = p3.reshape(tk, rows).astype(BF16)
        acc_sc[...] = alpha * acc_sc[...] + _dot(vst_sc[:, pl.ds(off, tk)], p)
        return m_new

    def sel_pair(p, carry):
        j = 2 * p
        sel_scores(j + 1, sb_sc)
        carry = sel_absorb(sa_sc, j, False, carry)
        sel_scores(j + 2, sa_sc)
        return sel_absorb(sb_sc, j + 1, False, carry)

    n_pairs = (q0 // tk + 2) // 2
    carry = lax.fori_loop(0, n_pairs - 1, sel_pair, jnp.full((1, rows), NEG_INF, F32))
    j_tail = 2 * (n_pairs - 1)
    sel_scores(j_tail + 1, sb_sc)
    carry = sel_absorb(sa_sc, j_tail, True, carry)
    sel_absorb(sb_sc, j_tail + 1, True, carry)
    o_s = acc_sc[:HEAD_DIM, :] * (1.0 / acc_sc[HEAD_DIM:HEAD_DIM + 1, :])

    gates = jax.nn.sigmoid(gate_ref[...].astype(F32))
    gates = jnp.concatenate([gates[r * LANES:(r + 1) * LANES].T for r in range(tq // LANES)],
                            axis=1)
    for h in range(NSA_HPG):
        sl = slice(h * tq, (h + 1) * tq)
        o = (gates[3 * h:3 * h + 1] * o_c[:, sl] + gates[3 * h + 1:3 * h + 2] * o_s[:, sl]
             + gates[3 * h + 2:3 * h + 3] * o_w[:, sl])
        for r in range(tq // LANES):
            o_ref[r * LANES:(r + 1) * LANES, h * HEAD_DIM:(h + 1) * HEAD_DIM] = (
                o[:, r * LANES:(r + 1) * LANES].T.astype(o_ref.dtype))


def _nsa_call(proj, cos_t, sin_t, kvc, ovt, batch, seq, tq, tk):
    nq = seq // tq
    assert (seq // tk) % 2 == 0, "key tiles are processed in pairs"
    nt = kvc.shape[2]
    ns = ovt.shape[0]
    width = NSA_HPG * HEAD_DIM
    kern = functools.partial(_nsa_kernel, tq=tq, tk=tk, seq=seq, top_k=min(SLC_TOPK, ns))

    def col(unit):
        return pl.BlockSpec((seq, HEAD_DIM), lambda b, g, i: (b, unit + g))

    return pl.pallas_call(
        kern,
        out_shape=jax.ShapeDtypeStruct((batch * seq, NSA_HEADS * HEAD_DIM), BF16),
        grid=(batch, NSA_KV_GROUPS, nq),
        in_specs=[
            pl.BlockSpec((tq, width), lambda b, g, i: (b * nq + i, U_NQ // 4 + g)),
            pl.BlockSpec((tq, LANES), lambda b, g, i: (i, 0)),
            pl.BlockSpec((tq, LANES), lambda b, g, i: (i, 0)),
            pl.BlockSpec((tq, LANES), lambda b, g, i: (b * nq + i, U_GATE + g)),
            pl.BlockSpec((None, None, nt, HEAD_DIM), lambda b, g, i: (b, g, 0, 0)),
            pl.BlockSpec((None, None, nt, HEAD_DIM), lambda b, g, i: (b, 2 + g, 0, 0)),
            col(U_KS), col(U_VS), col(U_KW), col(U_VW),
            pl.BlockSpec((ns, nt), lambda b, g, i: (0, 0)),
        ],
        out_specs=pl.BlockSpec((tq, width), lambda b, g, i: (b * nq + i, g)),
        scratch_shapes=[
            pltpu.VMEM((HEAD_DIM, nt), BF16),
            pltpu.VMEM((HEAD_DIM + ONES_ROWS, seq), BF16),
            pltpu.VMEM((HEAD_DIM + ONES_ROWS, seq), BF16),
            pltpu.VMEM((ns, tq), F32),
            pltpu.VMEM((HEAD_DIM + ONES_ROWS, NSA_HPG * tq), F32),
            pltpu.VMEM((tk, NSA_HPG * tq), F32),
            pltpu.VMEM((tk, NSA_HPG * tq), F32),
        ],
        compiler_params=_params(("arbitrary", "arbitrary", "arbitrary")),
        name="nsa_attn",
    )(proj, cos_t, sin_t, proj, kvc, kvc, proj, proj, proj, proj, ovt)


def _layernorm(y, g, b):
    mu = jnp.mean(y, axis=1, keepdims=True)
    yc = y - mu
    var = jnp.mean(yc * yc, axis=1, keepdims=True)
    return yc * lax.rsqrt(var + EPS) * g + b


N_CHUNKS = D_MODEL // LANES
ROW_PITCH = N_CHUNKS + 1


def _store_chunk_rows(ref, val):
    tm = val.shape[0]
    for k in range(N_CHUNKS):
        ref[pl.ds(k, tm, stride=ROW_PITCH), :] = val[:, k * LANES:(k + 1) * LANES]
    for k in range(N_CHUNKS, ROW_PITCH):
        ref[pl.ds(k, tm, stride=ROW_PITCH), :] = jnp.zeros((tm, LANES), ref.dtype)


def _load_chunk_rows(ref, tm):
    return jnp.concatenate([ref[pl.ds(k, tm, stride=ROW_PITCH), :] for k in range(N_CHUNKS)], axis=1)


OUT_ROW_CHUNK = 256


def _outproj_kernel(od_ref, on_ref, w_ref, x_ref, g_ref, b_ref, wr_ref, x1_ref, route_ref):
    wr = wr_ref[...]
    w_hi = wr.astype(BF16)
    w_lo = (wr - w_hi.astype(F32)).astype(BF16)
    for r0 in range(0, x_ref.shape[0], OUT_ROW_CHUNK):
        _outproj_rows(od_ref, on_ref, w_ref, x_ref, g_ref, b_ref, w_hi, w_lo, x1_ref, route_ref,
                      slice(r0, r0 + OUT_ROW_CHUNK))


def _outproj_rows(od_ref, on_ref, w_ref, x_ref, g_ref, b_ref, w_hi, w_lo, x1_ref, route_ref, rs):
    half = od_ref.shape[1]
    h = _dot(od_ref[rs, :], w_ref[:half, :]) + _dot(on_ref[rs, :], w_ref[half:, :])
    x1 = _layernorm(DN_ALPHA * x_ref[rs, :] + h, g_ref[...], b_ref[...])
    _store_chunk_rows(x1_ref.at[rs.start * ROW_PITCH:rs.stop * ROW_PITCH, :], x1)

    x_hi = x1.astype(BF16)
    x_lo = (x1 - x_hi.astype(F32)).astype(BF16)
    logits = _dot(x_hi, w_hi) + (_dot(x_lo, w_hi) + _dot(x_hi, w_lo))

    lane = lax.broadcasted_iota(jnp.int32, logits.shape, 1).astype(F32)
    ninf = -jnp.inf
    gl = jnp.where(lane < N_GROUPS, logits, ninf)
    gmax = jnp.max(gl, axis=1, keepdims=True)
    g_w = 1.0 / jnp.sum(jnp.exp(gl - gmax), axis=1, keepdims=True)
    g_sel = jnp.min(jnp.where(gl == gmax, lane, float(LANES)), axis=1, keepdims=True)
    lo = N_GROUPS + EXPERTS_PER_GROUP * g_sel
    el = jnp.where((lane >= lo) & (lane < lo + EXPERTS_PER_GROUP), logits, ninf)
    e1 = jnp.max(el, axis=1, keepdims=True)
    i1 = jnp.min(jnp.where(el == e1, lane, float(LANES)), axis=1, keepdims=True)
    el2 = jnp.where(lane == i1, ninf, el)
    e2 = jnp.max(el2, axis=1, keepdims=True)
    i2 = jnp.min(jnp.where(el2 == e2, lane, float(LANES)), axis=1, keepdims=True)
    r = jnp.exp(e2 - e1)
    w1 = g_w / (1.0 + r)
    w2 = g_w * r / (1.0 + r)
    route = jnp.where(lane == 0, i1 - N_GROUPS,
                      jnp.where(lane == 1, i2 - N_GROUPS,
                                jnp.where(lane == 2, w1, jnp.where(lane == 3, w2, 0.0))))
    route_ref[rs, :] = route


def _outproj_call(o_diff, o_nsa, w_out, x2, g, b, wr, tm):
    n, d = x2.shape
    half = o_diff.shape[1]
    return pl.pallas_call(
        _outproj_kernel,
        out_shape=(jax.ShapeDtypeStruct((n * ROW_PITCH, LANES), F32), jax.ShapeDtypeStruct((n, LANES), F32)),
        grid=(n // tm,),
        in_specs=[
            pl.BlockSpec((tm, half), lambda i: (i, 0)),
            pl.BlockSpec((tm, half), lambda i: (i, 0)),
            pl.BlockSpec((2 * half, d), lambda i: (0, 0)),
            pl.BlockSpec((tm, d), lambda i: (i, 0)),
            pl.BlockSpec((1, d), lambda i: (0, 0)),
            pl.BlockSpec((1, d), lambda i: (0, 0)),
            pl.BlockSpec((d, LANES), lambda i: (0, 0)),
        ],
        out_specs=(pl.BlockSpec((tm * ROW_PITCH, LANES), lambda i: (i, 0)),
                   pl.BlockSpec((tm, LANES), lambda i: (i, 0))),
        compiler_params=_params(("parallel",)),
        name="outproj_ln1_router",
    )(o_diff, o_nsa, w_out, x2, g, b, wr)


def _token_copy(src_hbm, dst, sem, src_tok, dst_tok):
    return pltpu.make_async_copy(src_hbm.at[pl.ds(src_tok * ROW_PITCH, N_CHUNKS)],
                                 dst.at[pl.ds(dst_tok * ROW_PITCH, N_CHUNKS)], sem)


def _wait_tokens(src_hbm, dst, sem, count):
    pltpu.make_async_copy(src_hbm.at[pl.ds(0, count * N_CHUNKS)], dst.at[pl.ds(0, count * N_CHUNKS)], sem).wait()


def _moe_kernel(te_ref, nu_ref, rows_ref, rows_next_ref, x_hbm, wg_ref, wu_ref, wd_ref, o_ref,
                xbuf, sem, wg_sc, wu_sc, wd_sc, *, tm):
    i = pl.program_id(0)
    n_used = nu_ref[0]
    slot = lax.rem(i, 2)

    @pl.when(i == 0)
    def _():
        def body(r, carry):
            _token_copy(x_hbm, xbuf.at[0], sem.at[0], rows_ref[0, 0, r], r).start()
            return carry

        lax.fori_loop(0, tm, body, 0, unroll=8)

    prev = te_ref[jnp.maximum(i - 1, 0)]

    @pl.when((i == 0) | (te_ref[i] != prev))
    def _():
        wg_sc[...] = wg_ref[...].astype(BF16)
        wu_sc[...] = wu_ref[...].astype(BF16)
        wd_sc[...] = wd_ref[...].astype(BF16)

    @pl.when(i < n_used)
    def _():
        def prefetch(part, parts=4):
            for r in range(part * tm // parts, (part + 1) * tm // parts):
                _token_copy(x_hbm, xbuf.at[1 - slot], sem.at[1 - slot], rows_next_ref[0, 0, r], r).start(
                    priority=r % 2)

        _wait_tokens(x_hbm, xbuf.at[slot], sem.at[slot], tm)
        xb = _load_chunk_rows(xbuf.at[slot], tm).astype(BF16)
        prefetch(0)
        gate = _dot(xb, wg_sc[...])
        prefetch(1)
        up = _dot(xb, wu_sc[...])
        prefetch(2)
        h = (gate * jax.nn.sigmoid(gate) * up).astype(BF16)
        y = _dot(h, wd_sc[...])
        prefetch(3)
        _store_chunk_rows(o_ref, y)

    @pl.when(i == n_used)
    def _():
        _wait_tokens(x_hbm, xbuf.at[slot], sem.at[slot], tm)

    @pl.when(i >= n_used)
    def _():
        o_ref[...] = jnp.zeros(o_ref.shape, o_ref.dtype)


def _moe_call(tile_expert, n_used, rows, x1c, w_gate, w_up, w_down, tm):
    n_tiles = rows.shape[0]
    d, f = w_gate.shape[1], w_gate.shape[2]
    grid_spec = pltpu.PrefetchScalarGridSpec(
        num_scalar_prefetch=2,
        grid=(n_tiles,),
        in_specs=[
            pl.BlockSpec((1, 1, tm), lambda i, te, nu: (i, 0, 0), memory_space=pltpu.SMEM),
            pl.BlockSpec((1, 1, tm), lambda i, te, nu: (jnp.minimum(i + 1, n_tiles - 1), 0, 0),
                         memory_space=pltpu.SMEM),
            pl.BlockSpec(memory_space=pl.ANY),
            pl.BlockSpec((None, d, f), lambda i, te, nu: (te[i], 0, 0)),
            pl.BlockSpec((None, d, f), lambda i, te, nu: (te[i], 0, 0)),
            pl.BlockSpec((None, f, d), lambda i, te, nu: (te[i], 0, 0)),
        ],
        out_specs=pl.BlockSpec((tm * ROW_PITCH, LANES), lambda i, te, nu: (i, 0)),
        scratch_shapes=[
            pltpu.VMEM((2, tm * ROW_PITCH, LANES), F32),
            pltpu.SemaphoreType.DMA((2,)),
            pltpu.VMEM((d, f), BF16), pltpu.VMEM((d, f), BF16), pltpu.VMEM((f, d), BF16),
        ],
    )
    return pl.pallas_call(
        functools.partial(_moe_kernel, tm=tm),
        out_shape=jax.ShapeDtypeStruct((n_tiles * tm * ROW_PITCH, LANES), F32),
        grid_spec=grid_spec,
        compiler_params=_params(("arbitrary",)),
        name="moe_experts",
    )(tile_expert, n_used, rows, rows, x1c, w_gate, w_up, w_down)


def _combine_kernel(slots_ref, slots_next_ref, y_hbm, x1_ref, route_ref, g_ref, b_ref, o_ref, buf, sem, *, tm):
    i = pl.program_id(0)
    slot = lax.rem(i, 2)

    @pl.when(i == 0)
    def _():
        def body(r, carry):
            for k in range(2):
                _token_copy(y_hbm, buf.at[0, k], sem.at[0], slots_ref[0, 0, 2 * r + k], r).start()
            return carry

        lax.fori_loop(0, tm, body, 0, unroll=8)

    def tile(prefetch):
        if prefetch:
            for r in range(tm):
                for k in range(2):
                    _token_copy(y_hbm, buf.at[1 - slot, k], sem.at[1 - slot],
                                slots_next_ref[0, 0, 2 * r + k], r).start(priority=k)
        for k in range(2):
            _wait_tokens(y_hbm, buf.at[slot, k], sem.at[slot], tm)
        route = route_ref[...]
        y = (route[:, 2:3] * _load_chunk_rows(buf.at[slot, 0], tm)
             + route[:, 3:4] * _load_chunk_rows(buf.at[slot, 1], tm))
        x1 = _load_chunk_rows(x1_ref, tm)
        o_ref[...] = _layernorm(DN_ALPHA * x1 + y, g_ref[...], b_ref[...])

    last = pl.num_programs(0) - 1
    pl.when(i < last)(lambda: tile(True))
    pl.when(i == last)(lambda: tile(False))


def _combine_call(slots, yc, x1c, route, g, b, tm):
    n = route.shape[0]
    d = D_MODEL
    nt = n // tm
    return pl.pallas_call(
        functools.partial(_combine_kernel, tm=tm),
        out_shape=jax.ShapeDtypeStruct((n, d), F32),
        grid=(nt,),
        in_specs=[
            pl.BlockSpec((1, 1, 2 * tm), lambda i: (i, 0, 0), memory_space=pltpu.SMEM),
            pl.BlockSpec((1, 1, 2 * tm), lambda i: (jnp.minimum(i + 1, nt - 1), 0, 0), memory_space=pltpu.SMEM),
            pl.BlockSpec(memory_space=pl.ANY),
            pl.BlockSpec((tm * ROW_PITCH, LANES), lambda i: (i, 0)),
            pl.BlockSpec((tm, LANES), lambda i: (i, 0)),
            pl.BlockSpec((1, d), lambda i: (0, 0)),
            pl.BlockSpec((1, d), lambda i: (0, 0)),
        ],
        out_specs=pl.BlockSpec((tm, d), lambda i: (i, 0)),
        scratch_shapes=[pltpu.VMEM((2, 2, tm * ROW_PITCH, LANES), F32), pltpu.SemaphoreType.DMA((2,))],
        compiler_params=_params(("arbitrary",)),
        name="moe_combine_ln2",
    )(slots, slots, yc, x1c, route, g, b)


def _moe_plan(e_idx, tm, n_tiles):
    n = e_idx.shape[0]
    flat = e_idx.reshape(-1)
    onehot = (flat[:, None] == jnp.arange(N_EXPERTS, dtype=jnp.int32)[None, :]).astype(jnp.int32)
    csum = jnp.cumsum(onehot, axis=0)
    rank = jnp.take_along_axis(csum, flat[:, None], axis=1)[:, 0] - 1
    counts = csum[-1]
    ptiles = (counts + tm - 1) // tm
    tile_end = jnp.cumsum(ptiles)
    slot = (tile_end - ptiles)[flat] * tm + rank
    rows = jnp.zeros((n_tiles * tm,), jnp.int32).at[slot].set(jnp.arange(2 * n, dtype=jnp.int32) // 2)
    tile_ids = jnp.arange(n_tiles, dtype=jnp.int32)
    tile_expert = jnp.minimum(jnp.sum((tile_end[None, :] <= tile_ids[:, None]).astype(jnp.int32), axis=1),
                              N_EXPERTS - 1)
    return slot.astype(jnp.int32), rows, tile_expert, tile_end[-1:].astype(jnp.int32)


def _rope_tables(seq):
    half = ROT_DIM // 2
    inv_freq = ROPE_THETA ** (-jnp.arange(0, ROT_DIM, 2, dtype=F32) / ROT_DIM)
    ang = jnp.arange(seq, dtype=F32)[:, None] * inv_freq[None, :]
    cos, sin = jnp.cos(ang), jnp.sin(ang)
    pad1 = jnp.ones((seq, HEAD_DIM - ROT_DIM), F32)
    pad0 = jnp.zeros((seq, HEAD_DIM - ROT_DIM), F32)
    return (jnp.concatenate([cos, cos, pad1], axis=1), jnp.concatenate([-sin, sin, pad0], axis=1))


def _layer(x, w_in, diff_lambda, diff_subln_g, cmp_pos, cmp_w1, cmp_b1, cmp_w2, cmp_b2, w_out,
           ln1_g, ln1_b, router_group, router_expert, w_gate, w_up, w_down, ln2_g, ln2_b, lambda_init):
    batch, seq, d = x.shape
    n = batch * seq
    x2 = x.reshape(n, d)

    seg = lambda a, b: w_in[:, a:b]
    pad = lambda w: jnp.pad(w, ((0, 0), (0, LANES - w.shape[1])))
    gate_w = seg(5632, 5656)
    w_aug = jnp.concatenate(
        [seg(3072, 4096), seg(0, 1024), seg(1024, 2048), seg(2048, 3072), seg(4096, 5632),
         pad(gate_w[:, :12]), pad(gate_w[:, 12:])], axis=1).astype(BF16)
    flags = jnp.zeros((N_UNITS // 2,), jnp.int32).at[jnp.array(ROPE_BLOCKS)].set(1)
    cos_t, sin_t = _rope_tables(seq)
    proj = _proj_call(x2, w_aug, flags, cos_t, sin_t, seq, tm=min(2048, seq))

    nt = seq // CMP_STRIDE
    half_feat = CMP_STRIDE * HEAD_DIM
    r = proj[:, U_KC * LANES:U_KS * LANES].reshape(batch, nt, CMP_STRIDE, 4, HEAD_DIM)
    r = jnp.transpose(r, (0, 3, 1, 2, 4)).reshape(batch, 4, nt, half_feat)
    w1cat = jnp.concatenate([cmp_w1[:, :half_feat], cmp_w1[:, half_feat:]], axis=2).astype(BF16)
    kvc = _compress_call(r, w1cat, cmp_pos.reshape(2, 2, half_feat), cmp_b1[:, None, :],
                         cmp_w2.astype(BF16), cmp_b2[:, None, :])

    o_diff = _diff_call(proj, diff_lambda, diff_subln_g[None, :], batch, seq, min(512, seq), lambda_init)

    ns = seq // SLC_BLOCK
    ci = np.arange(nt)[None, :] * CMP_STRIDE
    sj = np.arange(ns)[:, None] * SLC_BLOCK
    ovt = jnp.asarray((ci < sj + SLC_BLOCK) & (ci + CMP_BLOCK > sj) & (np.arange(nt)[None, :] < nt - 1), dtype=BF16)
    o_nsa = _nsa_call(proj, cos_t, sin_t, kvc, ovt, batch, seq, tq=2 * LANES, tk=min(512, seq))

    wr = jnp.pad(jnp.concatenate([router_group, router_expert], axis=1),
                 ((0, 0), (0, LANES - N_GROUPS - N_EXPERTS)))
    x1, route = _outproj_call(o_diff, o_nsa, w_out.astype(BF16), x2, ln1_g[None, :], ln1_b[None, :], wr, tm=512)

    tm = 256
    n_tiles = 2 * n // tm + N_EXPERTS + 1
    e_idx = route[:, 0:2].astype(jnp.int32)
    slot, rows, tile_expert, n_used = _moe_plan(e_idx, tm, n_tiles)
    ys = _moe_call(tile_expert, n_used, rows.reshape(n_tiles, 1, tm), x1, w_gate, w_up, w_down, tm)
    out = _combine_call(slot.reshape(n // tm, 1, 2 * tm), ys, x1, route, ln2_g[None, :], ln2_b[None, :], tm)
    return out.reshape(batch, seq, d)


def kernel(x, w_in, diff_lambda, diff_subln_g, cmp_pos, cmp_w1, cmp_b1, cmp_w2, cmp_b2, w_out, ln1_g, ln1_b,
           router_group, router_expert, expert_w_gate, expert_w_up, expert_w_down, ln2_g, ln2_b):
    for l in range(DEPTH):
        lambda_init = 0.8 - 0.6 * math.exp(-0.3 * l)
        x = _layer(x, w_in[l], diff_lambda[l], diff_subln_g[l], cmp_pos[l], cmp_w1[l], cmp_b1[l], cmp_w2[l],
                   cmp_b2[l], w_out[l], ln1_g[l], ln1_b[l], router_group[l], router_expert[l],
                   expert_w_gate[l], expert_w_up[l], expert_w_down[l], ln2_g[l], ln2_b[l], lambda_init)
    return x
```

```python
import functools
import math

import numpy as np
import jax
import jax.numpy as jnp
from jax import lax
from jax.experimental import pallas as pl
from jax.experimental.pallas import tpu as pltpu

F32 = jnp.float32
BF16 = jnp.bfloat16

D_MODEL = 2048
HEAD_DIM = 128
ROT_DIM = HEAD_DIM // 4
ROPE_THETA = 500000.0
NEG_INF = -1e30
BIG = 1e30
EPS = 1e-5

DIFF_HEADS = 4
DIFF_VDIM = 2 * HEAD_DIM

NSA_HEADS = 8
NSA_KV_GROUPS = 2
NSA_HPG = NSA_HEADS // NSA_KV_GROUPS
CMP_BLOCK = 32
CMP_STRIDE = 16
CMP_HIDDEN = 256
SLC_BLOCK = 64
SLC_TOPK = 16
WINDOW = 512

N_GROUPS = 4
EXPERTS_PER_GROUP = 8
N_EXPERTS = N_GROUPS * EXPERTS_PER_GROUP
EXPERT_HIDDEN = 512

DEPTH = 1
DN_ALPHA = (2.0 * DEPTH) ** 0.25

LANES = 128
VMEM_LIMIT = 56 * 1024 * 1024

U_NQ, U_DQ, U_DK, U_DV = 0, 8, 16, 24
U_KC, U_VC, U_KS, U_VS, U_KW, U_VW, U_GATE = 32, 34, 36, 38, 40, 42, 44
N_UNITS = 46
ROPE_BLOCKS = tuple(range(U_DQ // 2, U_DV // 2)) + (U_KS // 2, U_KW // 2)


def _dot(a, b):
    return jnp.dot(a, b, preferred_element_type=F32)


def _dot_nt(a, b):
    return lax.dot_general(a, b, (((1,), (1,)), ((), ())), preferred_element_type=F32)


def _params(sem, vmem=VMEM_LIMIT):
    return pltpu.CompilerParams(dimension_semantics=sem, vmem_limit_bytes=vmem)


PROJ_ROW_CHUNK = 512


def _rope(a, c, s):
    lane = lax.broadcasted_iota(jnp.int32, a.shape, 1)
    half = ROT_DIM // 2
    partner = jnp.where(lane < half, pltpu.roll(a, LANES - half, 1), pltpu.roll(a, half, 1))
    return a * c + partner * s


def _proj_kernel(flags_ref, x_ref, w_ref, cos_ref, sin_ref, o_ref, xb_sc):
    j = pl.program_id(1)

    @pl.when(j == 0)
    def _():
        xb_sc[...] = x_ref[...].astype(BF16)

    def block(rotary):
        for r0 in range(0, xb_sc.shape[0], PROJ_ROW_CHUNK):
            rs = slice(r0, r0 + PROJ_ROW_CHUNK)
            acc = _dot(xb_sc[rs, :], w_ref[...])
            if rotary:
                for hh in range(2):
                    cs = slice(hh * LANES, (hh + 1) * LANES)
                    o_ref[rs, cs] = _rope(acc[:, cs], cos_ref[rs, :], sin_ref[rs, :]).astype(o_ref.dtype)
            else:
                o_ref[rs, :] = acc.astype(o_ref.dtype)

    pl.when(flags_ref[j] == 0)(lambda: block(False))
    pl.when(flags_ref[j] != 0)(lambda: block(True))


def _proj_call(xb, w_aug, flags, cos_t, sin_t, seq, tm):
    n, d = xb.shape
    nj = w_aug.shape[1] // 256
    tpb = seq // tm
    grid_spec = pltpu.PrefetchScalarGridSpec(
        num_scalar_prefetch=1,
        grid=(n // tm, nj),
        in_specs=[
            pl.BlockSpec((tm, d), lambda i, j, f: (i, 0)),
            pl.BlockSpec((d, 256), lambda i, j, f: (0, j)),
            pl.BlockSpec((tm, LANES), lambda i, j, f: (i % tpb, 0)),
            pl.BlockSpec((tm, LANES), lambda i, j, f: (i % tpb, 0)),
        ],
        out_specs=pl.BlockSpec((tm, 256), lambda i, j, f: (i, j)),
        scratch_shapes=[pltpu.VMEM((tm, d), BF16)],
    )
    return pl.pallas_call(
        _proj_kernel,
        out_shape=jax.ShapeDtypeStruct((n, w_aug.shape[1]), BF16),
        grid_spec=grid_spec,
        compiler_params=_params(("parallel", "arbitrary")),
        name="proj",
    )(flags, xb, w_aug, cos_t, sin_t)


def _compress_kernel(t_ref, w1_ref, pos_ref, b1_ref, w2_ref, b2_ref, o_ref, t_sc):
    nt = t_sc.shape[0] // CMP_STRIDE
    t_sc[...] = t_ref[...].astype(F32)
    r = jnp.concatenate([t_sc[pl.ds(k, nt, stride=CMP_STRIDE), :] for k in range(CMP_STRIDE)], axis=1).astype(BF16)
    ab = _dot(r, w1_ref[...])
    pos = pos_ref[...]
    pa = jnp.broadcast_to(pos[0:1], (8, pos.shape[1])).astype(BF16)
    pb = jnp.broadcast_to(pos[1:2], (8, pos.shape[1])).astype(BF16)
    const = _dot(pa, w1_ref[:, :CMP_HIDDEN])[0:1] + _dot(pb, w1_ref[:, CMP_HIDDEN:])[0:1]
    h = ab[:, :CMP_HIDDEN] + pltpu.roll(ab[:, CMP_HIDDEN:], nt - 1, 0) + const + b1_ref[...]
    h = jax.nn.gelu(h)
    o_ref[...] = (_dot(h.astype(BF16), w2_ref[...]) + b2_ref[...]).astype(o_ref.dtype)


def _compress_call(proj, w1cat, pos2, b1, w2, b2, b, seq):
    four = 2 * NSA_KV_GROUPS
    nt = seq // CMP_STRIDE
    k = CMP_STRIDE * HEAD_DIM
    return pl.pallas_call(
        _compress_kernel,
        out_shape=jax.ShapeDtypeStruct((b, four, nt, HEAD_DIM), BF16),
        grid=(b, four),
        in_specs=[
            pl.BlockSpec((seq, HEAD_DIM), lambda i, c: (i, U_KC + c)),
            pl.BlockSpec((None, k, 2 * CMP_HIDDEN), lambda i, c: (c // 2, 0, 0)),
            pl.BlockSpec((None, 2, k), lambda i, c: (c // 2, 0, 0)),
            pl.BlockSpec((None, 1, CMP_HIDDEN), lambda i, c: (c // 2, 0, 0)),
            pl.BlockSpec((None, CMP_HIDDEN, HEAD_DIM), lambda i, c: (c // 2, 0, 0)),
            pl.BlockSpec((None, 1, HEAD_DIM), lambda i, c: (c // 2, 0, 0)),
        ],
        out_specs=pl.BlockSpec((None, None, nt, HEAD_DIM), lambda i, c: (i, c, 0, 0)),
        scratch_shapes=[pltpu.VMEM((seq, HEAD_DIM), F32)],
        compiler_params=_params(("parallel", "parallel")),
        name="compress",
    )(proj, w1cat, pos2, b1, w2, b2)


EXP2_SCALE = HEAD_DIM ** -0.5 * math.log2(math.e)
ONES_ROWS = 16


def _transpose_into(src_ref, dst_ref):
    def body(c, carry):
        off = pl.multiple_of(c * LANES, LANES)
        dst_ref[:, pl.ds(off, LANES)] = src_ref[pl.ds(off, LANES), :].astype(F32).T.astype(dst_ref.dtype)
        return carry

    lax.fori_loop(0, src_ref.shape[0] // LANES, body, 0)


def _diff_kernel(dl_ref, q_ref, k_ref, v_ref, g_ref, o_ref, vt_sc, acc_sc, sa_sc, sb_sc, *, tq, lambda_init):
    qi = pl.program_id(2)

    @pl.when(qi == 0)
    def _():
        for c in range(2):
            _transpose_into(v_ref.at[:, c * LANES:(c + 1) * LANES], vt_sc.at[c * LANES:(c + 1) * LANES, :])

    acc_sc[...] = jnp.zeros(acc_sc.shape, F32)
    q = q_ref[...]
    qpos = qi * tq + lax.broadcasted_iota(jnp.int32, (1, tq), 1)

    tk = tq // 2

    def scores(j, dst):
        kt = k_ref[pl.ds(pl.multiple_of(j * tk, tk), tk), :]
        for c in range(2):
            dst[c] = _dot_nt(kt[:, c * HEAD_DIM:(c + 1) * HEAD_DIM], q[:, c * HEAD_DIM:(c + 1) * HEAD_DIM])

    def absorb(src, j, masked, carry):
        off = pl.multiple_of(j * tk, tk)
        vt = vt_sc[:, pl.ds(off, tk)]
        out = []
        for c in range(2):
            m_old, l_old = carry[2 * c], carry[2 * c + 1]
            s = src[c]
            if masked:
                s = jnp.where(lax.broadcasted_iota(jnp.int32, s.shape, 0) <= qpos - off, s, NEG_INF)
            m_new = jnp.maximum(m_old, jnp.max(s, axis=0, keepdims=True))
            p = jnp.exp2((s - m_new) * EXP2_SCALE)
            alpha = jnp.exp2((m_old - m_new) * EXP2_SCALE)
            l_new = alpha * l_old + jnp.sum(p, axis=0, keepdims=True)
            acc_sc[c] = alpha * acc_sc[c] + _dot(vt, p.astype(BF16))
            out += [m_new, l_new]
        return tuple(out)

    def pair(p, carry):
        j = 2 * p
        scores(j + 1, sb_sc)
        carry = absorb(sa_sc, j, False, carry)
        scores(j + 2, sa_sc)
        return absorb(sb_sc, j + 1, False, carry)

    init = (jnp.full((1, tq), NEG_INF, F32), jnp.zeros((1, tq), F32)) * 2
    scores(0, sa_sc)
    carry = lax.fori_loop(0, qi, pair, init)
    j_tail = 2 * qi
    scores(j_tail + 1, sb_sc)
    carry = absorb(sa_sc, j_tail, True, carry)
    _, l0, _, l1 = absorb(sb_sc, j_tail + 1, True, carry)

    dl = dl_ref[...]
    lam = (jnp.exp(jnp.sum(dl[0:1] * dl[1:2], axis=1, keepdims=True))
           - jnp.exp(jnp.sum(dl[2:3] * dl[3:4], axis=1, keepdims=True)) + lambda_init)
    o = acc_sc[0] * (1.0 / l0) - lam * (acc_sc[1] * (1.0 / l1))
    o = o * (lax.rsqrt(jnp.mean(o * o, axis=0, keepdims=True) + EPS) * (1.0 - lambda_init))
    for c in range(DIFF_VDIM // LANES):
        for r in range(tq // LANES):
            blk = o[c * LANES:(c + 1) * LANES, r * LANES:(r + 1) * LANES].T
            o_ref[r * LANES:(r + 1) * LANES, c * LANES:(c + 1) * LANES] = (
                blk * g_ref[:, c * LANES:(c + 1) * LANES]).astype(o_ref.dtype)


def _diff_call(proj, dl, g, batch, seq, tq, lambda_init):
    nq = seq // tq
    kern = functools.partial(_diff_kernel, tq=tq, lambda_init=lambda_init)
    return pl.pallas_call(
        kern,
        out_shape=jax.ShapeDtypeStruct((batch * seq, DIFF_HEADS * DIFF_VDIM), BF16),
        grid=(batch, DIFF_HEADS, nq),
        in_specs=[
            pl.BlockSpec((4, HEAD_DIM), lambda b, h, i: (0, 0)),
            pl.BlockSpec((tq, 256), lambda b, h, i: (b * nq + i, U_DQ // 2 + h)),
            pl.BlockSpec((seq, 256), lambda b, h, i: (b, U_DK // 2 + h)),
            pl.BlockSpec((seq, 256), lambda b, h, i: (b, U_DV // 2 + h)),
            pl.BlockSpec((1, DIFF_VDIM), lambda b, h, i: (0, 0)),
        ],
        out_specs=pl.BlockSpec((tq, DIFF_VDIM), lambda b, h, i: (b * nq + i, h)),
        scratch_shapes=[
            pltpu.VMEM((DIFF_VDIM, seq), BF16),
            pltpu.VMEM((2, DIFF_VDIM, tq), F32),
            pltpu.VMEM((2, tq // 2, tq), F32),
            pltpu.VMEM((2, tq // 2, tq), F32),
        ],
        compiler_params=_params(("arbitrary", "arbitrary", "arbitrary")),
        name="diff_attn",
    )(dl, proj, proj, proj, g)


def _stack_heads(x):
    return jnp.concatenate([x[:, h * HEAD_DIM:(h + 1) * HEAD_DIM] for h in range(NSA_HPG)], axis=0)


def _nsa_kernel(q_ref, cos_ref, sin_ref, gate_ref, kc_ref, vc_ref, ks_ref, vs_ref, kw_ref, vw_ref, ovt_ref,
                o_ref, vct_sc, vst_sc, vwt_sc, bias_sc, acc_sc, sa_sc, sb_sc, *, tq, tk, seq, top_k):
    qi = pl.program_id(2)
    q0 = qi * tq
    rows = NSA_HPG * tq
    nt = kc_ref.shape[0]
    ns = ovt_ref.shape[0]
    nb = tk // SLC_BLOCK

    @pl.when(qi == 0)
    def _():
        _transpose_into(vc_ref, vct_sc)
        for v_ref, vt_sc in ((vs_ref, vst_sc), (vw_ref, vwt_sc)):
            _transpose_into(v_ref, vt_sc.at[:HEAD_DIM, :])
            vt_sc[HEAD_DIM:, :] = jnp.ones((ONES_ROWS, seq), vt_sc.dtype)

    q = q_ref[...]
    qs = _stack_heads(q)
    qr = jnp.concatenate(
        [_rope(q[:, h * HEAD_DIM:(h + 1) * HEAD_DIM].astype(F32), cos_ref[...], sin_ref[...]).astype(BF16)
         for h in range(NSA_HPG)], axis=0)
    qpos = q0 + (lax.broadcasted_iota(jnp.int32, (1, rows), 1) & (tq - 1))

    def sel_scores(j, dst):
        dst[...] = _dot_nt(ks_ref[pl.ds(pl.multiple_of(j * tk, tk), tk), :], qr)

    sel_scores(0, sa_sc)

    slab = min(WINDOW + tq, seq)
    start = pl.multiple_of(jnp.maximum(q0 - WINDOW, 0), tq)
    s_w = _dot_nt(kw_ref[pl.ds(start, slab), :], qr)
    krow = lax.broadcasted_iota(jnp.int32, (slab, rows), 0)
    newest = qpos - start
    s_w = jnp.where((krow <= newest) & (krow > newest - WINDOW), s_w, NEG_INF)
    p_w = jnp.exp2((s_w - jnp.max(s_w, axis=0, keepdims=True)) * EXP2_SCALE)
    o_w = _dot(vwt_sc[:, pl.ds(start, slab)], p_w.astype(BF16))
    o_w = o_w[:HEAD_DIM] * (1.0 / o_w[HEAD_DIM:HEAD_DIM + 1])

    s_c = _dot_nt(kc_ref[...], qs)
    last_c = jnp.right_shift(qpos - (CMP_BLOCK - 1), int(math.log2(CMP_STRIDE)))
    valid_c = lax.broadcasted_iota(jnp.int32, (nt, rows), 0) <= last_c
    s_c = jnp.where(valid_c, s_c, NEG_INF)
    m_c = jnp.max(s_c, axis=0, keepdims=True)
    e_c = jnp.where(valid_c, jnp.exp2((s_c - m_c) * EXP2_SCALE), 0.0)
    l_c = jnp.sum(e_c, axis=0, keepdims=True)
    p_c = e_c * (1.0 / jnp.where(l_c > 0.0, l_c, 1.0))
    o_c = _dot(vct_sc[...], p_c.astype(BF16))

    p_sum = p_c[:, 0:tq]
    for h in range(1, NSA_HPG):
        p_sum = p_sum + p_c[:, h * tq:(h + 1) * tq]
    p_hi = p_sum.astype(BF16)
    p_lo = (p_sum - p_hi.astype(F32)).astype(BF16)
    imp = _dot(ovt_ref[...], p_hi) + _dot(ovt_ref[...], p_lo)

    blk = lax.broadcasted_iota(jnp.int32, (ns, tq), 0)
    qpos_l = q0 + lax.broadcasted_iota(jnp.int32, (ns, tq), 1)
    cur = lax.shift_right_logical(qpos_l, int(math.log2(SLC_BLOCK)))
    valid_s = blk <= cur
    forced = (blk == 0) | (blk == cur) | (blk == cur - 1)
    work = jnp.where(forced, -jnp.inf, jnp.where(valid_s, imp, NEG_INF))
    sel = jnp.where(forced, 1.0, 0.0)
    blk_f = blk.astype(F32)
    for _ in range(top_k - 3):
        mx = jnp.max(work, axis=0, keepdims=True)
        idx = jnp.min(jnp.where(work == mx, blk_f, float(ns)), axis=0, keepdims=True)
        pick = blk_f == idx
        sel = jnp.where(pick, 1.0, sel)
        work = jnp.where(pick, -jnp.inf, work)
    bias_sc[...] = (sel - 1.0) * BIG

    acc_sc[...] = jnp.zeros(acc_sc.shape, F32)

    def sel_absorb(src, j, causal, m_old):
        off = pl.multiple_of(j * tk, tk)
        s = src[...]
        if causal:
            s = jnp.where(lax.broadcasted_iota(jnp.int32, (tk, rows), 0) <= qpos - off, s, NEG_INF)
        bias = bias_sc[pl.ds(pl.multiple_of(j * nb, nb), nb), :]
        bias = jnp.concatenate([bias] * NSA_HPG, axis=1)
        s3 = s.reshape(nb, SLC_BLOCK, rows) + bias[:, None, :]
        m_new = jnp.maximum(m_old, jnp.max(jnp.max(s3, axis=0), axis=0, keepdims=True))
        p3 = jnp.exp2((s3 - m_new) * EXP2_SCALE)
        alpha = jnp.exp2((m_old - m_new) * EXP2_SCALE)
        p = p3.reshape(tk, rows).astype(BF16)
        acc_sc[...] = alpha * acc_sc[...] + _dot(vst_sc[:, pl.ds(off, tk)], p)
        return m_new

    def sel_pair(p, carry):
        j = 2 * p
        sel_scores(j + 1, sb_sc)
        carry = sel_absorb(sa_sc, j, False, carry)
        sel_scores(j + 2, sa_sc)
        return sel_absorb(sb_sc, j + 1, False, carry)

    n_pairs = (q0 // tk + 2) // 2
    carry = lax.fori_loop(0, n_pairs - 1, sel_pair, jnp.full((1, rows), NEG_INF, F32))
    j_tail = 2 * (n_pairs - 1)
    sel_scores(j_tail + 1, sb_sc)
    carry = sel_absorb(sa_sc, j_tail, True, carry)
    sel_absorb(sb_sc, j_tail + 1, True, carry)
    o_s = acc_sc[:HEAD_DIM, :] * (1.0 / acc_sc[HEAD_DIM:HEAD_DIM + 1, :])

    gates = jax.nn.sigmoid(gate_ref[...].astype(F32))
    gates = jnp.concatenate([gates[r * LANES:(r + 1) * LANES].T for r in range(tq // LANES)],
                            axis=1)
    for h in range(NSA_HPG):
        sl = slice(h * tq, (h + 1) * tq)
        o = (gates[3 * h:3 * h + 1] * o_c[:, sl] + gates[3 * h + 1:3 * h + 2] * o_s[:, sl]
             + gates[3 * h + 2:3 * h + 3] * o_w[:, sl])
        for r in range(tq // LANES):
            o_ref[r * LANES:(r + 1) * LANES, h * HEAD_DIM:(h + 1) * HEAD_DIM] = (
                o[:, r * LANES:(r + 1) * LANES].T.astype(o_ref.dtype))


def _nsa_call(proj, cos_t, sin_t, kvc, ovt, batch, seq, tq, tk):
    nq = seq // tq
    assert (seq // tk) % 2 == 0, "key tiles are processed in pairs"
    nt = kvc.shape[2]
    ns = ovt.shape[0]
    width = NSA_HPG * HEAD_DIM
    kern = functools.partial(_nsa_kernel, tq=tq, tk=tk, seq=seq, top_k=min(SLC_TOPK, ns))

    def col(unit):
        return pl.BlockSpec((seq, HEAD_DIM), lambda b, g, i: (b, unit + g))

    return pl.pallas_call(
        kern,
        out_shape=jax.ShapeDtypeStruct((batch * seq, NSA_HEADS * HEAD_DIM), BF16),
        grid=(batch, NSA_KV_GROUPS, nq),
        in_specs=[
            pl.BlockSpec((tq, width), lambda b, g, i: (b * nq + i, U_NQ // 4 + g)),
            pl.BlockSpec((tq, LANES), lambda b, g, i: (i, 0)),
            pl.BlockSpec((tq, LANES), lambda b, g, i: (i, 0)),
            pl.BlockSpec((tq, LANES), lambda b, g, i: (b * nq + i, U_GATE + g)),
            pl.BlockSpec((None, None, nt, HEAD_DIM), lambda b, g, i: (b, g, 0, 0)),
            pl.BlockSpec((None, None, nt, HEAD_DIM), lambda b, g, i: (b, 2 + g, 0, 0)),
            col(U_KS), col(U_VS), col(U_KW), col(U_VW),
            pl.BlockSpec((ns, nt), lambda b, g, i: (0, 0)),
        ],
        out_specs=pl.BlockSpec((tq, width), lambda b, g, i: (b * nq + i, g)),
        scratch_shapes=[
            pltpu.VMEM((HEAD_DIM, nt), BF16),
            pltpu.VMEM((HEAD_DIM + ONES_ROWS, seq), BF16),
            pltpu.VMEM((HEAD_DIM + ONES_ROWS, seq), BF16),
            pltpu.VMEM((ns, tq), F32),
            pltpu.VMEM((HEAD_DIM + ONES_ROWS, NSA_HPG * tq), F32),
            pltpu.VMEM((tk, NSA_HPG * tq), F32),
            pltpu.VMEM((tk, NSA_HPG * tq), F32),
        ],
        compiler_params=_params(("arbitrary", "arbitrary", "arbitrary")),
        name="nsa_attn",
    )(proj, cos_t, sin_t, proj, kvc, kvc, proj, proj, proj, proj, ovt)


def _layernorm(y, g, b):
    mu = jnp.mean(y, axis=1, keepdims=True)
    yc = y - mu
    var = jnp.mean(yc * yc, axis=1, keepdims=True)
    return yc * lax.rsqrt(var + EPS) * g + b


N_CHUNKS = D_MODEL // LANES
ROW_PITCH = N_CHUNKS + 1


def _store_chunk_rows(ref, val):
    tm = val.shape[0]
    for k in range(N_CHUNKS):
        ref[pl.ds(k, tm, stride=ROW_PITCH), :] = val[:, k * LANES:(k + 1) * LANES]
    for k in range(N_CHUNKS, ROW_PITCH):
        ref[pl.ds(k, tm, stride=ROW_PITCH), :] = jnp.zeros((tm, LANES), ref.dtype)


def _load_chunk_rows(ref, tm):
    return jnp.concatenate([ref[pl.ds(k, tm, stride=ROW_PITCH), :] for k in range(N_CHUNKS)], axis=1)


OUT_ROW_CHUNK = 256


def _outproj_kernel(od_ref, on_ref, w_ref, x_ref, g_ref, b_ref, wr_ref, x1_ref, route_ref):
    wr = wr_ref[...]
    w_hi = wr.astype(BF16)
    w_lo = (wr - w_hi.astype(F32)).astype(BF16)
    for r0 in range(0, x_ref.shape[0], OUT_ROW_CHUNK):
        _outproj_rows(od_ref, on_ref, w_ref, x_ref, g_ref, b_ref, w_hi, w_lo, x1_ref, route_ref,
                      slice(r0, r0 + OUT_ROW_CHUNK))


def _outproj_rows(od_ref, on_ref, w_ref, x_ref, g_ref, b_ref, w_hi, w_lo, x1_ref, route_ref, rs):
    half = od_ref.shape[1]
    h = _dot(od_ref[rs, :], w_ref[:half, :]) + _dot(on_ref[rs, :], w_ref[half:, :])
    x1 = _layernorm(DN_ALPHA * x_ref[rs, :] + h, g_ref[...], b_ref[...])
    _store_chunk_rows(x1_ref.at[rs.start * ROW_PITCH:rs.stop * ROW_PITCH, :], x1)

    x_hi = x1.astype(BF16)
    x_lo = (x1 - x_hi.astype(F32)).astype(BF16)
    logits = _dot(x_hi, w_hi) + (_dot(x_lo, w_hi) + _dot(x_hi, w_lo))

    lane = lax.broadcasted_iota(jnp.int32, logits.shape, 1).astype(F32)
    ninf = -jnp.inf
    gl = jnp.where(lane < N_GROUPS, logits, ninf)
    gmax = jnp.max(gl, axis=1, keepdims=True)
    g_w = 1.0 / jnp.sum(jnp.exp(gl - gmax), axis=1, keepdims=True)
    g_sel = jnp.min(jnp.where(gl == gmax, lane, float(LANES)), axis=1, keepdims=True)
    lo = N_GROUPS + EXPERTS_PER_GROUP * g_sel
    el = jnp.where((lane >= lo) & (lane < lo + EXPERTS_PER_GROUP), logits, ninf)
    e1 = jnp.max(el, axis=1, keepdims=True)
    i1 = jnp.min(jnp.where(el == e1, lane, float(LANES)), axis=1, keepdims=True)
    el2 = jnp.where(lane == i1, ninf, el)
    e2 = jnp.max(el2, axis=1, keepdims=True)
    i2 = jnp.min(jnp.where(el2 == e2, lane, float(LANES)), axis=1, keepdims=True)
    r = jnp.exp(e2 - e1)
    w1 = g_w / (1.0 + r)
    w2 = g_w * r / (1.0 + r)
    route = jnp.where(lane == 0, i1 - N_GROUPS,
                      jnp.where(lane == 1, i2 - N_GROUPS,
                                jnp.where(lane == 2, w1, jnp.where(lane == 3, w2, 0.0))))
    route_ref[rs, :] = route


def _outproj_call(o_diff, o_nsa, w_out, x2, g, b, wr, tm):
    n, d = x2.shape
    half = o_diff.shape[1]
    return pl.pallas_call(
        _outproj_kernel,
        out_shape=(jax.ShapeDtypeStruct((n * ROW_PITCH, LANES), F32), jax.ShapeDtypeStruct((n, LANES), F32)),
        grid=(n // tm,),
        in_specs=[
            pl.BlockSpec((tm, half), lambda i: (i, 0)),
            pl.BlockSpec((tm, half), lambda i: (i, 0)),
            pl.BlockSpec((2 * half, d), lambda i: (0, 0)),
            pl.BlockSpec((tm, d), lambda i: (i, 0)),
            pl.BlockSpec((1, d), lambda i: (0, 0)),
            pl.BlockSpec((1, d), lambda i: (0, 0)),
            pl.BlockSpec((d, LANES), lambda i: (0, 0)),
        ],
        out_specs=(pl.BlockSpec((tm * ROW_PITCH, LANES), lambda i: (i, 0)),
                   pl.BlockSpec((tm, LANES), lambda i: (i, 0))),
        compiler_params=_params(("parallel",)),
        name="outproj_ln1_router",
    )(o_diff, o_nsa, w_out, x2, g, b, wr)


MOE_TILE_ROWS = 384
COMBINE_TILE_TOKENS = 256


def _token_copy(src_hbm, dst, sem, src_tok, dst_tok):
    return pltpu.make_async_copy(src_hbm.at[pl.ds(src_tok * ROW_PITCH, N_CHUNKS)],
                                 dst.at[pl.ds(dst_tok * ROW_PITCH, N_CHUNKS)], sem)


def _wait_tokens(src_hbm, dst, sem, count):
    pltpu.make_async_copy(src_hbm.at[pl.ds(0, count * N_CHUNKS)], dst.at[pl.ds(0, count * N_CHUNKS)], sem).wait()


def _moe_kernel(te_ref, nu_ref, rows_ref, rows_next_ref, x_hbm, wg_ref, wu_ref, wd_ref, o_ref,
                xbuf, sem, wg_sc, wu_sc, wd_sc, *, tm):
    i = pl.program_id(0)
    n_used = nu_ref[0]
    slot = lax.rem(i, 2)

    @pl.when(i == 0)
    def _():
        def body(r, carry):
            _token_copy(x_hbm, xbuf.at[0], sem.at[0], rows_ref[0, 0, r], r).start()
            return carry

        lax.fori_loop(0, tm, body, 0, unroll=8)

    prev = te_ref[jnp.maximum(i - 1, 0)]

    @pl.when((i == 0) | (te_ref[i] != prev))
    def _():
        wg_sc[...] = wg_ref[...].astype(BF16)
        wu_sc[...] = wu_ref[...].astype(BF16)
        wd_sc[...] = wd_ref[...].astype(BF16)

    @pl.when(i < n_used)
    def _():
        def prefetch(part, parts=4):
            for r in range(part * tm // parts, (part + 1) * tm // parts):
                _token_copy(x_hbm, xbuf.at[1 - slot], sem.at[1 - slot], rows_next_ref[0, 0, r], r).start(
                    priority=r % 2)

        _wait_tokens(x_hbm, xbuf.at[slot], sem.at[slot], tm)
        xb = _load_chunk_rows(xbuf.at[slot], tm).astype(BF16)
        prefetch(0)
        gate = _dot(xb, wg_sc[...])
        prefetch(1)
        up = _dot(xb, wu_sc[...])
        prefetch(2)
        h = (gate * jax.nn.sigmoid(gate) * up).astype(BF16)
        y = _dot(h, wd_sc[...])
        prefetch(3)
        _store_chunk_rows(o_ref, y)

    @pl.when(i == n_used)
    def _():
        _wait_tokens(x_hbm, xbuf.at[slot], sem.at[slot], tm)

    @pl.when(i >= n_used)
    def _():
        o_ref[...] = jnp.zeros(o_ref.shape, o_ref.dtype)


def _moe_call(tile_expert, n_used, rows, x1c, w_gate, w_up, w_down, tm):
    n_tiles = rows.shape[0]
    d, f = w_gate.shape[1], w_gate.shape[2]
    grid_spec = pltpu.PrefetchScalarGridSpec(
        num_scalar_prefetch=2,
        grid=(n_tiles,),
        in_specs=[
            pl.BlockSpec((1, 1, tm), lambda i, te, nu: (i, 0, 0), memory_space=pltpu.SMEM),
            pl.BlockSpec((1, 1, tm), lambda i, te, nu: (jnp.minimum(i + 1, n_tiles - 1), 0, 0),
                         memory_space=pltpu.SMEM),
            pl.BlockSpec(memory_space=pl.ANY),
            pl.BlockSpec((None, d, f), lambda i, te, nu: (te[i], 0, 0)),
            pl.BlockSpec((None, d, f), lambda i, te, nu: (te[i], 0, 0)),
            pl.BlockSpec((None, f, d), lambda i, te, nu: (te[i], 0, 0)),
        ],
        out_specs=pl.BlockSpec((tm * ROW_PITCH, LANES), lambda i, te, nu: (i, 0)),
        scratch_shapes=[
            pltpu.VMEM((2, tm * ROW_PITCH, LANES), F32),
            pltpu.SemaphoreType.DMA((2,)),
            pltpu.VMEM((d, f), BF16), pltpu.VMEM((d, f), BF16), pltpu.VMEM((f, d), BF16),
        ],
    )
    return pl.pallas_call(
        functools.partial(_moe_kernel, tm=tm),
        out_shape=jax.ShapeDtypeStruct((n_tiles * tm * ROW_PITCH, LANES), F32),
        grid_spec=grid_spec,
        compiler_params=_params(("arbitrary",)),
        name="moe_experts",
    )(tile_expert, n_used, rows, rows, x1c, w_gate, w_up, w_down)


def _combine_kernel(slots_ref, slots_next_ref, y_hbm, x1_ref, route_ref, g_ref, b_ref, o_ref, buf, sem, *, tm):
    i = pl.program_id(0)
    slot = lax.rem(i, 2)

    @pl.when(i == 0)
    def _():
        def body(r, carry):
            for k in range(2):
                _token_copy(y_hbm, buf.at[0, k], sem.at[0], slots_ref[0, 0, 2 * r + k], r).start()
            return carry

        lax.fori_loop(0, tm, body, 0, unroll=8)

    def tile(prefetch):
        if prefetch:
            for r in range(tm):
                for k in range(2):
                    _token_copy(y_hbm, buf.at[1 - slot, k], sem.at[1 - slot],
                                slots_next_ref[0, 0, 2 * r + k], r).start(priority=k)
        for k in range(2):
            _wait_tokens(y_hbm, buf.at[slot, k], sem.at[slot], tm)
        route = route_ref[...]
        y = (route[:, 2:3] * _load_chunk_rows(buf.at[slot, 0], tm)
             + route[:, 3:4] * _load_chunk_rows(buf.at[slot, 1], tm))
        x1 = _load_chunk_rows(x1_ref, tm)
        o_ref[...] = _layernorm(DN_ALPHA * x1 + y, g_ref[...], b_ref[...])

    last = pl.num_programs(0) - 1
    pl.when(i < last)(lambda: tile(True))
    pl.when(i == last)(lambda: tile(False))


def _combine_call(slots, yc, x1c, route, g, b, tm):
    n = route.shape[0]
    d = D_MODEL
    nt = n // tm
    return pl.pallas_call(
        functools.partial(_combine_kernel, tm=tm),
        out_shape=jax.ShapeDtypeStruct((n, d), F32),
        grid=(nt,),
        in_specs=[
            pl.BlockSpec((1, 1, 2 * tm), lambda i: (i, 0, 0), memory_space=pltpu.SMEM),
            pl.BlockSpec((1, 1, 2 * tm), lambda i: (jnp.minimum(i + 1, nt - 1), 0, 0), memory_space=pltpu.SMEM),
            pl.BlockSpec(memory_space=pl.ANY),
            pl.BlockSpec((tm * ROW_PITCH, LANES), lambda i: (i, 0)),
            pl.BlockSpec((tm, LANES), lambda i: (i, 0)),
            pl.BlockSpec((1, d), lambda i: (0, 0)),
            pl.BlockSpec((1, d), lambda i: (0, 0)),
        ],
        out_specs=pl.BlockSpec((tm, d), lambda i: (i, 0)),
        scratch_shapes=[pltpu.VMEM((2, 2, tm * ROW_PITCH, LANES), F32), pltpu.SemaphoreType.DMA((2,))],
        compiler_params=_params(("arbitrary",)),
        name="moe_combine_ln2",
    )(slots, slots, yc, x1c, route, g, b)


def _moe_plan(e_idx, tm, n_tiles):
    n = e_idx.shape[0]
    flat = e_idx.reshape(-1)
    onehot = (flat[:, None] == jnp.arange(N_EXPERTS, dtype=jnp.int32)[None, :]).astype(jnp.int32)
    csum = jnp.cumsum(onehot, axis=0)
    rank = jnp.take_along_axis(csum, flat[:, None], axis=1)[:, 0] - 1
    counts = csum[-1]
    ptiles = (counts + tm - 1) // tm
    tile_end = jnp.cumsum(ptiles)
    slot = (tile_end - ptiles)[flat] * tm + rank
    rows = jnp.zeros((n_tiles * tm,), jnp.int32).at[slot].set(jnp.arange(2 * n, dtype=jnp.int32) // 2)
    tile_ids = jnp.arange(n_tiles, dtype=jnp.int32)
    tile_expert = jnp.minimum(jnp.sum((tile_end[None, :] <= tile_ids[:, None]).astype(jnp.int32), axis=1),
                              N_EXPERTS - 1)
    return slot.astype(jnp.int32), rows, tile_expert, tile_end[-1:].astype(jnp.int32)


def _rope_tables(seq):
    half = ROT_DIM // 2
    inv_freq = ROPE_THETA ** (-jnp.arange(0, ROT_DIM, 2, dtype=F32) / ROT_DIM)
    ang = jnp.arange(seq, dtype=F32)[:, None] * inv_freq[None, :]
    cos, sin = jnp.cos(ang), jnp.sin(ang)
    pad1 = jnp.ones((seq, HEAD_DIM - ROT_DIM), F32)
    pad0 = jnp.zeros((seq, HEAD_DIM - ROT_DIM), F32)
    return (jnp.concatenate([cos, cos, pad1], axis=1), jnp.concatenate([-sin, sin, pad0], axis=1))


def _layer(x, w_in, diff_lambda, diff_subln_g, cmp_pos, cmp_w1, cmp_b1, cmp_w2, cmp_b2, w_out,
           ln1_g, ln1_b, router_group, router_expert, w_gate, w_up, w_down, ln2_g, ln2_b, lambda_init):
    batch, seq, d = x.shape
    n = batch * seq
    x2 = x.reshape(n, d)

    seg = lambda a, b: w_in[:, a:b]
    pad = lambda w: jnp.pad(w, ((0, 0), (0, LANES - w.shape[1])))
    gate_w = seg(5632, 5656)
    w_aug = jnp.concatenate(
        [seg(3072, 4096), seg(0, 1024), seg(1024, 2048), seg(2048, 3072), seg(4096, 5632),
         pad(gate_w[:, :12]), pad(gate_w[:, 12:])], axis=1).astype(BF16)
    flags = jnp.zeros((N_UNITS // 2,), jnp.int32).at[jnp.array(ROPE_BLOCKS)].set(1)
    cos_t, sin_t = _rope_tables(seq)
    proj = _proj_call(x2, w_aug, flags, cos_t, sin_t, seq, tm=min(2048, seq))

    nt = seq // CMP_STRIDE
    half_feat = CMP_STRIDE * HEAD_DIM
    w1cat = jnp.concatenate([cmp_w1[:, :half_feat], cmp_w1[:, half_feat:]], axis=2).astype(BF16)
    kvc = _compress_call(proj, w1cat, cmp_pos.reshape(2, 2, half_feat), cmp_b1[:, None, :],
                         cmp_w2.astype(BF16), cmp_b2[:, None, :], batch, seq)

    o_diff = _diff_call(proj, diff_lambda, diff_subln_g[None, :], batch, seq, min(1024, seq), lambda_init)

    ns = seq // SLC_BLOCK
    ci = np.arange(nt)[None, :] * CMP_STRIDE
    sj = np.arange(ns)[:, None] * SLC_BLOCK
    ovt = jnp.asarray((ci < sj + SLC_BLOCK) & (ci + CMP_BLOCK > sj) & (np.arange(nt)[None, :] < nt - 1), dtype=BF16)
    o_nsa = _nsa_call(proj, cos_t, sin_t, kvc, ovt, batch, seq, tq=2 * LANES, tk=min(512, seq))

    wr = jnp.pad(jnp.concatenate([router_group, router_expert], axis=1),
                 ((0, 0), (0, LANES - N_GROUPS - N_EXPERTS)))
    x1, route = _outproj_call(o_diff, o_nsa, w_out.astype(BF16), x2, ln1_g[None, :], ln1_b[None, :], wr, tm=512)

    tm = MOE_TILE_ROWS
    n_tiles = -(-2 * n // tm) + N_EXPERTS + 1
    e_idx = route[:, 0:2].astype(jnp.int32)
    slot, rows, tile_expert, n_used = _moe_plan(e_idx, tm, n_tiles)
    ys = _moe_call(tile_expert, n_used, rows.reshape(n_tiles, 1, tm), x1, w_gate, w_up, w_down, tm)
    tc = COMBINE_TILE_TOKENS
    out = _combine_call(slot.reshape(n // tc, 1, 2 * tc), ys, x1, route, ln2_g[None, :], ln2_b[None, :], tc)
    return out.reshape(batch, seq, d)


def kernel(x, w_in, diff_lambda, diff_subln_g, cmp_pos, cmp_w1, cmp_b1, cmp_w2, cmp_b2, w_out, ln1_g, ln1_b,
           router_group, router_expert, expert_w_gate, expert_w_up, expert_w_down, ln2_g, ln2_b):
    for l in range(DEPTH):
        lambda_init = 0.8 - 0.6 * math.exp(-0.3 * l)
        x = _layer(x, w_in[l], diff_lambda[l], diff_subln_g[l], cmp_pos[l], cmp_w1[l], cmp_b1[l], cmp_w2[l],
                   cmp_b2[l], w_out[l], ln1_g[l], ln1_b[l], router_group[l], router_expert[l],
                   expert_w_gate[l], expert_w_up[l], expert_w_down[l], ln2_g[l], ln2_b[l], lambda_init)
    return x
```

```python
import functools
import math

import numpy as np
import jax
import jax.numpy as jnp
from jax import lax
from jax.experimental import pallas as pl
from jax.experimental.pallas import tpu as pltpu

F32 = jnp.float32
BF16 = jnp.bfloat16

D_MODEL = 2048
HEAD_DIM = 128
ROT_DIM = HEAD_DIM // 4
ROPE_THETA = 500000.0
NEG_INF = -1e30
BIG = 1e30
EPS = 1e-5

DIFF_HEADS = 4
DIFF_VDIM = 2 * HEAD_DIM

NSA_HEADS = 8
NSA_KV_GROUPS = 2
NSA_HPG = NSA_HEADS // NSA_KV_GROUPS
CMP_BLOCK = 32
CMP_STRIDE = 16
CMP_HIDDEN = 256
SLC_BLOCK = 64
SLC_TOPK = 16
WINDOW = 512

N_GROUPS = 4
EXPERTS_PER_GROUP = 8
N_EXPERTS = N_GROUPS * EXPERTS_PER_GROUP
EXPERT_HIDDEN = 512

DEPTH = 1
DN_ALPHA = (2.0 * DEPTH) ** 0.25

LANES = 128
VMEM_LIMIT = 56 * 1024 * 1024

U_NQ, U_DQ, U_DK, U_DV = 0, 8, 16, 24
U_KC, U_VC, U_KS, U_VS, U_KW, U_VW, U_GATE = 32, 34, 36, 38, 40, 42, 44
N_UNITS = 46
ROPE_BLOCKS = tuple(range(U_DQ // 2, U_DV // 2)) + (U_KS // 2, U_KW // 2)


def _dot(a, b):
    return jnp.dot(a, b, preferred_element_type=F32)


def _dot_nt(a, b):
    return lax.dot_general(a, b, (((1,), (1,)), ((), ())), preferred_element_type=F32)


def _params(sem, vmem=VMEM_LIMIT):
    return pltpu.CompilerParams(dimension_semantics=sem, vmem_limit_bytes=vmem)


PROJ_ROW_CHUNK = 512


def _rope(a, c, s):
    lane = lax.broadcasted_iota(jnp.int32, a.shape, 1)
    half = ROT_DIM // 2
    partner = jnp.where(lane < half, pltpu.roll(a, LANES - half, 1), pltpu.roll(a, half, 1))
    return a * c + partner * s


def _proj_kernel(flags_ref, x_ref, w_ref, cos_ref, sin_ref, o_ref, xb_sc):
    j = pl.program_id(1)

    @pl.when(j == 0)
    def _():
        xb_sc[...] = x_ref[...].astype(BF16)

    def block(rotary):
        for r0 in range(0, xb_sc.shape[0], PROJ_ROW_CHUNK):
            rs = slice(r0, r0 + PROJ_ROW_CHUNK)
            acc = _dot(xb_sc[rs, :], w_ref[...])
            if rotary:
                for hh in range(2):
                    cs = slice(hh * LANES, (hh + 1) * LANES)
                    o_ref[rs, cs] = _rope(acc[:, cs], cos_ref[rs, :], sin_ref[rs, :]).astype(o_ref.dtype)
            else:
                o_ref[rs, :] = acc.astype(o_ref.dtype)

    pl.when(flags_ref[j] == 0)(lambda: block(False))
    pl.when(flags_ref[j] != 0)(lambda: block(True))


def _proj_call(xb, w_aug, flags, cos_t, sin_t, seq, tm):
    n, d = xb.shape
    nj = w_aug.shape[1] // 256
    tpb = seq // tm
    grid_spec = pltpu.PrefetchScalarGridSpec(
        num_scalar_prefetch=1,
        grid=(n // tm, nj),
        in_specs=[
            pl.BlockSpec((tm, d), lambda i, j, f: (i, 0)),
            pl.BlockSpec((d, 256), lambda i, j, f: (0, j)),
            pl.BlockSpec((tm, LANES), lambda i, j, f: (i % tpb, 0)),
            pl.BlockSpec((tm, LANES), lambda i, j, f: (i % tpb, 0)),
        ],
        out_specs=pl.BlockSpec((tm, 256), lambda i, j, f: (i, j)),
        scratch_shapes=[pltpu.VMEM((tm, d), BF16)],
    )
    return pl.pallas_call(
        _proj_kernel,
        out_shape=jax.ShapeDtypeStruct((n, w_aug.shape[1]), BF16),
        grid_spec=grid_spec,
        compiler_params=_params(("parallel", "arbitrary")),
        name="proj",
    )(flags, xb, w_aug, cos_t, sin_t)


def _compress_kernel(t_ref, w1_ref, pos_ref, b1_ref, w2_ref, b2_ref, o_ref, t_sc):
    nt = t_sc.shape[0] // CMP_STRIDE
    t_sc[...] = t_ref[...].astype(F32)
    r = jnp.concatenate([t_sc[pl.ds(k, nt, stride=CMP_STRIDE), :] for k in range(CMP_STRIDE)], axis=1).astype(BF16)
    ab = _dot(r, w1_ref[...])
    pos = pos_ref[...]
    pa = jnp.broadcast_to(pos[0:1], (8, pos.shape[1])).astype(BF16)
    pb = jnp.broadcast_to(pos[1:2], (8, pos.shape[1])).astype(BF16)
    const = _dot(pa, w1_ref[:, :CMP_HIDDEN])[0:1] + _dot(pb, w1_ref[:, CMP_HIDDEN:])[0:1]
    h = ab[:, :CMP_HIDDEN] + pltpu.roll(ab[:, CMP_HIDDEN:], nt - 1, 0) + const + b1_ref[...]
    h = jax.nn.gelu(h)
    o_ref[...] = (_dot(h.astype(BF16), w2_ref[...]) + b2_ref[...]).astype(o_ref.dtype)


def _compress_call(proj, w1cat, pos2, b1, w2, b2, b, seq):
    four = 2 * NSA_KV_GROUPS
    nt = seq // CMP_STRIDE
    k = CMP_STRIDE * HEAD_DIM
    return pl.pallas_call(
        _compress_kernel,
        out_shape=jax.ShapeDtypeStruct((b, four, nt, HEAD_DIM), BF16),
        grid=(b, four),
        in_specs=[
            pl.BlockSpec((seq, HEAD_DIM), lambda i, c: (i, U_KC + c)),
            pl.BlockSpec((None, k, 2 * CMP_HIDDEN), lambda i, c: (c // 2, 0, 0)),
            pl.BlockSpec((None, 2, k), lambda i, c: (c // 2, 0, 0)),
            pl.BlockSpec((None, 1, CMP_HIDDEN), lambda i, c: (c // 2, 0, 0)),
            pl.BlockSpec((None, CMP_HIDDEN, HEAD_DIM), lambda i, c: (c // 2, 0, 0)),
            pl.BlockSpec((None, 1, HEAD_DIM), lambda i, c: (c // 2, 0, 0)),
        ],
        out_specs=pl.BlockSpec((None, None, nt, HEAD_DIM), lambda i, c: (i, c, 0, 0)),
        scratch_shapes=[pltpu.VMEM((seq, HEAD_DIM), F32)],
        compiler_params=_params(("parallel", "parallel")),
        name="compress",
    )(proj, w1cat, pos2, b1, w2, b2)


EXP2_SCALE = HEAD_DIM ** -0.5 * math.log2(math.e)
ONES_ROWS = 16


def _transpose_into(src_ref, dst_ref):
    def body(c, carry):
        off = pl.multiple_of(c * LANES, LANES)
        dst_ref[:, pl.ds(off, LANES)] = src_ref[pl.ds(off, LANES), :].astype(F32).T.astype(dst_ref.dtype)
        return carry

    lax.fori_loop(0, src_ref.shape[0] // LANES, body, 0)


def _diff_kernel(dl_ref, q_ref, k_ref, v_ref, g_ref, o_ref, vt_sc, acc_sc, sa_sc, sb_sc, *, tq, lambda_init):
    qi = pl.program_id(2)

    @pl.when(qi == 0)
    def _():
        for c in range(2):
            _transpose_into(v_ref.at[:, c * LANES:(c + 1) * LANES], vt_sc.at[c * LANES:(c + 1) * LANES, :])

    acc_sc[...] = jnp.zeros(acc_sc.shape, F32)
    q = q_ref[...]
    qpos = qi * tq + lax.broadcasted_iota(jnp.int32, (1, tq), 1)

    tk = tq // 2

    def scores(j, dst, lo=0):
        kt = k_ref[pl.ds(pl.multiple_of(j * tk, tk), tk), :]
        for c in range(2):
            dst[c, :, lo:] = _dot_nt(kt[:, c * HEAD_DIM:(c + 1) * HEAD_DIM], q[lo:, c * HEAD_DIM:(c + 1) * HEAD_DIM])

    def absorb(src, j, masked, carry, lo=0):
        off = pl.multiple_of(j * tk, tk)
        vt = vt_sc[:, pl.ds(off, tk)]
        out = []
        for c in range(2):
            m_old, l_old = carry[2 * c][:, lo:], carry[2 * c + 1][:, lo:]
            s = src[c, :, lo:]
            if masked:
                s = jnp.where(lax.broadcasted_iota(jnp.int32, s.shape, 0) <= qpos[:, lo:] - off, s, NEG_INF)
            m_new = jnp.maximum(m_old, jnp.max(s, axis=0, keepdims=True))
            p = jnp.exp2((s - m_new) * EXP2_SCALE)
            alpha = jnp.exp2((m_old - m_new) * EXP2_SCALE)
            l_new = alpha * l_old + jnp.sum(p, axis=0, keepdims=True)
            acc_sc[c, :, lo:] = alpha * acc_sc[c, :, lo:] + _dot(vt, p.astype(BF16))
            if lo:
                m_new = jnp.concatenate([carry[2 * c][:, :lo], m_new], axis=1)
                l_new = jnp.concatenate([carry[2 * c + 1][:, :lo], l_new], axis=1)
            out += [m_new, l_new]
        return tuple(out)

    def pair(p, carry):
        j = 2 * p
        scores(j + 1, sb_sc)
        carry = absorb(sa_sc, j, False, carry)
        scores(j + 2, sa_sc)
        return absorb(sb_sc, j + 1, False, carry)

    init = (jnp.full((1, tq), NEG_INF, F32), jnp.zeros((1, tq), F32)) * 2
    scores(0, sa_sc)
    carry = lax.fori_loop(0, qi, pair, init)
    j_tail = 2 * qi
    scores(j_tail + 1, sb_sc, lo=tk)
    carry = absorb(sa_sc, j_tail, True, carry)
    _, l0, _, l1 = absorb(sb_sc, j_tail + 1, True, carry, lo=tk)

    dl = dl_ref[...]
    lam = (jnp.exp(jnp.sum(dl[0:1] * dl[1:2], axis=1, keepdims=True))
           - jnp.exp(jnp.sum(dl[2:3] * dl[3:4], axis=1, keepdims=True)) + lambda_init)
    o = acc_sc[0] * (1.0 / l0) - lam * (acc_sc[1] * (1.0 / l1))
    o = o * (lax.rsqrt(jnp.mean(o * o, axis=0, keepdims=True) + EPS) * (1.0 - lambda_init))
    for c in range(DIFF_VDIM // LANES):
        for r in range(tq // LANES):
            blk = o[c * LANES:(c + 1) * LANES, r * LANES:(r + 1) * LANES].T
            o_ref[r * LANES:(r + 1) * LANES, c * LANES:(c + 1) * LANES] = (
                blk * g_ref[:, c * LANES:(c + 1) * LANES]).astype(o_ref.dtype)


def _diff_call(proj, dl, g, batch, seq, tq, lambda_init):
    nq = seq // tq
    kern = functools.partial(_diff_kernel, tq=tq, lambda_init=lambda_init)
    return pl.pallas_call(
        kern,
        out_shape=jax.ShapeDtypeStruct((batch * seq, DIFF_HEADS * DIFF_VDIM), BF16),
        grid=(batch, DIFF_HEADS, nq),
        in_specs=[
            pl.BlockSpec((4, HEAD_DIM), lambda b, h, i: (0, 0)),
            pl.BlockSpec((tq, 256), lambda b, h, i: (b * nq + i, U_DQ // 2 + h)),
            pl.BlockSpec((seq, 256), lambda b, h, i: (b, U_DK // 2 + h)),
            pl.BlockSpec((seq, 256), lambda b, h, i: (b, U_DV // 2 + h)),
            pl.BlockSpec((1, DIFF_VDIM), lambda b, h, i: (0, 0)),
        ],
        out_specs=pl.BlockSpec((tq, DIFF_VDIM), lambda b, h, i: (b * nq + i, h)),
        scratch_shapes=[
            pltpu.VMEM((DIFF_VDIM, seq), BF16),
            pltpu.VMEM((2, DIFF_VDIM, tq), F32),
            pltpu.VMEM((2, tq // 2, tq), F32),
            pltpu.VMEM((2, tq // 2, tq), F32),
        ],
        compiler_params=_params(("arbitrary", "arbitrary", "arbitrary")),
        name="diff_attn",
    )(dl, proj, proj, proj, g)


def _stack_heads(x):
    return jnp.concatenate([x[:, h * HEAD_DIM:(h + 1) * HEAD_DIM] for h in range(NSA_HPG)], axis=0)


def _nsa_kernel(q_ref, cos_ref, sin_ref, gate_ref, kc_ref, vc_ref, ks_ref, vs_ref, kw_ref, vw_ref, ovt_ref,
                o_ref, vct_sc, vst_sc, vwt_sc, bias_sc, acc_sc, sa_sc, sb_sc, m_sc, *, tq, tk, seq, top_k):
    qi = pl.program_id(2)
    q0 = qi * tq
    rows = NSA_HPG * tq
    nt = kc_ref.shape[0]
    ns = ovt_ref.shape[0]
    nb = tk // SLC_BLOCK

    @pl.when(qi == 0)
    def _():
        _transpose_into(vc_ref, vct_sc)
        for v_ref, vt_sc in ((vs_ref, vst_sc), (vw_ref, vwt_sc)):
            _transpose_into(v_ref, vt_sc.at[:HEAD_DIM, :])
            vt_sc[HEAD_DIM:, :] = jnp.ones((ONES_ROWS, seq), vt_sc.dtype)

    q = q_ref[...]
    qs = _stack_heads(q)
    qr = jnp.concatenate(
        [_rope(q[:, h * HEAD_DIM:(h + 1) * HEAD_DIM].astype(F32), cos_ref[...], sin_ref[...]).astype(BF16)
         for h in range(NSA_HPG)], axis=0)
    qpos = q0 + (lax.broadcasted_iota(jnp.int32, (1, rows), 1) & (tq - 1))

    def sel_scores(j, dst):
        dst[...] = _dot_nt(ks_ref[pl.ds(pl.multiple_of(j * tk, tk), tk), :], qr)

    sel_scores(0, sa_sc)

    slab = min(WINDOW + tq, seq)
    start = pl.multiple_of(jnp.maximum(q0 - WINDOW, 0), tq)
    s_w = _dot_nt(kw_ref[pl.ds(start, slab), :], qr)
    krow = lax.broadcasted_iota(jnp.int32, (slab, rows), 0)
    newest = qpos - start
    s_w = jnp.where((krow <= newest) & (krow > newest - WINDOW), s_w, NEG_INF)
    p_w = jnp.exp2((s_w - jnp.max(s_w, axis=0, keepdims=True)) * EXP2_SCALE)
    o_w = _dot(vwt_sc[:, pl.ds(start, slab)], p_w.astype(BF16))
    o_w = o_w[:HEAD_DIM] * (1.0 / o_w[HEAD_DIM:HEAD_DIM + 1])

    s_c = _dot_nt(kc_ref[...], qs)
    last_c = jnp.right_shift(qpos - (CMP_BLOCK - 1), int(math.log2(CMP_STRIDE)))
    valid_c = lax.broadcasted_iota(jnp.int32, (nt, rows), 0) <= last_c
    s_c = jnp.where(valid_c, s_c, NEG_INF)
    m_c = jnp.max(s_c, axis=0, keepdims=True)
    e_c = jnp.where(valid_c, jnp.exp2((s_c - m_c) * EXP2_SCALE), 0.0)
    l_c = jnp.sum(e_c, axis=0, keepdims=True)
    p_c = e_c * (1.0 / jnp.where(l_c > 0.0, l_c, 1.0))
    o_c = _dot(vct_sc[...], p_c.astype(BF16))

    p_sum = p_c[:, 0:tq]
    for h in range(1, NSA_HPG):
        p_sum = p_sum + p_c[:, h * tq:(h + 1) * tq]
    p_hi = p_sum.astype(BF16)
    p_lo = (p_sum - p_hi.astype(F32)).astype(BF16)
    imp = _dot(ovt_ref[...], p_hi) + _dot(ovt_ref[...], p_lo)

    blk = lax.broadcasted_iota(jnp.int32, (ns, tq), 0)
    qpos_l = q0 + lax.broadcasted_iota(jnp.int32, (ns, tq), 1)
    cur = lax.shift_right_logical(qpos_l, int(math.log2(SLC_BLOCK)))
    valid_s = blk <= cur
    forced = (blk == 0) | (blk == cur) | (blk == cur - 1)
    work = jnp.where(forced, -jnp.inf, jnp.where(valid_s, imp, NEG_INF))
    sel = jnp.where(forced, 1.0, 0.0)
    blk_f = blk.astype(F32)
    for _ in range(top_k - 3):
        mx = jnp.max(work, axis=0, keepdims=True)
        idx = jnp.min(jnp.where(work == mx, blk_f, float(ns)), axis=0, keepdims=True)
        pick = blk_f == idx
        sel = jnp.where(pick, 1.0, sel)
        work = jnp.where(pick, -jnp.inf, work)
    bias_sc[...] = (sel - 1.0) * BIG

    acc_sc[...] = jnp.zeros(acc_sc.shape, F32)

    def sel_absorb(src, j, causal, m_old):
        off = pl.multiple_of(j * tk, tk)
        s = src[...]
        if causal:
            s = jnp.where(lax.broadcasted_iota(jnp.int32, (tk, rows), 0) <= qpos - off, s, NEG_INF)
        bias = bias_sc[pl.ds(pl.multiple_of(j * nb, nb), nb), :]
        bias = jnp.concatenate([bias] * NSA_HPG, axis=1)
        s3 = s.reshape(nb, SLC_BLOCK, rows) + bias[:, None, :]
        m_new = jnp.maximum(m_old, jnp.max(jnp.max(s3, axis=0), axis=0, keepdims=True))
        p3 = jnp.exp2((s3 - m_new) * EXP2_SCALE)
        alpha = jnp.exp2((m_old - m_new) * EXP2_SCALE)
        p = p3.reshape(tk, rows).astype(BF16)
        acc_sc[...] = alpha * acc_sc[...] + _dot(vst_sc[:, pl.ds(off, tk)], p)
        return m_new

    def sel_pair(p, carry):
        j = 2 * p
        sel_scores(j + 1, sb_sc)
        carry = sel_absorb(sa_sc, j, False, carry)
        sel_scores(j + 2, sa_sc)
        return sel_absorb(sb_sc, j + 1, False, carry)

    n_pairs = (q0 // tk + 2) // 2
    carry = lax.fori_loop(0, n_pairs - 1, sel_pair, jnp.full((1, rows), NEG_INF, F32))
    j_tail = 2 * (n_pairs - 1)
    m_sc[...] = sel_absorb(sa_sc, j_tail, True, carry)

    @pl.when(j_tail + 1 <= q0 // tk)
    def _():
        sel_scores(j_tail + 1, sb_sc)
        sel_absorb(sb_sc, j_tail + 1, True, m_sc[...])

    o_s = acc_sc[:HEAD_DIM, :] * (1.0 / acc_sc[HEAD_DIM:HEAD_DIM + 1, :])

    gates = jax.nn.sigmoid(gate_ref[...].astype(F32))
    gates = jnp.concatenate([gates[r * LANES:(r + 1) * LANES].T for r in range(tq // LANES)],
                            axis=1)
    for h in range(NSA_HPG):
        sl = slice(h * tq, (h + 1) * tq)
        o = (gates[3 * h:3 * h + 1] * o_c[:, sl] + gates[3 * h + 1:3 * h + 2] * o_s[:, sl]
             + gates[3 * h + 2:3 * h + 3] * o_w[:, sl])
        for r in range(tq // LANES):
            o_ref[r * LANES:(r + 1) * LANES, h * HEAD_DIM:(h + 1) * HEAD_DIM] = (
                o[:, r * LANES:(r + 1) * LANES].T.astype(o_ref.dtype))


def _nsa_call(proj, cos_t, sin_t, kvc, ovt, batch, seq, tq, tk):
    nq = seq // tq
    assert (seq // tk) % 2 == 0, "key tiles are processed in pairs"
    nt = kvc.shape[2]
    ns = ovt.shape[0]
    width = NSA_HPG * HEAD_DIM
    kern = functools.partial(_nsa_kernel, tq=tq, tk=tk, seq=seq, top_k=min(SLC_TOPK, ns))

    def col(unit):
        return pl.BlockSpec((seq, HEAD_DIM), lambda b, g, i: (b, unit + g))

    return pl.pallas_call(
        kern,
        out_shape=jax.ShapeDtypeStruct((batch * seq, NSA_HEADS * HEAD_DIM), BF16),
        grid=(batch, NSA_KV_GROUPS, nq),
        in_specs=[
            pl.BlockSpec((tq, width), lambda b, g, i: (b * nq + i, U_NQ // 4 + g)),
            pl.BlockSpec((tq, LANES), lambda b, g, i: (i, 0)),
            pl.BlockSpec((tq, LANES), lambda b, g, i: (i, 0)),
            pl.BlockSpec((tq, LANES), lambda b, g, i: (b * nq + i, U_GATE + g)),
            pl.BlockSpec((None, None, nt, HEAD_DIM), lambda b, g, i: (b, g, 0, 0)),
            pl.BlockSpec((None, None, nt, HEAD_DIM), lambda b, g, i: (b, 2 + g, 0, 0)),
            col(U_KS), col(U_VS), col(U_KW), col(U_VW),
            pl.BlockSpec((ns, nt), lambda b, g, i: (0, 0)),
        ],
        out_specs=pl.BlockSpec((tq, width), lambda b, g, i: (b * nq + i, g)),
        scratch_shapes=[
            pltpu.VMEM((HEAD_DIM, nt), BF16),
            pltpu.VMEM((HEAD_DIM + ONES_ROWS, seq), BF16),
            pltpu.VMEM((HEAD_DIM + ONES_ROWS, seq), BF16),
            pltpu.VMEM((ns, tq), F32),
            pltpu.VMEM((HEAD_DIM + ONES_ROWS, NSA_HPG * tq), F32),
            pltpu.VMEM((tk, NSA_HPG * tq), F32),
            pltpu.VMEM((tk, NSA_HPG * tq), F32),
            pltpu.VMEM((1, NSA_HPG * tq), F32),
        ],
        compiler_params=_params(("arbitrary", "arbitrary", "arbitrary")),
        name="nsa_attn",
    )(proj, cos_t, sin_t, proj, kvc, kvc, proj, proj, proj, proj, ovt)


def _layernorm(y, g, b):
    mu = jnp.mean(y, axis=1, keepdims=True)
    yc = y - mu
    var = jnp.mean(yc * yc, axis=1, keepdims=True)
    return yc * lax.rsqrt(var + EPS) * g + b


N_CHUNKS = D_MODEL // LANES
ROW_PITCH = N_CHUNKS + 1


def _store_chunk_rows(ref, val):
    tm = val.shape[0]
    for k in range(N_CHUNKS):
        ref[pl.ds(k, tm, stride=ROW_PITCH), :] = val[:, k * LANES:(k + 1) * LANES]
    for k in range(N_CHUNKS, ROW_PITCH):
        ref[pl.ds(k, tm, stride=ROW_PITCH), :] = jnp.zeros((tm, LANES), ref.dtype)


def _load_chunk_rows(ref, tm):
    return jnp.concatenate([ref[pl.ds(k, tm, stride=ROW_PITCH), :] for k in range(N_CHUNKS)], axis=1)


OUT_ROW_CHUNK = 256


def _outproj_kernel(od_ref, on_ref, w_ref, x_ref, g_ref, b_ref, wr_ref, x1_ref, route_ref):
    wr = wr_ref[...]
    w_hi = wr.astype(BF16)
    w_lo = (wr - w_hi.astype(F32)).astype(BF16)
    for r0 in range(0, x_ref.shape[0], OUT_ROW_CHUNK):
        _outproj_rows(od_ref, on_ref, w_ref, x_ref, g_ref, b_ref, w_hi, w_lo, x1_ref, route_ref,
                      slice(r0, r0 + OUT_ROW_CHUNK))


def _outproj_rows(od_ref, on_ref, w_ref, x_ref, g_ref, b_ref, w_hi, w_lo, x1_ref, route_ref, rs):
    half = od_ref.shape[1]
    h = _dot(od_ref[rs, :], w_ref[:half, :]) + _dot(on_ref[rs, :], w_ref[half:, :])
    x1 = _layernorm(DN_ALPHA * x_ref[rs, :] + h, g_ref[...], b_ref[...])
    _store_chunk_rows(x1_ref.at[rs.start * ROW_PITCH:rs.stop * ROW_PITCH, :], x1)

    x_hi = x1.astype(BF16)
    x_lo = (x1 - x_hi.astype(F32)).astype(BF16)
    w2 = jnp.concatenate([w_hi, w_lo], axis=1)
    a = _dot(x_hi, w2)
    b = _dot(x_lo, w2)
    logits = (a[:, :LANES] + a[:, LANES:]) + (b[:, :LANES] + b[:, LANES:])

    lane = lax.broadcasted_iota(jnp.int32, logits.shape, 1).astype(F32)
    ninf = -jnp.inf
    gl = jnp.where(lane < N_GROUPS, logits, ninf)
    gmax = jnp.max(gl, axis=1, keepdims=True)
    g_w = 1.0 / jnp.sum(jnp.exp(gl - gmax), axis=1, keepdims=True)
    g_sel = jnp.min(jnp.where(gl == gmax, lane, float(LANES)), axis=1, keepdims=True)
    lo = N_GROUPS + EXPERTS_PER_GROUP * g_sel
    el = jnp.where((lane >= lo) & (lane < lo + EXPERTS_PER_GROUP), logits, ninf)
    e1 = jnp.max(el, axis=1, keepdims=True)
    i1 = jnp.min(jnp.where(el == e1, lane, float(LANES)), axis=1, keepdims=True)
    el2 = jnp.where(lane == i1, ninf, el)
    e2 = jnp.max(el2, axis=1, keepdims=True)
    i2 = jnp.min(jnp.where(el2 == e2, lane, float(LANES)), axis=1, keepdims=True)
    r = jnp.exp(e2 - e1)
    w1 = g_w / (1.0 + r)
    w2 = g_w * r / (1.0 + r)
    route = jnp.where(lane == 0, i1 - N_GROUPS,
                      jnp.where(lane == 1, i2 - N_GROUPS,
                                jnp.where(lane == 2, w1, jnp.where(lane == 3, w2, 0.0))))
    route_ref[rs, :] = route


def _outproj_call(o_diff, o_nsa, w_out, x2, g, b, wr, tm):
    n, d = x2.shape
    half = o_diff.shape[1]
    return pl.pallas_call(
        _outproj_kernel,
        out_shape=(jax.ShapeDtypeStruct((n * ROW_PITCH, LANES), F32), jax.ShapeDtypeStruct((n, LANES), F32)),
        grid=(n // tm,),
        in_specs=[
            pl.BlockSpec((tm, half), lambda i: (i, 0)),
            pl.BlockSpec((tm, half), lambda i: (i, 0)),
            pl.BlockSpec((2 * half, d), lambda i: (0, 0)),
            pl.BlockSpec((tm, d), lambda i: (i, 0)),
            pl.BlockSpec((1, d), lambda i: (0, 0)),
            pl.BlockSpec((1, d), lambda i: (0, 0)),
            pl.BlockSpec((d, LANES), lambda i: (0, 0)),
        ],
        out_specs=(pl.BlockSpec((tm * ROW_PITCH, LANES), lambda i: (i, 0)),
                   pl.BlockSpec((tm, LANES), lambda i: (i, 0))),
        compiler_params=_params(("parallel",)),
        name="outproj_ln1_router",
    )(o_diff, o_nsa, w_out, x2, g, b, wr)


MOE_TILE_ROWS = 384
COMBINE_TILE_TOKENS = 256


def _token_copy(src_hbm, dst, sem, src_tok, dst_tok):
    return pltpu.make_async_copy(src_hbm.at[pl.ds(src_tok * ROW_PITCH, N_CHUNKS)],
                                 dst.at[pl.ds(dst_tok * ROW_PITCH, N_CHUNKS)], sem)


def _wait_tokens(src_hbm, dst, sem, count):
    pltpu.make_async_copy(src_hbm.at[pl.ds(0, count * N_CHUNKS)], dst.at[pl.ds(0, count * N_CHUNKS)], sem).wait()


def _moe_kernel(te_ref, nu_ref, rows_ref, rows_next_ref, x_hbm, wg_ref, wu_ref, wd_ref, o_ref,
                xbuf, sem, wg_sc, wu_sc, wd_sc, *, tm):
    i = pl.program_id(0)
    n_used = nu_ref[0]
    slot = lax.rem(i, 2)

    @pl.when(i == 0)
    def _():
        def body(r, carry):
            _token_copy(x_hbm, xbuf.at[0], sem.at[0], rows_ref[0, 0, r], r).start()
            return carry

        lax.fori_loop(0, tm, body, 0, unroll=8)

    prev = te_ref[jnp.maximum(i - 1, 0)]

    @pl.when((i == 0) | (te_ref[i] != prev))
    def _():
        wg_sc[...] = wg_ref[...].astype(BF16)
        wu_sc[...] = wu_ref[...].astype(BF16)
        wd_sc[...] = wd_ref[...].astype(BF16)

    @pl.when(i < n_used)
    def _():
        def prefetch(part, parts=4):
            for r in range(part * tm // parts, (part + 1) * tm // parts):
                _token_copy(x_hbm, xbuf.at[1 - slot], sem.at[1 - slot], rows_next_ref[0, 0, r], r).start(
                    priority=r % 2)

        _wait_tokens(x_hbm, xbuf.at[slot], sem.at[slot], tm)
        xb = _load_chunk_rows(xbuf.at[slot], tm).astype(BF16)
        prefetch(0)
        gate = _dot(xb, wg_sc[...])
        prefetch(1)
        up = _dot(xb, wu_sc[...])
        prefetch(2)
        h = (gate * jax.nn.sigmoid(gate) * up).astype(BF16)
        y = _dot(h, wd_sc[...])
        prefetch(3)
        _store_chunk_rows(o_ref, y)

    @pl.when(i == n_used)
    def _():
        _wait_tokens(x_hbm, xbuf.at[slot], sem.at[slot], tm)

    @pl.when(i >= n_used)
    def _():
        o_ref[...] = jnp.zeros(o_ref.shape, o_ref.dtype)


def _moe_call(tile_expert, n_used, rows, x1c, w_gate, w_up, w_down, tm):
    n_tiles = rows.shape[0]
    d, f = w_gate.shape[1], w_gate.shape[2]
    grid_spec = pltpu.PrefetchScalarGridSpec(
        num_scalar_prefetch=2,
        grid=(n_tiles,),
        in_specs=[
            pl.BlockSpec((1, 1, tm), lambda i, te, nu: (i, 0, 0), memory_space=pltpu.SMEM),
            pl.BlockSpec((1, 1, tm), lambda i, te, nu: (jnp.minimum(i + 1, n_tiles - 1), 0, 0),
                         memory_space=pltpu.SMEM),
            pl.BlockSpec(memory_space=pl.ANY),
            pl.BlockSpec((None, d, f), lambda i, te, nu: (te[i], 0, 0)),
            pl.BlockSpec((None, d, f), lambda i, te, nu: (te[i], 0, 0)),
            pl.BlockSpec((None, f, d), lambda i, te, nu: (te[i], 0, 0)),
        ],
        out_specs=pl.BlockSpec((tm * ROW_PITCH, LANES), lambda i, te, nu: (i, 0)),
        scratch_shapes=[
            pltpu.VMEM((2, tm * ROW_PITCH, LANES), F32),
            pltpu.SemaphoreType.DMA((2,)),
            pltpu.VMEM((d, f), BF16), pltpu.VMEM((d, f), BF16), pltpu.VMEM((f, d), BF16),
        ],
    )
    return pl.pallas_call(
        functools.partial(_moe_kernel, tm=tm),
        out_shape=jax.ShapeDtypeStruct((n_tiles * tm * ROW_PITCH, LANES), F32),
        grid_spec=grid_spec,
        compiler_params=_params(("arbitrary",)),
        name="moe_experts",
    )(tile_expert, n_used, rows, rows, x1c, w_gate, w_up, w_down)


def _combine_kernel(slots_ref, slots_next_ref, y_hbm, x1_ref, route_ref, g_ref, b_ref, o_ref, buf, sem, *, tm):
    i = pl.program_id(0)
    slot = lax.rem(i, 2)

    @pl.when(i == 0)
    def _():
        def body(r, carry):
            for k in range(2):
                _token_copy(y_hbm, buf.at[0, k], sem.at[0], slots_ref[0, 0, 2 * r + k], r).start()
            return carry

        lax.fori_loop(0, tm, body, 0, unroll=8)

    def tile(prefetch):
        if prefetch:
            for r in range(tm):
                for k in range(2):
                    _token_copy(y_hbm, buf.at[1 - slot, k], sem.at[1 - slot],
                                slots_next_ref[0, 0, 2 * r + k], r).start(priority=k)
        for k in range(2):
            _wait_tokens(y_hbm, buf.at[slot, k], sem.at[slot], tm)
        route = route_ref[...]
        y = (route[:, 2:3] * _load_chunk_rows(buf.at[slot, 0], tm)
             + route[:, 3:4] * _load_chunk_rows(buf.at[slot, 1], tm))
        x1 = _load_chunk_rows(x1_ref, tm)
        o_ref[...] = _layernorm(DN_ALPHA * x1 + y, g_ref[...], b_ref[...])

    last = pl.num_programs(0) - 1
    pl.when(i < last)(lambda: tile(True))
    pl.when(i == last)(lambda: tile(False))


def _combine_call(slots, yc, x1c, route, g, b, tm):
    n = route.shape[0]
    d = D_MODEL
    nt = n // tm
    return pl.pallas_call(
        functools.partial(_combine_kernel, tm=tm),
        out_shape=jax.ShapeDtypeStruct((n, d), F32),
        grid=(nt,),
        in_specs=[
            pl.BlockSpec((1, 1, 2 * tm), lambda i: (i, 0, 0), memory_space=pltpu.SMEM),
            pl.BlockSpec((1, 1, 2 * tm), lambda i: (jnp.minimum(i + 1, nt - 1), 0, 0), memory_space=pltpu.SMEM),
            pl.BlockSpec(memory_space=pl.ANY),
            pl.BlockSpec((tm * ROW_PITCH, LANES), lambda i: (i, 0)),
            pl.BlockSpec((tm, LANES), lambda i: (i, 0)),
            pl.BlockSpec((1, d), lambda i: (0, 0)),
            pl.BlockSpec((1, d), lambda i: (0, 0)),
        ],
        out_specs=pl.BlockSpec((tm, d), lambda i: (i, 0)),
        scratch_shapes=[pltpu.VMEM((2, 2, tm * ROW_PITCH, LANES), F32), pltpu.SemaphoreType.DMA((2,))],
        compiler_params=_params(("arbitrary",)),
        name="moe_combine_ln2",
    )(slots, slots, yc, x1c, route, g, b)


def _moe_plan(e_idx, tm, n_tiles):
    n = e_idx.shape[0]
    flat = e_idx.reshape(-1)
    onehot = (flat[:, None] == jnp.arange(N_EXPERTS, dtype=jnp.int32)[None, :]).astype(jnp.int32)
    csum = jnp.cumsum(onehot, axis=0)
    rank = jnp.take_along_axis(csum, flat[:, None], axis=1)[:, 0] - 1
    counts = csum[-1]
    ptiles = (counts + tm - 1) // tm
    tile_end = jnp.cumsum(ptiles)
    slot = (tile_end - ptiles)[flat] * tm + rank
    rows = jnp.zeros((n_tiles * tm,), jnp.int32).at[slot].set(jnp.arange(2 * n, dtype=jnp.int32) // 2)
    tile_ids = jnp.arange(n_tiles, dtype=jnp.int32)
    tile_expert = jnp.minimum(jnp.sum((tile_end[None, :] <= tile_ids[:, None]).astype(jnp.int32), axis=1),
                              N_EXPERTS - 1)
    return slot.astype(jnp.int32), rows, tile_expert, tile_end[-1:].astype(jnp.int32)


def _rope_tables(seq):
    half = ROT_DIM // 2
    inv_freq = ROPE_THETA ** (-jnp.arange(0, ROT_DIM, 2, dtype=F32) / ROT_DIM)
    ang = jnp.arange(seq, dtype=F32)[:, None] * inv_freq[None, :]
    cos, sin = jnp.cos(ang), jnp.sin(ang)
    pad1 = jnp.ones((seq, HEAD_DIM - ROT_DIM), F32)
    pad0 = jnp.zeros((seq, HEAD_DIM - ROT_DIM), F32)
    return (jnp.concatenate([cos, cos, pad1], axis=1), jnp.concatenate([-sin, sin, pad0], axis=1))


def _layer(x, w_in, diff_lambda, diff_subln_g, cmp_pos, cmp_w1, cmp_b1, cmp_w2, cmp_b2, w_out,
           ln1_g, ln1_b, router_group, router_expert, w_gate, w_up, w_down, ln2_g, ln2_b, lambda_init):
    batch, seq, d = x.shape
    n = batch * seq
    x2 = x.reshape(n, d)

    seg = lambda a, b: w_in[:, a:b]
    pad = lambda w: jnp.pad(w, ((0, 0), (0, LANES - w.shape[1])))
    gate_w = seg(5632, 5656)
    w_aug = jnp.concatenate(
        [seg(3072, 4096), seg(0, 1024), seg(1024, 2048), seg(2048, 3072), seg(4096, 5632),
         pad(gate_w[:, :12]), pad(gate_w[:, 12:])], axis=1).astype(BF16)
    flags = jnp.zeros((N_UNITS // 2,), jnp.int32).at[jnp.array(ROPE_BLOCKS)].set(1)
    cos_t, sin_t = _rope_tables(seq)
    proj = _proj_call(x2, w_aug, flags, cos_t, sin_t, seq, tm=min(2048, seq))

    nt = seq // CMP_STRIDE
    half_feat = CMP_STRIDE * HEAD_DIM
    w1cat = jnp.concatenate([cmp_w1[:, :half_feat], cmp_w1[:, half_feat:]], axis=2).astype(BF16)
    kvc = _compress_call(proj, w1cat, cmp_pos.reshape(2, 2, half_feat), cmp_b1[:, None, :],
                         cmp_w2.astype(BF16), cmp_b2[:, None, :], batch, seq)

    o_diff = _diff_call(proj, diff_lambda, diff_subln_g[None, :], batch, seq, min(1024, seq), lambda_init)

    ns = seq // SLC_BLOCK
    ci = np.arange(nt)[None, :] * CMP_STRIDE
    sj = np.arange(ns)[:, None] * SLC_BLOCK
    ovt = jnp.asarray((ci < sj + SLC_BLOCK) & (ci + CMP_BLOCK > sj) & (np.arange(nt)[None, :] < nt - 1), dtype=BF16)
    o_nsa = _nsa_call(proj, cos_t, sin_t, kvc, ovt, batch, seq, tq=2 * LANES, tk=min(512, seq))

    wr = jnp.pad(jnp.concatenate([router_group, router_expert], axis=1),
                 ((0, 0), (0, LANES - N_GROUPS - N_EXPERTS)))
    x1, route = _outproj_call(o_diff, o_nsa, w_out.astype(BF16), x2, ln1_g[None, :], ln1_b[None, :], wr, tm=512)

    tm = MOE_TILE_ROWS
    n_tiles = -(-2 * n // tm) + N_EXPERTS + 1
    e_idx = route[:, 0:2].astype(jnp.int32)
    slot, rows, tile_expert, n_used = _moe_plan(e_idx, tm, n_tiles)
    ys = _moe_call(tile_expert, n_used, rows.reshape(n_tiles, 1, tm), x1, w_gate, w_up, w_down, tm)
    tc = COMBINE_TILE_TOKENS
    out = _combine_call(slot.reshape(n // tc, 1, 2 * tc), ys, x1, route, ln2_g[None, :], ln2_b[None, :], tc)
    return out.reshape(batch, seq, d)


def kernel(x, w_in, diff_lambda, diff_subln_g, cmp_pos, cmp_w1, cmp_b1, cmp_w2, cmp_b2, w_out, ln1_g, ln1_b,
           router_group, router_expert, expert_w_gate, expert_w_up, expert_w_down, ln2_g, ln2_b):
    for l in range(DEPTH):
        lambda_init = 0.8 - 0.6 * math.exp(-0.3 * l)
        x = _layer(x, w_in[l], diff_lambda[l], diff_subln_g[l], cmp_pos[l], cmp_w1[l], cmp_b1[l], cmp_w2[l],
                   cmp_b2[l], w_out[l], ln1_g[l], ln1_b[l], router_group[l], router_expert[l],
                   expert_w_gate[l], expert_w_up[l], expert_w_down[l], ln2_g[l], ln2_b[l], lambda_init)
    return x
```

```python
import functools
import math

import numpy as np
import jax
import jax.numpy as jnp
from jax import lax
from jax.experimental import pallas as pl
from jax.experimental.pallas import tpu as pltpu

F32 = jnp.float32
BF16 = jnp.bfloat16

D_MODEL = 2048
HEAD_DIM = 128
ROT_DIM = HEAD_DIM // 4
ROPE_THETA = 500000.0
NEG_INF = -1e30
BIG = 1e30
EPS = 1e-5

DIFF_HEADS = 4
DIFF_VDIM = 2 * HEAD_DIM

NSA_HEADS = 8
NSA_KV_GROUPS = 2
NSA_HPG = NSA_HEADS // NSA_KV_GROUPS
CMP_BLOCK = 32
CMP_STRIDE = 16
CMP_HIDDEN = 256
SLC_BLOCK = 64
SLC_TOPK = 16
WINDOW = 512

N_GROUPS = 4
EXPERTS_PER_GROUP = 8
N_EXPERTS = N_GROUPS * EXPERTS_PER_GROUP
EXPERT_HIDDEN = 512

DEPTH = 1
DN_ALPHA = (2.0 * DEPTH) ** 0.25

LANES = 128
VMEM_LIMIT = 56 * 1024 * 1024

U_NQ, U_DQ, U_DK, U_DV = 0, 8, 16, 24
U_KC, U_VC, U_KS, U_VS, U_KW, U_VW, U_GATE = 32, 34, 36, 38, 40, 42, 44
N_UNITS = 46
ROPE_BLOCKS = tuple(range(U_DQ // 2, U_DV // 2)) + (U_KS // 2, U_KW // 2)


def _dot(a, b):
    return jnp.dot(a, b, preferred_element_type=F32)


def _dot_nt(a, b):
    return lax.dot_general(a, b, (((1,), (1,)), ((), ())), preferred_element_type=F32)


def _params(sem, vmem=VMEM_LIMIT):
    return pltpu.CompilerParams(dimension_semantics=sem, vmem_limit_bytes=vmem)


PROJ_ROW_CHUNK = 512


def _rope(a, c, s):
    lane = lax.broadcasted_iota(jnp.int32, a.shape, 1)
    half = ROT_DIM // 2
    partner = jnp.where(lane < half, pltpu.roll(a, LANES - half, 1), pltpu.roll(a, half, 1))
    return a * c + partner * s


def _proj_kernel(flags_ref, x_ref, w_ref, cos_ref, sin_ref, o_ref, xb_sc):
    j = pl.program_id(1)

    @pl.when(j == 0)
    def _():
        xb_sc[...] = x_ref[...].astype(BF16)

    def block(rotary):
        for r0 in range(0, xb_sc.shape[0], PROJ_ROW_CHUNK):
            rs = slice(r0, r0 + PROJ_ROW_CHUNK)
            acc = _dot(xb_sc[rs, :], w_ref[...])
            if rotary:
                for hh in range(2):
                    cs = slice(hh * LANES, (hh + 1) * LANES)
                    o_ref[rs, cs] = _rope(acc[:, cs], cos_ref[rs, :], sin_ref[rs, :]).astype(o_ref.dtype)
            else:
                o_ref[rs, :] = acc.astype(o_ref.dtype)

    pl.when(flags_ref[j] == 0)(lambda: block(False))
    pl.when(flags_ref[j] != 0)(lambda: block(True))


def _proj_call(xb, w_aug, flags, cos_t, sin_t, seq, tm):
    n, d = xb.shape
    nj = w_aug.shape[1] // 256
    tpb = seq // tm
    grid_spec = pltpu.PrefetchScalarGridSpec(
        num_scalar_prefetch=1,
        grid=(n // tm, nj),
        in_specs=[
            pl.BlockSpec((tm, d), lambda i, j, f: (i, 0)),
            pl.BlockSpec((d, 256), lambda i, j, f: (0, j)),
            pl.BlockSpec((tm, LANES), lambda i, j, f: (i % tpb, 0)),
            pl.BlockSpec((tm, LANES), lambda i, j, f: (i % tpb, 0)),
        ],
        out_specs=pl.BlockSpec((tm, 256), lambda i, j, f: (i, j)),
        scratch_shapes=[pltpu.VMEM((tm, d), BF16)],
    )
    return pl.pallas_call(
        _proj_kernel,
        out_shape=jax.ShapeDtypeStruct((n, w_aug.shape[1]), BF16),
        grid_spec=grid_spec,
        compiler_params=_params(("parallel", "arbitrary")),
        name="proj",
    )(flags, xb, w_aug, cos_t, sin_t)


def _compress_kernel(t_ref, w1_ref, pos_ref, b1_ref, w2_ref, b2_ref, o_ref, t_sc):
    nt = t_sc.shape[0] // CMP_STRIDE
    t_sc[...] = t_ref[...].astype(F32)
    r = jnp.concatenate([t_sc[pl.ds(k, nt, stride=CMP_STRIDE), :] for k in range(CMP_STRIDE)], axis=1).astype(BF16)
    ab = _dot(r, w1_ref[...])
    pos = pos_ref[...]
    pa = jnp.broadcast_to(pos[0:1], (8, pos.shape[1])).astype(BF16)
    pb = jnp.broadcast_to(pos[1:2], (8, pos.shape[1])).astype(BF16)
    const = _dot(pa, w1_ref[:, :CMP_HIDDEN])[0:1] + _dot(pb, w1_ref[:, CMP_HIDDEN:])[0:1]
    h = ab[:, :CMP_HIDDEN] + pltpu.roll(ab[:, CMP_HIDDEN:], nt - 1, 0) + const + b1_ref[...]
    h = jax.nn.gelu(h)
    o_ref[...] = (_dot(h.astype(BF16), w2_ref[...]) + b2_ref[...]).astype(o_ref.dtype)


def _compress_call(proj, w1cat, pos2, b1, w2, b2, b, seq):
    four = 2 * NSA_KV_GROUPS
    nt = seq // CMP_STRIDE
    k = CMP_STRIDE * HEAD_DIM
    return pl.pallas_call(
        _compress_kernel,
        out_shape=jax.ShapeDtypeStruct((b, four, nt, HEAD_DIM), BF16),
        grid=(b, four),
        in_specs=[
            pl.BlockSpec((seq, HEAD_DIM), lambda i, c: (i, U_KC + c)),
            pl.BlockSpec((None, k, 2 * CMP_HIDDEN), lambda i, c: (c // 2, 0, 0)),
            pl.BlockSpec((None, 2, k), lambda i, c: (c // 2, 0, 0)),
            pl.BlockSpec((None, 1, CMP_HIDDEN), lambda i, c: (c // 2, 0, 0)),
            pl.BlockSpec((None, CMP_HIDDEN, HEAD_DIM), lambda i, c: (c // 2, 0, 0)),
            pl.BlockSpec((None, 1, HEAD_DIM), lambda i, c: (c // 2, 0, 0)),
        ],
        out_specs=pl.BlockSpec((None, None, nt, HEAD_DIM), lambda i, c: (i, c, 0, 0)),
        scratch_shapes=[pltpu.VMEM((seq, HEAD_DIM), F32)],
        compiler_params=_params(("parallel", "parallel")),
        name="compress",
    )(proj, w1cat, pos2, b1, w2, b2)


EXP2_SCALE = HEAD_DIM ** -0.5 * math.log2(math.e)
ONES_ROWS = 16


def _transpose_into(src_ref, dst_ref):
    def body(c, carry):
        off = pl.multiple_of(c * LANES, LANES)
        dst_ref[:, pl.ds(off, LANES)] = src_ref[pl.ds(off, LANES), :].astype(F32).T.astype(dst_ref.dtype)
        return carry

    lax.fori_loop(0, src_ref.shape[0] // LANES, body, 0)


def _diff_kernel(dl_ref, q_ref, k_ref, v_ref, g_ref, o_ref, vt_sc, acc_sc, sa_sc, sb_sc, *, tq, lambda_init):
    qi = pl.program_id(2)

    @pl.when(qi == 0)
    def _():
        for c in range(2):
            _transpose_into(v_ref.at[:, c * LANES:(c + 1) * LANES], vt_sc.at[c * LANES:(c + 1) * LANES, :])

    acc_sc[...] = jnp.zeros(acc_sc.shape, F32)
    q = q_ref[...]
    qpos = qi * tq + lax.broadcasted_iota(jnp.int32, (1, tq), 1)

    tk = tq // 2

    def scores(j, dst, lo=0):
        kt = k_ref[pl.ds(pl.multiple_of(j * tk, tk), tk), :]
        for c in range(2):
            dst[c, :, lo:] = _dot_nt(kt[:, c * HEAD_DIM:(c + 1) * HEAD_DIM], q[lo:, c * HEAD_DIM:(c + 1) * HEAD_DIM])

    def absorb(src, j, masked, carry, lo=0):
        off = pl.multiple_of(j * tk, tk)
        vt = vt_sc[:, pl.ds(off, tk)]
        out = []
        for c in range(2):
            m_old, l_old = carry[2 * c][:, lo:], carry[2 * c + 1][:, lo:]
            s = src[c, :, lo:]
            if masked:
                s = jnp.where(lax.broadcasted_iota(jnp.int32, s.shape, 0) <= qpos[:, lo:] - off, s, NEG_INF)
            m_new = jnp.maximum(m_old, jnp.max(s, axis=0, keepdims=True))
            p = jnp.exp2((s - m_new) * EXP2_SCALE)
            alpha = jnp.exp2((m_old - m_new) * EXP2_SCALE)
            l_new = alpha * l_old + jnp.sum(p, axis=0, keepdims=True)
            acc_sc[c, :, lo:] = alpha * acc_sc[c, :, lo:] + _dot(vt, p.astype(BF16))
            if lo:
                m_new = jnp.concatenate([carry[2 * c][:, :lo], m_new], axis=1)
                l_new = jnp.concatenate([carry[2 * c + 1][:, :lo], l_new], axis=1)
            out += [m_new, l_new]
        return tuple(out)

    def pair(p, carry):
        j = 2 * p
        scores(j + 1, sb_sc)
        carry = absorb(sa_sc, j, False, carry)
        scores(j + 2, sa_sc)
        return absorb(sb_sc, j + 1, False, carry)

    init = (jnp.full((1, tq), NEG_INF, F32), jnp.zeros((1, tq), F32)) * 2
    scores(0, sa_sc)
    carry = lax.fori_loop(0, qi, pair, init)
    j_tail = 2 * qi
    scores(j_tail + 1, sb_sc, lo=tk)
    carry = absorb(sa_sc, j_tail, True, carry)
    _, l0, _, l1 = absorb(sb_sc, j_tail + 1, True, carry, lo=tk)

    dl = dl_ref[...]
    lam = (jnp.exp(jnp.sum(dl[0:1] * dl[1:2], axis=1, keepdims=True))
           - jnp.exp(jnp.sum(dl[2:3] * dl[3:4], axis=1, keepdims=True)) + lambda_init)
    o = acc_sc[0] * (1.0 / l0) - lam * (acc_sc[1] * (1.0 / l1))
    o = o * (lax.rsqrt(jnp.mean(o * o, axis=0, keepdims=True) + EPS) * (1.0 - lambda_init))
    for c in range(DIFF_VDIM // LANES):
        for r in range(tq // LANES):
            blk = o[c * LANES:(c + 1) * LANES, r * LANES:(r + 1) * LANES].T
            o_ref[r * LANES:(r + 1) * LANES, c * LANES:(c + 1) * LANES] = (
                blk * g_ref[:, c * LANES:(c + 1) * LANES]).astype(o_ref.dtype)


def _diff_call(proj, dl, g, batch, seq, tq, lambda_init):
    nq = seq // tq
    kern = functools.partial(_diff_kernel, tq=tq, lambda_init=lambda_init)
    return pl.pallas_call(
        kern,
        out_shape=jax.ShapeDtypeStruct((batch * seq, DIFF_HEADS * DIFF_VDIM), BF16),
        grid=(batch, DIFF_HEADS, nq),
        in_specs=[
            pl.BlockSpec((4, HEAD_DIM), lambda b, h, i: (0, 0)),
            pl.BlockSpec((tq, 256), lambda b, h, i: (b * nq + i, U_DQ // 2 + h)),
            pl.BlockSpec((seq, 256), lambda b, h, i: (b, U_DK // 2 + h)),
            pl.BlockSpec((seq, 256), lambda b, h, i: (b, U_DV // 2 + h)),
            pl.BlockSpec((1, DIFF_VDIM), lambda b, h, i: (0, 0)),
        ],
        out_specs=pl.BlockSpec((tq, DIFF_VDIM), lambda b, h, i: (b * nq + i, h)),
        scratch_shapes=[
            pltpu.VMEM((DIFF_VDIM, seq), BF16),
            pltpu.VMEM((2, DIFF_VDIM, tq), F32),
            pltpu.VMEM((2, tq // 2, tq), F32),
            pltpu.VMEM((2, tq // 2, tq), F32),
        ],
        compiler_params=_params(("arbitrary", "arbitrary", "arbitrary")),
        name="diff_attn",
    )(dl, proj, proj, proj, g)


def _stack_heads(x):
    return jnp.concatenate([x[:, h * HEAD_DIM:(h + 1) * HEAD_DIM] for h in range(NSA_HPG)], axis=0)


def _heads(x):
    return jnp.concatenate([x] * NSA_HPG, axis=1)


def _mask_tables(seq, tq, tk):
    nt = seq // CMP_STRIDE
    slab = min(WINDOW + tq, seq)
    q = np.arange(tq)[None, :]
    k = np.arange(slab)[:, None]
    win = []
    for i in range(WINDOW // tq + 1):
        newest = q + (i * tq if i < WINDOW // tq else WINDOW)
        win.append((k <= newest) & (k > newest - WINDOW))
    u = np.arange(2 * nt)[:, None] - nt
    cmp_ok = u <= ((q - (CMP_BLOCK - 1)) >> int(math.log2(CMP_STRIDE)))
    kk = np.arange(tk)[:, None]
    causal = [kk <= d * tq + q for d in range(tk // tq)] + [np.ones((tk, tq), bool)]
    to_bias = lambda m: jnp.asarray(np.where(np.asarray(m), 0.0, NEG_INF), dtype=F32)
    return to_bias(np.stack(win)), to_bias(cmp_ok), to_bias(np.stack(causal))


def _nsa_kernel(q_ref, cos_ref, sin_ref, gate_ref, kc_ref, vc_ref, ks_ref, vs_ref, kw_ref, vw_ref, ovt_ref,
                wtab_ref, ptab_ref, ctab_ref, o_ref, vct_sc, vst_sc, vwt_sc, bias_sc, acc_sc, sa_sc, sb_sc, m_sc, *, tq, tk, seq, top_k):
    qi = pl.program_id(2)
    q0 = qi * tq
    rows = NSA_HPG * tq
    nt = kc_ref.shape[0]
    ns = ovt_ref.shape[0]
    nb = tk // SLC_BLOCK

    @pl.when(qi == 0)
    def _():
        _transpose_into(vc_ref, vct_sc)
        for v_ref, vt_sc in ((vs_ref, vst_sc), (vw_ref, vwt_sc)):
            _transpose_into(v_ref, vt_sc.at[:HEAD_DIM, :])
            vt_sc[HEAD_DIM:, :] = jnp.ones((ONES_ROWS, seq), vt_sc.dtype)

    q = q_ref[...]
    qs = _stack_heads(q)
    qr = jnp.concatenate(
        [_rope(q[:, h * HEAD_DIM:(h + 1) * HEAD_DIM].astype(F32), cos_ref[...], sin_ref[...]).astype(BF16)
         for h in range(NSA_HPG)], axis=0)
    qpos = q0 + (lax.broadcasted_iota(jnp.int32, (1, rows), 1) & (tq - 1))

    def sel_scores(j, dst):
        dst[...] = _dot_nt(ks_ref[pl.ds(pl.multiple_of(j * tk, tk), tk), :], qr)

    sel_scores(0, sa_sc)

    slab = min(WINDOW + tq, seq)
    start = pl.multiple_of(jnp.maximum(q0 - WINDOW, 0), tq)
    s_w = _dot_nt(kw_ref[pl.ds(start, slab), :], qr)
    s_w = s_w + _heads(wtab_ref[jnp.minimum(qi, WINDOW // tq)])
    p_w = jnp.exp2((s_w - jnp.max(s_w, axis=0, keepdims=True)) * EXP2_SCALE)
    o_w = _dot(vwt_sc[:, pl.ds(start, slab)], p_w.astype(BF16))
    o_w = o_w[:HEAD_DIM] * (1.0 / o_w[HEAD_DIM:HEAD_DIM + 1])

    s_c = _dot_nt(kc_ref[...], qs)
    c0 = pl.multiple_of(nt - q0 // CMP_STRIDE, CMP_STRIDE)
    s_c = s_c + _heads(ptab_ref[pl.ds(c0, nt), :])
    m_c = jnp.max(s_c, axis=0, keepdims=True)
    e_c = jnp.exp2((s_c - m_c) * EXP2_SCALE)
    l_c = jnp.sum(e_c, axis=0, keepdims=True)
    p_c = e_c * jnp.where(qpos >= CMP_BLOCK - 1, 1.0 / l_c, 0.0)
    o_c = _dot(vct_sc[...], p_c.astype(BF16))

    p_sum = p_c[:, 0:tq]
    for h in range(1, NSA_HPG):
        p_sum = p_sum + p_c[:, h * tq:(h + 1) * tq]
    p_hi = p_sum.astype(BF16)
    p_lo = (p_sum - p_hi.astype(F32)).astype(BF16)
    imp = _dot(ovt_ref[...], p_hi) + _dot(ovt_ref[...], p_lo)

    blk = lax.broadcasted_iota(jnp.int32, (ns, tq), 0)
    qpos_l = q0 + lax.broadcasted_iota(jnp.int32, (ns, tq), 1)
    cur = lax.shift_right_logical(qpos_l, int(math.log2(SLC_BLOCK)))
    valid_s = blk <= cur
    forced = (blk == 0) | (blk == cur) | (blk == cur - 1)
    work = jnp.where(forced, -jnp.inf, jnp.where(valid_s, imp, NEG_INF))
    blk_f = blk.astype(F32)
    for _ in range(top_k - 3):
        mx = jnp.max(work, axis=0, keepdims=True)
        idx = jnp.min(jnp.where(work == mx, blk_f, float(ns)), axis=0, keepdims=True)
        work = jnp.where(blk_f == idx, -jnp.inf, work)
    bias_sc[...] = jnp.where(work == -jnp.inf, 0.0, NEG_INF)

    acc_sc[...] = jnp.zeros(acc_sc.shape, F32)

    def sel_absorb(src, j, causal, m_old):
        off = pl.multiple_of(j * tk, tk)
        s = src[...]
        if causal:
            s = s + _heads(ctab_ref[jnp.clip((q0 - off) // tq, 0, tk // tq)])
        bias = bias_sc[pl.ds(pl.multiple_of(j * nb, nb), nb), :]
        bias = jnp.concatenate([bias] * NSA_HPG, axis=1)
        s3 = s.reshape(nb, SLC_BLOCK, rows) + bias[:, None, :]
        m_new = jnp.maximum(m_old, jnp.max(jnp.max(s3, axis=0), axis=0, keepdims=True))
        p3 = jnp.exp2((s3 - m_new) * EXP2_SCALE)
        alpha = jnp.exp2((m_old - m_new) * EXP2_SCALE)
        p = p3.reshape(tk, rows).astype(BF16)
        acc_sc[...] = alpha * acc_sc[...] + _dot(vst_sc[:, pl.ds(off, tk)], p)
        return m_new

    def sel_pair(p, carry):
        j = 2 * p
        sel_scores(j + 1, sb_sc)
        carry = sel_absorb(sa_sc, j, False, carry)
        sel_scores(j + 2, sa_sc)
        return sel_absorb(sb_sc, j + 1, False, carry)

    n_pairs = (q0 // tk + 2) // 2
    carry = lax.fori_loop(0, n_pairs - 1, sel_pair, jnp.full((1, rows), NEG_INF, F32))
    j_tail = 2 * (n_pairs - 1)
    m_sc[...] = sel_absorb(sa_sc, j_tail, True, carry)

    @pl.when(j_tail + 1 <= q0 // tk)
    def _():
        sel_scores(j_tail + 1, sb_sc)
        sel_absorb(sb_sc, j_tail + 1, True, m_sc[...])

    o_s = acc_sc[:HEAD_DIM, :] * (1.0 / acc_sc[HEAD_DIM:HEAD_DIM + 1, :])

    gates = jax.nn.sigmoid(gate_ref[...].astype(F32))
    gates = jnp.concatenate([gates[r * LANES:(r + 1) * LANES].T for r in range(tq // LANES)],
                            axis=1)
    for h in range(NSA_HPG):
        sl = slice(h * tq, (h + 1) * tq)
        o = (gates[3 * h:3 * h + 1] * o_c[:, sl] + gates[3 * h + 1:3 * h + 2] * o_s[:, sl]
             + gates[3 * h + 2:3 * h + 3] * o_w[:, sl])
        for r in range(tq // LANES):
            o_ref[r * LANES:(r + 1) * LANES, h * HEAD_DIM:(h + 1) * HEAD_DIM] = (
                o[:, r * LANES:(r + 1) * LANES].T.astype(o_ref.dtype))


def _nsa_call(proj, cos_t, sin_t, kvc, ovt, batch, seq, tq, tk):
    nq = seq // tq
    assert WINDOW % tq == 0 and tk % tq == 0 and tq % CMP_STRIDE == 0
    wtab, ptab, ctab = _mask_tables(seq, tq, tk)
    nt = kvc.shape[2]
    ns = ovt.shape[0]
    width = NSA_HPG * HEAD_DIM
    kern = functools.partial(_nsa_kernel, tq=tq, tk=tk, seq=seq, top_k=min(SLC_TOPK, ns))

    def col(unit):
        return pl.BlockSpec((seq, HEAD_DIM), lambda b, g, i: (b, unit + g))

    return pl.pallas_call(
        kern,
        out_shape=jax.ShapeDtypeStruct((batch * seq, NSA_HEADS * HEAD_DIM), BF16),
        grid=(batch, NSA_KV_GROUPS, nq),
        in_specs=[
            pl.BlockSpec((tq, width), lambda b, g, i: (b * nq + i, U_NQ // 4 + g)),
            pl.BlockSpec((tq, LANES), lambda b, g, i: (i, 0)),
            pl.BlockSpec((tq, LANES), lambda b, g, i: (i, 0)),
            pl.BlockSpec((tq, LANES), lambda b, g, i: (b * nq + i, U_GATE + g)),
            pl.BlockSpec((None, None, nt, HEAD_DIM), lambda b, g, i: (b, g, 0, 0)),
            pl.BlockSpec((None, None, nt, HEAD_DIM), lambda b, g, i: (b, 2 + g, 0, 0)),
            col(U_KS), col(U_VS), col(U_KW), col(U_VW),
            pl.BlockSpec((ns, nt), lambda b, g, i: (0, 0)),
            pl.BlockSpec(wtab.shape, lambda b, g, i: (0, 0, 0)),
            pl.BlockSpec(ptab.shape, lambda b, g, i: (0, 0)),
            pl.BlockSpec(ctab.shape, lambda b, g, i: (0, 0, 0)),
        ],
        out_specs=pl.BlockSpec((tq, width), lambda b, g, i: (b * nq + i, g)),
        scratch_shapes=[
            pltpu.VMEM((HEAD_DIM, nt), BF16),
            pltpu.VMEM((HEAD_DIM + ONES_ROWS, seq), BF16),
            pltpu.VMEM((HEAD_DIM + ONES_ROWS, seq), BF16),
            pltpu.VMEM((ns, tq), F32),
            pltpu.VMEM((HEAD_DIM + ONES_ROWS, NSA_HPG * tq), F32),
            pltpu.VMEM((tk, NSA_HPG * tq), F32),
            pltpu.VMEM((tk, NSA_HPG * tq), F32),
            pltpu.VMEM((1, NSA_HPG * tq), F32),
        ],
        compiler_params=_params(("arbitrary", "arbitrary", "arbitrary")),
        name="nsa_attn",
    )(proj, cos_t, sin_t, proj, kvc, kvc, proj, proj, proj, proj, ovt, wtab, ptab, ctab)


def _layernorm(y, g, b):
    mu = jnp.mean(y, axis=1, keepdims=True)
    yc = y - mu
    var = jnp.mean(yc * yc, axis=1, keepdims=True)
    return yc * lax.rsqrt(var + EPS) * g + b


N_CHUNKS = D_MODEL // LANES
ROW_PITCH = N_CHUNKS + 1


def _store_chunk_rows(ref, val):
    tm = val.shape[0]
    for k in range(N_CHUNKS):
        ref[pl.ds(k, tm, stride=ROW_PITCH), :] = val[:, k * LANES:(k + 1) * LANES]
    for k in range(N_CHUNKS, ROW_PITCH):
        ref[pl.ds(k, tm, stride=ROW_PITCH), :] = jnp.zeros((tm, LANES), ref.dtype)


def _load_chunk_rows(ref, tm, tok0=0):
    return jnp.concatenate([ref[pl.ds(tok0 * ROW_PITCH + k, tm, stride=ROW_PITCH), :] for k in range(N_CHUNKS)],
                           axis=1)


OUT_ROW_CHUNK = 256


def _outproj_kernel(od_ref, on_ref, w_ref, x_ref, g_ref, b_ref, wr_ref, x1_ref, route_ref):
    wr = wr_ref[...]
    w_hi = wr.astype(BF16)
    w_lo = (wr - w_hi.astype(F32)).astype(BF16)
    for r0 in range(0, x_ref.shape[0], OUT_ROW_CHUNK):
        _outproj_rows(od_ref, on_ref, w_ref, x_ref, g_ref, b_ref, w_hi, w_lo, x1_ref, route_ref,
                      slice(r0, r0 + OUT_ROW_CHUNK))


def _outproj_rows(od_ref, on_ref, w_ref, x_ref, g_ref, b_ref, w_hi, w_lo, x1_ref, route_ref, rs):
    half = od_ref.shape[1]
    h = _dot(od_ref[rs, :], w_ref[:half, :]) + _dot(on_ref[rs, :], w_ref[half:, :])
    x1 = _layernorm(DN_ALPHA * x_ref[rs, :] + h, g_ref[...], b_ref[...])
    _store_chunk_rows(x1_ref.at[rs.start * ROW_PITCH:rs.stop * ROW_PITCH, :], x1)

    x_hi = x1.astype(BF16)
    x_lo = (x1 - x_hi.astype(F32)).astype(BF16)
    w2 = jnp.concatenate([w_hi, w_lo], axis=1)
    a = _dot(x_hi, w2)
    b = _dot(x_lo, w2)
    logits = (a[:, :LANES] + a[:, LANES:]) + (b[:, :LANES] + b[:, LANES:])

    lane = lax.broadcasted_iota(jnp.int32, logits.shape, 1).astype(F32)
    ninf = -jnp.inf
    gl = jnp.where(lane < N_GROUPS, logits, ninf)
    gmax = jnp.max(gl, axis=1, keepdims=True)
    g_w = 1.0 / jnp.sum(jnp.exp(gl - gmax), axis=1, keepdims=True)
    g_sel = jnp.min(jnp.where(gl == gmax, lane, float(LANES)), axis=1, keepdims=True)
    lo = N_GROUPS + EXPERTS_PER_GROUP * g_sel
    el = jnp.where((lane >= lo) & (lane < lo + EXPERTS_PER_GROUP), logits, ninf)
    e1 = jnp.max(el, axis=1, keepdims=True)
    i1 = jnp.min(jnp.where(el == e1, lane, float(LANES)), axis=1, keepdims=True)
    el2 = jnp.where(lane == i1, ninf, el)
    e2 = jnp.max(el2, axis=1, keepdims=True)
    i2 = jnp.min(jnp.where(el2 == e2, lane, float(LANES)), axis=1, keepdims=True)
    r = jnp.exp(e2 - e1)
    w1 = g_w / (1.0 + r)
    w2 = g_w * r / (1.0 + r)
    route = jnp.where(lane == 0, i1 - N_GROUPS,
                      jnp.where(lane == 1, i2 - N_GROUPS,
                                jnp.where(lane == 2, w1, jnp.where(lane == 3, w2, 0.0))))
    route_ref[rs, :] = route


def _outproj_call(o_diff, o_nsa, w_out, x2, g, b, wr, tm):
    n, d = x2.shape
    half = o_diff.shape[1]
    return pl.pallas_call(
        _outproj_kernel,
        out_shape=(jax.ShapeDtypeStruct((n * ROW_PITCH, LANES), F32), jax.ShapeDtypeStruct((n, LANES), F32)),
        grid=(n // tm,),
        in_specs=[
            pl.BlockSpec((tm, half), lambda i: (i, 0)),
            pl.BlockSpec((tm, half), lambda i: (i, 0)),
            pl.BlockSpec((2 * half, d), lambda i: (0, 0)),
            pl.BlockSpec((tm, d), lambda i: (i, 0)),
            pl.BlockSpec((1, d), lambda i: (0, 0)),
            pl.BlockSpec((1, d), lambda i: (0, 0)),
            pl.BlockSpec((d, LANES), lambda i: (0, 0)),
        ],
        out_specs=(pl.BlockSpec((tm * ROW_PITCH, LANES), lambda i: (i, 0)),
                   pl.BlockSpec((tm, LANES), lambda i: (i, 0))),
        compiler_params=_params(("parallel",)),
        name="outproj_ln1_router",
    )(o_diff, o_nsa, w_out, x2, g, b, wr)


MOE_TILE_ROWS = 384
COMBINE_TILE_TOKENS = 256
COMBINE_ROW_CHUNK = 32


def _token_copy(src_hbm, dst, sem, src_tok, dst_tok):
    return pltpu.make_async_copy(src_hbm.at[pl.ds(src_tok * ROW_PITCH, N_CHUNKS)],
                                 dst.at[pl.ds(dst_tok * ROW_PITCH, N_CHUNKS)], sem)


def _wait_tokens(src_hbm, dst, sem, count):
    pltpu.make_async_copy(src_hbm.at[pl.ds(0, count * N_CHUNKS)], dst.at[pl.ds(0, count * N_CHUNKS)], sem).wait()


def _moe_kernel(te_ref, nu_ref, rows_ref, rows_next_ref, x_hbm, wg_ref, wu_ref, wd_ref, o_ref,
                xbuf, sem, wg_sc, wu_sc, wd_sc, *, tm):
    i = pl.program_id(0)
    n_used = nu_ref[0]
    slot = lax.rem(i, 2)

    @pl.when(i == 0)
    def _():
        def body(r, carry):
            _token_copy(x_hbm, xbuf.at[0], sem.at[0], rows_ref[0, 0, r], r).start()
            return carry

        lax.fori_loop(0, tm, body, 0, unroll=8)

    prev = te_ref[jnp.maximum(i - 1, 0)]

    @pl.when((i == 0) | (te_ref[i] != prev))
    def _():
        wg_sc[...] = wg_ref[...].astype(BF16)
        wu_sc[...] = wu_ref[...].astype(BF16)
        wd_sc[...] = wd_ref[...].astype(BF16)

    @pl.when(i < n_used)
    def _():
        def prefetch(part, parts=4):
            for r in range(part * tm // parts, (part + 1) * tm // parts):
                _token_copy(x_hbm, xbuf.at[1 - slot], sem.at[1 - slot], rows_next_ref[0, 0, r], r).start(
                    priority=r % 2)

        _wait_tokens(x_hbm, xbuf.at[slot], sem.at[slot], tm)
        xb = _load_chunk_rows(xbuf.at[slot], tm).astype(BF16)
        prefetch(0)
        gate = _dot(xb, wg_sc[...])
        prefetch(1)
        up = _dot(xb, wu_sc[...])
        prefetch(2)
        h = (gate * jax.nn.sigmoid(gate) * up).astype(BF16)
        y = _dot(h, wd_sc[...])
        prefetch(3)
        _store_chunk_rows(o_ref, y)

    @pl.when(i == n_used)
    def _():
        _wait_tokens(x_hbm, xbuf.at[slot], sem.at[slot], tm)

    @pl.when(i >= n_used)
    def _():
        o_ref[...] = jnp.zeros(o_ref.shape, o_ref.dtype)


def _moe_call(tile_expert, n_used, rows, x1c, w_gate, w_up, w_down, tm):
    n_tiles = rows.shape[0]
    d, f = w_gate.shape[1], w_gate.shape[2]
    grid_spec = pltpu.PrefetchScalarGridSpec(
        num_scalar_prefetch=2,
        grid=(n_tiles,),
        in_specs=[
            pl.BlockSpec((1, 1, tm), lambda i, te, nu: (i, 0, 0), memory_space=pltpu.SMEM),
            pl.BlockSpec((1, 1, tm), lambda i, te, nu: (jnp.minimum(i + 1, n_tiles - 1), 0, 0),
                         memory_space=pltpu.SMEM),
            pl.BlockSpec(memory_space=pl.ANY),
            pl.BlockSpec((None, d, f), lambda i, te, nu: (te[i], 0, 0)),
            pl.BlockSpec((None, d, f), lambda i, te, nu: (te[i], 0, 0)),
            pl.BlockSpec((None, f, d), lambda i, te, nu: (te[i], 0, 0)),
        ],
        out_specs=pl.BlockSpec((tm * ROW_PITCH, LANES), lambda i, te, nu: (i, 0)),
        scratch_shapes=[
            pltpu.VMEM((2, tm * ROW_PITCH, LANES), F32),
            pltpu.SemaphoreType.DMA((2,)),
            pltpu.VMEM((d, f), BF16), pltpu.VMEM((d, f), BF16), pltpu.VMEM((f, d), BF16),
        ],
    )
    return pl.pallas_call(
        functools.partial(_moe_kernel, tm=tm),
        out_shape=jax.ShapeDtypeStruct((n_tiles * tm * ROW_PITCH, LANES), F32),
        grid_spec=grid_spec,
        compiler_params=_params(("arbitrary",)),
        name="moe_experts",
    )(tile_expert, n_used, rows, rows, x1c, w_gate, w_up, w_down)


def _combine_kernel(slots_ref, slots_next_ref, y_hbm, x1_ref, route_ref, g_ref, b_ref, o_ref, buf, sem, *, tm):
    i = pl.program_id(0)
    slot = lax.rem(i, 2)

    @pl.when(i == 0)
    def _():
        def body(r, carry):
            for k in range(2):
                _token_copy(y_hbm, buf.at[0, k], sem.at[0], slots_ref[0, 0, 2 * r + k], r).start()
            return carry

        lax.fori_loop(0, tm, body, 0, unroll=8)

    def tile(prefetch):
        for k in range(2):
            _wait_tokens(y_hbm, buf.at[slot, k], sem.at[slot], tm)
        for t0 in range(0, tm, COMBINE_ROW_CHUNK):
            rs = slice(t0, t0 + COMBINE_ROW_CHUNK)
            route = route_ref[rs, :]
            y = (route[:, 2:3] * _load_chunk_rows(buf.at[slot, 0], COMBINE_ROW_CHUNK, t0)
                 + route[:, 3:4] * _load_chunk_rows(buf.at[slot, 1], COMBINE_ROW_CHUNK, t0))
            x1 = _load_chunk_rows(x1_ref, COMBINE_ROW_CHUNK, t0)
            o_ref[rs, :] = _layernorm(DN_ALPHA * x1 + y, g_ref[...], b_ref[...])
            if prefetch:
                for r in range(t0, t0 + COMBINE_ROW_CHUNK):
                    for k in range(2):
                        _token_copy(y_hbm, buf.at[1 - slot, k], sem.at[1 - slot],
                                    slots_next_ref[0, 0, 2 * r + k], r).start(priority=k)

    last = pl.num_programs(0) - 1
    pl.when(i < last)(lambda: tile(True))
    pl.when(i == last)(lambda: tile(False))


def _combine_call(slots, yc, x1c, route, g, b, tm):
    n = route.shape[0]
    d = D_MODEL
    nt = n // tm
    return pl.pallas_call(
        functools.partial(_combine_kernel, tm=tm),
        out_shape=jax.ShapeDtypeStruct((n, d), F32),
        grid=(nt,),
        in_specs=[
            pl.BlockSpec((1, 1, 2 * tm), lambda i: (i, 0, 0), memory_space=pltpu.SMEM),
            pl.BlockSpec((1, 1, 2 * tm), lambda i: (jnp.minimum(i + 1, nt - 1), 0, 0), memory_space=pltpu.SMEM),
            pl.BlockSpec(memory_space=pl.ANY),
            pl.BlockSpec((tm * ROW_PITCH, LANES), lambda i: (i, 0)),
            pl.BlockSpec((tm, LANES), lambda i: (i, 0)),
            pl.BlockSpec((1, d), lambda i: (0, 0)),
            pl.BlockSpec((1, d), lambda i: (0, 0)),
        ],
        out_specs=pl.BlockSpec((tm, d), lambda i: (i, 0)),
        scratch_shapes=[pltpu.VMEM((2, 2, tm * ROW_PITCH, LANES), F32), pltpu.SemaphoreType.DMA((2,))],
        compiler_params=_params(("arbitrary",)),
        name="moe_combine_ln2",
    )(slots, slots, yc, x1c, route, g, b)


def _moe_plan(e_idx, tm, n_tiles):
    n = e_idx.shape[0]
    flat = e_idx.reshape(-1)
    onehot = (flat[:, None] == jnp.arange(N_EXPERTS, dtype=jnp.int32)[None, :]).astype(jnp.int32)
    csum = jnp.cumsum(onehot, axis=0)
    rank = jnp.take_along_axis(csum, flat[:, None], axis=1)[:, 0] - 1
    counts = csum[-1]
    ptiles = (counts + tm - 1) // tm
    tile_end = jnp.cumsum(ptiles)
    slot = (tile_end - ptiles)[flat] * tm + rank
    rows = jnp.zeros((n_tiles * tm,), jnp.int32).at[slot].set(jnp.arange(2 * n, dtype=jnp.int32) // 2)
    tile_ids = jnp.arange(n_tiles, dtype=jnp.int32)
    tile_expert = jnp.minimum(jnp.sum((tile_end[None, :] <= tile_ids[:, None]).astype(jnp.int32), axis=1),
                              N_EXPERTS - 1)
    return slot.astype(jnp.int32), rows, tile_expert, tile_end[-1:].astype(jnp.int32)


def _rope_tables(seq):
    half = ROT_DIM // 2
    inv_freq = ROPE_THETA ** (-jnp.arange(0, ROT_DIM, 2, dtype=F32) / ROT_DIM)
    ang = jnp.arange(seq, dtype=F32)[:, None] * inv_freq[None, :]
    cos, sin = jnp.cos(ang), jnp.sin(ang)
    pad1 = jnp.ones((seq, HEAD_DIM - ROT_DIM), F32)
    pad0 = jnp.zeros((seq, HEAD_DIM - ROT_DIM), F32)
    return (jnp.concatenate([cos, cos, pad1], axis=1), jnp.concatenate([-sin, sin, pad0], axis=1))


def _layer(x, w_in, diff_lambda, diff_subln_g, cmp_pos, cmp_w1, cmp_b1, cmp_w2, cmp_b2, w_out,
           ln1_g, ln1_b, router_group, router_expert, w_gate, w_up, w_down, ln2_g, ln2_b, lambda_init):
    batch, seq, d = x.shape
    n = batch * seq
    x2 = x.reshape(n, d)

    seg = lambda a, b: w_in[:, a:b]
    pad = lambda w: jnp.pad(w, ((0, 0), (0, LANES - w.shape[1])))
    gate_w = seg(5632, 5656)
    w_aug = jnp.concatenate(
        [seg(3072, 4096), seg(0, 1024), seg(1024, 2048), seg(2048, 3072), seg(4096, 5632),
         pad(gate_w[:, :12]), pad(gate_w[:, 12:])], axis=1).astype(BF16)
    flags = jnp.zeros((N_UNITS // 2,), jnp.int32).at[jnp.array(ROPE_BLOCKS)].set(1)
    cos_t, sin_t = _rope_tables(seq)
    proj = _proj_call(x2, w_aug, flags, cos_t, sin_t, seq, tm=min(2048, seq))

    nt = seq // CMP_STRIDE
    half_feat = CMP_STRIDE * HEAD_DIM
    w1cat = jnp.concatenate([cmp_w1[:, :half_feat], cmp_w1[:, half_feat:]], axis=2).astype(BF16)
    kvc = _compress_call(proj, w1cat, cmp_pos.reshape(2, 2, half_feat), cmp_b1[:, None, :],
                         cmp_w2.astype(BF16), cmp_b2[:, None, :], batch, seq)

    o_diff = _diff_call(proj, diff_lambda, diff_subln_g[None, :], batch, seq, min(1024, seq), lambda_init)

    ns = seq // SLC_BLOCK
    ci = np.arange(nt)[None, :] * CMP_STRIDE
    sj = np.arange(ns)[:, None] * SLC_BLOCK
    ovt = jnp.asarray((ci < sj + SLC_BLOCK) & (ci + CMP_BLOCK > sj) & (np.arange(nt)[None, :] < nt - 1), dtype=BF16)
    o_nsa = _nsa_call(proj, cos_t, sin_t, kvc, ovt, batch, seq, tq=2 * LANES, tk=min(512, seq))

    wr = jnp.pad(jnp.concatenate([router_group, router_expert], axis=1),
                 ((0, 0), (0, LANES - N_GROUPS - N_EXPERTS)))
    x1, route = _outproj_call(o_diff, o_nsa, w_out.astype(BF16), x2, ln1_g[None, :], ln1_b[None, :], wr, tm=512)

    tm = MOE_TILE_ROWS
    n_tiles = -(-2 * n // tm) + N_EXPERTS + 1
    e_idx = route[:, 0:2].astype(jnp.int32)
    slot, rows, tile_expert, n_used = _moe_plan(e_idx, tm, n_tiles)
    ys = _moe_call(tile_expert, n_used, rows.reshape(n_tiles, 1, tm), x1, w_gate, w_up, w_down, tm)
    tc = COMBINE_TILE_TOKENS
    out = _combine_call(slot.reshape(n // tc, 1, 2 * tc), ys, x1, route, ln2_g[None, :], ln2_b[None, :], tc)
    return out.reshape(batch, seq, d)


def kernel(x, w_in, diff_lambda, diff_subln_g, cmp_pos, cmp_w1, cmp_b1, cmp_w2, cmp_b2, w_out, ln1_g, ln1_b,
           router_group, router_expert, expert_w_gate, expert_w_up, expert_w_down, ln2_g, ln2_b):
    for l in range(DEPTH):
        lambda_init = 0.8 - 0.6 * math.exp(-0.3 * l)
        x = _layer(x, w_in[l], diff_lambda[l], diff_subln_g[l], cmp_pos[l], cmp_w1[l], cmp_b1[l], cmp_w2[l],
                   cmp_b2[l], w_out[l], ln1_g[l], ln1_b[l], router_group[l], router_expert[l],
                   expert_w_gate[l], expert_w_up[l], expert_w_down[l], ln2_g[l], ln2_b[l], lambda_init)
    return x
```

```python
import functools
import math

import numpy as np
import jax
import jax.numpy as jnp
from jax import lax
from jax.experimental import pallas as pl
from jax.experimental.pallas import tpu as pltpu

F32 = jnp.float32
BF16 = jnp.bfloat16

D_MODEL = 2048
HEAD_DIM = 128
ROT_DIM = HEAD_DIM // 4
ROPE_THETA = 500000.0
NEG_INF = -1e30
BIG = 1e30
EPS = 1e-5

DIFF_HEADS = 4
DIFF_VDIM = 2 * HEAD_DIM

NSA_HEADS = 8
NSA_KV_GROUPS = 2
NSA_HPG = NSA_HEADS // NSA_KV_GROUPS
CMP_BLOCK = 32
CMP_STRIDE = 16
CMP_HIDDEN = 256
SLC_BLOCK = 64
SLC_TOPK = 16
WINDOW = 512

N_GROUPS = 4
EXPERTS_PER_GROUP = 8
N_EXPERTS = N_GROUPS * EXPERTS_PER_GROUP
EXPERT_HIDDEN = 512

DEPTH = 1
DN_ALPHA = (2.0 * DEPTH) ** 0.25

LANES = 128
VMEM_LIMIT = 56 * 1024 * 1024

U_NQ, U_DQ, U_DK, U_DV = 0, 8, 16, 24
U_KC, U_VC, U_KS, U_VS, U_KW, U_VW, U_GATE = 32, 34, 36, 38, 40, 42, 44
N_UNITS = 46
ROPE_BLOCKS = tuple(range(U_DQ // 2, U_DV // 2)) + (U_KS // 2, U_KW // 2)


def _dot(a, b):
    return jnp.dot(a, b, preferred_element_type=F32)


def _dot_nt(a, b):
    return lax.dot_general(a, b, (((1,), (1,)), ((), ())), preferred_element_type=F32)


def _params(sem, vmem=VMEM_LIMIT):
    return pltpu.CompilerParams(dimension_semantics=sem, vmem_limit_bytes=vmem)


PROJ_ROW_CHUNK = 512


def _rope(a, c, s):
    lane = lax.broadcasted_iota(jnp.int32, a.shape, 1)
    half = ROT_DIM // 2
    partner = jnp.where(lane < half, pltpu.roll(a, LANES - half, 1), pltpu.roll(a, half, 1))
    return a * c + partner * s


def _proj_kernel(flags_ref, x_ref, w_ref, cos_ref, sin_ref, o_ref, xb_sc):
    j = pl.program_id(1)

    @pl.when(j == 0)
    def _():
        xb_sc[...] = x_ref[...].astype(BF16)

    def block(rotary):
        for r0 in range(0, xb_sc.shape[0], PROJ_ROW_CHUNK):
            rs = slice(r0, r0 + PROJ_ROW_CHUNK)
            acc = _dot(xb_sc[rs, :], w_ref[...])
            if rotary:
                for hh in range(2):
                    cs = slice(hh * LANES, (hh + 1) * LANES)
                    o_ref[rs, cs] = _rope(acc[:, cs], cos_ref[rs, :], sin_ref[rs, :]).astype(o_ref.dtype)
            else:
                o_ref[rs, :] = acc.astype(o_ref.dtype)

    pl.when(flags_ref[j] == 0)(lambda: block(False))
    pl.when(flags_ref[j] != 0)(lambda: block(True))


def _proj_call(xb, w_aug, flags, cos_t, sin_t, seq, tm):
    n, d = xb.shape
    nj = w_aug.shape[1] // 256
    tpb = seq // tm
    grid_spec = pltpu.PrefetchScalarGridSpec(
        num_scalar_prefetch=1,
        grid=(n // tm, nj),
        in_specs=[
            pl.BlockSpec((tm, d), lambda i, j, f: (i, 0)),
            pl.BlockSpec((d, 256), lambda i, j, f: (0, j)),
            pl.BlockSpec((tm, LANES), lambda i, j, f: (i % tpb, 0)),
            pl.BlockSpec((tm, LANES), lambda i, j, f: (i % tpb, 0)),
        ],
        out_specs=pl.BlockSpec((tm, 256), lambda i, j, f: (i, j)),
        scratch_shapes=[pltpu.VMEM((tm, d), BF16)],
    )
    return pl.pallas_call(
        _proj_kernel,
        out_shape=jax.ShapeDtypeStruct((n, w_aug.shape[1]), BF16),
        grid_spec=grid_spec,
        compiler_params=_params(("parallel", "arbitrary")),
        name="proj",
    )(flags, xb, w_aug, cos_t, sin_t)


def _compress_kernel(t_ref, w1_ref, pos_ref, b1_ref, w2_ref, b2_ref, o_ref, t_sc):
    nt = t_sc.shape[0] // CMP_STRIDE
    t_sc[...] = t_ref[...].astype(F32)
    r = jnp.concatenate([t_sc[pl.ds(k, nt, stride=CMP_STRIDE), :] for k in range(CMP_STRIDE)], axis=1).astype(BF16)
    ab = _dot(r, w1_ref[...])
    pos = pos_ref[...]
    pa = jnp.broadcast_to(pos[0:1], (8, pos.shape[1])).astype(BF16)
    pb = jnp.broadcast_to(pos[1:2], (8, pos.shape[1])).astype(BF16)
    const = _dot(pa, w1_ref[:, :CMP_HIDDEN])[0:1] + _dot(pb, w1_ref[:, CMP_HIDDEN:])[0:1]
    h = ab[:, :CMP_HIDDEN] + pltpu.roll(ab[:, CMP_HIDDEN:], nt - 1, 0) + const + b1_ref[...]
    h = jax.nn.gelu(h)
    o_ref[...] = (_dot(h.astype(BF16), w2_ref[...]) + b2_ref[...]).astype(o_ref.dtype)


def _compress_call(proj, w1cat, pos2, b1, w2, b2, b, seq):
    four = 2 * NSA_KV_GROUPS
    nt = seq // CMP_STRIDE
    k = CMP_STRIDE * HEAD_DIM
    return pl.pallas_call(
        _compress_kernel,
        out_shape=jax.ShapeDtypeStruct((b, four, nt, HEAD_DIM), BF16),
        grid=(b, four),
        in_specs=[
            pl.BlockSpec((seq, HEAD_DIM), lambda i, c: (i, U_KC + c)),
            pl.BlockSpec((None, k, 2 * CMP_HIDDEN), lambda i, c: (c // 2, 0, 0)),
            pl.BlockSpec((None, 2, k), lambda i, c: (c // 2, 0, 0)),
            pl.BlockSpec((None, 1, CMP_HIDDEN), lambda i, c: (c // 2, 0, 0)),
            pl.BlockSpec((None, CMP_HIDDEN, HEAD_DIM), lambda i, c: (c // 2, 0, 0)),
            pl.BlockSpec((None, 1, HEAD_DIM), lambda i, c: (c // 2, 0, 0)),
        ],
        out_specs=pl.BlockSpec((None, None, nt, HEAD_DIM), lambda i, c: (i, c, 0, 0)),
        scratch_shapes=[pltpu.VMEM((seq, HEAD_DIM), F32)],
        compiler_params=_params(("parallel", "parallel")),
        name="compress",
    )(proj, w1cat, pos2, b1, w2, b2)


EXP2_SCALE = HEAD_DIM ** -0.5 * math.log2(math.e)
ONES_ROWS = 16


def _transpose_into(src_ref, dst_ref):
    def body(c, carry):
        off = pl.multiple_of(c * LANES, LANES)
        dst_ref[:, pl.ds(off, LANES)] = src_ref[pl.ds(off, LANES), :].astype(F32).T.astype(dst_ref.dtype)
        return carry

    lax.fori_loop(0, src_ref.shape[0] // LANES, body, 0)


def _diff_kernel(dl_ref, q_ref, k_ref, v_ref, g_ref, o_ref, vt_sc, acc_sc, sa_sc, sb_sc, *, tq, lambda_init):
    qi = pl.program_id(2)

    @pl.when(qi == 0)
    def _():
        for c in range(2):
            _transpose_into(v_ref.at[:, c * LANES:(c + 1) * LANES], vt_sc.at[c * LANES:(c + 1) * LANES, :])

    acc_sc[...] = jnp.zeros(acc_sc.shape, F32)
    q = q_ref[...]
    qpos = qi * tq + lax.broadcasted_iota(jnp.int32, (1, tq), 1)

    tk = tq // 2

    def scores(j, dst, lo=0):
        kt = k_ref[pl.ds(pl.multiple_of(j * tk, tk), tk), :]
        for c in range(2):
            dst[c, :, lo:] = _dot_nt(kt[:, c * HEAD_DIM:(c + 1) * HEAD_DIM], q[lo:, c * HEAD_DIM:(c + 1) * HEAD_DIM])

    def absorb(src, j, masked, carry, lo=0):
        off = pl.multiple_of(j * tk, tk)
        vt = vt_sc[:, pl.ds(off, tk)]
        out = []
        for c in range(2):
            m_old, l_old = carry[2 * c][:, lo:], carry[2 * c + 1][:, lo:]
            s = src[c, :, lo:]
            if masked:
                s = jnp.where(lax.broadcasted_iota(jnp.int32, s.shape, 0) <= qpos[:, lo:] - off, s, NEG_INF)
            m_new = jnp.maximum(m_old, jnp.max(s, axis=0, keepdims=True))
            p = jnp.exp2((s - m_new) * EXP2_SCALE)
            alpha = jnp.exp2((m_old - m_new) * EXP2_SCALE)
            l_new = alpha * l_old + jnp.sum(p, axis=0, keepdims=True)
            acc_sc[c, :, lo:] = alpha * acc_sc[c, :, lo:] + _dot(vt, p.astype(BF16))
            if lo:
                m_new = jnp.concatenate([carry[2 * c][:, :lo], m_new], axis=1)
                l_new = jnp.concatenate([carry[2 * c + 1][:, :lo], l_new], axis=1)
            out += [m_new, l_new]
        return tuple(out)

    def pair(p, carry):
        j = 2 * p
        scores(j + 1, sb_sc)
        carry = absorb(sa_sc, j, False, carry)
        scores(j + 2, sa_sc)
        return absorb(sb_sc, j + 1, False, carry)

    init = (jnp.full((1, tq), NEG_INF, F32), jnp.zeros((1, tq), F32)) * 2
    scores(0, sa_sc)
    carry = lax.fori_loop(0, qi, pair, init)
    j_tail = 2 * qi
    scores(j_tail + 1, sb_sc, lo=tk)
    carry = absorb(sa_sc, j_tail, True, carry)
    _, l0, _, l1 = absorb(sb_sc, j_tail + 1, True, carry, lo=tk)

    dl = dl_ref[...]
    lam = (jnp.exp(jnp.sum(dl[0:1] * dl[1:2], axis=1, keepdims=True))
           - jnp.exp(jnp.sum(dl[2:3] * dl[3:4], axis=1, keepdims=True)) + lambda_init)
    o = acc_sc[0] * (1.0 / l0) - lam * (acc_sc[1] * (1.0 / l1))
    o = o * (lax.rsqrt(jnp.mean(o * o, axis=0, keepdims=True) + EPS) * (1.0 - lambda_init))
    for c in range(DIFF_VDIM // LANES):
        for r in range(tq // LANES):
            blk = o[c * LANES:(c + 1) * LANES, r * LANES:(r + 1) * LANES].T
            o_ref[r * LANES:(r + 1) * LANES, c * LANES:(c + 1) * LANES] = (
                blk * g_ref[:, c * LANES:(c + 1) * LANES]).astype(o_ref.dtype)


def _diff_call(proj, dl, g, batch, seq, tq, lambda_init):
    nq = seq // tq
    kern = functools.partial(_diff_kernel, tq=tq, lambda_init=lambda_init)
    return pl.pallas_call(
        kern,
        out_shape=jax.ShapeDtypeStruct((batch * seq, DIFF_HEADS * DIFF_VDIM), BF16),
        grid=(batch, DIFF_HEADS, nq),
        in_specs=[
            pl.BlockSpec((4, HEAD_DIM), lambda b, h, i: (0, 0)),
            pl.BlockSpec((tq, 256), lambda b, h, i: (b * nq + i, U_DQ // 2 + h)),
            pl.BlockSpec((seq, 256), lambda b, h, i: (b, U_DK // 2 + h)),
            pl.BlockSpec((seq, 256), lambda b, h, i: (b, U_DV // 2 + h)),
            pl.BlockSpec((1, DIFF_VDIM), lambda b, h, i: (0, 0)),
        ],
        out_specs=pl.BlockSpec((tq, DIFF_VDIM), lambda b, h, i: (b * nq + i, h)),
        scratch_shapes=[
            pltpu.VMEM((DIFF_VDIM, seq), BF16),
            pltpu.VMEM((2, DIFF_VDIM, tq), F32),
            pltpu.VMEM((2, tq // 2, tq), F32),
            pltpu.VMEM((2, tq // 2, tq), F32),
        ],
        compiler_params=_params(("arbitrary", "arbitrary", "arbitrary")),
        name="diff_attn",
    )(dl, proj, proj, proj, g)


def _stack_heads(x):
    return jnp.concatenate([x[:, h * HEAD_DIM:(h + 1) * HEAD_DIM] for h in range(NSA_HPG)], axis=0)


def _heads(x):
    return jnp.concatenate([x] * NSA_HPG, axis=1)


def _mask_tables(seq, tq, tk):
    nt = seq // CMP_STRIDE
    slab = min(WINDOW + tq, seq)
    q = np.arange(tq)[None, :]
    k = np.arange(slab)[:, None]
    win = []
    for i in range(WINDOW // tq + 1):
        newest = q + (i * tq if i < WINDOW // tq else WINDOW)
        win.append((k <= newest) & (k > newest - WINDOW))
    u = np.arange(2 * nt)[:, None] - nt
    cmp_ok = u <= ((q - (CMP_BLOCK - 1)) >> int(math.log2(CMP_STRIDE)))
    kk = np.arange(tk)[:, None]
    causal = [kk <= d * tq + q for d in range(tk // tq)] + [np.ones((tk, tq), bool)]
    to_bias = lambda m: jnp.asarray(np.where(np.asarray(m), 0.0, NEG_INF), dtype=F32)
    return to_bias(np.stack(win)), to_bias(cmp_ok), to_bias(np.stack(causal))


def _nsa_kernel(q_ref, cos_ref, sin_ref, gate_ref, kc_ref, vc_ref, ks_ref, vs_ref, kw_ref, vw_ref, ovt_ref,
                wtab_ref, ptab_ref, ctab_ref, o_ref, vct_sc, vst_sc, vwt_sc, bias_sc, acc_sc, sa_sc, sb_sc, m_sc, *, tq, tk, seq, top_k):
    qi = pl.program_id(2)
    q0 = qi * tq
    rows = NSA_HPG * tq
    nt = kc_ref.shape[0]
    ns = ovt_ref.shape[0]
    nb = tk // SLC_BLOCK

    @pl.when(qi == 0)
    def _():
        _transpose_into(vc_ref, vct_sc)
        for v_ref, vt_sc in ((vs_ref, vst_sc), (vw_ref, vwt_sc)):
            _transpose_into(v_ref, vt_sc.at[:HEAD_DIM, :])
            vt_sc[HEAD_DIM:, :] = jnp.ones((ONES_ROWS, seq), vt_sc.dtype)

    q = q_ref[...]
    qs = _stack_heads(q)
    qr = jnp.concatenate(
        [_rope(q[:, h * HEAD_DIM:(h + 1) * HEAD_DIM].astype(F32), cos_ref[...], sin_ref[...]).astype(BF16)
         for h in range(NSA_HPG)], axis=0)
    qpos = q0 + (lax.broadcasted_iota(jnp.int32, (1, rows), 1) & (tq - 1))

    def sel_scores(j, dst):
        dst[...] = _dot_nt(ks_ref[pl.ds(pl.multiple_of(j * tk, tk), tk), :], qr)

    sel_scores(0, sa_sc)

    slab = min(WINDOW + tq, seq)
    start = pl.multiple_of(jnp.maximum(q0 - WINDOW, 0), tq)
    s_w = _dot_nt(kw_ref[pl.ds(start, slab), :], qr)
    s_w = s_w + _heads(wtab_ref[jnp.minimum(qi, WINDOW // tq)])
    p_w = jnp.exp2((s_w - jnp.max(s_w, axis=0, keepdims=True)) * EXP2_SCALE)
    o_w = _dot(vwt_sc[:, pl.ds(start, slab)], p_w.astype(BF16))
    o_w = o_w[:HEAD_DIM] * (1.0 / o_w[HEAD_DIM:HEAD_DIM + 1])

    s_c = _dot_nt(kc_ref[...], qs)
    c0 = pl.multiple_of(nt - q0 // CMP_STRIDE, CMP_STRIDE)
    s_c = s_c + _heads(ptab_ref[pl.ds(c0, nt), :])
    m_c = jnp.max(s_c, axis=0, keepdims=True)
    e_c = jnp.exp2((s_c - m_c) * EXP2_SCALE)
    l_c = jnp.sum(e_c, axis=0, keepdims=True)
    p_c = e_c * jnp.where(qpos >= CMP_BLOCK - 1, 1.0 / l_c, 0.0)
    o_c = _dot(vct_sc[...], p_c.astype(BF16))

    p_sum = p_c[:, 0:tq]
    for h in range(1, NSA_HPG):
        p_sum = p_sum + p_c[:, h * tq:(h + 1) * tq]
    p_hi = p_sum.astype(BF16)
    p_lo = (p_sum - p_hi.astype(F32)).astype(BF16)
    imp = _dot(ovt_ref[...], p_hi) + _dot(ovt_ref[...], p_lo)

    blk = lax.broadcasted_iota(jnp.int32, (ns, tq), 0)
    qpos_l = q0 + lax.broadcasted_iota(jnp.int32, (ns, tq), 1)
    cur = lax.shift_right_logical(qpos_l, int(math.log2(SLC_BLOCK)))
    valid_s = blk <= cur
    forced = (blk == 0) | (blk == cur) | (blk == cur - 1)
    work = jnp.where(forced, -jnp.inf, jnp.where(valid_s, imp, NEG_INF))
    blk_f = blk.astype(F32)
    for _ in range(top_k - 3):
        mx = jnp.max(work, axis=0, keepdims=True)
        idx = jnp.min(jnp.where(work == mx, blk_f, float(ns)), axis=0, keepdims=True)
        work = jnp.where(blk_f == idx, -jnp.inf, work)
    bias_sc[...] = jnp.where(work == -jnp.inf, 0.0, NEG_INF)

    acc_sc[...] = jnp.zeros(acc_sc.shape, F32)

    def sel_absorb(src, j, causal, m_old):
        off = pl.multiple_of(j * tk, tk)
        s = src[...]
        if causal:
            s = s + _heads(ctab_ref[jnp.clip((q0 - off) // tq, 0, tk // tq)])
        bias = bias_sc[pl.ds(pl.multiple_of(j * nb, nb), nb), :]
        bias = jnp.concatenate([bias] * NSA_HPG, axis=1)
        s3 = s.reshape(nb, SLC_BLOCK, rows) + bias[:, None, :]
        m_new = jnp.maximum(m_old, jnp.max(jnp.max(s3, axis=0), axis=0, keepdims=True))
        p3 = jnp.exp2((s3 - m_new) * EXP2_SCALE)
        alpha = jnp.exp2((m_old - m_new) * EXP2_SCALE)
        p = p3.reshape(tk, rows).astype(BF16)
        acc_sc[...] = alpha * acc_sc[...] + _dot(vst_sc[:, pl.ds(off, tk)], p)
        return m_new

    def sel_pair(p, carry):
        j = 2 * p
        sel_scores(j + 1, sb_sc)
        carry = sel_absorb(sa_sc, j, False, carry)
        sel_scores(j + 2, sa_sc)
        return sel_absorb(sb_sc, j + 1, False, carry)

    n_pairs = (q0 // tk + 2) // 2
    carry = lax.fori_loop(0, n_pairs - 1, sel_pair, jnp.full((1, rows), NEG_INF, F32))
    j_tail = 2 * (n_pairs - 1)
    m_sc[...] = sel_absorb(sa_sc, j_tail, True, carry)

    @pl.when(j_tail + 1 <= q0 // tk)
    def _():
        sel_scores(j_tail + 1, sb_sc)
        sel_absorb(sb_sc, j_tail + 1, True, m_sc[...])

    o_s = acc_sc[:HEAD_DIM, :] * (1.0 / acc_sc[HEAD_DIM:HEAD_DIM + 1, :])

    gates = jax.nn.sigmoid(gate_ref[...].astype(F32))
    gates = jnp.concatenate([gates[r * LANES:(r + 1) * LANES].T for r in range(tq // LANES)],
                            axis=1)
    for h in range(NSA_HPG):
        sl = slice(h * tq, (h + 1) * tq)
        o = (gates[3 * h:3 * h + 1] * o_c[:, sl] + gates[3 * h + 1:3 * h + 2] * o_s[:, sl]
             + gates[3 * h + 2:3 * h + 3] * o_w[:, sl])
        for r in range(tq // LANES):
            o_ref[r * LANES:(r + 1) * LANES, h * HEAD_DIM:(h + 1) * HEAD_DIM] = (
                o[:, r * LANES:(r + 1) * LANES].T.astype(o_ref.dtype))


def _nsa_call(proj, cos_t, sin_t, kvc, ovt, batch, seq, tq, tk):
    nq = seq // tq
    assert WINDOW % tq == 0 and tk % tq == 0 and tq % CMP_STRIDE == 0
    wtab, ptab, ctab = _mask_tables(seq, tq, tk)
    nt = kvc.shape[2]
    ns = ovt.shape[0]
    width = NSA_HPG * HEAD_DIM
    kern = functools.partial(_nsa_kernel, tq=tq, tk=tk, seq=seq, top_k=min(SLC_TOPK, ns))

    def col(unit):
        return pl.BlockSpec((seq, HEAD_DIM), lambda b, g, i: (b, unit + g))

    return pl.pallas_call(
        kern,
        out_shape=jax.ShapeDtypeStruct((batch * seq, NSA_HEADS * HEAD_DIM), BF16),
        grid=(batch, NSA_KV_GROUPS, nq),
        in_specs=[
            pl.BlockSpec((tq, width), lambda b, g, i: (b * nq + i, U_NQ // 4 + g)),
            pl.BlockSpec((tq, LANES), lambda b, g, i: (i, 0)),
            pl.BlockSpec((tq, LANES), lambda b, g, i: (i, 0)),
            pl.BlockSpec((tq, LANES), lambda b, g, i: (b * nq + i, U_GATE + g)),
            pl.BlockSpec((None, None, nt, HEAD_DIM), lambda b, g, i: (b, g, 0, 0)),
            pl.BlockSpec((None, None, nt, HEAD_DIM), lambda b, g, i: (b, 2 + g, 0, 0)),
            col(U_KS), col(U_VS), col(U_KW), col(U_VW),
            pl.BlockSpec((ns, nt), lambda b, g, i: (0, 0)),
            pl.BlockSpec(wtab.shape, lambda b, g, i: (0, 0, 0)),
            pl.BlockSpec(ptab.shape, lambda b, g, i: (0, 0)),
            pl.BlockSpec(ctab.shape, lambda b, g, i: (0, 0, 0)),
        ],
        out_specs=pl.BlockSpec((tq, width), lambda b, g, i: (b * nq + i, g)),
        scratch_shapes=[
            pltpu.VMEM((HEAD_DIM, nt), BF16),
            pltpu.VMEM((HEAD_DIM + ONES_ROWS, seq), BF16),
            pltpu.VMEM((HEAD_DIM + ONES_ROWS, seq), BF16),
            pltpu.VMEM((ns, tq), F32),
            pltpu.VMEM((HEAD_DIM + ONES_ROWS, NSA_HPG * tq), F32),
            pltpu.VMEM((tk, NSA_HPG * tq), F32),
            pltpu.VMEM((tk, NSA_HPG * tq), F32),
            pltpu.VMEM((1, NSA_HPG * tq), F32),
        ],
        compiler_params=_params(("arbitrary", "arbitrary", "arbitrary")),
        name="nsa_attn",
    )(proj, cos_t, sin_t, proj, kvc, kvc, proj, proj, proj, proj, ovt, wtab, ptab, ctab)


def _layernorm(y, g, b):
    mu = jnp.mean(y, axis=1, keepdims=True)
    yc = y - mu
    var = jnp.mean(yc * yc, axis=1, keepdims=True)
    return yc * lax.rsqrt(var + EPS) * g + b


N_CHUNKS = D_MODEL // LANES
ROW_PITCH = N_CHUNKS + 1


def _store_chunk_rows(ref, val):
    tm = val.shape[0]
    for k in range(N_CHUNKS):
        ref[pl.ds(k, tm, stride=ROW_PITCH), :] = val[:, k * LANES:(k + 1) * LANES]
    for k in range(N_CHUNKS, ROW_PITCH):
        ref[pl.ds(k, tm, stride=ROW_PITCH), :] = jnp.zeros((tm, LANES), ref.dtype)


def _load_chunk_rows(ref, tm):
    return jnp.concatenate([ref[pl.ds(k, tm, stride=ROW_PITCH), :] for k in range(N_CHUNKS)], axis=1)


OUT_ROW_CHUNK = 256


def _outproj_kernel(od_ref, on_ref, w_ref, x_ref, g_ref, b_ref, wr_ref, x1_ref, route_ref):
    wr = wr_ref[...]
    w_hi = wr.astype(BF16)
    w_lo = (wr - w_hi.astype(F32)).astype(BF16)
    for r0 in range(0, x_ref.shape[0], OUT_ROW_CHUNK):
        _outproj_rows(od_ref, on_ref, w_ref, x_ref, g_ref, b_ref, w_hi, w_lo, x1_ref, route_ref,
                      slice(r0, r0 + OUT_ROW_CHUNK))


def _outproj_rows(od_ref, on_ref, w_ref, x_ref, g_ref, b_ref, w_hi, w_lo, x1_ref, route_ref, rs):
    half = od_ref.shape[1]
    h = _dot(od_ref[rs, :], w_ref[:half, :]) + _dot(on_ref[rs, :], w_ref[half:, :])
    x1 = _layernorm(DN_ALPHA * x_ref[rs, :] + h, g_ref[...], b_ref[...])
    _store_chunk_rows(x1_ref.at[rs.start * ROW_PITCH:rs.stop * ROW_PITCH, :], x1)

    x_hi = x1.astype(BF16)
    x_lo = (x1 - x_hi.astype(F32)).astype(BF16)
    w2 = jnp.concatenate([w_hi, w_lo], axis=1)
    a = _dot(x_hi, w2)
    b = _dot(x_lo, w2)
    logits = (a[:, :LANES] + a[:, LANES:]) + (b[:, :LANES] + b[:, LANES:])

    lane = lax.broadcasted_iota(jnp.int32, logits.shape, 1).astype(F32)
    ninf = -jnp.inf
    gl = jnp.where(lane < N_GROUPS, logits, ninf)
    gmax = jnp.max(gl, axis=1, keepdims=True)
    g_w = 1.0 / jnp.sum(jnp.exp(gl - gmax), axis=1, keepdims=True)
    g_sel = jnp.min(jnp.where(gl == gmax, lane, float(LANES)), axis=1, keepdims=True)
    lo = N_GROUPS + EXPERTS_PER_GROUP * g_sel
    el = jnp.where((lane >= lo) & (lane < lo + EXPERTS_PER_GROUP), logits, ninf)
    e1 = jnp.max(el, axis=1, keepdims=True)
    i1 = jnp.min(jnp.where(el == e1, lane, float(LANES)), axis=1, keepdims=True)
    el2 = jnp.where(lane == i1, ninf, el)
    e2 = jnp.max(el2, axis=1, keepdims=True)
    i2 = jnp.min(jnp.where(el2 == e2, lane, float(LANES)), axis=1, keepdims=True)
    r = jnp.exp(e2 - e1)
    w1 = g_w / (1.0 + r)
    w2 = g_w * r / (1.0 + r)
    route = jnp.where(lane == 0, i1 - N_GROUPS,
                      jnp.where(lane == 1, i2 - N_GROUPS,
                                jnp.where(lane == 2, w1, jnp.where(lane == 3, w2, 0.0))))
    route_ref[rs, :] = route


def _outproj_call(o_diff, o_nsa, w_out, x2, g, b, wr, tm):
    n, d = x2.shape
    half = o_diff.shape[1]
    return pl.pallas_call(
        _outproj_kernel,
        out_shape=(jax.ShapeDtypeStruct((n * ROW_PITCH, LANES), F32), jax.ShapeDtypeStruct((n, LANES), F32)),
        grid=(n // tm,),
        in_specs=[
            pl.BlockSpec((tm, half), lambda i: (i, 0)),
            pl.BlockSpec((tm, half), lambda i: (i, 0)),
            pl.BlockSpec((2 * half, d), lambda i: (0, 0)),
            pl.BlockSpec((tm, d), lambda i: (i, 0)),
            pl.BlockSpec((1, d), lambda i: (0, 0)),
            pl.BlockSpec((1, d), lambda i: (0, 0)),
            pl.BlockSpec((d, LANES), lambda i: (0, 0)),
        ],
        out_specs=(pl.BlockSpec((tm * ROW_PITCH, LANES), lambda i: (i, 0)),
                   pl.BlockSpec((tm, LANES), lambda i: (i, 0))),
        compiler_params=_params(("parallel",)),
        name="outproj_ln1_router",
    )(o_diff, o_nsa, w_out, x2, g, b, wr)


MOE_TILE_ROWS = 384
COMBINE_TILE_TOKENS = 256


def _token_copy(src_hbm, dst, sem, src_tok, dst_tok):
    return pltpu.make_async_copy(src_hbm.at[pl.ds(src_tok * ROW_PITCH, N_CHUNKS)],
                                 dst.at[pl.ds(dst_tok * ROW_PITCH, N_CHUNKS)], sem)


def _wait_tokens(src_hbm, dst, sem, count):
    pltpu.make_async_copy(src_hbm.at[pl.ds(0, count * N_CHUNKS)], dst.at[pl.ds(0, count * N_CHUNKS)], sem).wait()


def _moe_kernel(te_ref, nu_ref, ws_ref, nxt_ref, rows_ref, rows_next_ref, x_hbm, wg_hbm, wu_hbm, wd_hbm, o_ref,
                xbuf, sem, wg_buf, wu_buf, wd_buf, wsem, wg_sc, wu_sc, wd_sc, *, tm):
    i = pl.program_id(0)
    n_used = nu_ref[0]
    slot = lax.rem(i, 2)

    def weight_copies(e, s):
        return [pltpu.make_async_copy(src.at[e], dst.at[s], wsem.at[s])
                for src, dst in ((wg_hbm, wg_buf), (wu_hbm, wu_buf), (wd_hbm, wd_buf))]

    @pl.when(i == 0)
    def _():
        for cp in weight_copies(te_ref[0], 0):
            cp.start()

        def body(r, carry):
            _token_copy(x_hbm, xbuf.at[0], sem.at[0], rows_ref[0, 0, r], r).start()
            return carry

        lax.fori_loop(0, tm, body, 0, unroll=8)

    prev = te_ref[jnp.maximum(i - 1, 0)]

    @pl.when(((i == 0) | (te_ref[i] != prev)) & (i < n_used))
    def _():
        s = ws_ref[i]
        for cp in weight_copies(te_ref[i], s):
            cp.wait()
        wg_sc[...] = wg_buf[s].astype(BF16)
        wu_sc[...] = wu_buf[s].astype(BF16)
        wd_sc[...] = wd_buf[s].astype(BF16)

        @pl.when(nxt_ref[i] >= 0)
        def _():
            for cp in weight_copies(nxt_ref[i], 1 - s):
                cp.start()

    @pl.when(i < n_used)
    def _():
        def prefetch(part, parts=4):
            for r in range(part * tm // parts, (part + 1) * tm // parts):
                _token_copy(x_hbm, xbuf.at[1 - slot], sem.at[1 - slot], rows_next_ref[0, 0, r], r).start(
                    priority=r % 2)

        _wait_tokens(x_hbm, xbuf.at[slot], sem.at[slot], tm)
        xb = _load_chunk_rows(xbuf.at[slot], tm).astype(BF16)
        prefetch(0)
        gate = _dot(xb, wg_sc[...])
        prefetch(1)
        up = _dot(xb, wu_sc[...])
        prefetch(2)
        h = (gate * jax.nn.sigmoid(gate) * up).astype(BF16)
        y = _dot(h, wd_sc[...])
        prefetch(3)
        _store_chunk_rows(o_ref, y)

    @pl.when(i == n_used)
    def _():
        _wait_tokens(x_hbm, xbuf.at[slot], sem.at[slot], tm)

    @pl.when(i >= n_used)
    def _():
        o_ref[...] = jnp.zeros(o_ref.shape, o_ref.dtype)


def _moe_call(tile_expert, n_used, w_slot, next_expert, rows, x1c, w_gate, w_up, w_down, tm):
    n_tiles = rows.shape[0]
    d, f = w_gate.shape[1], w_gate.shape[2]
    grid_spec = pltpu.PrefetchScalarGridSpec(
        num_scalar_prefetch=4,
        grid=(n_tiles,),
        in_specs=[
            pl.BlockSpec((1, 1, tm), lambda i, *_: (i, 0, 0), memory_space=pltpu.SMEM),
            pl.BlockSpec((1, 1, tm), lambda i, *_: (jnp.minimum(i + 1, n_tiles - 1), 0, 0), memory_space=pltpu.SMEM),
            pl.BlockSpec(memory_space=pl.ANY),
            pl.BlockSpec(memory_space=pl.ANY),
            pl.BlockSpec(memory_space=pl.ANY),
            pl.BlockSpec(memory_space=pl.ANY),
        ],
        out_specs=pl.BlockSpec((tm * ROW_PITCH, LANES), lambda i, *_: (i, 0)),
        scratch_shapes=[
            pltpu.VMEM((2, tm * ROW_PITCH, LANES), F32),
            pltpu.SemaphoreType.DMA((2,)),
            pltpu.VMEM((2, d, f), F32), pltpu.VMEM((2, d, f), F32), pltpu.VMEM((2, f, d), F32),
            pltpu.SemaphoreType.DMA((2,)),
            pltpu.VMEM((d, f), BF16), pltpu.VMEM((d, f), BF16), pltpu.VMEM((f, d), BF16),
        ],
    )
    return pl.pallas_call(
        functools.partial(_moe_kernel, tm=tm),
        out_shape=jax.ShapeDtypeStruct((n_tiles * tm * ROW_PITCH, LANES), F32),
        grid_spec=grid_spec,
        compiler_params=_params(("arbitrary",)),
        name="moe_experts",
    )(tile_expert, n_used, w_slot, next_expert, rows, rows, x1c, w_gate, w_up, w_down)


def _combine_kernel(slots_ref, slots_next_ref, y_hbm, x1_ref, route_ref, g_ref, b_ref, o_ref, buf, sem, *, tm):
    i = pl.program_id(0)
    slot = lax.rem(i, 2)

    @pl.when(i == 0)
    def _():
        def body(r, carry):
            for k in range(2):
                _token_copy(y_hbm, buf.at[0, k], sem.at[0], slots_ref[0, 0, 2 * r + k], r).start()
            return carry

        lax.fori_loop(0, tm, body, 0, unroll=8)

    def tile(prefetch):
        if prefetch:
            for r in range(tm):
                for k in range(2):
                    _token_copy(y_hbm, buf.at[1 - slot, k], sem.at[1 - slot],
                                slots_next_ref[0, 0, 2 * r + k], r).start(priority=k)
        for k in range(2):
            _wait_tokens(y_hbm, buf.at[slot, k], sem.at[slot], tm)
        route = route_ref[...]
        y = (route[:, 2:3] * _load_chunk_rows(buf.at[slot, 0], tm)
             + route[:, 3:4] * _load_chunk_rows(buf.at[slot, 1], tm))
        x1 = _load_chunk_rows(x1_ref, tm)
        o_ref[...] = _layernorm(DN_ALPHA * x1 + y, g_ref[...], b_ref[...])

    last = pl.num_programs(0) - 1
    pl.when(i < last)(lambda: tile(True))
    pl.when(i == last)(lambda: tile(False))


def _combine_call(slots, yc, x1c, route, g, b, tm):
    n = route.shape[0]
    d = D_MODEL
    nt = n // tm
    return pl.pallas_call(
        functools.partial(_combine_kernel, tm=tm),
        out_shape=jax.ShapeDtypeStruct((n, d), F32),
        grid=(nt,),
        in_specs=[
            pl.BlockSpec((1, 1, 2 * tm), lambda i: (i, 0, 0), memory_space=pltpu.SMEM),
            pl.BlockSpec((1, 1, 2 * tm), lambda i: (jnp.minimum(i + 1, nt - 1), 0, 0), memory_space=pltpu.SMEM),
            pl.BlockSpec(memory_space=pl.ANY),
            pl.BlockSpec((tm * ROW_PITCH, LANES), lambda i: (i, 0)),
            pl.BlockSpec((tm, LANES), lambda i: (i, 0)),
            pl.BlockSpec((1, d), lambda i: (0, 0)),
            pl.BlockSpec((1, d), lambda i: (0, 0)),
        ],
        out_specs=pl.BlockSpec((tm, d), lambda i: (i, 0)),
        scratch_shapes=[pltpu.VMEM((2, 2, tm * ROW_PITCH, LANES), F32), pltpu.SemaphoreType.DMA((2,))],
        compiler_params=_params(("arbitrary",)),
        name="moe_combine_ln2",
    )(slots, slots, yc, x1c, route, g, b)


def _moe_plan(e_idx, tm, n_tiles):
    n = e_idx.shape[0]
    lanes = jnp.arange(N_EXPERTS, dtype=jnp.int32)[None, :]
    hit = ((e_idx[:, 0:1] == lanes) | (e_idx[:, 1:2] == lanes)).astype(jnp.int32)
    before = jnp.cumsum(hit, axis=0) - hit
    counts = jnp.sum(hit, axis=0)
    ptiles = (counts + tm - 1) // tm
    tile_end = jnp.cumsum(ptiles)
    base = (tile_end - ptiles) * tm
    slot = jnp.take_along_axis(before + base[None, :], e_idx, axis=1).reshape(-1)
    rows = jnp.zeros((n_tiles * tm,), jnp.int32).at[slot].set(jnp.arange(2 * n, dtype=jnp.int32) // 2)
    tile_ids = jnp.arange(n_tiles, dtype=jnp.int32)
    tile_expert = jnp.minimum(jnp.sum((tile_end[None, :] <= tile_ids[:, None]).astype(jnp.int32), axis=1),
                              N_EXPERTS - 1)
    used = ptiles > 0
    e_ids = jnp.arange(N_EXPERTS, dtype=jnp.int32)
    used_rank = jnp.cumsum(used.astype(jnp.int32)) - 1
    later_used = used[None, :] & (e_ids[None, :] > e_ids[:, None])
    next_used = jnp.min(jnp.where(later_used, e_ids[None, :], N_EXPERTS), axis=1)
    next_used = jnp.where(next_used < N_EXPERTS, next_used, -1)
    w_slot = (used_rank % 2)[tile_expert].astype(jnp.int32)
    next_expert = next_used[tile_expert].astype(jnp.int32)
    return (slot.astype(jnp.int32), rows, tile_expert, tile_end[-1:].astype(jnp.int32), w_slot, next_expert)


def _rope_tables(seq):
    inv_freq = ROPE_THETA ** (-np.arange(0, ROT_DIM, 2, dtype=np.float64) / ROT_DIM)
    ang = np.arange(seq, dtype=np.float64)[:, None] * inv_freq[None, :]
    cos, sin = np.cos(ang), np.sin(ang)
    pad1 = np.ones((seq, HEAD_DIM - ROT_DIM))
    pad0 = np.zeros((seq, HEAD_DIM - ROT_DIM))
    return (jnp.asarray(np.concatenate([cos, cos, pad1], axis=1), dtype=F32),
            jnp.asarray(np.concatenate([-sin, sin, pad0], axis=1), dtype=F32))


def _layer(x, w_in, diff_lambda, diff_subln_g, cmp_pos, cmp_w1, cmp_b1, cmp_w2, cmp_b2, w_out,
           ln1_g, ln1_b, router_group, router_expert, w_gate, w_up, w_down, ln2_g, ln2_b, lambda_init):
    batch, seq, d = x.shape
    n = batch * seq
    x2 = x.reshape(n, d)

    seg = lambda a, b: w_in[:, a:b]
    pad = lambda w: jnp.pad(w, ((0, 0), (0, LANES - w.shape[1])))
    gate_w = seg(5632, 5656)
    w_aug = jnp.concatenate(
        [seg(3072, 4096), seg(0, 1024), seg(1024, 2048), seg(2048, 3072), seg(4096, 5632),
         pad(gate_w[:, :12]), pad(gate_w[:, 12:])], axis=1).astype(BF16)
    flags = jnp.zeros((N_UNITS // 2,), jnp.int32).at[jnp.array(ROPE_BLOCKS)].set(1)
    cos_t, sin_t = _rope_tables(seq)
    proj = _proj_call(x2, w_aug, flags, cos_t, sin_t, seq, tm=min(2048, seq))

    nt = seq // CMP_STRIDE
    half_feat = CMP_STRIDE * HEAD_DIM
    w1cat = jnp.concatenate([cmp_w1[:, :half_feat], cmp_w1[:, half_feat:]], axis=2).astype(BF16)
    kvc = _compress_call(proj, w1cat, cmp_pos.reshape(2, 2, half_feat), cmp_b1[:, None, :],
                         cmp_w2.astype(BF16), cmp_b2[:, None, :], batch, seq)

    o_diff = _diff_call(proj, diff_lambda, diff_subln_g[None, :], batch, seq, min(1024, seq), lambda_init)

    ns = seq // SLC_BLOCK
    ci = np.arange(nt)[None, :] * CMP_STRIDE
    sj = np.arange(ns)[:, None] * SLC_BLOCK
    ovt = jnp.asarray((ci < sj + SLC_BLOCK) & (ci + CMP_BLOCK > sj) & (np.arange(nt)[None, :] < nt - 1), dtype=BF16)
    o_nsa = _nsa_call(proj, cos_t, sin_t, kvc, ovt, batch, seq, tq=2 * LANES, tk=min(512, seq))

    wr = jnp.pad(jnp.concatenate([router_group, router_expert], axis=1),
                 ((0, 0), (0, LANES - N_GROUPS - N_EXPERTS)))
    x1, route = _outproj_call(o_diff, o_nsa, w_out.astype(BF16), x2, ln1_g[None, :], ln1_b[None, :], wr, tm=512)

    tm = MOE_TILE_ROWS
    n_tiles = -(-2 * n // tm) + N_EXPERTS + 1
    e_idx = route[:, 0:2].astype(jnp.int32)
    slot, rows, tile_expert, n_used, w_slot, next_expert = _moe_plan(e_idx, tm, n_tiles)
    ys = _moe_call(tile_expert, n_used, w_slot, next_expert, rows.reshape(n_tiles, 1, tm), x1,
                   w_gate, w_up, w_down, tm)
    tc = COMBINE_TILE_TOKENS
    out = _combine_call(slot.reshape(n // tc, 1, 2 * tc), ys, x1, route, ln2_g[None, :], ln2_b[None, :], tc)
    return out.reshape(batch, seq, d)


def kernel(x, w_in, diff_lambda, diff_subln_g, cmp_pos, cmp_w1, cmp_b1, cmp_w2, cmp_b2, w_out, ln1_g, ln1_b,
           router_group, router_expert, expert_w_gate, expert_w_up, expert_w_down, ln2_g, ln2_b):
    for l in range(DEPTH):
        lambda_init = 0.8 - 0.6 * math.exp(-0.3 * l)
        x = _layer(x, w_in[l], diff_lambda[l], diff_subln_g[l], cmp_pos[l], cmp_w1[l], cmp_b1[l], cmp_w2[l],
                   cmp_b2[l], w_out[l], ln1_g[l], ln1_b[l], router_group[l], router_expert[l],
                   expert_w_gate[l], expert_w_up[l], expert_w_down[l], ln2_g[l], ln2_b[l], lambda_init)
    return x
```

```python
import functools
import math

import numpy as np
import jax
import jax.numpy as jnp
from jax import lax
from jax.experimental import pallas as pl
from jax.experimental.pallas import tpu as pltpu

F32 = jnp.float32
BF16 = jnp.bfloat16

D_MODEL = 2048
HEAD_DIM = 128
ROT_DIM = HEAD_DIM // 4
ROPE_THETA = 500000.0
NEG_INF = -1e30
BIG = 1e30
EPS = 1e-5

DIFF_HEADS = 4
DIFF_VDIM = 2 * HEAD_DIM

NSA_HEADS = 8
NSA_KV_GROUPS = 2
NSA_HPG = NSA_HEADS // NSA_KV_GROUPS
CMP_BLOCK = 32
CMP_STRIDE = 16
CMP_HIDDEN = 256
SLC_BLOCK = 64
SLC_TOPK = 16
WINDOW = 512

N_GROUPS = 4
EXPERTS_PER_GROUP = 8
N_EXPERTS = N_GROUPS * EXPERTS_PER_GROUP
EXPERT_HIDDEN = 512

DEPTH = 1
DN_ALPHA = (2.0 * DEPTH) ** 0.25

LANES = 128
VMEM_LIMIT = 56 * 1024 * 1024

U_NQ, U_DQ, U_DK, U_DV = 0, 8, 16, 24
U_KC, U_VC, U_KS, U_VS, U_KW, U_VW, U_GATE = 32, 34, 36, 38, 40, 42, 44
N_UNITS = 46
ROPE_BLOCKS = tuple(range(U_DQ // 2, U_DV // 2)) + (U_KS // 2, U_KW // 2)


def _dot(a, b):
    return jnp.dot(a, b, preferred_element_type=F32)


def _dot_nt(a, b):
    return lax.dot_general(a, b, (((1,), (1,)), ((), ())), preferred_element_type=F32)


def _params(sem, vmem=VMEM_LIMIT):
    return pltpu.CompilerParams(dimension_semantics=sem, vmem_limit_bytes=vmem)


PROJ_ROW_CHUNK = 512


def _rope(a, c, s):
    lane = lax.broadcasted_iota(jnp.int32, a.shape, 1)
    half = ROT_DIM // 2
    partner = jnp.where(lane < half, pltpu.roll(a, LANES - half, 1), pltpu.roll(a, half, 1))
    return a * c + partner * s


def _proj_kernel(flags_ref, x_ref, w_ref, cos_ref, sin_ref, o_ref, xb_sc):
    j = pl.program_id(1)

    @pl.when(j == 0)
    def _():
        xb_sc[...] = x_ref[...].astype(BF16)

    def block(rotary):
        for r0 in range(0, xb_sc.shape[0], PROJ_ROW_CHUNK):
            rs = slice(r0, r0 + PROJ_ROW_CHUNK)
            acc = _dot(xb_sc[rs, :], w_ref[...])
            if rotary:
                for hh in range(2):
                    cs = slice(hh * LANES, (hh + 1) * LANES)
                    o_ref[rs, cs] = _rope(acc[:, cs], cos_ref[rs, :], sin_ref[rs, :]).astype(o_ref.dtype)
            else:
                o_ref[rs, :] = acc.astype(o_ref.dtype)

    pl.when(flags_ref[j] == 0)(lambda: block(False))
    pl.when(flags_ref[j] != 0)(lambda: block(True))


def _proj_call(xb, w_aug, flags, cos_t, sin_t, seq, tm):
    n, d = xb.shape
    nj = w_aug.shape[1] // 256
    tpb = seq // tm
    grid_spec = pltpu.PrefetchScalarGridSpec(
        num_scalar_prefetch=1,
        grid=(n // tm, nj),
        in_specs=[
            pl.BlockSpec((tm, d), lambda i, j, f: (i, 0)),
            pl.BlockSpec((d, 256), lambda i, j, f: (0, j)),
            pl.BlockSpec((tm, LANES), lambda i, j, f: (i % tpb, 0)),
            pl.BlockSpec((tm, LANES), lambda i, j, f: (i % tpb, 0)),
        ],
        out_specs=pl.BlockSpec((tm, 256), lambda i, j, f: (i, j)),
        scratch_shapes=[pltpu.VMEM((tm, d), BF16)],
    )
    return pl.pallas_call(
        _proj_kernel,
        out_shape=jax.ShapeDtypeStruct((n, w_aug.shape[1]), BF16),
        grid_spec=grid_spec,
        compiler_params=_params(("parallel", "arbitrary")),
        name="proj",
    )(flags, xb, w_aug, cos_t, sin_t)


def _compress_kernel(t_ref, w1_ref, pos_ref, b1_ref, w2_ref, b2_ref, o_ref, t_sc):
    nt = t_sc.shape[0] // CMP_STRIDE
    t_sc[...] = t_ref[...].astype(F32)
    r = jnp.concatenate([t_sc[pl.ds(k, nt, stride=CMP_STRIDE), :] for k in range(CMP_STRIDE)], axis=1).astype(BF16)
    ab = _dot(r, w1_ref[...])
    pos = pos_ref[...]
    pa = jnp.broadcast_to(pos[0:1], (8, pos.shape[1])).astype(BF16)
    pb = jnp.broadcast_to(pos[1:2], (8, pos.shape[1])).astype(BF16)
    const = _dot(pa, w1_ref[:, :CMP_HIDDEN])[0:1] + _dot(pb, w1_ref[:, CMP_HIDDEN:])[0:1]
    h = ab[:, :CMP_HIDDEN] + pltpu.roll(ab[:, CMP_HIDDEN:], nt - 1, 0) + const + b1_ref[...]
    h = jax.nn.gelu(h)
    o_ref[...] = (_dot(h.astype(BF16), w2_ref[...]) + b2_ref[...]).astype(o_ref.dtype)


def _compress_call(proj, w1cat, pos2, b1, w2, b2, b, seq):
    four = 2 * NSA_KV_GROUPS
    nt = seq // CMP_STRIDE
    k = CMP_STRIDE * HEAD_DIM
    return pl.pallas_call(
        _compress_kernel,
        out_shape=jax.ShapeDtypeStruct((b, four, nt, HEAD_DIM), BF16),
        grid=(b, four),
        in_specs=[
            pl.BlockSpec((seq, HEAD_DIM), lambda i, c: (i, U_KC + c)),
            pl.BlockSpec((None, k, 2 * CMP_HIDDEN), lambda i, c: (c // 2, 0, 0)),
            pl.BlockSpec((None, 2, k), lambda i, c: (c // 2, 0, 0)),
            pl.BlockSpec((None, 1, CMP_HIDDEN), lambda i, c: (c // 2, 0, 0)),
            pl.BlockSpec((None, CMP_HIDDEN, HEAD_DIM), lambda i, c: (c // 2, 0, 0)),
            pl.BlockSpec((None, 1, HEAD_DIM), lambda i, c: (c // 2, 0, 0)),
        ],
        out_specs=pl.BlockSpec((None, None, nt, HEAD_DIM), lambda i, c: (i, c, 0, 0)),
        scratch_shapes=[pltpu.VMEM((seq, HEAD_DIM), F32)],
        compiler_params=_params(("parallel", "parallel")),
        name="compress",
    )(proj, w1cat, pos2, b1, w2, b2)


EXP2_SCALE = HEAD_DIM ** -0.5 * math.log2(math.e)
ONES_ROWS = 16


def _transpose_into(src_ref, dst_ref):
    def body(c, carry):
        off = pl.multiple_of(c * LANES, LANES)
        dst_ref[:, pl.ds(off, LANES)] = src_ref[pl.ds(off, LANES), :].astype(F32).T.astype(dst_ref.dtype)
        return carry

    lax.fori_loop(0, src_ref.shape[0] // LANES, body, 0)


def _diff_kernel(dl_ref, q_ref, k_ref, v_ref, g_ref, o_ref, vt_sc, acc_sc, sa_sc, sb_sc, *, tq, lambda_init):
    qi = pl.program_id(2)

    @pl.when(qi == 0)
    def _():
        for c in range(2):
            _transpose_into(v_ref.at[:, c * LANES:(c + 1) * LANES], vt_sc.at[c * LANES:(c + 1) * LANES, :])

    acc_sc[...] = jnp.zeros(acc_sc.shape, F32)
    q = q_ref[...]
    qpos = qi * tq + lax.broadcasted_iota(jnp.int32, (1, tq), 1)

    tk = tq // 2

    def scores(j, dst, lo=0):
        kt = k_ref[pl.ds(pl.multiple_of(j * tk, tk), tk), :]
        for c in range(2):
            dst[c, :, lo:] = _dot_nt(kt[:, c * HEAD_DIM:(c + 1) * HEAD_DIM], q[lo:, c * HEAD_DIM:(c + 1) * HEAD_DIM])

    def absorb(src, j, masked, carry, lo=0):
        off = pl.multiple_of(j * tk, tk)
        vt = vt_sc[:, pl.ds(off, tk)]
        out = []
        for c in range(2):
            m_old, l_old = carry[2 * c][:, lo:], carry[2 * c + 1][:, lo:]
            s = src[c, :, lo:]
            if masked:
                s = jnp.where(lax.broadcasted_iota(jnp.int32, s.shape, 0) <= qpos[:, lo:] - off, s, NEG_INF)
            m_new = jnp.maximum(m_old, jnp.max(s, axis=0, keepdims=True))
            p = jnp.exp2((s - m_new) * EXP2_SCALE)
            alpha = jnp.exp2((m_old - m_new) * EXP2_SCALE)
            l_new = alpha * l_old + jnp.sum(p, axis=0, keepdims=True)
            acc_sc[c, :, lo:] = alpha * acc_sc[c, :, lo:] + _dot(vt, p.astype(BF16))
            if lo:
                m_new = jnp.concatenate([carry[2 * c][:, :lo], m_new], axis=1)
                l_new = jnp.concatenate([carry[2 * c + 1][:, :lo], l_new], axis=1)
            out += [m_new, l_new]
        return tuple(out)

    def pair(p, carry):
        j = 2 * p
        scores(j + 1, sb_sc)
        carry = absorb(sa_sc, j, False, carry)
        scores(j + 2, sa_sc)
        return absorb(sb_sc, j + 1, False, carry)

    init = (jnp.full((1, tq), NEG_INF, F32), jnp.zeros((1, tq), F32)) * 2
    scores(0, sa_sc)
    carry = lax.fori_loop(0, qi, pair, init)
    j_tail = 2 * qi
    scores(j_tail + 1, sb_sc, lo=tk)
    carry = absorb(sa_sc, j_tail, True, carry)
    _, l0, _, l1 = absorb(sb_sc, j_tail + 1, True, carry, lo=tk)

    dl = dl_ref[...]
    lam = (jnp.exp(jnp.sum(dl[0:1] * dl[1:2], axis=1, keepdims=True))
           - jnp.exp(jnp.sum(dl[2:3] * dl[3:4], axis=1, keepdims=True)) + lambda_init)
    o = acc_sc[0] * (1.0 / l0) - lam * (acc_sc[1] * (1.0 / l1))
    o = o * (lax.rsqrt(jnp.mean(o * o, axis=0, keepdims=True) + EPS) * (1.0 - lambda_init))
    for c in range(DIFF_VDIM // LANES):
        for r in range(tq // LANES):
            blk = o[c * LANES:(c + 1) * LANES, r * LANES:(r + 1) * LANES].T
            o_ref[r * LANES:(r + 1) * LANES, c * LANES:(c + 1) * LANES] = (
                blk * g_ref[:, c * LANES:(c + 1) * LANES]).astype(o_ref.dtype)


def _diff_call(proj, dl, g, batch, seq, tq, lambda_init):
    nq = seq // tq
    kern = functools.partial(_diff_kernel, tq=tq, lambda_init=lambda_init)
    return pl.pallas_call(
        kern,
        out_shape=jax.ShapeDtypeStruct((batch * seq, DIFF_HEADS * DIFF_VDIM), BF16),
        grid=(batch, DIFF_HEADS, nq),
        in_specs=[
            pl.BlockSpec((4, HEAD_DIM), lambda b, h, i: (0, 0)),
            pl.BlockSpec((tq, 256), lambda b, h, i: (b * nq + i, U_DQ // 2 + h)),
            pl.BlockSpec((seq, 256), lambda b, h, i: (b, U_DK // 2 + h)),
            pl.BlockSpec((seq, 256), lambda b, h, i: (b, U_DV // 2 + h)),
            pl.BlockSpec((1, DIFF_VDIM), lambda b, h, i: (0, 0)),
        ],
        out_specs=pl.BlockSpec((tq, DIFF_VDIM), lambda b, h, i: (b * nq + i, h)),
        scratch_shapes=[
            pltpu.VMEM((DIFF_VDIM, seq), BF16),
            pltpu.VMEM((2, DIFF_VDIM, tq), F32),
            pltpu.VMEM((2, tq // 2, tq), F32),
            pltpu.VMEM((2, tq // 2, tq), F32),
        ],
        compiler_params=_params(("arbitrary", "arbitrary", "arbitrary")),
        name="diff_attn",
    )(dl, proj, proj, proj, g)


def _stack_heads(x):
    return jnp.concatenate([x[:, h * HEAD_DIM:(h + 1) * HEAD_DIM] for h in range(NSA_HPG)], axis=0)


def _heads(x):
    return jnp.concatenate([x] * NSA_HPG, axis=1)


def _mask_tables(seq, tq, tk):
    nt = seq // CMP_STRIDE
    slab = min(WINDOW + tq, seq)
    q = np.arange(tq)[None, :]
    k = np.arange(slab)[:, None]
    win = []
    for i in range(WINDOW // tq + 1):
        newest = q + (i * tq if i < WINDOW // tq else WINDOW)
        win.append((k <= newest) & (k > newest - WINDOW))
    u = np.arange(2 * nt)[:, None] - nt
    cmp_ok = u <= ((q - (CMP_BLOCK - 1)) >> int(math.log2(CMP_STRIDE)))
    kk = np.arange(tk)[:, None]
    causal = [kk <= d * tq + q for d in range(tk // tq)] + [np.ones((tk, tq), bool)]
    to_bias = lambda m: jnp.asarray(np.where(np.asarray(m), 0.0, NEG_INF), dtype=F32)
    return to_bias(np.stack(win)), to_bias(cmp_ok), to_bias(np.stack(causal))


def _nsa_kernel(q_ref, cos_ref, sin_ref, gate_ref, kc_ref, vc_ref, ks_ref, vs_ref, kw_ref, vw_ref, ovt_ref,
                wtab_ref, ptab_ref, ctab_ref, o_ref, vct_sc, vst_sc, vwt_sc, bias_sc, acc_sc, sa_sc, sb_sc, m_sc, *, tq, tk, seq, top_k):
    qi = pl.program_id(2)
    q0 = qi * tq
    rows = NSA_HPG * tq
    nt = kc_ref.shape[0]
    ns = ovt_ref.shape[0]
    nb = tk // SLC_BLOCK

    @pl.when(qi == 0)
    def _():
        _transpose_into(vc_ref, vct_sc)
        for v_ref, vt_sc in ((vs_ref, vst_sc), (vw_ref, vwt_sc)):
            _transpose_into(v_ref, vt_sc.at[:HEAD_DIM, :])
            vt_sc[HEAD_DIM:, :] = jnp.ones((ONES_ROWS, seq), vt_sc.dtype)

    q = q_ref[...]
    qs = _stack_heads(q)
    qr = jnp.concatenate(
        [_rope(q[:, h * HEAD_DIM:(h + 1) * HEAD_DIM].astype(F32), cos_ref[...], sin_ref[...]).astype(BF16)
         for h in range(NSA_HPG)], axis=0)
    qpos = q0 + (lax.broadcasted_iota(jnp.int32, (1, rows), 1) & (tq - 1))

    def sel_scores(j, dst):
        dst[...] = _dot_nt(ks_ref[pl.ds(pl.multiple_of(j * tk, tk), tk), :], qr)

    sel_scores(0, sa_sc)

    slab = min(WINDOW + tq, seq)
    start = pl.multiple_of(jnp.maximum(q0 - WINDOW, 0), tq)
    s_w = _dot_nt(kw_ref[pl.ds(start, slab), :], qr)
    s_w = s_w + _heads(wtab_ref[jnp.minimum(qi, WINDOW // tq)])
    p_w = jnp.exp2((s_w - jnp.max(s_w, axis=0, keepdims=True)) * EXP2_SCALE)
    o_w = _dot(vwt_sc[:, pl.ds(start, slab)], p_w.astype(BF16))
    o_w = o_w[:HEAD_DIM] * (1.0 / o_w[HEAD_DIM:HEAD_DIM + 1])

    s_c = _dot_nt(kc_ref[...], qs)
    c0 = pl.multiple_of(nt - q0 // CMP_STRIDE, CMP_STRIDE)
    s_c = s_c + _heads(ptab_ref[pl.ds(c0, nt), :])
    m_c = jnp.max(s_c, axis=0, keepdims=True)
    e_c = jnp.exp2((s_c - m_c) * EXP2_SCALE)
    l_c = jnp.sum(e_c, axis=0, keepdims=True)
    p_c = e_c * jnp.where(qpos >= CMP_BLOCK - 1, 1.0 / l_c, 0.0)
    o_c = _dot(vct_sc[...], p_c.astype(BF16))

    p_sum = p_c[:, 0:tq]
    for h in range(1, NSA_HPG):
        p_sum = p_sum + p_c[:, h * tq:(h + 1) * tq]
    p_hi = p_sum.astype(BF16)
    p_lo = (p_sum - p_hi.astype(F32)).astype(BF16)
    imp = _dot(ovt_ref[...], p_hi) + _dot(ovt_ref[...], p_lo)

    blk = lax.broadcasted_iota(jnp.int32, (ns, tq), 0)
    qpos_l = q0 + lax.broadcasted_iota(jnp.int32, (ns, tq), 1)
    cur = lax.shift_right_logical(qpos_l, int(math.log2(SLC_BLOCK)))
    valid_s = blk <= cur
    forced = (blk == 0) | (blk == cur) | (blk == cur - 1)
    work = jnp.where(forced, -jnp.inf, jnp.where(valid_s, imp, NEG_INF))
    blk_f = blk.astype(F32)
    for _ in range(top_k - 3):
        mx = jnp.max(work, axis=0, keepdims=True)
        idx = jnp.min(jnp.where(work == mx, blk_f, float(ns)), axis=0, keepdims=True)
        work = jnp.where(blk_f == idx, -jnp.inf, work)
    bias_sc[...] = jnp.where(work == -jnp.inf, 0.0, NEG_INF)

    acc_sc[...] = jnp.zeros(acc_sc.shape, F32)

    def sel_absorb(src, j, causal, m_old):
        off = pl.multiple_of(j * tk, tk)
        s = src[...]
        if causal:
            s = s + _heads(ctab_ref[jnp.clip((q0 - off) // tq, 0, tk // tq)])
        bias = bias_sc[pl.ds(pl.multiple_of(j * nb, nb), nb), :]
        bias = jnp.concatenate([bias] * NSA_HPG, axis=1)
        s3 = s.reshape(nb, SLC_BLOCK, rows) + bias[:, None, :]
        m_new = jnp.maximum(m_old, jnp.max(jnp.max(s3, axis=0), axis=0, keepdims=True))
        p3 = jnp.exp2((s3 - m_new) * EXP2_SCALE)
        alpha = jnp.exp2((m_old - m_new) * EXP2_SCALE)
        p = p3.reshape(tk, rows).astype(BF16)
        acc_sc[...] = alpha * acc_sc[...] + _dot(vst_sc[:, pl.ds(off, tk)], p)
        return m_new

    def sel_pair(p, carry):
        j = 2 * p
        sel_scores(j + 1, sb_sc)
        carry = sel_absorb(sa_sc, j, False, carry)
        sel_scores(j + 2, sa_sc)
        return sel_absorb(sb_sc, j + 1, False, carry)

    n_pairs = (q0 // tk + 2) // 2
    carry = lax.fori_loop(0, n_pairs - 1, sel_pair, jnp.full((1, rows), NEG_INF, F32))
    j_tail = 2 * (n_pairs - 1)
    m_sc[...] = sel_absorb(sa_sc, j_tail, True, carry)

    @pl.when(j_tail + 1 <= q0 // tk)
    def _():
        sel_scores(j_tail + 1, sb_sc)
        sel_absorb(sb_sc, j_tail + 1, True, m_sc[...])

    o_s = acc_sc[:HEAD_DIM, :] * (1.0 / acc_sc[HEAD_DIM:HEAD_DIM + 1, :])

    gates = jax.nn.sigmoid(gate_ref[...].astype(F32))
    gates = jnp.concatenate([gates[r * LANES:(r + 1) * LANES].T for r in range(tq // LANES)],
                            axis=1)
    for h in range(NSA_HPG):
        sl = slice(h * tq, (h + 1) * tq)
        o = (gates[3 * h:3 * h + 1] * o_c[:, sl] + gates[3 * h + 1:3 * h + 2] * o_s[:, sl]
             + gates[3 * h + 2:3 * h + 3] * o_w[:, sl])
        for r in range(tq // LANES):
            o_ref[r * LANES:(r + 1) * LANES, h * HEAD_DIM:(h + 1) * HEAD_DIM] = (
                o[:, r * LANES:(r + 1) * LANES].T.astype(o_ref.dtype))


def _nsa_call(proj, cos_t, sin_t, kvc, ovt, batch, seq, tq, tk):
    nq = seq // tq
    assert WINDOW % tq == 0 and tk % tq == 0 and tq % CMP_STRIDE == 0
    wtab, ptab, ctab = _mask_tables(seq, tq, tk)
    nt = kvc.shape[2]
    ns = ovt.shape[0]
    width = NSA_HPG * HEAD_DIM
    kern = functools.partial(_nsa_kernel, tq=tq, tk=tk, seq=seq, top_k=min(SLC_TOPK, ns))

    def col(unit):
        return pl.BlockSpec((seq, HEAD_DIM), lambda b, g, i: (b, unit + g))

    return pl.pallas_call(
        kern,
        out_shape=jax.ShapeDtypeStruct((batch * seq, NSA_HEADS * HEAD_DIM), BF16),
        grid=(batch, NSA_KV_GROUPS, nq),
        in_specs=[
            pl.BlockSpec((tq, width), lambda b, g, i: (b * nq + i, U_NQ // 4 + g)),
            pl.BlockSpec((tq, LANES), lambda b, g, i: (i, 0)),
            pl.BlockSpec((tq, LANES), lambda b, g, i: (i, 0)),
            pl.BlockSpec((tq, LANES), lambda b, g, i: (b * nq + i, U_GATE + g)),
            pl.BlockSpec((None, None, nt, HEAD_DIM), lambda b, g, i: (b, g, 0, 0)),
            pl.BlockSpec((None, None, nt, HEAD_DIM), lambda b, g, i: (b, 2 + g, 0, 0)),
            col(U_KS), col(U_VS), col(U_KW), col(U_VW),
            pl.BlockSpec((ns, nt), lambda b, g, i: (0, 0)),
            pl.BlockSpec(wtab.shape, lambda b, g, i: (0, 0, 0)),
            pl.BlockSpec(ptab.shape, lambda b, g, i: (0, 0)),
            pl.BlockSpec(ctab.shape, lambda b, g, i: (0, 0, 0)),
        ],
        out_specs=pl.BlockSpec((tq, width), lambda b, g, i: (b * nq + i, g)),
        scratch_shapes=[
            pltpu.VMEM((HEAD_DIM, nt), BF16),
            pltpu.VMEM((HEAD_DIM + ONES_ROWS, seq), BF16),
            pltpu.VMEM((HEAD_DIM + ONES_ROWS, seq), BF16),
            pltpu.VMEM((ns, tq), F32),
            pltpu.VMEM((HEAD_DIM + ONES_ROWS, NSA_HPG * tq), F32),
            pltpu.VMEM((tk, NSA_HPG * tq), F32),
            pltpu.VMEM((tk, NSA_HPG * tq), F32),
            pltpu.VMEM((1, NSA_HPG * tq), F32),
        ],
        compiler_params=_params(("arbitrary", "arbitrary", "arbitrary")),
        name="nsa_attn",
    )(proj, cos_t, sin_t, proj, kvc, kvc, proj, proj, proj, proj, ovt, wtab, ptab, ctab)


def _layernorm(y, g, b):
    mu = jnp.mean(y, axis=1, keepdims=True)
    yc = y - mu
    var = jnp.mean(yc * yc, axis=1, keepdims=True)
    return yc * lax.rsqrt(var + EPS) * g + b


N_CHUNKS = D_MODEL // LANES
ROW_PITCH = N_CHUNKS + 1


def _store_chunk_rows(ref, val):
    tm = val.shape[0]
    for k in range(N_CHUNKS):
        ref[pl.ds(k, tm, stride=ROW_PITCH), :] = val[:, k * LANES:(k + 1) * LANES]
    for k in range(N_CHUNKS, ROW_PITCH):
        ref[pl.ds(k, tm, stride=ROW_PITCH), :] = jnp.zeros((tm, LANES), ref.dtype)


def _load_chunk_rows(ref, tm):
    return jnp.concatenate([ref[pl.ds(k, tm, stride=ROW_PITCH), :] for k in range(N_CHUNKS)], axis=1)


OUT_ROW_CHUNK = 256


def _outproj_kernel(od_ref, on_ref, w_ref, x_ref, g_ref, b_ref, wr_ref, x1_ref, route_ref):
    wr = wr_ref[...]
    w_hi = wr.astype(BF16)
    w_lo = (wr - w_hi.astype(F32)).astype(BF16)
    for r0 in range(0, x_ref.shape[0], OUT_ROW_CHUNK):
        _outproj_rows(od_ref, on_ref, w_ref, x_ref, g_ref, b_ref, w_hi, w_lo, x1_ref, route_ref,
                      slice(r0, r0 + OUT_ROW_CHUNK))


def _outproj_rows(od_ref, on_ref, w_ref, x_ref, g_ref, b_ref, w_hi, w_lo, x1_ref, route_ref, rs):
    half = od_ref.shape[1]
    h = _dot(od_ref[rs, :], w_ref[:half, :]) + _dot(on_ref[rs, :], w_ref[half:, :])
    x1 = _layernorm(DN_ALPHA * x_ref[rs, :] + h, g_ref[...], b_ref[...])
    _store_chunk_rows(x1_ref.at[rs.start * ROW_PITCH:rs.stop * ROW_PITCH, :], x1)

    x_hi = x1.astype(BF16)
    x_lo = (x1 - x_hi.astype(F32)).astype(BF16)
    w2 = jnp.concatenate([w_hi, w_lo], axis=1)
    a = _dot(x_hi, w2)
    b = _dot(x_lo, w2)
    logits = (a[:, :LANES] + a[:, LANES:]) + (b[:, :LANES] + b[:, LANES:])

    lane = lax.broadcasted_iota(jnp.int32, logits.shape, 1).astype(F32)
    ninf = -jnp.inf
    gl = jnp.where(lane < N_GROUPS, logits, ninf)
    gmax = jnp.max(gl, axis=1, keepdims=True)
    g_w = 1.0 / jnp.sum(jnp.exp(gl - gmax), axis=1, keepdims=True)
    g_sel = jnp.min(jnp.where(gl == gmax, lane, float(LANES)), axis=1, keepdims=True)
    lo = N_GROUPS + EXPERTS_PER_GROUP * g_sel
    el = jnp.where((lane >= lo) & (lane < lo + EXPERTS_PER_GROUP), logits, ninf)
    e1 = jnp.max(el, axis=1, keepdims=True)
    i1 = jnp.min(jnp.where(el == e1, lane, float(LANES)), axis=1, keepdims=True)
    el2 = jnp.where(lane == i1, ninf, el)
    e2 = jnp.max(el2, axis=1, keepdims=True)
    i2 = jnp.min(jnp.where(el2 == e2, lane, float(LANES)), axis=1, keepdims=True)
    r = jnp.exp(e2 - e1)
    w1 = g_w / (1.0 + r)
    w2 = g_w * r / (1.0 + r)
    route = jnp.where(lane == 0, i1 - N_GROUPS,
                      jnp.where(lane == 1, i2 - N_GROUPS,
                                jnp.where(lane == 2, w1, jnp.where(lane == 3, w2, 0.0))))
    route_ref[rs, :] = route


def _outproj_call(o_diff, o_nsa, w_out, x2, g, b, wr, tm):
    n, d = x2.shape
    half = o_diff.shape[1]
    return pl.pallas_call(
        _outproj_kernel,
        out_shape=(jax.ShapeDtypeStruct((n * ROW_PITCH, LANES), F32), jax.ShapeDtypeStruct((n, LANES), F32)),
        grid=(n // tm,),
        in_specs=[
            pl.BlockSpec((tm, half), lambda i: (i, 0)),
            pl.BlockSpec((tm, half), lambda i: (i, 0)),
            pl.BlockSpec((2 * half, d), lambda i: (0, 0)),
            pl.BlockSpec((tm, d), lambda i: (i, 0)),
            pl.BlockSpec((1, d), lambda i: (0, 0)),
            pl.BlockSpec((1, d), lambda i: (0, 0)),
            pl.BlockSpec((d, LANES), lambda i: (0, 0)),
        ],
        out_specs=(pl.BlockSpec((tm * ROW_PITCH, LANES), lambda i: (i, 0)),
                   pl.BlockSpec((tm, LANES), lambda i: (i, 0))),
        compiler_params=_params(("parallel",)),
        name="outproj_ln1_router",
    )(o_diff, o_nsa, w_out, x2, g, b, wr)


MOE_TILE_ROWS = 256
COMBINE_TILE_TOKENS = 256


def _token_copy(src_hbm, dst, sem, src_tok, dst_tok):
    return pltpu.make_async_copy(src_hbm.at[pl.ds(src_tok * ROW_PITCH, N_CHUNKS)],
                                 dst.at[pl.ds(dst_tok * ROW_PITCH, N_CHUNKS)], sem)


def _wait_tokens(src_hbm, dst, sem, count):
    pltpu.make_async_copy(src_hbm.at[pl.ds(0, count * N_CHUNKS)], dst.at[pl.ds(0, count * N_CHUNKS)], sem).wait()


def _moe_kernel(te_ref, nu_ref, rows_ref, rows_next_ref, x_hbm, wg_ref, wu_ref, wd_ref, o_ref,
                xbuf, sem, wg_sc, wu_sc, wd_sc, *, tm):
    i = pl.program_id(0)
    n_used = nu_ref[0]
    slot = lax.rem(i, 2)

    def issue(ids_ref, s):
        def body(r, carry):
            _token_copy(x_hbm, xbuf.at[s], sem.at[s], ids_ref[0, 0, r], r).start()
            return carry

        lax.fori_loop(0, tm, body, 0, unroll=8)

    @pl.when(i == 0)
    def _():
        issue(rows_ref, 0)

    @pl.when(i + 1 < n_used)
    def _():
        issue(rows_next_ref, 1 - slot)

    prev = te_ref[jnp.maximum(i - 1, 0)]

    @pl.when((i == 0) | (te_ref[i] != prev))
    def _():
        wg_sc[...] = wg_ref[...].astype(BF16)
        wu_sc[...] = wu_ref[...].astype(BF16)
        wd_sc[...] = wd_ref[...].astype(BF16)

    @pl.when(i < n_used)
    def _():
        _wait_tokens(x_hbm, xbuf.at[slot], sem.at[slot], tm)
        xb = _load_chunk_rows(xbuf.at[slot], tm).astype(BF16)
        gate = _dot(xb, wg_sc[...])
        up = _dot(xb, wu_sc[...])
        h = (gate * jax.nn.sigmoid(gate) * up).astype(BF16)
        _store_chunk_rows(o_ref, _dot(h, wd_sc[...]))

    @pl.when(i >= n_used)
    def _():
        o_ref[...] = jnp.zeros(o_ref.shape, o_ref.dtype)


def _moe_call(tile_expert, n_used, rows, x1c, w_gate, w_up, w_down, tm):
    n_tiles = rows.shape[0]
    d, f = w_gate.shape[1], w_gate.shape[2]
    grid_spec = pltpu.PrefetchScalarGridSpec(
        num_scalar_prefetch=2,
        grid=(n_tiles,),
        in_specs=[
            pl.BlockSpec((1, 1, tm), lambda i, te, nu: (i, 0, 0), memory_space=pltpu.SMEM),
            pl.BlockSpec((1, 1, tm), lambda i, te, nu: (jnp.minimum(i + 1, n_tiles - 1), 0, 0),
                         memory_space=pltpu.SMEM),
            pl.BlockSpec(memory_space=pl.ANY),
            pl.BlockSpec((None, d, f), lambda i, te, nu: (te[i], 0, 0)),
            pl.BlockSpec((None, d, f), lambda i, te, nu: (te[i], 0, 0)),
            pl.BlockSpec((None, f, d), lambda i, te, nu: (te[i], 0, 0)),
        ],
        out_specs=pl.BlockSpec((tm * ROW_PITCH, LANES), lambda i, te, nu: (i, 0)),
        scratch_shapes=[
            pltpu.VMEM((2, tm * ROW_PITCH, LANES), F32),
            pltpu.SemaphoreType.DMA((2,)),
            pltpu.VMEM((d, f), BF16), pltpu.VMEM((d, f), BF16), pltpu.VMEM((f, d), BF16),
        ],
    )
    return pl.pallas_call(
        functools.partial(_moe_kernel, tm=tm),
        out_shape=jax.ShapeDtypeStruct((n_tiles * tm * ROW_PITCH, LANES), F32),
        grid_spec=grid_spec,
        compiler_params=_params(("arbitrary",)),
        name="moe_experts",
    )(tile_expert, n_used, rows, rows, x1c, w_gate, w_up, w_down)


def _combine_kernel(slots_ref, slots_next_ref, y_hbm, x1_ref, route_ref, g_ref, b_ref, o_ref, buf, sem, *, tm):
    i = pl.program_id(0)
    slot = lax.rem(i, 2)

    @pl.when(i == 0)
    def _():
        def body(r, carry):
            for k in range(2):
                _token_copy(y_hbm, buf.at[0, k], sem.at[0], slots_ref[0, 0, 2 * r + k], r).start()
            return carry

        lax.fori_loop(0, tm, body, 0, unroll=8)

    def tile(prefetch):
        if prefetch:
            for r in range(tm):
                for k in range(2):
                    _token_copy(y_hbm, buf.at[1 - slot, k], sem.at[1 - slot],
                                slots_next_ref[0, 0, 2 * r + k], r).start(priority=k)
        for k in range(2):
            _wait_tokens(y_hbm, buf.at[slot, k], sem.at[slot], tm)
        route = route_ref[...]
        y = (route[:, 2:3] * _load_chunk_rows(buf.at[slot, 0], tm)
             + route[:, 3:4] * _load_chunk_rows(buf.at[slot, 1], tm))
        x1 = _load_chunk_rows(x1_ref, tm)
        o_ref[...] = _layernorm(DN_ALPHA * x1 + y, g_ref[...], b_ref[...])

    last = pl.num_programs(0) - 1
    pl.when(i < last)(lambda: tile(True))
    pl.when(i == last)(lambda: tile(False))


def _combine_call(slots, yc, x1c, route, g, b, tm):
    n = route.shape[0]
    d = D_MODEL
    nt = n // tm
    return pl.pallas_call(
        functools.partial(_combine_kernel, tm=tm),
        out_shape=jax.ShapeDtypeStruct((n, d), F32),
        grid=(nt,),
        in_specs=[
            pl.BlockSpec((1, 1, 2 * tm), lambda i: (i, 0, 0), memory_space=pltpu.SMEM),
            pl.BlockSpec((1, 1, 2 * tm), lambda i: (jnp.minimum(i + 1, nt - 1), 0, 0), memory_space=pltpu.SMEM),
            pl.BlockSpec(memory_space=pl.ANY),
            pl.BlockSpec((tm * ROW_PITCH, LANES), lambda i: (i, 0)),
            pl.BlockSpec((tm, LANES), lambda i: (i, 0)),
            pl.BlockSpec((1, d), lambda i: (0, 0)),
            pl.BlockSpec((1, d), lambda i: (0, 0)),
        ],
        out_specs=pl.BlockSpec((tm, d), lambda i: (i, 0)),
        scratch_shapes=[pltpu.VMEM((2, 2, tm * ROW_PITCH, LANES), F32), pltpu.SemaphoreType.DMA((2,))],
        compiler_params=_params(("arbitrary",)),
        name="moe_combine_ln2",
    )(slots, slots, yc, x1c, route, g, b)


def _moe_plan(e_idx, tm, n_tiles):
    n = e_idx.shape[0]
    lanes = jnp.arange(N_EXPERTS, dtype=jnp.int32)[None, :]
    hit = ((e_idx[:, 0:1] == lanes) | (e_idx[:, 1:2] == lanes)).astype(jnp.int32)
    before = jnp.cumsum(hit, axis=0) - hit
    counts = jnp.sum(hit, axis=0)
    ptiles = (counts + tm - 1) // tm
    tile_end = jnp.cumsum(ptiles)
    base = (tile_end - ptiles) * tm
    slot = jnp.take_along_axis(before + base[None, :], e_idx, axis=1).reshape(-1)
    rows = jnp.zeros((n_tiles * tm,), jnp.int32).at[slot].set(jnp.arange(2 * n, dtype=jnp.int32) // 2)
    tile_ids = jnp.arange(n_tiles, dtype=jnp.int32)
    tile_expert = jnp.minimum(jnp.sum((tile_end[None, :] <= tile_ids[:, None]).astype(jnp.int32), axis=1),
                              N_EXPERTS - 1)
    return slot.astype(jnp.int32), rows, tile_expert, tile_end[-1:].astype(jnp.int32)


def _rope_tables(seq):
    inv_freq = ROPE_THETA ** (-np.arange(0, ROT_DIM, 2, dtype=np.float64) / ROT_DIM)
    ang = np.arange(seq, dtype=np.float64)[:, None] * inv_freq[None, :]
    cos, sin = np.cos(ang), np.sin(ang)
    pad1 = np.ones((seq, HEAD_DIM - ROT_DIM))
    pad0 = np.zeros((seq, HEAD_DIM - ROT_DIM))
    return (jnp.asarray(np.concatenate([cos, cos, pad1], axis=1), dtype=F32),
            jnp.asarray(np.concatenate([-sin, sin, pad0], axis=1), dtype=F32))


def _layer(x, w_in, diff_lambda, diff_subln_g, cmp_pos, cmp_w1, cmp_b1, cmp_w2, cmp_b2, w_out,
           ln1_g, ln1_b, router_group, router_expert, w_gate, w_up, w_down, ln2_g, ln2_b, lambda_init):
    batch, seq, d = x.shape
    n = batch * seq
    x2 = x.reshape(n, d)

    seg = lambda a, b: w_in[:, a:b]
    pad = lambda w: jnp.pad(w, ((0, 0), (0, LANES - w.shape[1])))
    gate_w = seg(5632, 5656)
    w_aug = jnp.concatenate(
        [seg(3072, 4096), seg(0, 1024), seg(1024, 2048), seg(2048, 3072), seg(4096, 5632),
         pad(gate_w[:, :12]), pad(gate_w[:, 12:])], axis=1).astype(BF16)
    flags = jnp.zeros((N_UNITS // 2,), jnp.int32).at[jnp.array(ROPE_BLOCKS)].set(1)
    cos_t, sin_t = _rope_tables(seq)
    proj = _proj_call(x2, w_aug, flags, cos_t, sin_t, seq, tm=min(2048, seq))

    nt = seq // CMP_STRIDE
    half_feat = CMP_STRIDE * HEAD_DIM
    w1cat = jnp.concatenate([cmp_w1[:, :half_feat], cmp_w1[:, half_feat:]], axis=2).astype(BF16)
    kvc = _compress_call(proj, w1cat, cmp_pos.reshape(2, 2, half_feat), cmp_b1[:, None, :],
                         cmp_w2.astype(BF16), cmp_b2[:, None, :], batch, seq)

    o_diff = _diff_call(proj, diff_lambda, diff_subln_g[None, :], batch, seq, min(1024, seq), lambda_init)

    ns = seq // SLC_BLOCK
    ci = np.arange(nt)[None, :] * CMP_STRIDE
    sj = np.arange(ns)[:, None] * SLC_BLOCK
    ovt = jnp.asarray((ci < sj + SLC_BLOCK) & (ci + CMP_BLOCK > sj) & (np.arange(nt)[None, :] < nt - 1), dtype=BF16)
    o_nsa = _nsa_call(proj, cos_t, sin_t, kvc, ovt, batch, seq, tq=2 * LANES, tk=min(512, seq))

    wr = jnp.pad(jnp.concatenate([router_group, router_expert], axis=1),
                 ((0, 0), (0, LANES - N_GROUPS - N_EXPERTS)))
    x1, route = _outproj_call(o_diff, o_nsa, w_out.astype(BF16), x2, ln1_g[None, :], ln1_b[None, :], wr, tm=512)

    tm = MOE_TILE_ROWS
    n_tiles = -(-2 * n // tm) + N_EXPERTS
    e_idx = route[:, 0:2].astype(jnp.int32)
    slot, rows, tile_expert, n_used = _moe_plan(e_idx, tm, n_tiles)
    ys = _moe_call(tile_expert, n_used, rows.reshape(n_tiles, 1, tm), x1, w_gate, w_up, w_down, tm)
    tc = COMBINE_TILE_TOKENS
    out = _combine_call(slot.reshape(n // tc, 1, 2 * tc), ys, x1, route, ln2_g[None, :], ln2_b[None, :], tc)
    return out.reshape(batch, seq, d)


def kernel(x, w_in, diff_lambda, diff_subln_g, cmp_pos, cmp_w1, cmp_b1, cmp_w2, cmp_b2, w_out, ln1_g, ln1_b,
           router_group, router_expert, expert_w_gate, expert_w_up, expert_w_down, ln2_g, ln2_b):
    for l in range(DEPTH):
        lambda_init = 0.8 - 0.6 * math.exp(-0.3 * l)
        x = _layer(x, w_in[l], diff_lambda[l], diff_subln_g[l], cmp_pos[l], cmp_w1[l], cmp_b1[l], cmp_w2[l],
                   cmp_b2[l], w_out[l], ln1_g[l], ln1_b[l], router_group[l], router_expert[l],
                   expert_w_gate[l], expert_w_up[l], expert_w_down[l], ln2_g[l], ln2_b[l], lambda_init)
    return x
```

```python
import functools
import math

import numpy as np
import jax
import jax.numpy as jnp
from jax import lax
from jax.experimental import pallas as pl
from jax.experimental.pallas import tpu as pltpu

F32 = jnp.float32
BF16 = jnp.bfloat16

D_MODEL = 2048
HEAD_DIM = 128
ROT_DIM = HEAD_DIM // 4
ROPE_THETA = 500000.0
NEG_INF = -1e30
BIG = 1e30
EPS = 1e-5

DIFF_HEADS = 4
DIFF_VDIM = 2 * HEAD_DIM

NSA_HEADS = 8
NSA_KV_GROUPS = 2
NSA_HPG = NSA_HEADS // NSA_KV_GROUPS
CMP_BLOCK = 32
CMP_STRIDE = 16
CMP_HIDDEN = 256
SLC_BLOCK = 64
SLC_TOPK = 16
WINDOW = 512

N_GROUPS = 4
EXPERTS_PER_GROUP = 8
N_EXPERTS = N_GROUPS * EXPERTS_PER_GROUP
EXPERT_HIDDEN = 512

DEPTH = 1
DN_ALPHA = (2.0 * DEPTH) ** 0.25

LANES = 128
VMEM_LIMIT = 56 * 1024 * 1024

U_NQ, U_DQ, U_DK, U_DV = 0, 8, 16, 24
U_KC, U_VC, U_KS, U_VS, U_KW, U_VW, U_GATE = 32, 34, 36, 38, 40, 42, 44
N_UNITS = 46
ROPE_BLOCKS = tuple(range(U_DQ // 2, U_DV // 2)) + (U_KS // 2, U_KW // 2)


def _dot(a, b):
    return jnp.dot(a, b, preferred_element_type=F32)


def _dot_nt(a, b):
    return lax.dot_general(a, b, (((1,), (1,)), ((), ())), preferred_element_type=F32)


def _params(sem, vmem=VMEM_LIMIT):
    return pltpu.CompilerParams(dimension_semantics=sem, vmem_limit_bytes=vmem)


PROJ_ROW_CHUNK = 512


def _rope(a, c, s):
    lane = lax.broadcasted_iota(jnp.int32, a.shape, 1)
    half = ROT_DIM // 2
    partner = jnp.where(lane < half, pltpu.roll(a, LANES - half, 1), pltpu.roll(a, half, 1))
    return a * c + partner * s


def _proj_kernel(flags_ref, x_ref, w_ref, cos_ref, sin_ref, o_ref, xb_sc):
    j = pl.program_id(1)

    @pl.when(j == 0)
    def _():
        xb_sc[...] = x_ref[...].astype(BF16)

    def block(rotary):
        for r0 in range(0, xb_sc.shape[0], PROJ_ROW_CHUNK):
            rs = slice(r0, r0 + PROJ_ROW_CHUNK)
            acc = _dot(xb_sc[rs, :], w_ref[...])
            if rotary:
                for hh in range(2):
                    cs = slice(hh * LANES, (hh + 1) * LANES)
                    o_ref[rs, cs] = _rope(acc[:, cs], cos_ref[rs, :], sin_ref[rs, :]).astype(o_ref.dtype)
            else:
                o_ref[rs, :] = acc.astype(o_ref.dtype)

    pl.when(flags_ref[j] == 0)(lambda: block(False))
    pl.when(flags_ref[j] != 0)(lambda: block(True))


def _proj_call(xb, w_aug, flags, cos_t, sin_t, seq, tm):
    n, d = xb.shape
    nj = w_aug.shape[1] // 256
    tpb = seq // tm
    grid_spec = pltpu.PrefetchScalarGridSpec(
        num_scalar_prefetch=1,
        grid=(n // tm, nj),
        in_specs=[
            pl.BlockSpec((tm, d), lambda i, j, f: (i, 0)),
            pl.BlockSpec((d, 256), lambda i, j, f: (0, j)),
            pl.BlockSpec((tm, LANES), lambda i, j, f: (i % tpb, 0)),
            pl.BlockSpec((tm, LANES), lambda i, j, f: (i % tpb, 0)),
        ],
        out_specs=pl.BlockSpec((tm, 256), lambda i, j, f: (i, j)),
        scratch_shapes=[pltpu.VMEM((tm, d), BF16)],
    )
    return pl.pallas_call(
        _proj_kernel,
        out_shape=jax.ShapeDtypeStruct((n, w_aug.shape[1]), BF16),
        grid_spec=grid_spec,
        compiler_params=_params(("parallel", "arbitrary")),
        name="proj",
    )(flags, xb, w_aug, cos_t, sin_t)


def _compress_kernel(t_ref, w1_ref, pos_ref, b1_ref, w2_ref, b2_ref, o_ref, t_sc):
    nt = t_sc.shape[0] // CMP_STRIDE
    t_sc[...] = t_ref[...].astype(F32)
    r = jnp.concatenate([t_sc[pl.ds(k, nt, stride=CMP_STRIDE), :] for k in range(CMP_STRIDE)], axis=1).astype(BF16)
    ab = _dot(r, w1_ref[...])
    pos = pos_ref[...]
    pa = jnp.broadcast_to(pos[0:1], (8, pos.shape[1])).astype(BF16)
    pb = jnp.broadcast_to(pos[1:2], (8, pos.shape[1])).astype(BF16)
    const = _dot(pa, w1_ref[:, :CMP_HIDDEN])[0:1] + _dot(pb, w1_ref[:, CMP_HIDDEN:])[0:1]
    h = ab[:, :CMP_HIDDEN] + pltpu.roll(ab[:, CMP_HIDDEN:], nt - 1, 0) + const + b1_ref[...]
    h = jax.nn.gelu(h)
    o_ref[...] = (_dot(h.astype(BF16), w2_ref[...]) + b2_ref[...]).astype(o_ref.dtype)


def _compress_call(proj, w1cat, pos2, b1, w2, b2, b, seq):
    four = 2 * NSA_KV_GROUPS
    nt = seq // CMP_STRIDE
    k = CMP_STRIDE * HEAD_DIM
    return pl.pallas_call(
        _compress_kernel,
        out_shape=jax.ShapeDtypeStruct((b, four, nt, HEAD_DIM), BF16),
        grid=(b, four),
        in_specs=[
            pl.BlockSpec((seq, HEAD_DIM), lambda i, c: (i, U_KC + c)),
            pl.BlockSpec((None, k, 2 * CMP_HIDDEN), lambda i, c: (c // 2, 0, 0)),
            pl.BlockSpec((None, 2, k), lambda i, c: (c // 2, 0, 0)),
            pl.BlockSpec((None, 1, CMP_HIDDEN), lambda i, c: (c // 2, 0, 0)),
            pl.BlockSpec((None, CMP_HIDDEN, HEAD_DIM), lambda i, c: (c // 2, 0, 0)),
            pl.BlockSpec((None, 1, HEAD_DIM), lambda i, c: (c // 2, 0, 0)),
        ],
        out_specs=pl.BlockSpec((None, None, nt, HEAD_DIM), lambda i, c: (i, c, 0, 0)),
        scratch_shapes=[pltpu.VMEM((seq, HEAD_DIM), F32)],
        compiler_params=_params(("parallel", "parallel")),
        name="compress",
    )(proj, w1cat, pos2, b1, w2, b2)


EXP2_SCALE = HEAD_DIM ** -0.5 * math.log2(math.e)
ONES_ROWS = 16


def _transpose_into(src_ref, dst_ref):
    def body(c, carry):
        off = pl.multiple_of(c * LANES, LANES)
        dst_ref[:, pl.ds(off, LANES)] = src_ref[pl.ds(off, LANES), :].astype(F32).T.astype(dst_ref.dtype)
        return carry

    lax.fori_loop(0, src_ref.shape[0] // LANES, body, 0)


def _diff_kernel(dl_ref, q_ref, k_ref, v_ref, g_ref, o_ref, vt_sc, acc_sc, sa_sc, sb_sc, *, tq, lambda_init):
    qi = pl.program_id(2)

    @pl.when(qi == 0)
    def _():
        for c in range(2):
            _transpose_into(v_ref.at[:, c * LANES:(c + 1) * LANES], vt_sc.at[c * LANES:(c + 1) * LANES, :])

    acc_sc[...] = jnp.zeros(acc_sc.shape, F32)
    q = q_ref[...]
    qpos = qi * tq + lax.broadcasted_iota(jnp.int32, (1, tq), 1)

    tk = tq // 2

    def scores(j, dst, lo=0):
        kt = k_ref[pl.ds(pl.multiple_of(j * tk, tk), tk), :]
        for c in range(2):
            dst[c, :, lo:] = _dot_nt(kt[:, c * HEAD_DIM:(c + 1) * HEAD_DIM], q[lo:, c * HEAD_DIM:(c + 1) * HEAD_DIM])

    def absorb(src, j, masked, carry, lo=0):
        off = pl.multiple_of(j * tk, tk)
        vt = vt_sc[:, pl.ds(off, tk)]
        out = []
        for c in range(2):
            m_old, l_old = carry[2 * c][:, lo:], carry[2 * c + 1][:, lo:]
            s = src[c, :, lo:]
            if masked:
                s = jnp.where(lax.broadcasted_iota(jnp.int32, s.shape, 0) <= qpos[:, lo:] - off, s, NEG_INF)
            m_new = jnp.maximum(m_old, jnp.max(s, axis=0, keepdims=True))
            p = jnp.exp2(s - m_new)
            alpha = jnp.exp2(m_old - m_new)
            l_new = alpha * l_old + jnp.sum(p, axis=0, keepdims=True)
            acc_sc[c, :, lo:] = alpha * acc_sc[c, :, lo:] + _dot(vt, p.astype(BF16))
            if lo:
                m_new = jnp.concatenate([carry[2 * c][:, :lo], m_new], axis=1)
                l_new = jnp.concatenate([carry[2 * c + 1][:, :lo], l_new], axis=1)
            out += [m_new, l_new]
        return tuple(out)

    def pair(p, carry):
        j = 2 * p
        scores(j + 1, sb_sc)
        carry = absorb(sa_sc, j, False, carry)
        scores(j + 2, sa_sc)
        return absorb(sb_sc, j + 1, False, carry)

    init = (jnp.full((1, tq), NEG_INF, F32), jnp.zeros((1, tq), F32)) * 2
    scores(0, sa_sc)
    carry = lax.fori_loop(0, qi, pair, init)
    j_tail = 2 * qi
    scores(j_tail + 1, sb_sc, lo=tk)
    carry = absorb(sa_sc, j_tail, True, carry)
    _, l0, _, l1 = absorb(sb_sc, j_tail + 1, True, carry, lo=tk)

    dl = dl_ref[...]
    lam = (jnp.exp(jnp.sum(dl[0:1] * dl[1:2], axis=1, keepdims=True))
           - jnp.exp(jnp.sum(dl[2:3] * dl[3:4], axis=1, keepdims=True)) + lambda_init)
    o = acc_sc[0] * (1.0 / l0) - lam * (acc_sc[1] * (1.0 / l1))
    o = o * (lax.rsqrt(jnp.mean(o * o, axis=0, keepdims=True) + EPS) * (1.0 - lambda_init))
    for c in range(DIFF_VDIM // LANES):
        for r in range(tq // LANES):
            blk = o[c * LANES:(c + 1) * LANES, r * LANES:(r + 1) * LANES].T
            o_ref[r * LANES:(r + 1) * LANES, c * LANES:(c + 1) * LANES] = (
                blk * g_ref[:, c * LANES:(c + 1) * LANES]).astype(o_ref.dtype)


def _diff_call(proj, dl, g, batch, seq, tq, lambda_init):
    nq = seq // tq
    kern = functools.partial(_diff_kernel, tq=tq, lambda_init=lambda_init)
    return pl.pallas_call(
        kern,
        out_shape=jax.ShapeDtypeStruct((batch * seq, DIFF_HEADS * DIFF_VDIM), BF16),
        grid=(batch, DIFF_HEADS, nq),
        in_specs=[
            pl.BlockSpec((4, HEAD_DIM), lambda b, h, i: (0, 0)),
            pl.BlockSpec((tq, 256), lambda b, h, i: (b * nq + i, U_DQ // 2 + h)),
            pl.BlockSpec((seq, 256), lambda b, h, i: (b, U_DK // 2 + h)),
            pl.BlockSpec((seq, 256), lambda b, h, i: (b, U_DV // 2 + h)),
            pl.BlockSpec((1, DIFF_VDIM), lambda b, h, i: (0, 0)),
        ],
        out_specs=pl.BlockSpec((tq, DIFF_VDIM), lambda b, h, i: (b * nq + i, h)),
        scratch_shapes=[
            pltpu.VMEM((DIFF_VDIM, seq), BF16),
            pltpu.VMEM((2, DIFF_VDIM, tq), F32),
            pltpu.VMEM((2, tq // 2, tq), F32),
            pltpu.VMEM((2, tq // 2, tq), F32),
        ],
        compiler_params=_params(("arbitrary", "arbitrary", "arbitrary")),
        name="diff_attn",
    )(dl, proj, proj, proj, g)


def _stack_heads(x):
    return jnp.concatenate([x[:, h * HEAD_DIM:(h + 1) * HEAD_DIM] for h in range(NSA_HPG)], axis=0)


def _heads(x):
    return jnp.concatenate([x] * NSA_HPG, axis=1)


def _mask_tables(seq, tq, tk):
    nt = seq // CMP_STRIDE
    slab = min(WINDOW + tq, seq)
    q = np.arange(tq)[None, :]
    k = np.arange(slab)[:, None]
    win = []
    for i in range(WINDOW // tq + 1):
        newest = q + (i * tq if i < WINDOW // tq else WINDOW)
        win.append((k <= newest) & (k > newest - WINDOW))
    u = np.arange(2 * nt)[:, None] - nt
    cmp_ok = u <= ((q - (CMP_BLOCK - 1)) >> int(math.log2(CMP_STRIDE)))
    kk = np.arange(tk)[:, None]
    causal = [kk <= d * tq + q for d in range(tk // tq)] + [np.ones((tk, tq), bool)]
    to_bias = lambda m: jnp.asarray(np.where(np.asarray(m), 0.0, NEG_INF), dtype=F32)
    return to_bias(np.stack(win)), to_bias(cmp_ok), to_bias(np.stack(causal))


def _nsa_kernel(q_ref, cos_ref, sin_ref, gate_ref, kc_ref, vc_ref, ks_ref, vs_ref, kw_ref, vw_ref, ovt_ref,
                wtab_ref, ptab_ref, ctab_ref, o_ref, vct_sc, vst_sc, vwt_sc, bias_sc, acc_sc, sa_sc, sb_sc, m_sc, *, tq, tk, seq, top_k):
    qi = pl.program_id(2)
    q0 = qi * tq
    rows = NSA_HPG * tq
    nt = kc_ref.shape[0]
    ns = ovt_ref.shape[0]
    nb = tk // SLC_BLOCK

    @pl.when(qi == 0)
    def _():
        _transpose_into(vc_ref, vct_sc)
        for v_ref, vt_sc in ((vs_ref, vst_sc), (vw_ref, vwt_sc)):
            _transpose_into(v_ref, vt_sc.at[:HEAD_DIM, :])
            vt_sc[HEAD_DIM:, :] = jnp.ones((ONES_ROWS, seq), vt_sc.dtype)

    q = q_ref[...]
    qs = _stack_heads(q)
    qr = jnp.concatenate(
        [_rope(q[:, h * HEAD_DIM:(h + 1) * HEAD_DIM].astype(F32), cos_ref[...], sin_ref[...]).astype(BF16)
         for h in range(NSA_HPG)], axis=0)
    qpos = q0 + (lax.broadcasted_iota(jnp.int32, (1, rows), 1) & (tq - 1))

    def sel_scores(j, dst):
        dst[...] = _dot_nt(ks_ref[pl.ds(pl.multiple_of(j * tk, tk), tk), :], qr)

    sel_scores(0, sa_sc)

    slab = min(WINDOW + tq, seq)
    start = pl.multiple_of(jnp.maximum(q0 - WINDOW, 0), tq)
    s_w = _dot_nt(kw_ref[pl.ds(start, slab), :], qr)
    s_w = s_w + _heads(wtab_ref[jnp.minimum(qi, WINDOW // tq)])
    p_w = jnp.exp2(s_w - jnp.max(s_w, axis=0, keepdims=True))
    o_w = _dot(vwt_sc[:, pl.ds(start, slab)], p_w.astype(BF16))
    o_w = o_w[:HEAD_DIM] * (1.0 / o_w[HEAD_DIM:HEAD_DIM + 1])

    s_c = _dot_nt(kc_ref[...], qs)
    c0 = pl.multiple_of(nt - q0 // CMP_STRIDE, CMP_STRIDE)
    s_c = s_c + _heads(ptab_ref[pl.ds(c0, nt), :])
    m_c = jnp.max(s_c, axis=0, keepdims=True)
    e_c = jnp.exp2(s_c - m_c)
    l_c = jnp.sum(e_c, axis=0, keepdims=True)
    p_c = e_c * jnp.where(qpos >= CMP_BLOCK - 1, 1.0 / l_c, 0.0)
    o_c = _dot(vct_sc[...], p_c.astype(BF16))

    p_sum = p_c[:, 0:tq]
    for h in range(1, NSA_HPG):
        p_sum = p_sum + p_c[:, h * tq:(h + 1) * tq]
    p_hi = p_sum.astype(BF16)
    p_lo = (p_sum - p_hi.astype(F32)).astype(BF16)
    imp = _dot(ovt_ref[...], p_hi) + _dot(ovt_ref[...], p_lo)

    blk = lax.broadcasted_iota(jnp.int32, (ns, tq), 0)
    qpos_l = q0 + lax.broadcasted_iota(jnp.int32, (ns, tq), 1)
    cur = lax.shift_right_logical(qpos_l, int(math.log2(SLC_BLOCK)))
    valid_s = blk <= cur
    forced = (blk == 0) | (blk == cur) | (blk == cur - 1)
    work = jnp.where(forced, -jnp.inf, jnp.where(valid_s, imp, NEG_INF))
    blk_f = blk.astype(F32)
    for _ in range(top_k - 3):
        mx = jnp.max(work, axis=0, keepdims=True)
        idx = jnp.min(jnp.where(work == mx, blk_f, float(ns)), axis=0, keepdims=True)
        work = jnp.where(blk_f == idx, -jnp.inf, work)
    bias_sc[...] = jnp.where(work == -jnp.inf, 0.0, NEG_INF)

    acc_sc[...] = jnp.zeros(acc_sc.shape, F32)

    def sel_absorb(src, j, causal, m_old):
        off = pl.multiple_of(j * tk, tk)
        s = src[...]
        if causal:
            s = s + _heads(ctab_ref[jnp.clip((q0 - off) // tq, 0, tk // tq)])
        bias = bias_sc[pl.ds(pl.multiple_of(j * nb, nb), nb), :]
        bias = jnp.concatenate([bias] * NSA_HPG, axis=1)
        s3 = s.reshape(nb, SLC_BLOCK, rows) + bias[:, None, :]
        m_new = jnp.maximum(m_old, jnp.max(jnp.max(s3, axis=0), axis=0, keepdims=True))
        p3 = jnp.exp2(s3 - m_new)
        alpha = jnp.exp2(m_old - m_new)
        p = p3.reshape(tk, rows).astype(BF16)
        acc_sc[...] = alpha * acc_sc[...] + _dot(vst_sc[:, pl.ds(off, tk)], p)
        return m_new

    def sel_pair(p, carry):
        j = 2 * p
        sel_scores(j + 1, sb_sc)
        carry = sel_absorb(sa_sc, j, False, carry)
        sel_scores(j + 2, sa_sc)
        return sel_absorb(sb_sc, j + 1, False, carry)

    n_pairs = (q0 // tk + 2) // 2
    carry = lax.fori_loop(0, n_pairs - 1, sel_pair, jnp.full((1, rows), NEG_INF, F32))
    j_tail = 2 * (n_pairs - 1)
    m_sc[...] = sel_absorb(sa_sc, j_tail, True, carry)

    @pl.when(j_tail + 1 <= q0 // tk)
    def _():
        sel_scores(j_tail + 1, sb_sc)
        sel_absorb(sb_sc, j_tail + 1, True, m_sc[...])

    o_s = acc_sc[:HEAD_DIM, :] * (1.0 / acc_sc[HEAD_DIM:HEAD_DIM + 1, :])

    gates = jax.nn.sigmoid(gate_ref[...].astype(F32))
    gates = jnp.concatenate([gates[r * LANES:(r + 1) * LANES].T for r in range(tq // LANES)],
                            axis=1)
    for h in range(NSA_HPG):
        sl = slice(h * tq, (h + 1) * tq)
        o = (gates[3 * h:3 * h + 1] * o_c[:, sl] + gates[3 * h + 1:3 * h + 2] * o_s[:, sl]
             + gates[3 * h + 2:3 * h + 3] * o_w[:, sl])
        for r in range(tq // LANES):
            o_ref[r * LANES:(r + 1) * LANES, h * HEAD_DIM:(h + 1) * HEAD_DIM] = (
                o[:, r * LANES:(r + 1) * LANES].T.astype(o_ref.dtype))


def _nsa_call(proj, cos_t, sin_t, kvc, ovt, batch, seq, tq, tk):
    nq = seq // tq
    assert WINDOW % tq == 0 and tk % tq == 0 and tq % CMP_STRIDE == 0
    wtab, ptab, ctab = _mask_tables(seq, tq, tk)
    nt = kvc.shape[2]
    ns = ovt.shape[0]
    width = NSA_HPG * HEAD_DIM
    kern = functools.partial(_nsa_kernel, tq=tq, tk=tk, seq=seq, top_k=min(SLC_TOPK, ns))

    def col(unit):
        return pl.BlockSpec((seq, HEAD_DIM), lambda b, g, i: (b, unit + g))

    return pl.pallas_call(
        kern,
        out_shape=jax.ShapeDtypeStruct((batch * seq, NSA_HEADS * HEAD_DIM), BF16),
        grid=(batch, NSA_KV_GROUPS, nq),
        in_specs=[
            pl.BlockSpec((tq, width), lambda b, g, i: (b * nq + i, U_NQ // 4 + g)),
            pl.BlockSpec((tq, LANES), lambda b, g, i: (i, 0)),
            pl.BlockSpec((tq, LANES), lambda b, g, i: (i, 0)),
            pl.BlockSpec((tq, LANES), lambda b, g, i: (b * nq + i, U_GATE + g)),
            pl.BlockSpec((None, None, nt, HEAD_DIM), lambda b, g, i: (b, g, 0, 0)),
            pl.BlockSpec((None, None, nt, HEAD_DIM), lambda b, g, i: (b, 2 + g, 0, 0)),
            col(U_KS), col(U_VS), col(U_KW), col(U_VW),
            pl.BlockSpec((ns, nt), lambda b, g, i: (0, 0)),
            pl.BlockSpec(wtab.shape, lambda b, g, i: (0, 0, 0)),
            pl.BlockSpec(ptab.shape, lambda b, g, i: (0, 0)),
            pl.BlockSpec(ctab.shape, lambda b, g, i: (0, 0, 0)),
        ],
        out_specs=pl.BlockSpec((tq, width), lambda b, g, i: (b * nq + i, g)),
        scratch_shapes=[
            pltpu.VMEM((HEAD_DIM, nt), BF16),
            pltpu.VMEM((HEAD_DIM + ONES_ROWS, seq), BF16),
            pltpu.VMEM((HEAD_DIM + ONES_ROWS, seq), BF16),
            pltpu.VMEM((ns, tq), F32),
            pltpu.VMEM((HEAD_DIM + ONES_ROWS, NSA_HPG * tq), F32),
            pltpu.VMEM((tk, NSA_HPG * tq), F32),
            pltpu.VMEM((tk, NSA_HPG * tq), F32),
            pltpu.VMEM((1, NSA_HPG * tq), F32),
        ],
        compiler_params=_params(("arbitrary", "arbitrary", "arbitrary")),
        name="nsa_attn",
    )(proj, cos_t, sin_t, proj, kvc, kvc, proj, proj, proj, proj, ovt, wtab, ptab, ctab)


def _layernorm(y, g, b):
    mu = jnp.mean(y, axis=1, keepdims=True)
    yc = y - mu
    var = jnp.mean(yc * yc, axis=1, keepdims=True)
    return yc * lax.rsqrt(var + EPS) * g + b


N_CHUNKS = D_MODEL // LANES
ROW_PITCH = N_CHUNKS + 1


def _store_chunk_rows(ref, val):
    tm = val.shape[0]
    for k in range(N_CHUNKS):
        ref[pl.ds(k, tm, stride=ROW_PITCH), :] = val[:, k * LANES:(k + 1) * LANES]
    for k in range(N_CHUNKS, ROW_PITCH):
        ref[pl.ds(k, tm, stride=ROW_PITCH), :] = jnp.zeros((tm, LANES), ref.dtype)


def _load_chunk_rows(ref, tm):
    return jnp.concatenate([ref[pl.ds(k, tm, stride=ROW_PITCH), :] for k in range(N_CHUNKS)], axis=1)


OUT_ROW_CHUNK = 256


def _outproj_kernel(od_ref, on_ref, w_ref, x_ref, g_ref, b_ref, wr_ref, x1_ref, route_ref):
    wr = wr_ref[...]
    w_hi = wr.astype(BF16)
    w_lo = (wr - w_hi.astype(F32)).astype(BF16)
    for r0 in range(0, x_ref.shape[0], OUT_ROW_CHUNK):
        _outproj_rows(od_ref, on_ref, w_ref, x_ref, g_ref, b_ref, w_hi, w_lo, x1_ref, route_ref,
                      slice(r0, r0 + OUT_ROW_CHUNK))


def _outproj_rows(od_ref, on_ref, w_ref, x_ref, g_ref, b_ref, w_hi, w_lo, x1_ref, route_ref, rs):
    half = od_ref.shape[1]
    h = _dot(od_ref[rs, :], w_ref[:half, :]) + _dot(on_ref[rs, :], w_ref[half:, :])
    x1 = _layernorm(DN_ALPHA * x_ref[rs, :] + h, g_ref[...], b_ref[...])
    _store_chunk_rows(x1_ref.at[rs.start * ROW_PITCH:rs.stop * ROW_PITCH, :], x1)

    x_hi = x1.astype(BF16)
    x_lo = (x1 - x_hi.astype(F32)).astype(BF16)
    w2 = jnp.concatenate([w_hi, w_lo], axis=1)
    a = _dot(x_hi, w2)
    b = _dot(x_lo, w2)
    logits = (a[:, :LANES] + a[:, LANES:]) + (b[:, :LANES] + b[:, LANES:])

    lane = lax.broadcasted_iota(jnp.int32, logits.shape, 1).astype(F32)
    ninf = -jnp.inf
    gl = jnp.where(lane < N_GROUPS, logits, ninf)
    gmax = jnp.max(gl, axis=1, keepdims=True)
    g_w = 1.0 / jnp.sum(jnp.exp(gl - gmax), axis=1, keepdims=True)
    g_sel = jnp.min(jnp.where(gl == gmax, lane, float(LANES)), axis=1, keepdims=True)
    lo = N_GROUPS + EXPERTS_PER_GROUP * g_sel
    el = jnp.where((lane >= lo) & (lane < lo + EXPERTS_PER_GROUP), logits, ninf)
    e1 = jnp.max(el, axis=1, keepdims=True)
    i1 = jnp.min(jnp.where(el == e1, lane, float(LANES)), axis=1, keepdims=True)
    el2 = jnp.where(lane == i1, ninf, el)
    e2 = jnp.max(el2, axis=1, keepdims=True)
    i2 = jnp.min(jnp.where(el2 == e2, lane, float(LANES)), axis=1, keepdims=True)
    r = jnp.exp(e2 - e1)
    w1 = g_w / (1.0 + r)
    w2 = g_w * r / (1.0 + r)
    route = jnp.where(lane == 0, i1 - N_GROUPS,
                      jnp.where(lane == 1, i2 - N_GROUPS,
                                jnp.where(lane == 2, w1, jnp.where(lane == 3, w2, 0.0))))
    route_ref[rs, :] = route


def _outproj_call(o_diff, o_nsa, w_out, x2, g, b, wr, tm):
    n, d = x2.shape
    half = o_diff.shape[1]
    return pl.pallas_call(
        _outproj_kernel,
        out_shape=(jax.ShapeDtypeStruct((n * ROW_PITCH, LANES), F32), jax.ShapeDtypeStruct((n, LANES), F32)),
        grid=(n // tm,),
        in_specs=[
            pl.BlockSpec((tm, half), lambda i: (i, 0)),
            pl.BlockSpec((tm, half), lambda i: (i, 0)),
            pl.BlockSpec((2 * half, d), lambda i: (0, 0)),
            pl.BlockSpec((tm, d), lambda i: (i, 0)),
            pl.BlockSpec((1, d), lambda i: (0, 0)),
            pl.BlockSpec((1, d), lambda i: (0, 0)),
            pl.BlockSpec((d, LANES), lambda i: (0, 0)),
        ],
        out_specs=(pl.BlockSpec((tm * ROW_PITCH, LANES), lambda i: (i, 0)),
                   pl.BlockSpec((tm, LANES), lambda i: (i, 0))),
        compiler_params=_params(("parallel",)),
        name="outproj_ln1_router",
    )(o_diff, o_nsa, w_out, x2, g, b, wr)


MOE_TILE_ROWS = 256
COMBINE_TILE_TOKENS = 256


def _token_copy(src_hbm, dst, sem, src_row, dst_tok):
    return pltpu.make_async_copy(src_hbm.at[pl.ds(src_row, N_CHUNKS)],
                                 dst.at[pl.ds(dst_tok * ROW_PITCH, N_CHUNKS)], sem)


def _wait_tokens(src_hbm, dst, sem, count):
    pltpu.make_async_copy(src_hbm.at[pl.ds(0, count * N_CHUNKS)], dst.at[pl.ds(0, count * N_CHUNKS)], sem).wait()


def _moe_kernel(te_ref, nu_ref, rows_ref, rows_next_ref, x_hbm, wg_ref, wu_ref, wd_ref, o_ref,
                xbuf, sem, wg_sc, wu_sc, wd_sc, *, tm):
    i = pl.program_id(0)
    n_used = nu_ref[0]
    slot = lax.rem(i, 2)

    def issue(ids_ref, s):
        def body(r, carry):
            _token_copy(x_hbm, xbuf.at[s], sem.at[s], ids_ref[0, 0, r], r).start()
            return carry

        lax.fori_loop(0, tm, body, 0, unroll=8)

    @pl.when(i == 0)
    def _():
        issue(rows_ref, 0)

    @pl.when(i + 1 < n_used)
    def _():
        issue(rows_next_ref, 1 - slot)

    prev = te_ref[jnp.maximum(i - 1, 0)]

    @pl.when((i == 0) | (te_ref[i] != prev))
    def _():
        wg_sc[...] = wg_ref[...].astype(BF16)
        wu_sc[...] = wu_ref[...].astype(BF16)
        wd_sc[...] = wd_ref[...].astype(BF16)

    @pl.when(i < n_used)
    def _():
        _wait_tokens(x_hbm, xbuf.at[slot], sem.at[slot], tm)
        xb = _load_chunk_rows(xbuf.at[slot], tm).astype(BF16)
        gate = _dot(xb, wg_sc[...])
        up = _dot(xb, wu_sc[...])
        h = (gate * jax.nn.sigmoid(gate) * up).astype(BF16)
        _store_chunk_rows(o_ref, _dot(h, wd_sc[...]))

    @pl.when(i >= n_used)
    def _():
        o_ref[...] = jnp.zeros(o_ref.shape, o_ref.dtype)


def _moe_call(tile_expert, n_used, rows, x1c, w_gate, w_up, w_down, tm):
    n_tiles = rows.shape[0]
    d, f = w_gate.shape[1], w_gate.shape[2]
    grid_spec = pltpu.PrefetchScalarGridSpec(
        num_scalar_prefetch=2,
        grid=(n_tiles,),
        in_specs=[
            pl.BlockSpec((1, 1, tm), lambda i, te, nu: (i, 0, 0), memory_space=pltpu.SMEM),
            pl.BlockSpec((1, 1, tm), lambda i, te, nu: (jnp.minimum(i + 1, n_tiles - 1), 0, 0),
                         memory_space=pltpu.SMEM),
            pl.BlockSpec(memory_space=pl.ANY),
            pl.BlockSpec((None, d, f), lambda i, te, nu: (te[i], 0, 0)),
            pl.BlockSpec((None, d, f), lambda i, te, nu: (te[i], 0, 0)),
            pl.BlockSpec((None, f, d), lambda i, te, nu: (te[i], 0, 0)),
        ],
        out_specs=pl.BlockSpec((tm * ROW_PITCH, LANES), lambda i, te, nu: (i, 0)),
        scratch_shapes=[
            pltpu.VMEM((2, tm * ROW_PITCH, LANES), F32),
            pltpu.SemaphoreType.DMA((2,)),
            pltpu.VMEM((d, f), BF16), pltpu.VMEM((d, f), BF16), pltpu.VMEM((f, d), BF16),
        ],
    )
    return pl.pallas_call(
        functools.partial(_moe_kernel, tm=tm),
        out_shape=jax.ShapeDtypeStruct((n_tiles * tm * ROW_PITCH, LANES), F32),
        grid_spec=grid_spec,
        compiler_params=_params(("arbitrary",)),
        name="moe_experts",
    )(tile_expert, n_used, rows, rows, x1c, w_gate, w_up, w_down)


def _combine_kernel(slots_ref, slots_next_ref, y_hbm, x1_ref, route_ref, g_ref, b_ref, o_ref, buf, sem, *, tm):
    i = pl.program_id(0)
    slot = lax.rem(i, 2)

    @pl.when(i == 0)
    def _():
        def body(r, carry):
            for k in range(2):
                _token_copy(y_hbm, buf.at[0, k], sem.at[0], slots_ref[0, 0, 2 * r + k], r).start()
            return carry

        lax.fori_loop(0, tm, body, 0, unroll=8)

    def tile(prefetch):
        if prefetch:
            for r in range(tm):
                for k in range(2):
                    _token_copy(y_hbm, buf.at[1 - slot, k], sem.at[1 - slot],
                                slots_next_ref[0, 0, 2 * r + k], r).start(priority=k)
        for k in range(2):
            _wait_tokens(y_hbm, buf.at[slot, k], sem.at[slot], tm)
        route = route_ref[...]
        y = (route[:, 2:3] * _load_chunk_rows(buf.at[slot, 0], tm)
             + route[:, 3:4] * _load_chunk_rows(buf.at[slot, 1], tm))
        x1 = _load_chunk_rows(x1_ref, tm)
        o_ref[...] = _layernorm(DN_ALPHA * x1 + y, g_ref[...], b_ref[...])

    last = pl.num_programs(0) - 1
    pl.when(i < last)(lambda: tile(True))
    pl.when(i == last)(lambda: tile(False))


def _combine_call(slots, yc, x1c, route, g, b, tm):
    n = route.shape[0]
    d = D_MODEL
    nt = n // tm
    return pl.pallas_call(
        functools.partial(_combine_kernel, tm=tm),
        out_shape=jax.ShapeDtypeStruct((n, d), F32),
        grid=(nt,),
        in_specs=[
            pl.BlockSpec((1, 1, 2 * tm), lambda i: (i, 0, 0), memory_space=pltpu.SMEM),
            pl.BlockSpec((1, 1, 2 * tm), lambda i: (jnp.minimum(i + 1, nt - 1), 0, 0), memory_space=pltpu.SMEM),
            pl.BlockSpec(memory_space=pl.ANY),
            pl.BlockSpec((tm * ROW_PITCH, LANES), lambda i: (i, 0)),
            pl.BlockSpec((tm, LANES), lambda i: (i, 0)),
            pl.BlockSpec((1, d), lambda i: (0, 0)),
            pl.BlockSpec((1, d), lambda i: (0, 0)),
        ],
        out_specs=pl.BlockSpec((tm, d), lambda i: (i, 0)),
        scratch_shapes=[pltpu.VMEM((2, 2, tm * ROW_PITCH, LANES), F32), pltpu.SemaphoreType.DMA((2,))],
        compiler_params=_params(("arbitrary",)),
        name="moe_combine_ln2",
    )(slots, slots, yc, x1c, route, g, b)


def _moe_plan(e_idx, tm, n_tiles):
    n = e_idx.shape[0]
    lanes = jnp.arange(N_EXPERTS, dtype=jnp.int32)[None, :]
    hit = ((e_idx[:, 0:1] == lanes) | (e_idx[:, 1:2] == lanes)).astype(jnp.int32)
    before = jnp.cumsum(hit, axis=0) - hit
    counts = jnp.sum(hit, axis=0)
    ptiles = (counts + tm - 1) // tm
    tile_end = jnp.cumsum(ptiles)
    base = (tile_end - ptiles) * tm
    slot = jnp.take_along_axis(before + base[None, :], e_idx, axis=1).reshape(-1)
    rows = jnp.zeros((n_tiles * tm,), jnp.int32).at[slot].set(jnp.arange(2 * n, dtype=jnp.int32) // 2)
    tile_ids = jnp.arange(n_tiles, dtype=jnp.int32)
    tile_expert = jnp.minimum(jnp.sum((tile_end[None, :] <= tile_ids[:, None]).astype(jnp.int32), axis=1),
                              N_EXPERTS - 1)
    return slot.astype(jnp.int32), rows, tile_expert, tile_end[-1:].astype(jnp.int32)


def _rope_tables(seq):
    inv_freq = ROPE_THETA ** (-np.arange(0, ROT_DIM, 2, dtype=np.float64) / ROT_DIM)
    ang = np.arange(seq, dtype=np.float64)[:, None] * inv_freq[None, :]
    cos, sin = np.cos(ang), np.sin(ang)
    pad1 = np.ones((seq, HEAD_DIM - ROT_DIM))
    pad0 = np.zeros((seq, HEAD_DIM - ROT_DIM))
    return (jnp.asarray(np.concatenate([cos, cos, pad1], axis=1), dtype=F32),
            jnp.asarray(np.concatenate([-sin, sin, pad0], axis=1), dtype=F32))


def _layer(x, w_in, diff_lambda, diff_subln_g, cmp_pos, cmp_w1, cmp_b1, cmp_w2, cmp_b2, w_out,
           ln1_g, ln1_b, router_group, router_expert, w_gate, w_up, w_down, ln2_g, ln2_b, lambda_init):
    batch, seq, d = x.shape
    n = batch * seq
    x2 = x.reshape(n, d)

    seg = lambda a, b: w_in[:, a:b]
    pad = lambda w: jnp.pad(w, ((0, 0), (0, LANES - w.shape[1])))
    gate_w = seg(5632, 5656)
    w_aug = jnp.concatenate(
        [seg(3072, 4096) * EXP2_SCALE, seg(0, 1024) * EXP2_SCALE, seg(1024, 2048), seg(2048, 3072), seg(4096, 5632),
         pad(gate_w[:, :12]), pad(gate_w[:, 12:])], axis=1).astype(BF16)
    flags = jnp.zeros((N_UNITS // 2,), jnp.int32).at[jnp.array(ROPE_BLOCKS)].set(1)
    cos_t, sin_t = _rope_tables(seq)
    proj = _proj_call(x2, w_aug, flags, cos_t, sin_t, seq, tm=min(2048, seq))

    nt = seq // CMP_STRIDE
    half_feat = CMP_STRIDE * HEAD_DIM
    w1cat = jnp.concatenate([cmp_w1[:, :half_feat], cmp_w1[:, half_feat:]], axis=2).astype(BF16)
    kvc = _compress_call(proj, w1cat, cmp_pos.reshape(2, 2, half_feat), cmp_b1[:, None, :],
                         cmp_w2.astype(BF16), cmp_b2[:, None, :], batch, seq)

    o_diff = _diff_call(proj, diff_lambda, diff_subln_g[None, :], batch, seq, min(1024, seq), lambda_init)

    ns = seq // SLC_BLOCK
    ci = np.arange(nt)[None, :] * CMP_STRIDE
    sj = np.arange(ns)[:, None] * SLC_BLOCK
    ovt = jnp.asarray((ci < sj + SLC_BLOCK) & (ci + CMP_BLOCK > sj) & (np.arange(nt)[None, :] < nt - 1), dtype=BF16)
    o_nsa = _nsa_call(proj, cos_t, sin_t, kvc, ovt, batch, seq, tq=2 * LANES, tk=min(512, seq))

    wr = jnp.pad(jnp.concatenate([router_group, router_expert], axis=1),
                 ((0, 0), (0, LANES - N_GROUPS - N_EXPERTS)))
    x1, route = _outproj_call(o_diff, o_nsa, w_out.astype(BF16), x2, ln1_g[None, :], ln1_b[None, :], wr, tm=512)

    tm = MOE_TILE_ROWS
    n_tiles = -(-2 * n // tm) + N_EXPERTS
    e_idx = route[:, 0:2].astype(jnp.int32)
    slot, rows, tile_expert, n_used = _moe_plan(e_idx, tm, n_tiles)
    ys = _moe_call(tile_expert, n_used, (rows * ROW_PITCH).reshape(n_tiles, 1, tm), x1, w_gate, w_up, w_down, tm)
    tc = COMBINE_TILE_TOKENS
    out = _combine_call((slot * ROW_PITCH).reshape(n // tc, 1, 2 * tc), ys, x1, route,
                        ln2_g[None, :], ln2_b[None, :], tc)
    return out.reshape(batch, seq, d)


def kernel(x, w_in, diff_lambda, diff_subln_g, cmp_pos, cmp_w1, cmp_b1, cmp_w2, cmp_b2, w_out, ln1_g, ln1_b,
           router_group, router_expert, expert_w_gate, expert_w_up, expert_w_down, ln2_g, ln2_b):
    for l in range(DEPTH):
        lambda_init = 0.8 - 0.6 * math.exp(-0.3 * l)
        x = _layer(x, w_in[l], diff_lambda[l], diff_subln_g[l], cmp_pos[l], cmp_w1[l], cmp_b1[l], cmp_w2[l],
                   cmp_b2[l], w_out[l], ln1_g[l], ln1_b[l], router_group[l], router_expert[l],
                   expert_w_gate[l], expert_w_up[l], expert_w_down[l], ln2_g[l], ln2_b[l], lambda_init)
    return x
```

```python
import functools
import math

import numpy as np
import jax
import jax.numpy as jnp
from jax import lax
from jax.experimental import pallas as pl
from jax.experimental.pallas import tpu as pltpu

F32 = jnp.float32
BF16 = jnp.bfloat16

D_MODEL = 2048
HEAD_DIM = 128
ROT_DIM = HEAD_DIM // 4
ROPE_THETA = 500000.0
NEG_INF = -1e30
BIG = 1e30
EPS = 1e-5

DIFF_HEADS = 4
DIFF_VDIM = 2 * HEAD_DIM

NSA_HEADS = 8
NSA_KV_GROUPS = 2
NSA_HPG = NSA_HEADS // NSA_KV_GROUPS
CMP_BLOCK = 32
CMP_STRIDE = 16
CMP_HIDDEN = 256
SLC_BLOCK = 64
SLC_TOPK = 16
WINDOW = 512

N_GROUPS = 4
EXPERTS_PER_GROUP = 8
N_EXPERTS = N_GROUPS * EXPERTS_PER_GROUP
EXPERT_HIDDEN = 512

DEPTH = 1
DN_ALPHA = (2.0 * DEPTH) ** 0.25

LANES = 128
VMEM_LIMIT = 56 * 1024 * 1024

PROJ_ROW_TILE = 2048
DIFF_Q_TILE = 1024
NSA_Q_TILE = 256
NSA_K_TILE = 512
OUT_ROW_TILE = 512

U_NQ, U_DQ, U_DK, U_DV = 0, 8, 16, 24
U_KC, U_VC, U_KS, U_VS, U_KW, U_VW, U_GATE = 32, 34, 36, 38, 40, 42, 44
N_UNITS = 46
ROPE_BLOCKS = tuple(range(U_DQ // 2, U_DV // 2)) + (U_KS // 2, U_KW // 2)


def _dot(a, b):
    return jnp.dot(a, b, preferred_element_type=F32)


def _dot_nt(a, b):
    return lax.dot_general(a, b, (((1,), (1,)), ((), ())), preferred_element_type=F32)


def _params(sem, vmem=VMEM_LIMIT):
    return pltpu.CompilerParams(dimension_semantics=sem, vmem_limit_bytes=vmem)


PROJ_ROW_CHUNK = 256


def _rope(a, c, s):
    lane = lax.broadcasted_iota(jnp.int32, a.shape, 1)
    half = ROT_DIM // 2
    partner = jnp.where(lane < half, pltpu.roll(a, LANES - half, 1), pltpu.roll(a, half, 1))
    return a * c + partner * s


def _proj_kernel(flags_ref, x_ref, w_ref, cos_ref, sin_ref, o_ref, xb_sc):
    j = pl.program_id(1)

    @pl.when(j == 0)
    def _():
        xb_sc[...] = x_ref[...].astype(BF16)

    def block(rotary):
        for r0 in range(0, xb_sc.shape[0], PROJ_ROW_CHUNK):
            rs = slice(r0, r0 + PROJ_ROW_CHUNK)
            acc = _dot(xb_sc[rs, :], w_ref[...])
            if rotary:
                for hh in range(2):
                    cs = slice(hh * LANES, (hh + 1) * LANES)
                    o_ref[rs, cs] = _rope(acc[:, cs], cos_ref[rs, :], sin_ref[rs, :]).astype(o_ref.dtype)
            else:
                o_ref[rs, :] = acc.astype(o_ref.dtype)

    pl.when(flags_ref[j] == 0)(lambda: block(False))
    pl.when(flags_ref[j] != 0)(lambda: block(True))


def _proj_call(xb, w_aug, flags, cos_t, sin_t, seq, tm):
    n, d = xb.shape
    nj = w_aug.shape[1] // 256
    tpb = seq // tm
    grid_spec = pltpu.PrefetchScalarGridSpec(
        num_scalar_prefetch=1,
        grid=(n // tm, nj),
        in_specs=[
            pl.BlockSpec((tm, d), lambda i, j, f: (i, 0)),
            pl.BlockSpec((d, 256), lambda i, j, f: (0, j)),
            pl.BlockSpec((tm, LANES), lambda i, j, f: (i % tpb, 0)),
            pl.BlockSpec((tm, LANES), lambda i, j, f: (i % tpb, 0)),
        ],
        out_specs=pl.BlockSpec((tm, 256), lambda i, j, f: (i, j)),
        scratch_shapes=[pltpu.VMEM((tm, d), BF16)],
    )
    return pl.pallas_call(
        _proj_kernel,
        out_shape=jax.ShapeDtypeStruct((n, w_aug.shape[1]), BF16),
        grid_spec=grid_spec,
        compiler_params=_params(("parallel", "arbitrary")),
        name="proj",
    )(flags, xb, w_aug, cos_t, sin_t)


def _compress_kernel(t_ref, w1_ref, pos_ref, b1_ref, w2_ref, b2_ref, o_ref, t_sc):
    nt = t_sc.shape[0] // CMP_STRIDE
    t_sc[...] = t_ref[...].astype(F32)
    r = jnp.concatenate([t_sc[pl.ds(k, nt, stride=CMP_STRIDE), :] for k in range(CMP_STRIDE)], axis=1).astype(BF16)
    ab = _dot(r, w1_ref[...])
    pos = pos_ref[...]
    pa = jnp.broadcast_to(pos[0:1], (8, pos.shape[1])).astype(BF16)
    pb = jnp.broadcast_to(pos[1:2], (8, pos.shape[1])).astype(BF16)
    const = _dot(pa, w1_ref[:, :CMP_HIDDEN])[0:1] + _dot(pb, w1_ref[:, CMP_HIDDEN:])[0:1]
    h = ab[:, :CMP_HIDDEN] + pltpu.roll(ab[:, CMP_HIDDEN:], nt - 1, 0) + const + b1_ref[...]
    h = jax.nn.gelu(h)
    o_ref[...] = (_dot(h.astype(BF16), w2_ref[...]) + b2_ref[...]).astype(o_ref.dtype)


def _compress_call(proj, w1cat, pos2, b1, w2, b2, b, seq):
    four = 2 * NSA_KV_GROUPS
    nt = seq // CMP_STRIDE
    k = CMP_STRIDE * HEAD_DIM
    return pl.pallas_call(
        _compress_kernel,
        out_shape=jax.ShapeDtypeStruct((b, four, nt, HEAD_DIM), BF16),
        grid=(b, four),
        in_specs=[
            pl.BlockSpec((seq, HEAD_DIM), lambda i, c: (i, U_KC + c)),
            pl.BlockSpec((None, k, 2 * CMP_HIDDEN), lambda i, c: (c // 2, 0, 0)),
            pl.BlockSpec((None, 2, k), lambda i, c: (c // 2, 0, 0)),
            pl.BlockSpec((None, 1, CMP_HIDDEN), lambda i, c: (c // 2, 0, 0)),
            pl.BlockSpec((None, CMP_HIDDEN, HEAD_DIM), lambda i, c: (c // 2, 0, 0)),
            pl.BlockSpec((None, 1, HEAD_DIM), lambda i, c: (c // 2, 0, 0)),
        ],
        out_specs=pl.BlockSpec((None, None, nt, HEAD_DIM), lambda i, c: (i, c, 0, 0)),
        scratch_shapes=[pltpu.VMEM((seq, HEAD_DIM), F32)],
        compiler_params=_params(("parallel", "parallel")),
        name="compress",
    )(proj, w1cat, pos2, b1, w2, b2)


EXP2_SCALE = HEAD_DIM ** -0.5 * math.log2(math.e)
ONES_ROWS = 16


def _transpose_into(src_ref, dst_ref):
    def body(c, carry):
        off = pl.multiple_of(c * LANES, LANES)
        dst_ref[:, pl.ds(off, LANES)] = src_ref[pl.ds(off, LANES), :].astype(F32).T.astype(dst_ref.dtype)
        return carry

    n_tiles = src_ref.shape[0] // LANES
    lax.fori_loop(0, n_tiles, body, 0, unroll=math.gcd(n_tiles, 8))


def _diff_kernel(dl_ref, q_ref, k_ref, v_ref, g_ref, o_ref, vt_sc, acc_sc, sa_sc, sb_sc, *, tq, lambda_init):
    qi = pl.program_id(2)

    @pl.when(qi == 0)
    def _():
        for c in range(2):
            _transpose_into(v_ref.at[:, c * LANES:(c + 1) * LANES], vt_sc.at[c * LANES:(c + 1) * LANES, :])

    acc_sc[...] = jnp.zeros(acc_sc.shape, F32)
    q = q_ref[...]
    qpos = qi * tq + lax.broadcasted_iota(jnp.int32, (1, tq), 1)

    tk = tq // 2

    def scores(j, dst, lo=0):
        kt = k_ref[pl.ds(pl.multiple_of(j * tk, tk), tk), :]
        for c in range(2):
            dst[c, :, lo:] = _dot_nt(kt[:, c * HEAD_DIM:(c + 1) * HEAD_DIM], q[lo:, c * HEAD_DIM:(c + 1) * HEAD_DIM])

    def absorb(src, j, masked, carry, lo=0):
        off = pl.multiple_of(j * tk, tk)
        vt = vt_sc[:, pl.ds(off, tk)]
        out = []
        for c in range(2):
            m_old, l_old = carry[2 * c][:, lo:], carry[2 * c + 1][:, lo:]
            s = src[c, :, lo:]
            if masked:
                s = jnp.where(lax.broadcasted_iota(jnp.int32, s.shape, 0) <= qpos[:, lo:] - off, s, NEG_INF)
            m_new = jnp.maximum(m_old, jnp.max(s, axis=0, keepdims=True))
            p = jnp.exp2(s - m_new)
            alpha = jnp.exp2(m_old - m_new)
            l_new = alpha * l_old + jnp.sum(p, axis=0, keepdims=True)
            acc_sc[c, :, lo:] = alpha * acc_sc[c, :, lo:] + _dot(vt, p.astype(BF16))
            if lo:
                m_new = jnp.concatenate([carry[2 * c][:, :lo], m_new], axis=1)
                l_new = jnp.concatenate([carry[2 * c + 1][:, :lo], l_new], axis=1)
            out += [m_new, l_new]
        return tuple(out)

    def pair(p, carry):
        j = 2 * p
        scores(j + 1, sb_sc)
        carry = absorb(sa_sc, j, False, carry)
        scores(j + 2, sa_sc)
        return absorb(sb_sc, j + 1, False, carry)

    init = (jnp.full((1, tq), NEG_INF, F32), jnp.zeros((1, tq), F32)) * 2
    scores(0, sa_sc)
    carry = lax.fori_loop(0, qi, pair, init)
    j_tail = 2 * qi
    scores(j_tail + 1, sb_sc, lo=tk)
    carry = absorb(sa_sc, j_tail, True, carry)
    _, l0, _, l1 = absorb(sb_sc, j_tail + 1, True, carry, lo=tk)

    dl = dl_ref[...]
    lam = (jnp.exp(jnp.sum(dl[0:1] * dl[1:2], axis=1, keepdims=True))
           - jnp.exp(jnp.sum(dl[2:3] * dl[3:4], axis=1, keepdims=True)) + lambda_init)
    o = acc_sc[0] * (1.0 / l0) - lam * (acc_sc[1] * (1.0 / l1))
    o = o * (lax.rsqrt(jnp.mean(o * o, axis=0, keepdims=True) + EPS) * (1.0 - lambda_init))
    for c in range(DIFF_VDIM // LANES):
        for r in range(tq // LANES):
            blk = o[c * LANES:(c + 1) * LANES, r * LANES:(r + 1) * LANES].T
            o_ref[r * LANES:(r + 1) * LANES, c * LANES:(c + 1) * LANES] = (
                blk * g_ref[:, c * LANES:(c + 1) * LANES]).astype(o_ref.dtype)


def _diff_call(proj, dl, g, batch, seq, tq, lambda_init):
    nq = seq // tq
    kern = functools.partial(_diff_kernel, tq=tq, lambda_init=lambda_init)
    return pl.pallas_call(
        kern,
        out_shape=jax.ShapeDtypeStruct((batch * seq, DIFF_HEADS * DIFF_VDIM), BF16),
        grid=(batch, DIFF_HEADS, nq),
        in_specs=[
            pl.BlockSpec((4, HEAD_DIM), lambda b, h, i: (0, 0)),
            pl.BlockSpec((tq, 256), lambda b, h, i: (b * nq + i, U_DQ // 2 + h)),
            pl.BlockSpec((seq, 256), lambda b, h, i: (b, U_DK // 2 + h)),
            pl.BlockSpec((seq, 256), lambda b, h, i: (b, U_DV // 2 + h)),
            pl.BlockSpec((1, DIFF_VDIM), lambda b, h, i: (0, 0)),
        ],
        out_specs=pl.BlockSpec((tq, DIFF_VDIM), lambda b, h, i: (b * nq + i, h)),
        scratch_shapes=[
            pltpu.VMEM((DIFF_VDIM, seq), BF16),
            pltpu.VMEM((2, DIFF_VDIM, tq), F32),
            pltpu.VMEM((2, tq // 2, tq), F32),
            pltpu.VMEM((2, tq // 2, tq), F32),
        ],
        compiler_params=_params(("arbitrary", "arbitrary", "arbitrary")),
        name="diff_attn",
    )(dl, proj, proj, proj, g)


def _stack_heads(x):
    return jnp.concatenate([x[:, h * HEAD_DIM:(h + 1) * HEAD_DIM] for h in range(NSA_HPG)], axis=0)


def _heads(x):
    return jnp.concatenate([x] * NSA_HPG, axis=1)


def _mask_tables(seq, tq, tk):
    nt = seq // CMP_STRIDE
    slab = min(WINDOW + tq, seq)
    q = np.arange(tq)[None, :]
    k = np.arange(slab)[:, None]
    win = []
    for i in range(WINDOW // tq + 1):
        newest = q + (i * tq if i < WINDOW // tq else WINDOW)
        win.append((k <= newest) & (k > newest - WINDOW))
    u = np.arange(2 * nt)[:, None] - nt
    cmp_ok = u <= ((q - (CMP_BLOCK - 1)) >> int(math.log2(CMP_STRIDE)))
    kk = np.arange(tk)[:, None]
    causal = [kk <= d * tq + q for d in range(tk // tq)] + [np.ones((tk, tq), bool)]
    to_bias = lambda m: jnp.asarray(np.where(np.asarray(m), 0.0, NEG_INF), dtype=F32)
    return to_bias(np.stack(win)), to_bias(cmp_ok), to_bias(np.stack(causal))


def _nsa_kernel(q_ref, cos_ref, sin_ref, gate_ref, kc_ref, vc_ref, ks_ref, vs_ref, kw_ref, vw_ref, ovt_ref,
                wtab_ref, ptab_ref, ctab_ref, o_ref, vct_sc, vst_sc, vwt_sc, bias_sc, acc_sc, sa_sc, sb_sc, m_sc, *, tq, tk, seq, top_k):
    qi = pl.program_id(2)
    q0 = qi * tq
    rows = NSA_HPG * tq
    nt = kc_ref.shape[0]
    ns = ovt_ref.shape[0]
    nb = tk // SLC_BLOCK

    @pl.when(qi == 0)
    def _():
        _transpose_into(vc_ref, vct_sc)
        for v_ref, vt_sc in ((vs_ref, vst_sc), (vw_ref, vwt_sc)):
            _transpose_into(v_ref, vt_sc.at[:HEAD_DIM, :])
            vt_sc[HEAD_DIM:, :] = jnp.ones((ONES_ROWS, seq), vt_sc.dtype)

    q = q_ref[...]
    qs = _stack_heads(q)
    qr = jnp.concatenate(
        [_rope(q[:, h * HEAD_DIM:(h + 1) * HEAD_DIM].astype(F32), cos_ref[...], sin_ref[...]).astype(BF16)
         for h in range(NSA_HPG)], axis=0)
    qpos = q0 + (lax.broadcasted_iota(jnp.int32, (1, rows), 1) & (tq - 1))

    def sel_scores(j, dst):
        dst[...] = _dot_nt(ks_ref[pl.ds(pl.multiple_of(j * tk, tk), tk), :], qr)

    sel_scores(0, sa_sc)

    slab = min(WINDOW + tq, seq)
    start = pl.multiple_of(jnp.maximum(q0 - WINDOW, 0), tq)
    s_w = _dot_nt(kw_ref[pl.ds(start, slab), :], qr)
    s_w = s_w + _heads(wtab_ref[jnp.minimum(qi, WINDOW // tq)])
    p_w = jnp.exp2(s_w - jnp.max(s_w, axis=0, keepdims=True))
    o_w = _dot(vwt_sc[:, pl.ds(start, slab)], p_w.astype(BF16))
    o_w = o_w[:HEAD_DIM] * (1.0 / o_w[HEAD_DIM:HEAD_DIM + 1])

    s_c = _dot_nt(kc_ref[...], qs)
    c0 = pl.multiple_of(nt - q0 // CMP_STRIDE, CMP_STRIDE)
    s_c = s_c + _heads(ptab_ref[pl.ds(c0, nt), :])
    m_c = jnp.max(s_c, axis=0, keepdims=True)
    e_c = jnp.exp2(s_c - m_c)
    l_c = jnp.sum(e_c, axis=0, keepdims=True)
    p_c = e_c * jnp.where(qpos >= CMP_BLOCK - 1, 1.0 / l_c, 0.0)
    o_c = _dot(vct_sc[...], p_c.astype(BF16))

    p_sum = p_c[:, 0:tq]
    for h in range(1, NSA_HPG):
        p_sum = p_sum + p_c[:, h * tq:(h + 1) * tq]
    p_hi = p_sum.astype(BF16)
    p_lo = (p_sum - p_hi.astype(F32)).astype(BF16)
    imp = _dot(ovt_ref[...], p_hi) + _dot(ovt_ref[...], p_lo)

    blk = lax.broadcasted_iota(jnp.int32, (ns, tq), 0)
    qpos_l = q0 + lax.broadcasted_iota(jnp.int32, (ns, tq), 1)
    cur = lax.shift_right_logical(qpos_l, int(math.log2(SLC_BLOCK)))
    valid_s = blk <= cur
    forced = (blk == 0) | (blk == cur) | (blk == cur - 1)
    work = jnp.where(forced, -jnp.inf, jnp.where(valid_s, imp, NEG_INF))
    blk_f = blk.astype(F32)
    for _ in range(top_k - 3):
        mx = jnp.max(work, axis=0, keepdims=True)
        idx = jnp.min(jnp.where(work == mx, blk_f, float(ns)), axis=0, keepdims=True)
        work = jnp.where(blk_f == idx, -jnp.inf, work)
    bias_sc[...] = jnp.where(work == -jnp.inf, 0.0, NEG_INF)

    acc_sc[...] = jnp.zeros(acc_sc.shape, F32)

    def sel_absorb(src, j, causal, m_old):
        off = pl.multiple_of(j * tk, tk)
        s = src[...]
        if causal:
            s = s + _heads(ctab_ref[jnp.clip((q0 - off) // tq, 0, tk // tq)])
        bias = bias_sc[pl.ds(pl.multiple_of(j * nb, nb), nb), :]
        bias = jnp.concatenate([bias] * NSA_HPG, axis=1)
        s3 = s.reshape(nb, SLC_BLOCK, rows) + bias[:, None, :]
        m_new = jnp.maximum(m_old, jnp.max(jnp.max(s3, axis=0), axis=0, keepdims=True))
        p3 = jnp.exp2(s3 - m_new)
        alpha = jnp.exp2(m_old - m_new)
        p = p3.reshape(tk, rows).astype(BF16)
        acc_sc[...] = alpha * acc_sc[...] + _dot(vst_sc[:, pl.ds(off, tk)], p)
        return m_new

    def sel_pair(p, carry):
        j = 2 * p
        sel_scores(j + 1, sb_sc)
        carry = sel_absorb(sa_sc, j, False, carry)
        sel_scores(j + 2, sa_sc)
        return sel_absorb(sb_sc, j + 1, False, carry)

    n_pairs = (q0 // tk + 2) // 2
    carry = lax.fori_loop(0, n_pairs - 1, sel_pair, jnp.full((1, rows), NEG_INF, F32))
    j_tail = 2 * (n_pairs - 1)
    m_sc[...] = sel_absorb(sa_sc, j_tail, True, carry)

    @pl.when(j_tail + 1 <= q0 // tk)
    def _():
        sel_scores(j_tail + 1, sb_sc)
        sel_absorb(sb_sc, j_tail + 1, True, m_sc[...])

    o_s = acc_sc[:HEAD_DIM, :] * (1.0 / acc_sc[HEAD_DIM:HEAD_DIM + 1, :])

    gates = jax.nn.sigmoid(gate_ref[...].astype(F32))
    gates = jnp.concatenate([gates[r * LANES:(r + 1) * LANES].T for r in range(tq // LANES)],
                            axis=1)
    for h in range(NSA_HPG):
        sl = slice(h * tq, (h + 1) * tq)
        o = (gates[3 * h:3 * h + 1] * o_c[:, sl] + gates[3 * h + 1:3 * h + 2] * o_s[:, sl]
             + gates[3 * h + 2:3 * h + 3] * o_w[:, sl])
        for r in range(tq // LANES):
            o_ref[r * LANES:(r + 1) * LANES, h * HEAD_DIM:(h + 1) * HEAD_DIM] = (
                o[:, r * LANES:(r + 1) * LANES].T.astype(o_ref.dtype))


def _nsa_call(proj, cos_t, sin_t, kvc, ovt, batch, seq, tq, tk):
    nq = seq // tq
    assert WINDOW % tq == 0 and tk % tq == 0 and tq % CMP_STRIDE == 0
    wtab, ptab, ctab = _mask_tables(seq, tq, tk)
    nt = kvc.shape[2]
    ns = ovt.shape[0]
    width = NSA_HPG * HEAD_DIM
    kern = functools.partial(_nsa_kernel, tq=tq, tk=tk, seq=seq, top_k=min(SLC_TOPK, ns))

    def col(unit):
        return pl.BlockSpec((seq, HEAD_DIM), lambda b, g, i: (b, unit + g))

    return pl.pallas_call(
        kern,
        out_shape=jax.ShapeDtypeStruct((batch * seq, NSA_HEADS * HEAD_DIM), BF16),
        grid=(batch, NSA_KV_GROUPS, nq),
        in_specs=[
            pl.BlockSpec((tq, width), lambda b, g, i: (b * nq + i, U_NQ // 4 + g)),
            pl.BlockSpec((tq, LANES), lambda b, g, i: (i, 0)),
            pl.BlockSpec((tq, LANES), lambda b, g, i: (i, 0)),
            pl.BlockSpec((tq, LANES), lambda b, g, i: (b * nq + i, U_GATE + g)),
            pl.BlockSpec((None, None, nt, HEAD_DIM), lambda b, g, i: (b, g, 0, 0)),
            pl.BlockSpec((None, None, nt, HEAD_DIM), lambda b, g, i: (b, 2 + g, 0, 0)),
            col(U_KS), col(U_VS), col(U_KW), col(U_VW),
            pl.BlockSpec((ns, nt), lambda b, g, i: (0, 0)),
            pl.BlockSpec(wtab.shape, lambda b, g, i: (0, 0, 0)),
            pl.BlockSpec(ptab.shape, lambda b, g, i: (0, 0)),
            pl.BlockSpec(ctab.shape, lambda b, g, i: (0, 0, 0)),
        ],
        out_specs=pl.BlockSpec((tq, width), lambda b, g, i: (b * nq + i, g)),
        scratch_shapes=[
            pltpu.VMEM((HEAD_DIM, nt), BF16),
            pltpu.VMEM((HEAD_DIM + ONES_ROWS, seq), BF16),
            pltpu.VMEM((HEAD_DIM + ONES_ROWS, seq), BF16),
            pltpu.VMEM((ns, tq), F32),
            pltpu.VMEM((HEAD_DIM + ONES_ROWS, NSA_HPG * tq), F32),
            pltpu.VMEM((tk, NSA_HPG * tq), F32),
            pltpu.VMEM((tk, NSA_HPG * tq), F32),
            pltpu.VMEM((1, NSA_HPG * tq), F32),
        ],
        compiler_params=_params(("arbitrary", "arbitrary", "arbitrary")),
        name="nsa_attn",
    )(proj, cos_t, sin_t, proj, kvc, kvc, proj, proj, proj, proj, ovt, wtab, ptab, ctab)


def _layernorm(y, g, b):
    mu = jnp.mean(y, axis=1, keepdims=True)
    yc = y - mu
    var = jnp.mean(yc * yc, axis=1, keepdims=True)
    return yc * lax.rsqrt(var + EPS) * g + b


N_CHUNKS = D_MODEL // LANES
ROW_PITCH = N_CHUNKS + 1


def _store_chunk_rows(ref, val):
    tm = val.shape[0]
    for k in range(N_CHUNKS):
        ref[pl.ds(k, tm, stride=ROW_PITCH), :] = val[:, k * LANES:(k + 1) * LANES]
    for k in range(N_CHUNKS, ROW_PITCH):
        ref[pl.ds(k, tm, stride=ROW_PITCH), :] = jnp.zeros((tm, LANES), ref.dtype)


def _load_chunk_rows(ref, tm):
    return jnp.concatenate([ref[pl.ds(k, tm, stride=ROW_PITCH), :] for k in range(N_CHUNKS)], axis=1)


OUT_ROW_CHUNK = 256


def _outproj_kernel(od_ref, on_ref, w_ref, x_ref, g_ref, b_ref, wr_ref, x1_ref, route_ref):
    wr = wr_ref[...]
    w_hi = wr.astype(BF16)
    w_lo = (wr - w_hi.astype(F32)).astype(BF16)
    for r0 in range(0, x_ref.shape[0], OUT_ROW_CHUNK):
        _outproj_rows(od_ref, on_ref, w_ref, x_ref, g_ref, b_ref, w_hi, w_lo, x1_ref, route_ref,
                      slice(r0, r0 + OUT_ROW_CHUNK))


def _outproj_rows(od_ref, on_ref, w_ref, x_ref, g_ref, b_ref, w_hi, w_lo, x1_ref, route_ref, rs):
    half = od_ref.shape[1]
    h = _dot(od_ref[rs, :], w_ref[:half, :]) + _dot(on_ref[rs, :], w_ref[half:, :])
    x1 = _layernorm(DN_ALPHA * x_ref[rs, :] + h, g_ref[...], b_ref[...])
    _store_chunk_rows(x1_ref.at[rs.start * ROW_PITCH:rs.stop * ROW_PITCH, :], x1)

    x_hi = x1.astype(BF16)
    x_lo = (x1 - x_hi.astype(F32)).astype(BF16)
    w2 = jnp.concatenate([w_hi, w_lo], axis=1)
    a = _dot(x_hi, w2)
    b = _dot(x_lo, w2)
    logits = (a[:, :LANES] + a[:, LANES:]) + (b[:, :LANES] + b[:, LANES:])

    lane = lax.broadcasted_iota(jnp.int32, logits.shape, 1).astype(F32)
    ninf = -jnp.inf
    gl = jnp.where(lane < N_GROUPS, logits, ninf)
    gmax = jnp.max(gl, axis=1, keepdims=True)
    g_w = 1.0 / jnp.sum(jnp.exp(gl - gmax), axis=1, keepdims=True)
    g_sel = jnp.min(jnp.where(gl == gmax, lane, float(LANES)), axis=1, keepdims=True)
    lo = N_GROUPS + EXPERTS_PER_GROUP * g_sel
    el = jnp.where((lane >= lo) & (lane < lo + EXPERTS_PER_GROUP), logits, ninf)
    e1 = jnp.max(el, axis=1, keepdims=True)
    i1 = jnp.min(jnp.where(el == e1, lane, float(LANES)), axis=1, keepdims=True)
    el2 = jnp.where(lane == i1, ninf, el)
    e2 = jnp.max(el2, axis=1, keepdims=True)
    i2 = jnp.min(jnp.where(el2 == e2, lane, float(LANES)), axis=1, keepdims=True)
    r = jnp.exp(e2 - e1)
    w1 = g_w / (1.0 + r)
    w2 = g_w * r / (1.0 + r)
    route = jnp.where(lane == 0, i1 - N_GROUPS,
                      jnp.where(lane == 1, i2 - N_GROUPS,
                                jnp.where(lane == 2, w1, jnp.where(lane == 3, w2, 0.0))))
    route_ref[rs, :] = route


def _outproj_call(o_diff, o_nsa, w_out, x2, g, b, wr, tm):
    n, d = x2.shape
    half = o_diff.shape[1]
    return pl.pallas_call(
        _outproj_kernel,
        out_shape=(jax.ShapeDtypeStruct((n * ROW_PITCH, LANES), F32), jax.ShapeDtypeStruct((n, LANES), F32)),
        grid=(n // tm,),
        in_specs=[
            pl.BlockSpec((tm, half), lambda i: (i, 0)),
            pl.BlockSpec((tm, half), lambda i: (i, 0)),
            pl.BlockSpec((2 * half, d), lambda i: (0, 0)),
            pl.BlockSpec((tm, d), lambda i: (i, 0)),
            pl.BlockSpec((1, d), lambda i: (0, 0)),
            pl.BlockSpec((1, d), lambda i: (0, 0)),
            pl.BlockSpec((d, LANES), lambda i: (0, 0)),
        ],
        out_specs=(pl.BlockSpec((tm * ROW_PITCH, LANES), lambda i: (i, 0)),
                   pl.BlockSpec((tm, LANES), lambda i: (i, 0))),
        compiler_params=_params(("parallel",)),
        name="outproj_ln1_router",
    )(o_diff, o_nsa, w_out, x2, g, b, wr)


MOE_TILE_ROWS = 256
COMBINE_TILE_TOKENS = 256


def _token_copy(src_hbm, dst, sem, src_row, dst_tok):
    return pltpu.make_async_copy(src_hbm.at[pl.ds(src_row, N_CHUNKS)],
                                 dst.at[pl.ds(dst_tok * ROW_PITCH, N_CHUNKS)], sem)


def _wait_tokens(src_hbm, dst, sem, count):
    pltpu.make_async_copy(src_hbm.at[pl.ds(0, count * N_CHUNKS)], dst.at[pl.ds(0, count * N_CHUNKS)], sem).wait()


def _moe_kernel(te_ref, nu_ref, rows_ref, rows_next_ref, x_hbm, wg_ref, wu_ref, wd_ref, o_ref,
                xbuf, sem, wg_sc, wu_sc, wd_sc, *, tm):
    i = pl.program_id(0)
    n_used = nu_ref[0]
    slot = lax.rem(i, 2)

    def issue(ids_ref, s):
        def body(r, carry):
            _token_copy(x_hbm, xbuf.at[s], sem.at[s], ids_ref[0, 0, r], r).start()
            return carry

        lax.fori_loop(0, tm, body, 0, unroll=8)

    @pl.when(i == 0)
    def _():
        issue(rows_ref, 0)

    @pl.when(i + 1 < n_used)
    def _():
        issue(rows_next_ref, 1 - slot)

    prev = te_ref[jnp.maximum(i - 1, 0)]

    @pl.when((i == 0) | (te_ref[i] != prev))
    def _():
        wg_sc[...] = wg_ref[...].astype(BF16)
        wu_sc[...] = wu_ref[...].astype(BF16)
        wd_sc[...] = wd_ref[...].astype(BF16)

    @pl.when(i < n_used)
    def _():
        _wait_tokens(x_hbm, xbuf.at[slot], sem.at[slot], tm)
        xb = _load_chunk_rows(xbuf.at[slot], tm).astype(BF16)
        gate = _dot(xb, wg_sc[...])
        up = _dot(xb, wu_sc[...])
        h = (gate * jax.nn.sigmoid(gate) * up).astype(BF16)
        _store_chunk_rows(o_ref, _dot(h, wd_sc[...]))

    @pl.when(i >= n_used)
    def _():
        o_ref[...] = jnp.zeros(o_ref.shape, o_ref.dtype)


def _moe_call(tile_expert, n_used, rows, x1c, w_gate, w_up, w_down, tm):
    n_tiles = rows.shape[0]
    d, f = w_gate.shape[1], w_gate.shape[2]
    grid_spec = pltpu.PrefetchScalarGridSpec(
        num_scalar_prefetch=2,
        grid=(n_tiles,),
        in_specs=[
            pl.BlockSpec((1, 1, tm), lambda i, te, nu: (i, 0, 0), memory_space=pltpu.SMEM),
            pl.BlockSpec((1, 1, tm), lambda i, te, nu: (jnp.minimum(i + 1, n_tiles - 1), 0, 0),
                         memory_space=pltpu.SMEM),
            pl.BlockSpec(memory_space=pl.ANY),
            pl.BlockSpec((None, d, f), lambda i, te, nu: (te[i], 0, 0)),
            pl.BlockSpec((None, d, f), lambda i, te, nu: (te[i], 0, 0)),
            pl.BlockSpec((None, f, d), lambda i, te, nu: (te[i], 0, 0)),
        ],
        out_specs=pl.BlockSpec((tm * ROW_PITCH, LANES), lambda i, te, nu: (i, 0)),
        scratch_shapes=[
            pltpu.VMEM((2, tm * ROW_PITCH, LANES), F32),
            pltpu.SemaphoreType.DMA((2,)),
            pltpu.VMEM((d, f), BF16), pltpu.VMEM((d, f), BF16), pltpu.VMEM((f, d), BF16),
        ],
    )
    return pl.pallas_call(
        functools.partial(_moe_kernel, tm=tm),
        out_shape=jax.ShapeDtypeStruct((n_tiles * tm * ROW_PITCH, LANES), F32),
        grid_spec=grid_spec,
        compiler_params=_params(("arbitrary",)),
        name="moe_experts",
    )(tile_expert, n_used, rows, rows, x1c, w_gate, w_up, w_down)


def _combine_kernel(slots_ref, slots_next_ref, y_hbm, x1_ref, route_ref, g_ref, b_ref, o_ref, buf, sem, *, tm):
    i = pl.program_id(0)
    slot = lax.rem(i, 2)

    @pl.when(i == 0)
    def _():
        def body(r, carry):
            for k in range(2):
                _token_copy(y_hbm, buf.at[0, k], sem.at[0], slots_ref[0, 0, 2 * r + k], r).start()
            return carry

        lax.fori_loop(0, tm, body, 0, unroll=8)

    def tile(prefetch):
        if prefetch:
            for r in range(tm):
                for k in range(2):
                    _token_copy(y_hbm, buf.at[1 - slot, k], sem.at[1 - slot],
                                slots_next_ref[0, 0, 2 * r + k], r).start(priority=k)
        for k in range(2):
            _wait_tokens(y_hbm, buf.at[slot, k], sem.at[slot], tm)
        route = route_ref[...]
        y = (route[:, 2:3] * _load_chunk_rows(buf.at[slot, 0], tm)
             + route[:, 3:4] * _load_chunk_rows(buf.at[slot, 1], tm))
        x1 = _load_chunk_rows(x1_ref, tm)
        o_ref[...] = _layernorm(DN_ALPHA * x1 + y, g_ref[...], b_ref[...])

    last = pl.num_programs(0) - 1
    pl.when(i < last)(lambda: tile(True))
    pl.when(i == last)(lambda: tile(False))


def _combine_call(slots, yc, x1c, route, g, b, tm):
    n = route.shape[0]
    d = D_MODEL
    nt = n // tm
    return pl.pallas_call(
        functools.partial(_combine_kernel, tm=tm),
        out_shape=jax.ShapeDtypeStruct((n, d), F32),
        grid=(nt,),
        in_specs=[
            pl.BlockSpec((1, 1, 2 * tm), lambda i: (i, 0, 0), memory_space=pltpu.SMEM),
            pl.BlockSpec((1, 1, 2 * tm), lambda i: (jnp.minimum(i + 1, nt - 1), 0, 0), memory_space=pltpu.SMEM),
            pl.BlockSpec(memory_space=pl.ANY),
            pl.BlockSpec((tm * ROW_PITCH, LANES), lambda i: (i, 0)),
            pl.BlockSpec((tm, LANES), lambda i: (i, 0)),
            pl.BlockSpec((1, d), lambda i: (0, 0)),
            pl.BlockSpec((1, d), lambda i: (0, 0)),
        ],
        out_specs=pl.BlockSpec((tm, d), lambda i: (i, 0)),
        scratch_shapes=[pltpu.VMEM((2, 2, tm * ROW_PITCH, LANES), F32), pltpu.SemaphoreType.DMA((2,))],
        compiler_params=_params(("arbitrary",)),
        name="moe_combine_ln2",
    )(slots, slots, yc, x1c, route, g, b)


def _moe_plan(e_idx, tm, n_tiles):
    n = e_idx.shape[0]
    lanes = jnp.arange(N_EXPERTS, dtype=jnp.int32)[None, :]
    hit = ((e_idx[:, 0:1] == lanes) | (e_idx[:, 1:2] == lanes)).astype(jnp.int32)
    before = jnp.cumsum(hit, axis=0) - hit
    counts = jnp.sum(hit, axis=0)
    ptiles = (counts + tm - 1) // tm
    tile_end = jnp.cumsum(ptiles)
    base = (tile_end - ptiles) * tm
    slot = jnp.take_along_axis(before + base[None, :], e_idx, axis=1).reshape(-1)
    rows = jnp.zeros((n_tiles * tm,), jnp.int32).at[slot].set(jnp.arange(2 * n, dtype=jnp.int32) // 2)
    tile_ids = jnp.arange(n_tiles, dtype=jnp.int32)
    tile_expert = jnp.minimum(jnp.sum((tile_end[None, :] <= tile_ids[:, None]).astype(jnp.int32), axis=1),
                              N_EXPERTS - 1)
    return slot.astype(jnp.int32), rows, tile_expert, tile_end[-1:].astype(jnp.int32)


def _rope_tables(seq):
    inv_freq = ROPE_THETA ** (-np.arange(0, ROT_DIM, 2, dtype=np.float64) / ROT_DIM)
    ang = np.arange(seq, dtype=np.float64)[:, None] * inv_freq[None, :]
    cos, sin = np.cos(ang), np.sin(ang)
    pad1 = np.ones((seq, HEAD_DIM - ROT_DIM))
    pad0 = np.zeros((seq, HEAD_DIM - ROT_DIM))
    return (jnp.asarray(np.concatenate([cos, cos, pad1], axis=1), dtype=F32),
            jnp.asarray(np.concatenate([-sin, sin, pad0], axis=1), dtype=F32))


def _layer(x, w_in, diff_lambda, diff_subln_g, cmp_pos, cmp_w1, cmp_b1, cmp_w2, cmp_b2, w_out,
           ln1_g, ln1_b, router_group, router_expert, w_gate, w_up, w_down, ln2_g, ln2_b, lambda_init):
    batch, seq, d = x.shape
    n = batch * seq
    x2 = x.reshape(n, d)

    seg = lambda a, b: w_in[:, a:b]
    pad = lambda w: jnp.pad(w, ((0, 0), (0, LANES - w.shape[1])))
    gate_w = seg(5632, 5656)
    w_aug = jnp.concatenate(
        [seg(3072, 4096) * EXP2_SCALE, seg(0, 1024) * EXP2_SCALE, seg(1024, 2048), seg(2048, 3072), seg(4096, 5632),
         pad(gate_w[:, :12]), pad(gate_w[:, 12:])], axis=1).astype(BF16)
    flags = jnp.zeros((N_UNITS // 2,), jnp.int32).at[jnp.array(ROPE_BLOCKS)].set(1)
    cos_t, sin_t = _rope_tables(seq)
    proj = _proj_call(x2, w_aug, flags, cos_t, sin_t, seq, tm=min(PROJ_ROW_TILE, seq))

    nt = seq // CMP_STRIDE
    half_feat = CMP_STRIDE * HEAD_DIM
    w1cat = jnp.concatenate([cmp_w1[:, :half_feat], cmp_w1[:, half_feat:]], axis=2).astype(BF16)
    kvc = _compress_call(proj, w1cat, cmp_pos.reshape(2, 2, half_feat), cmp_b1[:, None, :],
                         cmp_w2.astype(BF16), cmp_b2[:, None, :], batch, seq)

    o_diff = _diff_call(proj, diff_lambda, diff_subln_g[None, :], batch, seq, min(DIFF_Q_TILE, seq), lambda_init)

    ns = seq // SLC_BLOCK
    ci = np.arange(nt)[None, :] * CMP_STRIDE
    sj = np.arange(ns)[:, None] * SLC_BLOCK
    ovt = jnp.asarray((ci < sj + SLC_BLOCK) & (ci + CMP_BLOCK > sj) & (np.arange(nt)[None, :] < nt - 1), dtype=BF16)
    o_nsa = _nsa_call(proj, cos_t, sin_t, kvc, ovt, batch, seq, tq=NSA_Q_TILE, tk=min(NSA_K_TILE, seq))

    wr = jnp.pad(jnp.concatenate([router_group, router_expert], axis=1),
                 ((0, 0), (0, LANES - N_GROUPS - N_EXPERTS)))
    x1, route = _outproj_call(o_diff, o_nsa, w_out.astype(BF16), x2, ln1_g[None, :], ln1_b[None, :], wr,
                              tm=OUT_ROW_TILE)

    tm = MOE_TILE_ROWS
    n_tiles = -(-2 * n // tm) + N_EXPERTS
    e_idx = route[:, 0:2].astype(jnp.int32)
    slot, rows, tile_expert, n_used = _moe_plan(e_idx, tm, n_tiles)
    ys = _moe_call(tile_expert, n_used, (rows * ROW_PITCH).reshape(n_tiles, 1, tm), x1, w_gate, w_up, w_down, tm)
    tc = COMBINE_TILE_TOKENS
    out = _combine_call((slot * ROW_PITCH).reshape(n // tc, 1, 2 * tc), ys, x1, route,
                        ln2_g[None, :], ln2_b[None, :], tc)
    return out.reshape(batch, seq, d)


def kernel(x, w_in, diff_lambda, diff_subln_g, cmp_pos, cmp_w1, cmp_b1, cmp_w2, cmp_b2, w_out, ln1_g, ln1_b,
           router_group, router_expert, expert_w_gate, expert_w_up, expert_w_down, ln2_g, ln2_b):
    for l in range(DEPTH):
        lambda_init = 0.8 - 0.6 * math.exp(-0.3 * l)
        x = _layer(x, w_in[l], diff_lambda[l], diff_subln_g[l], cmp_pos[l], cmp_w1[l], cmp_b1[l], cmp_w2[l],
                   cmp_b2[l], w_out[l], ln1_g[l], ln1_b[l], router_group[l], router_expert[l],
                   expert_w_gate[l], expert_w_up[l], expert_w_down[l], ln2_g[l], ln2_b[l], lambda_init)
    return x
```

```python
import functools
import math

import numpy as np
import jax
import jax.numpy as jnp
from jax import lax
from jax.experimental import pallas as pl
from jax.experimental.pallas import tpu as pltpu

F32 = jnp.float32
BF16 = jnp.bfloat16

D_MODEL = 2048
HEAD_DIM = 128
ROT_DIM = HEAD_DIM // 4
ROPE_THETA = 500000.0
NEG_INF = -1e30
BIG = 1e30
EPS = 1e-5

DIFF_HEADS = 4
DIFF_VDIM = 2 * HEAD_DIM

NSA_HEADS = 8
NSA_KV_GROUPS = 2
NSA_HPG = NSA_HEADS // NSA_KV_GROUPS
CMP_BLOCK = 32
CMP_STRIDE = 16
CMP_HIDDEN = 256
SLC_BLOCK = 64
SLC_TOPK = 16
WINDOW = 512

N_GROUPS = 4
EXPERTS_PER_GROUP = 8
N_EXPERTS = N_GROUPS * EXPERTS_PER_GROUP
EXPERT_HIDDEN = 512

DEPTH = 1
DN_ALPHA = (2.0 * DEPTH) ** 0.25

LANES = 128
VMEM_LIMIT = 56 * 1024 * 1024

PROJ_ROW_TILE = 2048
DIFF_Q_TILE = 1024
NSA_Q_TILE = 256
NSA_K_TILE = 512
OUT_ROW_TILE = 512

U_NQ, U_DQ, U_DK, U_DV = 0, 8, 16, 24
U_KC, U_VC, U_KS, U_VS, U_KW, U_VW, U_GATE = 32, 34, 36, 38, 40, 42, 44
N_UNITS = 46
ROPE_BLOCKS = tuple(range(U_DQ // 2, U_DV // 2)) + (U_KS // 2, U_KW // 2)


def _dot(a, b):
    return jnp.dot(a, b, preferred_element_type=F32)


def _dot_nt(a, b):
    return lax.dot_general(a, b, (((1,), (1,)), ((), ())), preferred_element_type=F32)


def _params(sem, vmem=VMEM_LIMIT):
    return pltpu.CompilerParams(dimension_semantics=sem, vmem_limit_bytes=vmem)


PROJ_ROW_CHUNK = 256


def _rope(a, c, s):
    lane = lax.broadcasted_iota(jnp.int32, a.shape, 1)
    half = ROT_DIM // 2
    partner = jnp.where(lane < half, pltpu.roll(a, LANES - half, 1), pltpu.roll(a, half, 1))
    return a * c + partner * s


def _proj_kernel(flags_ref, x_ref, w_ref, cos_ref, sin_ref, o_ref, xb_sc):
    j = pl.program_id(1)

    @pl.when(j == 0)
    def _():
        xb_sc[...] = x_ref[...].astype(BF16)

    def block(rotary):
        for r0 in range(0, xb_sc.shape[0], PROJ_ROW_CHUNK):
            rs = slice(r0, r0 + PROJ_ROW_CHUNK)
            acc = _dot(xb_sc[rs, :], w_ref[...])
            if rotary:
                for hh in range(2):
                    cs = slice(hh * LANES, (hh + 1) * LANES)
                    o_ref[rs, cs] = _rope(acc[:, cs], cos_ref[rs, :], sin_ref[rs, :]).astype(o_ref.dtype)
            else:
                o_ref[rs, :] = acc.astype(o_ref.dtype)

    pl.when(flags_ref[j] == 0)(lambda: block(False))
    pl.when(flags_ref[j] != 0)(lambda: block(True))


def _proj_call(xb, w_aug, flags, cos_t, sin_t, seq, tm):
    n, d = xb.shape
    nj = w_aug.shape[1] // 256
    tpb = seq // tm
    grid_spec = pltpu.PrefetchScalarGridSpec(
        num_scalar_prefetch=1,
        grid=(n // tm, nj),
        in_specs=[
            pl.BlockSpec((tm, d), lambda i, j, f: (i, 0)),
            pl.BlockSpec((d, 256), lambda i, j, f: (0, j)),
            pl.BlockSpec((tm, LANES), lambda i, j, f: (i % tpb, 0)),
            pl.BlockSpec((tm, LANES), lambda i, j, f: (i % tpb, 0)),
        ],
        out_specs=pl.BlockSpec((tm, 256), lambda i, j, f: (i, j)),
        scratch_shapes=[pltpu.VMEM((tm, d), BF16)],
    )
    return pl.pallas_call(
        _proj_kernel,
        out_shape=jax.ShapeDtypeStruct((n, w_aug.shape[1]), BF16),
        grid_spec=grid_spec,
        compiler_params=_params(("parallel", "arbitrary")),
        name="proj",
    )(flags, xb, w_aug, cos_t, sin_t)


def _compress_kernel(t_ref, w1_ref, pos_ref, b1_ref, w2_ref, b2_ref, o_ref, t_sc):
    nt = t_sc.shape[0] // CMP_STRIDE
    t_sc[...] = t_ref[...].astype(F32)
    r = jnp.concatenate([t_sc[pl.ds(k, nt, stride=CMP_STRIDE), :] for k in range(CMP_STRIDE)], axis=1).astype(BF16)
    ab = _dot(r, w1_ref[...])
    pos = pos_ref[...]
    pa = jnp.broadcast_to(pos[0:1], (8, pos.shape[1])).astype(BF16)
    pb = jnp.broadcast_to(pos[1:2], (8, pos.shape[1])).astype(BF16)
    const = _dot(pa, w1_ref[:, :CMP_HIDDEN])[0:1] + _dot(pb, w1_ref[:, CMP_HIDDEN:])[0:1]
    h = ab[:, :CMP_HIDDEN] + pltpu.roll(ab[:, CMP_HIDDEN:], nt - 1, 0) + const + b1_ref[...]
    h = jax.nn.gelu(h)
    o_ref[...] = (_dot(h.astype(BF16), w2_ref[...]) + b2_ref[...]).astype(o_ref.dtype)


def _compress_call(proj, w1cat, pos2, b1, w2, b2, b, seq):
    four = 2 * NSA_KV_GROUPS
    nt = seq // CMP_STRIDE
    k = CMP_STRIDE * HEAD_DIM
    return pl.pallas_call(
        _compress_kernel,
        out_shape=jax.ShapeDtypeStruct((b, four, nt, HEAD_DIM), BF16),
        grid=(b, four),
        in_specs=[
            pl.BlockSpec((seq, HEAD_DIM), lambda i, c: (i, U_KC + c)),
            pl.BlockSpec((None, k, 2 * CMP_HIDDEN), lambda i, c: (c // 2, 0, 0)),
            pl.BlockSpec((None, 2, k), lambda i, c: (c // 2, 0, 0)),
            pl.BlockSpec((None, 1, CMP_HIDDEN), lambda i, c: (c // 2, 0, 0)),
            pl.BlockSpec((None, CMP_HIDDEN, HEAD_DIM), lambda i, c: (c // 2, 0, 0)),
            pl.BlockSpec((None, 1, HEAD_DIM), lambda i, c: (c // 2, 0, 0)),
        ],
        out_specs=pl.BlockSpec((None, None, nt, HEAD_DIM), lambda i, c: (i, c, 0, 0)),
        scratch_shapes=[pltpu.VMEM((seq, HEAD_DIM), F32)],
        compiler_params=_params(("parallel", "parallel")),
        name="compress",
    )(proj, w1cat, pos2, b1, w2, b2)


EXP2_SCALE = HEAD_DIM ** -0.5 * math.log2(math.e)
ONES_ROWS = 16


def _transpose_into(src_ref, dst_ref):
    def body(c, carry):
        off = pl.multiple_of(c * LANES, LANES)
        dst_ref[:, pl.ds(off, LANES)] = src_ref[pl.ds(off, LANES), :].astype(F32).T.astype(dst_ref.dtype)
        return carry

    n_tiles = src_ref.shape[0] // LANES
    lax.fori_loop(0, n_tiles, body, 0, unroll=math.gcd(n_tiles, 8))


def _diff_kernel(dl_ref, q_ref, k_ref, v_ref, g_ref, o_ref, vt_sc, acc_sc, sa_sc, sb_sc, *, tq, lambda_init):
    qi = pl.program_id(2)

    @pl.when(qi == 0)
    def _():
        for c in range(2):
            _transpose_into(v_ref.at[:, c * LANES:(c + 1) * LANES], vt_sc.at[c * LANES:(c + 1) * LANES, :])

    acc_sc[...] = jnp.zeros(acc_sc.shape, F32)
    q = q_ref[...]
    qpos = qi * tq + lax.broadcasted_iota(jnp.int32, (1, tq), 1)

    tk = tq // 2

    def scores(j, dst, lo=0):
        kt = k_ref[pl.ds(pl.multiple_of(j * tk, tk), tk), :]
        for c in range(2):
            dst[c, :, lo:] = _dot_nt(kt[:, c * HEAD_DIM:(c + 1) * HEAD_DIM], q[lo:, c * HEAD_DIM:(c + 1) * HEAD_DIM])

    def absorb(src, j, masked, carry, lo=0):
        off = pl.multiple_of(j * tk, tk)
        vt = vt_sc[:, pl.ds(off, tk)]
        out = []
        for c in range(2):
            m_old, l_old = carry[2 * c][:, lo:], carry[2 * c + 1][:, lo:]
            s = src[c, :, lo:]
            if masked:
                s = jnp.where(lax.broadcasted_iota(jnp.int32, s.shape, 0) <= qpos[:, lo:] - off, s, NEG_INF)
            m_new = jnp.maximum(m_old, jnp.max(s, axis=0, keepdims=True))
            p = jnp.exp2(s - m_new)
            alpha = jnp.exp2(m_old - m_new)
            l_new = alpha * l_old + jnp.sum(p, axis=0, keepdims=True)
            acc_sc[c, :, lo:] = alpha * acc_sc[c, :, lo:] + _dot(vt, p.astype(BF16))
            if lo:
                m_new = jnp.concatenate([carry[2 * c][:, :lo], m_new], axis=1)
                l_new = jnp.concatenate([carry[2 * c + 1][:, :lo], l_new], axis=1)
            out += [m_new, l_new]
        return tuple(out)

    def pair(p, carry):
        j = 2 * p
        scores(j + 1, sb_sc)
        carry = absorb(sa_sc, j, False, carry)
        scores(j + 2, sa_sc)
        return absorb(sb_sc, j + 1, False, carry)

    init = (jnp.full((1, tq), NEG_INF, F32), jnp.zeros((1, tq), F32)) * 2
    scores(0, sa_sc)
    carry = lax.fori_loop(0, qi, pair, init)
    j_tail = 2 * qi
    scores(j_tail + 1, sb_sc, lo=tk)
    carry = absorb(sa_sc, j_tail, True, carry)
    _, l0, _, l1 = absorb(sb_sc, j_tail + 1, True, carry, lo=tk)

    dl = dl_ref[...]
    lam = (jnp.exp(jnp.sum(dl[0:1] * dl[1:2], axis=1, keepdims=True))
           - jnp.exp(jnp.sum(dl[2:3] * dl[3:4], axis=1, keepdims=True)) + lambda_init)
    o = acc_sc[0] * (1.0 / l0) - lam * (acc_sc[1] * (1.0 / l1))
    o = o * (lax.rsqrt(jnp.mean(o * o, axis=0, keepdims=True) + EPS) * (1.0 - lambda_init))
    for c in range(DIFF_VDIM // LANES):
        for r in range(tq // LANES):
            blk = o[c * LANES:(c + 1) * LANES, r * LANES:(r + 1) * LANES].T
            o_ref[r * LANES:(r + 1) * LANES, c * LANES:(c + 1) * LANES] = (
                blk * g_ref[:, c * LANES:(c + 1) * LANES]).astype(o_ref.dtype)


def _diff_call(proj, dl, g, batch, seq, tq, lambda_init):
    nq = seq // tq
    kern = functools.partial(_diff_kernel, tq=tq, lambda_init=lambda_init)
    return pl.pallas_call(
        kern,
        out_shape=jax.ShapeDtypeStruct((batch * seq, DIFF_HEADS * DIFF_VDIM), BF16),
        grid=(batch, DIFF_HEADS, nq),
        in_specs=[
            pl.BlockSpec((4, HEAD_DIM), lambda b, h, i: (0, 0)),
            pl.BlockSpec((tq, 256), lambda b, h, i: (b * nq + i, U_DQ // 2 + h)),
            pl.BlockSpec((seq, 256), lambda b, h, i: (b, U_DK // 2 + h)),
            pl.BlockSpec((seq, 256), lambda b, h, i: (b, U_DV // 2 + h)),
            pl.BlockSpec((1, DIFF_VDIM), lambda b, h, i: (0, 0)),
        ],
        out_specs=pl.BlockSpec((tq, DIFF_VDIM), lambda b, h, i: (b * nq + i, h)),
        scratch_shapes=[
            pltpu.VMEM((DIFF_VDIM, seq), BF16),
            pltpu.VMEM((2, DIFF_VDIM, tq), F32),
            pltpu.VMEM((2, tq // 2, tq), F32),
            pltpu.VMEM((2, tq // 2, tq), F32),
        ],
        compiler_params=_params(("arbitrary", "arbitrary", "arbitrary")),
        name="diff_attn",
    )(dl, proj, proj, proj, g)


def _stack_heads(x):
    return jnp.concatenate([x[:, h * HEAD_DIM:(h + 1) * HEAD_DIM] for h in range(NSA_HPG)], axis=0)


def _heads(x):
    return jnp.concatenate([x] * NSA_HPG, axis=1)


def _mask_tables(seq, tq, tk):
    nt = seq // CMP_STRIDE
    slab = min(WINDOW + tq, seq)
    q = np.arange(tq)[None, :]
    k = np.arange(slab)[:, None]
    win = []
    for i in range(WINDOW // tq + 1):
        newest = q + (i * tq if i < WINDOW // tq else WINDOW)
        win.append((k <= newest) & (k > newest - WINDOW))
    u = np.arange(2 * nt)[:, None] - nt
    cmp_ok = u <= ((q - (CMP_BLOCK - 1)) >> int(math.log2(CMP_STRIDE)))
    kk = np.arange(tk)[:, None]
    causal = [kk <= d * tq + q for d in range(tk // tq)] + [np.ones((tk, tq), bool)]
    to_bias = lambda m: jnp.asarray(np.where(np.asarray(m), 0.0, NEG_INF), dtype=F32)
    return to_bias(np.stack(win)), to_bias(cmp_ok), to_bias(np.stack(causal))


def _nsa_kernel(q_ref, cos_ref, sin_ref, gate_ref, kc_ref, vc_ref, ks_ref, vs_ref, kw_ref, vw_ref, ovt_ref,
                wtab_ref, ptab_ref, ctab_ref, o_ref, vct_sc, vst_sc, vwt_sc, bias_sc, acc_sc, sa_sc, sb_sc, m_sc, *, tq, tk, seq, top_k):
    qi = pl.program_id(2)
    q0 = qi * tq
    rows = NSA_HPG * tq
    nt = kc_ref.shape[0]
    ns = ovt_ref.shape[0]
    nb = tk // SLC_BLOCK

    @pl.when(qi == 0)
    def _():
        _transpose_into(vc_ref, vct_sc)
        for v_ref, vt_sc in ((vs_ref, vst_sc), (vw_ref, vwt_sc)):
            _transpose_into(v_ref, vt_sc.at[:HEAD_DIM, :])
            vt_sc[HEAD_DIM:, :] = jnp.ones((ONES_ROWS, seq), vt_sc.dtype)

    q = q_ref[...]
    qs = _stack_heads(q)
    qr = jnp.concatenate(
        [_rope(q[:, h * HEAD_DIM:(h + 1) * HEAD_DIM].astype(F32), cos_ref[...], sin_ref[...]).astype(BF16)
         for h in range(NSA_HPG)], axis=0)
    qpos = q0 + (lax.broadcasted_iota(jnp.int32, (1, rows), 1) & (tq - 1))

    def sel_scores(j, dst):
        dst[...] = _dot_nt(ks_ref[pl.ds(pl.multiple_of(j * tk, tk), tk), :], qr)

    sel_scores(0, sa_sc)

    s_c = _dot_nt(kc_ref[...], qs)
    c0 = pl.multiple_of(nt - q0 // CMP_STRIDE, CMP_STRIDE)
    s_c = s_c + _heads(ptab_ref[pl.ds(c0, nt), :])
    m_c = jnp.max(s_c, axis=0, keepdims=True)
    e_c = jnp.exp2(s_c - m_c)
    l_c = jnp.sum(e_c, axis=0, keepdims=True)
    p_c = e_c * jnp.where(qpos >= CMP_BLOCK - 1, 1.0 / l_c, 0.0)
    o_c = _dot(vct_sc[...], p_c.astype(BF16))

    p_sum = p_c[:, 0:tq]
    for h in range(1, NSA_HPG):
        p_sum = p_sum + p_c[:, h * tq:(h + 1) * tq]
    p_hi = p_sum.astype(BF16)
    p_lo = (p_sum - p_hi.astype(F32)).astype(BF16)
    imp = _dot(ovt_ref[...], p_hi) + _dot(ovt_ref[...], p_lo)

    blk = lax.broadcasted_iota(jnp.int32, (ns, tq), 0)
    qpos_l = q0 + lax.broadcasted_iota(jnp.int32, (ns, tq), 1)
    cur = lax.shift_right_logical(qpos_l, int(math.log2(SLC_BLOCK)))
    valid_s = blk <= cur
    forced = (blk == 0) | (blk == cur) | (blk == cur - 1)
    work = jnp.where(forced, -jnp.inf, jnp.where(valid_s, imp, NEG_INF))
    blk_f = blk.astype(F32)
    for _ in range(top_k - 3):
        mx = jnp.max(work, axis=0, keepdims=True)
        idx = jnp.min(jnp.where(work == mx, blk_f, float(ns)), axis=0, keepdims=True)
        work = jnp.where(blk_f == idx, -jnp.inf, work)
    bias_sc[...] = jnp.where(work == -jnp.inf, 0.0, NEG_INF)

    slab = min(WINDOW + tq, seq)
    start = pl.multiple_of(jnp.maximum(q0 - WINDOW, 0), tq)
    s_w = _dot_nt(kw_ref[pl.ds(start, slab), :], qr)
    s_w = s_w + _heads(wtab_ref[jnp.minimum(qi, WINDOW // tq)])
    p_w = jnp.exp2(s_w - jnp.max(s_w, axis=0, keepdims=True))
    o_w = _dot(vwt_sc[:, pl.ds(start, slab)], p_w.astype(BF16))
    o_w = o_w[:HEAD_DIM] * (1.0 / o_w[HEAD_DIM:HEAD_DIM + 1])

    acc_sc[...] = jnp.zeros(acc_sc.shape, F32)

    def sel_absorb(src, j, causal, m_old):
        off = pl.multiple_of(j * tk, tk)
        s = src[...]
        if causal:
            s = s + _heads(ctab_ref[jnp.clip((q0 - off) // tq, 0, tk // tq)])
        bias = bias_sc[pl.ds(pl.multiple_of(j * nb, nb), nb), :]
        bias = jnp.concatenate([bias] * NSA_HPG, axis=1)
        s3 = s.reshape(nb, SLC_BLOCK, rows) + bias[:, None, :]
        m_new = jnp.maximum(m_old, jnp.max(jnp.max(s3, axis=0), axis=0, keepdims=True))
        p3 = jnp.exp2(s3 - m_new)
        alpha = jnp.exp2(m_old - m_new)
        p = p3.reshape(tk, rows).astype(BF16)
        acc_sc[...] = alpha * acc_sc[...] + _dot(vst_sc[:, pl.ds(off, tk)], p)
        return m_new

    def sel_pair(p, carry):
        j = 2 * p
        sel_scores(j + 1, sb_sc)
        carry = sel_absorb(sa_sc, j, False, carry)
        sel_scores(j + 2, sa_sc)
        return sel_absorb(sb_sc, j + 1, False, carry)

    n_pairs = (q0 // tk + 2) // 2
    carry = lax.fori_loop(0, n_pairs - 1, sel_pair, jnp.full((1, rows), NEG_INF, F32))
    j_tail = 2 * (n_pairs - 1)
    m_sc[...] = sel_absorb(sa_sc, j_tail, True, carry)

    @pl.when(j_tail + 1 <= q0 // tk)
    def _():
        sel_scores(j_tail + 1, sb_sc)
        sel_absorb(sb_sc, j_tail + 1, True, m_sc[...])

    o_s = acc_sc[:HEAD_DIM, :] * (1.0 / acc_sc[HEAD_DIM:HEAD_DIM + 1, :])

    gates = jax.nn.sigmoid(gate_ref[...].astype(F32))
    gates = jnp.concatenate([gates[r * LANES:(r + 1) * LANES].T for r in range(tq // LANES)],
                            axis=1)
    for h in range(NSA_HPG):
        sl = slice(h * tq, (h + 1) * tq)
        o = (gates[3 * h:3 * h + 1] * o_c[:, sl] + gates[3 * h + 1:3 * h + 2] * o_s[:, sl]
             + gates[3 * h + 2:3 * h + 3] * o_w[:, sl])
        for r in range(tq // LANES):
            o_ref[r * LANES:(r + 1) * LANES, h * HEAD_DIM:(h + 1) * HEAD_DIM] = (
                o[:, r * LANES:(r + 1) * LANES].T.astype(o_ref.dtype))


def _nsa_call(proj, cos_t, sin_t, kvc, ovt, batch, seq, tq, tk):
    nq = seq // tq
    assert WINDOW % tq == 0 and tk % tq == 0 and tq % CMP_STRIDE == 0
    wtab, ptab, ctab = _mask_tables(seq, tq, tk)
    nt = kvc.shape[2]
    ns = ovt.shape[0]
    width = NSA_HPG * HEAD_DIM
    kern = functools.partial(_nsa_kernel, tq=tq, tk=tk, seq=seq, top_k=min(SLC_TOPK, ns))

    def col(unit):
        return pl.BlockSpec((seq, HEAD_DIM), lambda b, g, i: (b, unit + g))

    return pl.pallas_call(
        kern,
        out_shape=jax.ShapeDtypeStruct((batch * seq, NSA_HEADS * HEAD_DIM), BF16),
        grid=(batch, NSA_KV_GROUPS, nq),
        in_specs=[
            pl.BlockSpec((tq, width), lambda b, g, i: (b * nq + i, U_NQ // 4 + g)),
            pl.BlockSpec((tq, LANES), lambda b, g, i: (i, 0)),
            pl.BlockSpec((tq, LANES), lambda b, g, i: (i, 0)),
            pl.BlockSpec((tq, LANES), lambda b, g, i: (b * nq + i, U_GATE + g)),
            pl.BlockSpec((None, None, nt, HEAD_DIM), lambda b, g, i: (b, g, 0, 0)),
            pl.BlockSpec((None, None, nt, HEAD_DIM), lambda b, g, i: (b, 2 + g, 0, 0)),
            col(U_KS), col(U_VS), col(U_KW), col(U_VW),
            pl.BlockSpec((ns, nt), lambda b, g, i: (0, 0)),
            pl.BlockSpec(wtab.shape, lambda b, g, i: (0, 0, 0)),
            pl.BlockSpec(ptab.shape, lambda b, g, i: (0, 0)),
            pl.BlockSpec(ctab.shape, lambda b, g, i: (0, 0, 0)),
        ],
        out_specs=pl.BlockSpec((tq, width), lambda b, g, i: (b * nq + i, g)),
        scratch_shapes=[
            pltpu.VMEM((HEAD_DIM, nt), BF16),
            pltpu.VMEM((HEAD_DIM + ONES_ROWS, seq), BF16),
            pltpu.VMEM((HEAD_DIM + ONES_ROWS, seq), BF16),
            pltpu.VMEM((ns, tq), F32),
            pltpu.VMEM((HEAD_DIM + ONES_ROWS, NSA_HPG * tq), F32),
            pltpu.VMEM((tk, NSA_HPG * tq), F32),
            pltpu.VMEM((tk, NSA_HPG * tq), F32),
            pltpu.VMEM((1, NSA_HPG * tq), F32),
        ],
        compiler_params=_params(("arbitrary", "arbitrary", "arbitrary")),
        name="nsa_attn",
    )(proj, cos_t, sin_t, proj, kvc, kvc, proj, proj, proj, proj, ovt, wtab, ptab, ctab)


def _layernorm(y, g, b):
    mu = jnp.mean(y, axis=1, keepdims=True)
    yc = y - mu
    var = jnp.mean(yc * yc, axis=1, keepdims=True)
    return yc * lax.rsqrt(var + EPS) * g + b


N_CHUNKS = D_MODEL // LANES
ROW_PITCH = N_CHUNKS + 1


def _store_chunk_rows(ref, val):
    tm = val.shape[0]
    for k in range(N_CHUNKS):
        ref[pl.ds(k, tm, stride=ROW_PITCH), :] = val[:, k * LANES:(k + 1) * LANES]
    for k in range(N_CHUNKS, ROW_PITCH):
        ref[pl.ds(k, tm, stride=ROW_PITCH), :] = jnp.zeros((tm, LANES), ref.dtype)


def _load_chunk_rows(ref, tm):
    return jnp.concatenate([ref[pl.ds(k, tm, stride=ROW_PITCH), :] for k in range(N_CHUNKS)], axis=1)


OUT_ROW_CHUNK = 256


def _outproj_kernel(od_ref, on_ref, w_ref, x_ref, g_ref, b_ref, wr_ref, x1_ref, route_ref):
    wr = wr_ref[...]
    w_hi = wr.astype(BF16)
    w_lo = (wr - w_hi.astype(F32)).astype(BF16)
    for r0 in range(0, x_ref.shape[0], OUT_ROW_CHUNK):
        _outproj_rows(od_ref, on_ref, w_ref, x_ref, g_ref, b_ref, w_hi, w_lo, x1_ref, route_ref,
                      slice(r0, r0 + OUT_ROW_CHUNK))


def _outproj_rows(od_ref, on_ref, w_ref, x_ref, g_ref, b_ref, w_hi, w_lo, x1_ref, route_ref, rs):
    half = od_ref.shape[1]
    h = _dot(od_ref[rs, :], w_ref[:half, :]) + _dot(on_ref[rs, :], w_ref[half:, :])
    x1 = _layernorm(DN_ALPHA * x_ref[rs, :] + h, g_ref[...], b_ref[...])
    _store_chunk_rows(x1_ref.at[rs.start * ROW_PITCH:rs.stop * ROW_PITCH, :], x1)

    x_hi = x1.astype(BF16)
    x_lo = (x1 - x_hi.astype(F32)).astype(BF16)
    w2 = jnp.concatenate([w_hi, w_lo], axis=1)
    a = _dot(x_hi, w2)
    b = _dot(x_lo, w2)
    logits = (a[:, :LANES] + a[:, LANES:]) + (b[:, :LANES] + b[:, LANES:])

    lane = lax.broadcasted_iota(jnp.int32, logits.shape, 1).astype(F32)
    ninf = -jnp.inf
    gl = jnp.where(lane < N_GROUPS, logits, ninf)
    gmax = jnp.max(gl, axis=1, keepdims=True)
    g_w = 1.0 / jnp.sum(jnp.exp(gl - gmax), axis=1, keepdims=True)
    g_sel = jnp.min(jnp.where(gl == gmax, lane, float(LANES)), axis=1, keepdims=True)
    lo = N_GROUPS + EXPERTS_PER_GROUP * g_sel
    el = jnp.where((lane >= lo) & (lane < lo + EXPERTS_PER_GROUP), logits, ninf)
    e1 = jnp.max(el, axis=1, keepdims=True)
    i1 = jnp.min(jnp.where(el == e1, lane, float(LANES)), axis=1, keepdims=True)
    el2 = jnp.where(lane == i1, ninf, el)
    e2 = jnp.max(el2, axis=1, keepdims=True)
    i2 = jnp.min(jnp.where(el2 == e2, lane, float(LANES)), axis=1, keepdims=True)
    r = jnp.exp(e2 - e1)
    w1 = g_w / (1.0 + r)
    w2 = g_w * r / (1.0 + r)
    route = jnp.where(lane == 0, i1 - N_GROUPS,
                      jnp.where(lane == 1, i2 - N_GROUPS,
                                jnp.where(lane == 2, w1, jnp.where(lane == 3, w2, 0.0))))
    route_ref[rs, :] = route


def _outproj_call(o_diff, o_nsa, w_out, x2, g, b, wr, tm):
    n, d = x2.shape
    half = o_diff.shape[1]
    return pl.pallas_call(
        _outproj_kernel,
        out_shape=(jax.ShapeDtypeStruct((n * ROW_PITCH, LANES), F32), jax.ShapeDtypeStruct((n, LANES), F32)),
        grid=(n // tm,),
        in_specs=[
            pl.BlockSpec((tm, half), lambda i: (i, 0)),
            pl.BlockSpec((tm, half), lambda i: (i, 0)),
            pl.BlockSpec((2 * half, d), lambda i: (0, 0)),
            pl.BlockSpec((tm, d), lambda i: (i, 0)),
            pl.BlockSpec((1, d), lambda i: (0, 0)),
            pl.BlockSpec((1, d), lambda i: (0, 0)),
            pl.BlockSpec((d, LANES), lambda i: (0, 0)),
        ],
        out_specs=(pl.BlockSpec((tm * ROW_PITCH, LANES), lambda i: (i, 0)),
                   pl.BlockSpec((tm, LANES), lambda i: (i, 0))),
        compiler_params=_params(("parallel",)),
        name="outproj_ln1_router",
    )(o_diff, o_nsa, w_out, x2, g, b, wr)


MOE_TILE_ROWS = 256
COMBINE_TILE_TOKENS = 256


def _token_copy(src_hbm, dst, sem, src_row, dst_tok):
    return pltpu.make_async_copy(src_hbm.at[pl.ds(src_row, N_CHUNKS)],
                                 dst.at[pl.ds(dst_tok * ROW_PITCH, N_CHUNKS)], sem)


def _wait_tokens(src_hbm, dst, sem, count):
    pltpu.make_async_copy(src_hbm.at[pl.ds(0, count * N_CHUNKS)], dst.at[pl.ds(0, count * N_CHUNKS)], sem).wait()


def _moe_kernel(te_ref, nu_ref, rows_ref, rows_next_ref, x_hbm, wg_ref, wu_ref, wd_ref, o_ref,
                xbuf, sem, wg_sc, wu_sc, wd_sc, *, tm):
    i = pl.program_id(0)
    n_used = nu_ref[0]
    slot = lax.rem(i, 2)

    def issue(ids_ref, s):
        def body(r, carry):
            _token_copy(x_hbm, xbuf.at[s], sem.at[s], ids_ref[0, 0, r], r).start()
            return carry

        lax.fori_loop(0, tm, body, 0, unroll=8)

    @pl.when(i == 0)
    def _():
        issue(rows_ref, 0)

    @pl.when(i + 1 < n_used)
    def _():
        for r in range(tm):
            _token_copy(x_hbm, xbuf.at[1 - slot], sem.at[1 - slot], rows_next_ref[0, 0, r], r).start(priority=r % 2)

    prev = te_ref[jnp.maximum(i - 1, 0)]

    @pl.when((i == 0) | (te_ref[i] != prev))
    def _():
        wg_sc[...] = wg_ref[...].astype(BF16)
        wu_sc[...] = wu_ref[...].astype(BF16)
        wd_sc[...] = wd_ref[...].astype(BF16)

    @pl.when(i < n_used)
    def _():
        _wait_tokens(x_hbm, xbuf.at[slot], sem.at[slot], tm)
        xb = _load_chunk_rows(xbuf.at[slot], tm).astype(BF16)
        gate = _dot(xb, wg_sc[...])
        up = _dot(xb, wu_sc[...])
        h = (gate * jax.nn.sigmoid(gate) * up).astype(BF16)
        _store_chunk_rows(o_ref, _dot(h, wd_sc[...]))

    @pl.when(i >= n_used)
    def _():
        o_ref[...] = jnp.zeros(o_ref.shape, o_ref.dtype)


def _moe_call(tile_expert, n_used, rows, x1c, w_gate, w_up, w_down, tm):
    n_tiles = rows.shape[0]
    d, f = w_gate.shape[1], w_gate.shape[2]
    grid_spec = pltpu.PrefetchScalarGridSpec(
        num_scalar_prefetch=2,
        grid=(n_tiles,),
        in_specs=[
            pl.BlockSpec((1, 1, tm), lambda i, te, nu: (i, 0, 0), memory_space=pltpu.SMEM),
            pl.BlockSpec((1, 1, tm), lambda i, te, nu: (jnp.minimum(i + 1, n_tiles - 1), 0, 0),
                         memory_space=pltpu.SMEM),
            pl.BlockSpec(memory_space=pl.ANY),
            pl.BlockSpec((None, d, f), lambda i, te, nu: (te[i], 0, 0)),
            pl.BlockSpec((None, d, f), lambda i, te, nu: (te[i], 0, 0)),
            pl.BlockSpec((None, f, d), lambda i, te, nu: (te[i], 0, 0)),
        ],
        out_specs=pl.BlockSpec((tm * ROW_PITCH, LANES), lambda i, te, nu: (i, 0)),
        scratch_shapes=[
            pltpu.VMEM((2, tm * ROW_PITCH, LANES), F32),
            pltpu.SemaphoreType.DMA((2,)),
            pltpu.VMEM((d, f), BF16), pltpu.VMEM((d, f), BF16), pltpu.VMEM((f, d), BF16),
        ],
    )
    return pl.pallas_call(
        functools.partial(_moe_kernel, tm=tm),
        out_shape=jax.ShapeDtypeStruct((n_tiles * tm * ROW_PITCH, LANES), F32),
        grid_spec=grid_spec,
        compiler_params=_params(("arbitrary",)),
        name="moe_experts",
    )(tile_expert, n_used, rows, rows, x1c, w_gate, w_up, w_down)


def _combine_kernel(slots_ref, slots_next_ref, y_hbm, x1_ref, route_ref, g_ref, b_ref, o_ref, buf, sem, *, tm):
    i = pl.program_id(0)
    slot = lax.rem(i, 2)

    @pl.when(i == 0)
    def _():
        def body(r, carry):
            for k in range(2):
                _token_copy(y_hbm, buf.at[0, k], sem.at[0], slots_ref[0, 0, 2 * r + k], r).start()
            return carry

        lax.fori_loop(0, tm, body, 0, unroll=8)

    def tile(prefetch):
        if prefetch:
            for r in range(tm):
                for k in range(2):
                    _token_copy(y_hbm, buf.at[1 - slot, k], sem.at[1 - slot],
                                slots_next_ref[0, 0, 2 * r + k], r).start(priority=k)
        for k in range(2):
            _wait_tokens(y_hbm, buf.at[slot, k], sem.at[slot], tm)
        route = route_ref[...]
        y = (route[:, 2:3] * _load_chunk_rows(buf.at[slot, 0], tm)
             + route[:, 3:4] * _load_chunk_rows(buf.at[slot, 1], tm))
        x1 = _load_chunk_rows(x1_ref, tm)
        o_ref[...] = _layernorm(DN_ALPHA * x1 + y, g_ref[...], b_ref[...])

    last = pl.num_programs(0) - 1
    pl.when(i < last)(lambda: tile(True))
    pl.when(i == last)(lambda: tile(False))


def _combine_call(slots, yc, x1c, route, g, b, tm):
    n = route.shape[0]
    d = D_MODEL
    nt = n // tm
    return pl.pallas_call(
        functools.partial(_combine_kernel, tm=tm),
        out_shape=jax.ShapeDtypeStruct((n, d), F32),
        grid=(nt,),
        in_specs=[
            pl.BlockSpec((1, 1, 2 * tm), lambda i: (i, 0, 0), memory_space=pltpu.SMEM),
            pl.BlockSpec((1, 1, 2 * tm), lambda i: (jnp.minimum(i + 1, nt - 1), 0, 0), memory_space=pltpu.SMEM),
            pl.BlockSpec(memory_space=pl.ANY),
            pl.BlockSpec((tm * ROW_PITCH, LANES), lambda i: (i, 0)),
            pl.BlockSpec((tm, LANES), lambda i: (i, 0)),
            pl.BlockSpec((1, d), lambda i: (0, 0)),
            pl.BlockSpec((1, d), lambda i: (0, 0)),
        ],
        out_specs=pl.BlockSpec((tm, d), lambda i: (i, 0)),
        scratch_shapes=[pltpu.VMEM((2, 2, tm * ROW_PITCH, LANES), F32), pltpu.SemaphoreType.DMA((2,))],
        compiler_params=_params(("arbitrary",)),
        name="moe_combine_ln2",
    )(slots, slots, yc, x1c, route, g, b)


def _moe_plan(e_idx, tm, n_tiles):
    n = e_idx.shape[0]
    lanes = jnp.arange(N_EXPERTS, dtype=jnp.int32)[None, :]
    hit = ((e_idx[:, 0:1] == lanes) | (e_idx[:, 1:2] == lanes)).astype(jnp.int32)
    before = jnp.cumsum(hit, axis=0) - hit
    counts = jnp.sum(hit, axis=0)
    ptiles = (counts + tm - 1) // tm
    tile_end = jnp.cumsum(ptiles)
    base = (tile_end - ptiles) * tm
    slot = jnp.take_along_axis(before + base[None, :], e_idx, axis=1).reshape(-1)
    rows = jnp.zeros((n_tiles * tm,), jnp.int32).at[slot].set(jnp.arange(2 * n, dtype=jnp.int32) // 2)
    tile_ids = jnp.arange(n_tiles, dtype=jnp.int32)
    tile_expert = jnp.minimum(jnp.sum((tile_end[None, :] <= tile_ids[:, None]).astype(jnp.int32), axis=1),
                              N_EXPERTS - 1)
    return slot.astype(jnp.int32), rows, tile_expert, tile_end[-1:].astype(jnp.int32)


def _rope_tables(seq):
    inv_freq = ROPE_THETA ** (-np.arange(0, ROT_DIM, 2, dtype=np.float64) / ROT_DIM)
    ang = np.arange(seq, dtype=np.float64)[:, None] * inv_freq[None, :]
    cos, sin = np.cos(ang), np.sin(ang)
    pad1 = np.ones((seq, HEAD_DIM - ROT_DIM))
    pad0 = np.zeros((seq, HEAD_DIM - ROT_DIM))
    return (jnp.asarray(np.concatenate([cos, cos, pad1], axis=1), dtype=F32),
            jnp.asarray(np.concatenate([-sin, sin, pad0], axis=1), dtype=F32))


def _layer(x, w_in, diff_lambda, diff_subln_g, cmp_pos, cmp_w1, cmp_b1, cmp_w2, cmp_b2, w_out,
           ln1_g, ln1_b, router_group, router_expert, w_gate, w_up, w_down, ln2_g, ln2_b, lambda_init):
    batch, seq, d = x.shape
    n = batch * seq
    x2 = x.reshape(n, d)

    seg = lambda a, b: w_in[:, a:b]
    pad = lambda w: jnp.pad(w, ((0, 0), (0, LANES - w.shape[1])))
    gate_w = seg(5632, 5656)
    w_aug = jnp.concatenate(
        [seg(3072, 4096) * EXP2_SCALE, seg(0, 1024) * EXP2_SCALE, seg(1024, 2048), seg(2048, 3072), seg(4096, 5632),
         pad(gate_w[:, :12]), pad(gate_w[:, 12:])], axis=1).astype(BF16)
    flags = jnp.zeros((N_UNITS // 2,), jnp.int32).at[jnp.array(ROPE_BLOCKS)].set(1)
    cos_t, sin_t = _rope_tables(seq)
    proj = _proj_call(x2, w_aug, flags, cos_t, sin_t, seq, tm=min(PROJ_ROW_TILE, seq))

    nt = seq // CMP_STRIDE
    half_feat = CMP_STRIDE * HEAD_DIM
    w1cat = jnp.concatenate([cmp_w1[:, :half_feat], cmp_w1[:, half_feat:]], axis=2).astype(BF16)
    kvc = _compress_call(proj, w1cat, cmp_pos.reshape(2, 2, half_feat), cmp_b1[:, None, :],
                         cmp_w2.astype(BF16), cmp_b2[:, None, :], batch, seq)

    o_diff = _diff_call(proj, diff_lambda, diff_subln_g[None, :], batch, seq, min(DIFF_Q_TILE, seq), lambda_init)

    ns = seq // SLC_BLOCK
    ci = np.arange(nt)[None, :] * CMP_STRIDE
    sj = np.arange(ns)[:, None] * SLC_BLOCK
    ovt = jnp.asarray((ci < sj + SLC_BLOCK) & (ci + CMP_BLOCK > sj) & (np.arange(nt)[None, :] < nt - 1), dtype=BF16)
    o_nsa = _nsa_call(proj, cos_t, sin_t, kvc, ovt, batch, seq, tq=NSA_Q_TILE, tk=min(NSA_K_TILE, seq))

    wr = jnp.pad(jnp.concatenate([router_group, router_expert], axis=1),
                 ((0, 0), (0, LANES - N_GROUPS - N_EXPERTS)))
    x1, route = _outproj_call(o_diff, o_nsa, w_out.astype(BF16), x2, ln1_g[None, :], ln1_b[None, :], wr,
                              tm=OUT_ROW_TILE)

    tm = MOE_TILE_ROWS
    n_tiles = -(-2 * n // tm) + N_EXPERTS
    e_idx = route[:, 0:2].astype(jnp.int32)
    slot, rows, tile_expert, n_used = _moe_plan(e_idx, tm, n_tiles)
    ys = _moe_call(tile_expert, n_used, (rows * ROW_PITCH).reshape(n_tiles, 1, tm), x1, w_gate, w_up, w_down, tm)
    tc = COMBINE_TILE_TOKENS
    out = _combine_call((slot * ROW_PITCH).reshape(n // tc, 1, 2 * tc), ys, x1, route,
                        ln2_g[None, :], ln2_b[None, :], tc)
    return out.reshape(batch, seq, d)


def kernel(x, w_in, diff_lambda, diff_subln_g, cmp_pos, cmp_w1, cmp_b1, cmp_w2, cmp_b2, w_out, ln1_g, ln1_b,
           router_group, router_expert, expert_w_gate, expert_w_up, expert_w_down, ln2_g, ln2_b):
    for l in range(DEPTH):
        lambda_init = 0.8 - 0.6 * math.exp(-0.3 * l)
        x = _layer(x, w_in[l], diff_lambda[l], diff_subln_g[l], cmp_pos[l], cmp_w1[l], cmp_b1[l], cmp_w2[l],
                   cmp_b2[l], w_out[l], ln1_g[l], ln1_b[l], router_group[l], router_expert[l],
                   expert_w_gate[l], expert_w_up[l], expert_w_down[l], ln2_g[l], ln2_b[l], lambda_init)
    return x
```

```python
import functools
import math

import numpy as np
import jax
import jax.numpy as jnp
from jax import lax
from jax.experimental import pallas as pl
from jax.experimental.pallas import tpu as pltpu

F32 = jnp.float32
BF16 = jnp.bfloat16

D_MODEL = 2048
HEAD_DIM = 128
ROT_DIM = HEAD_DIM // 4
ROPE_THETA = 500000.0
NEG_INF = -1e30
BIG = 1e30
EPS = 1e-5

DIFF_HEADS = 4
DIFF_VDIM = 2 * HEAD_DIM

NSA_HEADS = 8
NSA_KV_GROUPS = 2
NSA_HPG = NSA_HEADS // NSA_KV_GROUPS
CMP_BLOCK = 32
CMP_STRIDE = 16
CMP_HIDDEN = 256
SLC_BLOCK = 64
SLC_TOPK = 16
WINDOW = 512

N_GROUPS = 4
EXPERTS_PER_GROUP = 8
N_EXPERTS = N_GROUPS * EXPERTS_PER_GROUP
EXPERT_HIDDEN = 512

DEPTH = 1
DN_ALPHA = (2.0 * DEPTH) ** 0.25

LANES = 128
VMEM_LIMIT = 56 * 1024 * 1024

PROJ_ROW_TILE = 2048
DIFF_Q_TILE = 1024
NSA_Q_TILE = 256
NSA_K_TILE = 512
OUT_ROW_TILE = 512

U_NQ, U_DQ, U_DK, U_DV = 0, 8, 16, 24
U_KC, U_VC, U_KS, U_VS, U_KW, U_VW, U_GATE = 32, 34, 36, 38, 40, 42, 44
N_UNITS = 46
ROPE_BLOCKS = tuple(range(U_DQ // 2, U_DV // 2)) + (U_KS // 2, U_KW // 2)


def _dot(a, b):
    return jnp.dot(a, b, preferred_element_type=F32)


def _dot_nt(a, b):
    return lax.dot_general(a, b, (((1,), (1,)), ((), ())), preferred_element_type=F32)


def _params(sem, vmem=VMEM_LIMIT):
    return pltpu.CompilerParams(dimension_semantics=sem, vmem_limit_bytes=vmem)


PROJ_ROW_CHUNK = 256


def _rope(a, c, s):
    lane = lax.broadcasted_iota(jnp.int32, a.shape, 1)
    half = ROT_DIM // 2
    partner = jnp.where(lane < half, pltpu.roll(a, LANES - half, 1), pltpu.roll(a, half, 1))
    return a * c + partner * s


def _proj_kernel(flags_ref, x_ref, w_ref, cos_ref, sin_ref, o_ref, xb_sc):
    j = pl.program_id(1)

    @pl.when(j == 0)
    def _():
        xb_sc[...] = x_ref[...].astype(BF16)

    def block(rotary):
        for r0 in range(0, xb_sc.shape[0], PROJ_ROW_CHUNK):
            rs = slice(r0, r0 + PROJ_ROW_CHUNK)
            acc = _dot(xb_sc[rs, :], w_ref[...])
            if rotary:
                for hh in range(2):
                    cs = slice(hh * LANES, (hh + 1) * LANES)
                    o_ref[rs, cs] = _rope(acc[:, cs], cos_ref[rs, :], sin_ref[rs, :]).astype(o_ref.dtype)
            else:
                o_ref[rs, :] = acc.astype(o_ref.dtype)

    pl.when(flags_ref[j] == 0)(lambda: block(False))
    pl.when(flags_ref[j] != 0)(lambda: block(True))


def _proj_call(xb, w_aug, flags, cos_t, sin_t, seq, tm):
    n, d = xb.shape
    nj = w_aug.shape[1] // 256
    tpb = seq // tm
    grid_spec = pltpu.PrefetchScalarGridSpec(
        num_scalar_prefetch=1,
        grid=(n // tm, nj),
        in_specs=[
            pl.BlockSpec((tm, d), lambda i, j, f: (i, 0)),
            pl.BlockSpec((d, 256), lambda i, j, f: (0, j)),
            pl.BlockSpec((tm, LANES), lambda i, j, f: (i % tpb, 0)),
            pl.BlockSpec((tm, LANES), lambda i, j, f: (i % tpb, 0)),
        ],
        out_specs=pl.BlockSpec((tm, 256), lambda i, j, f: (i, j)),
        scratch_shapes=[pltpu.VMEM((tm, d), BF16)],
    )
    return pl.pallas_call(
        _proj_kernel,
        out_shape=jax.ShapeDtypeStruct((n, w_aug.shape[1]), BF16),
        grid_spec=grid_spec,
        compiler_params=_params(("parallel", "arbitrary")),
        name="proj",
    )(flags, xb, w_aug, cos_t, sin_t)


def _compress_kernel(t_ref, w1_ref, pos_ref, b1_ref, w2_ref, b2_ref, o_ref, t_sc):
    nt = t_sc.shape[0] // CMP_STRIDE
    t_sc[...] = t_ref[...].astype(F32)
    r = jnp.concatenate([t_sc[pl.ds(k, nt, stride=CMP_STRIDE), :] for k in range(CMP_STRIDE)], axis=1).astype(BF16)
    ab = _dot(r, w1_ref[...])
    pos = pos_ref[...]
    pa = jnp.broadcast_to(pos[0:1], (8, pos.shape[1])).astype(BF16)
    pb = jnp.broadcast_to(pos[1:2], (8, pos.shape[1])).astype(BF16)
    const = _dot(pa, w1_ref[:, :CMP_HIDDEN])[0:1] + _dot(pb, w1_ref[:, CMP_HIDDEN:])[0:1]
    h = ab[:, :CMP_HIDDEN] + pltpu.roll(ab[:, CMP_HIDDEN:], nt - 1, 0) + const + b1_ref[...]
    h = jax.nn.gelu(h)
    o_ref[...] = (_dot(h.astype(BF16), w2_ref[...]) + b2_ref[...]).astype(o_ref.dtype)


def _compress_call(proj, w1cat, pos2, b1, w2, b2, b, seq):
    four = 2 * NSA_KV_GROUPS
    nt = seq // CMP_STRIDE
    k = CMP_STRIDE * HEAD_DIM
    return pl.pallas_call(
        _compress_kernel,
        out_shape=jax.ShapeDtypeStruct((b, four, nt, HEAD_DIM), BF16),
        grid=(b, four),
        in_specs=[
            pl.BlockSpec((seq, HEAD_DIM), lambda i, c: (i, U_KC + c)),
            pl.BlockSpec((None, k, 2 * CMP_HIDDEN), lambda i, c: (c // 2, 0, 0)),
            pl.BlockSpec((None, 2, k), lambda i, c: (c // 2, 0, 0)),
            pl.BlockSpec((None, 1, CMP_HIDDEN), lambda i, c: (c // 2, 0, 0)),
            pl.BlockSpec((None, CMP_HIDDEN, HEAD_DIM), lambda i, c: (c // 2, 0, 0)),
            pl.BlockSpec((None, 1, HEAD_DIM), lambda i, c: (c // 2, 0, 0)),
        ],
        out_specs=pl.BlockSpec((None, None, nt, HEAD_DIM), lambda i, c: (i, c, 0, 0)),
        scratch_shapes=[pltpu.VMEM((seq, HEAD_DIM), F32)],
        compiler_params=_params(("parallel", "parallel")),
        name="compress",
    )(proj, w1cat, pos2, b1, w2, b2)


EXP2_SCALE = HEAD_DIM ** -0.5 * math.log2(math.e)
ONES_ROWS = 16


def _transpose_into(src_ref, dst_ref):
    def body(c, carry):
        off = pl.multiple_of(c * LANES, LANES)
        dst_ref[:, pl.ds(off, LANES)] = src_ref[pl.ds(off, LANES), :].astype(F32).T.astype(dst_ref.dtype)
        return carry

    n_tiles = src_ref.shape[0] // LANES
    lax.fori_loop(0, n_tiles, body, 0, unroll=math.gcd(n_tiles, 8))


def _diff_kernel(dl_ref, q_ref, k_ref, v_ref, g_ref, tri_ref, o_ref, vt_sc, acc_sc, sa_sc, sb_sc, *, tq, lambda_init):
    qi = pl.program_id(2)

    @pl.when(qi == 0)
    def _():
        for c in range(2):
            _transpose_into(v_ref.at[:, c * LANES:(c + 1) * LANES], vt_sc.at[c * LANES:(c + 1) * LANES, :])

    acc_sc[...] = jnp.zeros(acc_sc.shape, F32)
    q = q_ref[...]
    tk = tq // 2

    def scores(j, dst, lo=0):
        kt = k_ref[pl.ds(pl.multiple_of(j * tk, tk), tk), :]
        for c in range(2):
            dst[c, :, lo:] = _dot_nt(kt[:, c * HEAD_DIM:(c + 1) * HEAD_DIM], q[lo:, c * HEAD_DIM:(c + 1) * HEAD_DIM])

    def absorb(src, j, masked, carry, lo=0):
        off = pl.multiple_of(j * tk, tk)
        vt = vt_sc[:, pl.ds(off, tk)]
        out = []
        for c in range(2):
            m_old, l_old = carry[2 * c][:, lo:], carry[2 * c + 1][:, lo:]
            s = src[c, :, lo:]
            if masked and lo:
                s = s + tri_ref[...]
            elif masked:
                s = jnp.concatenate([s[:, :tk] + tri_ref[...], s[:, tk:]], axis=1)
            m_new = jnp.maximum(m_old, jnp.max(s, axis=0, keepdims=True))
            p = jnp.exp2(s - m_new)
            alpha = jnp.exp2(m_old - m_new)
            l_new = alpha * l_old + jnp.sum(p, axis=0, keepdims=True)
            acc_sc[c, :, lo:] = alpha * acc_sc[c, :, lo:] + _dot(vt, p.astype(BF16))
            if lo:
                m_new = jnp.concatenate([carry[2 * c][:, :lo], m_new], axis=1)
                l_new = jnp.concatenate([carry[2 * c + 1][:, :lo], l_new], axis=1)
            out += [m_new, l_new]
        return tuple(out)

    def pair(p, carry):
        j = 2 * p
        scores(j + 1, sb_sc)
        carry = absorb(sa_sc, j, False, carry)
        scores(j + 2, sa_sc)
        return absorb(sb_sc, j + 1, False, carry)

    init = (jnp.full((1, tq), NEG_INF, F32), jnp.zeros((1, tq), F32)) * 2
    scores(0, sa_sc)
    carry = lax.fori_loop(0, qi, pair, init)
    j_tail = 2 * qi
    scores(j_tail + 1, sb_sc, lo=tk)
    carry = absorb(sa_sc, j_tail, True, carry)
    _, l0, _, l1 = absorb(sb_sc, j_tail + 1, True, carry, lo=tk)

    dl = dl_ref[...]
    lam = (jnp.exp(jnp.sum(dl[0:1] * dl[1:2], axis=1, keepdims=True))
           - jnp.exp(jnp.sum(dl[2:3] * dl[3:4], axis=1, keepdims=True)) + lambda_init)
    o = acc_sc[0] * (1.0 / l0) - lam * (acc_sc[1] * (1.0 / l1))
    o = o * (lax.rsqrt(jnp.mean(o * o, axis=0, keepdims=True) + EPS) * (1.0 - lambda_init))
    for c in range(DIFF_VDIM // LANES):
        for r in range(tq // LANES):
            blk = o[c * LANES:(c + 1) * LANES, r * LANES:(r + 1) * LANES].T
            o_ref[r * LANES:(r + 1) * LANES, c * LANES:(c + 1) * LANES] = (
                blk * g_ref[:, c * LANES:(c + 1) * LANES]).astype(o_ref.dtype)


def _diff_call(proj, dl, g, batch, seq, tq, lambda_init):
    nq = seq // tq
    kern = functools.partial(_diff_kernel, tq=tq, lambda_init=lambda_init)
    tk = tq // 2
    tri = jnp.asarray(np.where(np.arange(tk)[:, None] <= np.arange(tk)[None, :], 0.0, NEG_INF), dtype=F32)
    return pl.pallas_call(
        kern,
        out_shape=jax.ShapeDtypeStruct((batch * seq, DIFF_HEADS * DIFF_VDIM), BF16),
        grid=(batch, DIFF_HEADS, nq),
        in_specs=[
            pl.BlockSpec((4, HEAD_DIM), lambda b, h, i: (0, 0)),
            pl.BlockSpec((tq, 256), lambda b, h, i: (b * nq + i, U_DQ // 2 + h)),
            pl.BlockSpec((seq, 256), lambda b, h, i: (b, U_DK // 2 + h)),
            pl.BlockSpec((seq, 256), lambda b, h, i: (b, U_DV // 2 + h)),
            pl.BlockSpec((1, DIFF_VDIM), lambda b, h, i: (0, 0)),
            pl.BlockSpec((tk, tk), lambda b, h, i: (0, 0)),
        ],
        out_specs=pl.BlockSpec((tq, DIFF_VDIM), lambda b, h, i: (b * nq + i, h)),
        scratch_shapes=[
            pltpu.VMEM((DIFF_VDIM, seq), BF16),
            pltpu.VMEM((2, DIFF_VDIM, tq), F32),
            pltpu.VMEM((2, tq // 2, tq), F32),
            pltpu.VMEM((2, tq // 2, tq), F32),
        ],
        compiler_params=_params(("arbitrary", "arbitrary", "arbitrary")),
        name="diff_attn",
    )(dl, proj, proj, proj, g, tri)


def _stack_heads(x):
    return jnp.concatenate([x[:, h * HEAD_DIM:(h + 1) * HEAD_DIM] for h in range(NSA_HPG)], axis=0)


def _heads(x):
    return jnp.concatenate([x] * NSA_HPG, axis=1)


def _mask_tables(seq, tq, tk):
    nt = seq // CMP_STRIDE
    slab = min(WINDOW + tq, seq)
    q = np.arange(tq)[None, :]
    k = np.arange(slab)[:, None]
    win = []
    for i in range(WINDOW // tq + 1):
        newest = q + (i * tq if i < WINDOW // tq else WINDOW)
        win.append((k <= newest) & (k > newest - WINDOW))
    u = np.arange(2 * nt)[:, None] - nt
    cmp_ok = u <= ((q - (CMP_BLOCK - 1)) >> int(math.log2(CMP_STRIDE)))
    kk = np.arange(tk)[:, None]
    causal = [kk <= d * tq + q for d in range(tk // tq)] + [np.ones((tk, tq), bool)]
    to_bias = lambda m: jnp.asarray(np.where(np.asarray(m), 0.0, NEG_INF), dtype=F32)
    return to_bias(np.stack(win)), to_bias(cmp_ok), to_bias(np.stack(causal))


def _nsa_kernel(q_ref, cos_ref, sin_ref, gate_ref, kc_ref, vc_ref, ks_ref, vs_ref, kw_ref, vw_ref, ovt_ref,
                wtab_ref, ptab_ref, ctab_ref, o_ref, vct_sc, vst_sc, vwt_sc, bias_sc, acc_sc, sa_sc, sb_sc, m_sc, *, tq, tk, seq, top_k):
    qi = pl.program_id(2)
    q0 = qi * tq
    rows = NSA_HPG * tq
    nt = kc_ref.shape[0]
    ns = ovt_ref.shape[0]
    nb = tk // SLC_BLOCK

    @pl.when(qi == 0)
    def _():
        _transpose_into(vc_ref, vct_sc)
        for v_ref, vt_sc in ((vs_ref, vst_sc), (vw_ref, vwt_sc)):
            _transpose_into(v_ref, vt_sc.at[:HEAD_DIM, :])
            vt_sc[HEAD_DIM:, :] = jnp.ones((ONES_ROWS, seq), vt_sc.dtype)

    q = q_ref[...]
    qs = _stack_heads(q)
    qr = jnp.concatenate(
        [_rope(q[:, h * HEAD_DIM:(h + 1) * HEAD_DIM].astype(F32), cos_ref[...], sin_ref[...]).astype(BF16)
         for h in range(NSA_HPG)], axis=0)
    qpos = q0 + (lax.broadcasted_iota(jnp.int32, (1, rows), 1) & (tq - 1))

    def sel_scores(j, dst):
        dst[...] = _dot_nt(ks_ref[pl.ds(pl.multiple_of(j * tk, tk), tk), :], qr)

    sel_scores(0, sa_sc)

    slab = min(WINDOW + tq, seq)
    start = pl.multiple_of(jnp.maximum(q0 - WINDOW, 0), tq)
    s_w = _dot_nt(kw_ref[pl.ds(start, slab), :], qr)
    s_w = s_w + _heads(wtab_ref[jnp.minimum(qi, WINDOW // tq)])
    p_w = jnp.exp2(s_w - jnp.max(s_w, axis=0, keepdims=True))
    o_w = _dot(vwt_sc[:, pl.ds(start, slab)], p_w.astype(BF16))
    o_w = o_w[:HEAD_DIM] * (1.0 / o_w[HEAD_DIM:HEAD_DIM + 1])

    s_c = _dot_nt(kc_ref[...], qs)
    c0 = pl.multiple_of(nt - q0 // CMP_STRIDE, CMP_STRIDE)
    s_c = s_c + _heads(ptab_ref[pl.ds(c0, nt), :])
    m_c = jnp.max(s_c, axis=0, keepdims=True)
    e_c = jnp.exp2(s_c - m_c)
    l_c = jnp.sum(e_c, axis=0, keepdims=True)
    p_c = e_c * jnp.where(qpos >= CMP_BLOCK - 1, 1.0 / l_c, 0.0)
    o_c = _dot(vct_sc[...], p_c.astype(BF16))

    p_sum = p_c[:, 0:tq]
    for h in range(1, NSA_HPG):
        p_sum = p_sum + p_c[:, h * tq:(h + 1) * tq]
    p_hi = p_sum.astype(BF16)
    p_lo = (p_sum - p_hi.astype(F32)).astype(BF16)
    imp = _dot(ovt_ref[...], p_hi) + _dot(ovt_ref[...], p_lo)

    blk = lax.broadcasted_iota(jnp.int32, (ns, tq), 0)
    qpos_l = q0 + lax.broadcasted_iota(jnp.int32, (ns, tq), 1)
    cur = lax.shift_right_logical(qpos_l, int(math.log2(SLC_BLOCK)))
    valid_s = blk <= cur
    forced = (blk == 0) | (blk == cur) | (blk == cur - 1)
    work = jnp.where(forced, -jnp.inf, jnp.where(valid_s, imp, NEG_INF))
    blk_f = blk.astype(F32)
    for _ in range(top_k - 3):
        mx = jnp.max(work, axis=0, keepdims=True)
        idx = jnp.min(jnp.where(work == mx, blk_f, float(ns)), axis=0, keepdims=True)
        work = jnp.where(blk_f == idx, -jnp.inf, work)
    bias_sc[...] = jnp.where(work == -jnp.inf, 0.0, NEG_INF)

    acc_sc[...] = jnp.zeros(acc_sc.shape, F32)

    def sel_absorb(src, j, causal, m_old):
        off = pl.multiple_of(j * tk, tk)
        s = src[...]
        if causal:
            s = s + _heads(ctab_ref[jnp.clip((q0 - off) // tq, 0, tk // tq)])
        bias = bias_sc[pl.ds(pl.multiple_of(j * nb, nb), nb), :]
        bias = jnp.concatenate([bias] * NSA_HPG, axis=1)
        s3 = s.reshape(nb, SLC_BLOCK, rows) + bias[:, None, :]
        m_new = jnp.maximum(m_old, jnp.max(jnp.max(s3, axis=0), axis=0, keepdims=True))
        p3 = jnp.exp2(s3 - m_new)
        alpha = jnp.exp2(m_old - m_new)
        p = p3.reshape(tk, rows).astype(BF16)
        acc_sc[...] = alpha * acc_sc[...] + _dot(vst_sc[:, pl.ds(off, tk)], p)
        return m_new

    def sel_pair(p, carry):
        j = 2 * p
        sel_scores(j + 1, sb_sc)
        carry = sel_absorb(sa_sc, j, False, carry)
        sel_scores(j + 2, sa_sc)
        return sel_absorb(sb_sc, j + 1, False, carry)

    n_pairs = (q0 // tk + 2) // 2
    carry = lax.fori_loop(0, n_pairs - 1, sel_pair, jnp.full((1, rows), NEG_INF, F32))
    j_tail = 2 * (n_pairs - 1)
    m_sc[...] = sel_absorb(sa_sc, j_tail, True, carry)

    @pl.when(j_tail + 1 <= q0 // tk)
    def _():
        sel_scores(j_tail + 1, sb_sc)
        sel_absorb(sb_sc, j_tail + 1, True, m_sc[...])

    o_s = acc_sc[:HEAD_DIM, :] * (1.0 / acc_sc[HEAD_DIM:HEAD_DIM + 1, :])

    gates = jax.nn.sigmoid(gate_ref[...].astype(F32))
    gates = jnp.concatenate([gates[r * LANES:(r + 1) * LANES].T for r in range(tq // LANES)],
                            axis=1)
    for h in range(NSA_HPG):
        sl = slice(h * tq, (h + 1) * tq)
        o = (gates[3 * h:3 * h + 1] * o_c[:, sl] + gates[3 * h + 1:3 * h + 2] * o_s[:, sl]
             + gates[3 * h + 2:3 * h + 3] * o_w[:, sl])
        for r in range(tq // LANES):
            o_ref[r * LANES:(r + 1) * LANES, h * HEAD_DIM:(h + 1) * HEAD_DIM] = (
                o[:, r * LANES:(r + 1) * LANES].T.astype(o_ref.dtype))


def _nsa_call(proj, cos_t, sin_t, kvc, ovt, batch, seq, tq, tk):
    nq = seq // tq
    assert WINDOW % tq == 0 and tk % tq == 0 and tq % CMP_STRIDE == 0
    wtab, ptab, ctab = _mask_tables(seq, tq, tk)
    nt = kvc.shape[2]
    ns = ovt.shape[0]
    width = NSA_HPG * HEAD_DIM
    kern = functools.partial(_nsa_kernel, tq=tq, tk=tk, seq=seq, top_k=min(SLC_TOPK, ns))

    def col(unit):
        return pl.BlockSpec((seq, HEAD_DIM), lambda b, g, i: (b, unit + g))

    return pl.pallas_call(
        kern,
        out_shape=jax.ShapeDtypeStruct((batch * seq, NSA_HEADS * HEAD_DIM), BF16),
        grid=(batch, NSA_KV_GROUPS, nq),
        in_specs=[
            pl.BlockSpec((tq, width), lambda b, g, i: (b * nq + i, U_NQ // 4 + g)),
            pl.BlockSpec((tq, LANES), lambda b, g, i: (i, 0)),
            pl.BlockSpec((tq, LANES), lambda b, g, i: (i, 0)),
            pl.BlockSpec((tq, LANES), lambda b, g, i: (b * nq + i, U_GATE + g)),
            pl.BlockSpec((None, None, nt, HEAD_DIM), lambda b, g, i: (b, g, 0, 0)),
            pl.BlockSpec((None, None, nt, HEAD_DIM), lambda b, g, i: (b, 2 + g, 0, 0)),
            col(U_KS), col(U_VS), col(U_KW), col(U_VW),
            pl.BlockSpec((ns, nt), lambda b, g, i: (0, 0)),
            pl.BlockSpec(wtab.shape, lambda b, g, i: (0, 0, 0)),
            pl.BlockSpec(ptab.shape, lambda b, g, i: (0, 0)),
            pl.BlockSpec(ctab.shape, lambda b, g, i: (0, 0, 0)),
        ],
        out_specs=pl.BlockSpec((tq, width), lambda b, g, i: (b * nq + i, g)),
        scratch_shapes=[
            pltpu.VMEM((HEAD_DIM, nt), BF16),
            pltpu.VMEM((HEAD_DIM + ONES_ROWS, seq), BF16),
            pltpu.VMEM((HEAD_DIM + ONES_ROWS, seq), BF16),
            pltpu.VMEM((ns, tq), F32),
            pltpu.VMEM((HEAD_DIM + ONES_ROWS, NSA_HPG * tq), F32),
            pltpu.VMEM((tk, NSA_HPG * tq), F32),
            pltpu.VMEM((tk, NSA_HPG * tq), F32),
            pltpu.VMEM((1, NSA_HPG * tq), F32),
        ],
        compiler_params=_params(("arbitrary", "arbitrary", "arbitrary")),
        name="nsa_attn",
    )(proj, cos_t, sin_t, proj, kvc, kvc, proj, proj, proj, proj, ovt, wtab, ptab, ctab)


def _layernorm(y, g, b):
    mu = jnp.mean(y, axis=1, keepdims=True)
    yc = y - mu
    var = jnp.mean(yc * yc, axis=1, keepdims=True)
    return yc * lax.rsqrt(var + EPS) * g + b


N_CHUNKS = D_MODEL // LANES
ROW_PITCH = N_CHUNKS + 1


def _store_chunk_rows(ref, val):
    tm = val.shape[0]
    for k in range(N_CHUNKS):
        ref[pl.ds(k, tm, stride=ROW_PITCH), :] = val[:, k * LANES:(k + 1) * LANES]
    for k in range(N_CHUNKS, ROW_PITCH):
        ref[pl.ds(k, tm, stride=ROW_PITCH), :] = jnp.zeros((tm, LANES), ref.dtype)


def _load_chunk_rows(ref, tm):
    return jnp.concatenate([ref[pl.ds(k, tm, stride=ROW_PITCH), :] for k in range(N_CHUNKS)], axis=1)


OUT_ROW_CHUNK = 256


def _outproj_kernel(od_ref, on_ref, w_ref, x_ref, g_ref, b_ref, wr_ref, x1_ref, route_ref):
    wr = wr_ref[...]
    w_hi = wr.astype(BF16)
    w_lo = (wr - w_hi.astype(F32)).astype(BF16)
    for r0 in range(0, x_ref.shape[0], OUT_ROW_CHUNK):
        _outproj_rows(od_ref, on_ref, w_ref, x_ref, g_ref, b_ref, w_hi, w_lo, x1_ref, route_ref,
                      slice(r0, r0 + OUT_ROW_CHUNK))


def _outproj_rows(od_ref, on_ref, w_ref, x_ref, g_ref, b_ref, w_hi, w_lo, x1_ref, route_ref, rs):
    half = od_ref.shape[1]
    h = _dot(od_ref[rs, :], w_ref[:half, :]) + _dot(on_ref[rs, :], w_ref[half:, :])
    x1 = _layernorm(DN_ALPHA * x_ref[rs, :] + h, g_ref[...], b_ref[...])
    _store_chunk_rows(x1_ref.at[rs.start * ROW_PITCH:rs.stop * ROW_PITCH, :], x1)

    x_hi = x1.astype(BF16)
    x_lo = (x1 - x_hi.astype(F32)).astype(BF16)
    w2 = jnp.concatenate([w_hi, w_lo], axis=1)
    a = _dot(x_hi, w2)
    b = _dot(x_lo, w2)
    logits = (a[:, :LANES] + a[:, LANES:]) + (b[:, :LANES] + b[:, LANES:])

    lane = lax.broadcasted_iota(jnp.int32, logits.shape, 1).astype(F32)
    ninf = -jnp.inf
    gl = jnp.where(lane < N_GROUPS, logits, ninf)
    gmax = jnp.max(gl, axis=1, keepdims=True)
    g_w = 1.0 / jnp.sum(jnp.exp(gl - gmax), axis=1, keepdims=True)
    g_sel = jnp.min(jnp.where(gl == gmax, lane, float(LANES)), axis=1, keepdims=True)
    lo = N_GROUPS + EXPERTS_PER_GROUP * g_sel
    el = jnp.where((lane >= lo) & (lane < lo + EXPERTS_PER_GROUP), logits, ninf)
    e1 = jnp.max(el, axis=1, keepdims=True)
    i1 = jnp.min(jnp.where(el == e1, lane, float(LANES)), axis=1, keepdims=True)
    el2 = jnp.where(lane == i1, ninf, el)
    e2 = jnp.max(el2, axis=1, keepdims=True)
    i2 = jnp.min(jnp.where(el2 == e2, lane, float(LANES)), axis=1, keepdims=True)
    r = jnp.exp(e2 - e1)
    w1 = g_w / (1.0 + r)
    w2 = g_w * r / (1.0 + r)
    route = jnp.where(lane == 0, i1 - N_GROUPS,
                      jnp.where(lane == 1, i2 - N_GROUPS,
                                jnp.where(lane == 2, w1, jnp.where(lane == 3, w2, 0.0))))
    route_ref[rs, :] = route


def _outproj_call(o_diff, o_nsa, w_out, x2, g, b, wr, tm):
    n, d = x2.shape
    half = o_diff.shape[1]
    return pl.pallas_call(
        _outproj_kernel,
        out_shape=(jax.ShapeDtypeStruct((n * ROW_PITCH, LANES), F32), jax.ShapeDtypeStruct((n, LANES), F32)),
        grid=(n // tm,),
        in_specs=[
            pl.BlockSpec((tm, half), lambda i: (i, 0)),
            pl.BlockSpec((tm, half), lambda i: (i, 0)),
            pl.BlockSpec((2 * half, d), lambda i: (0, 0)),
            pl.BlockSpec((tm, d), lambda i: (i, 0)),
            pl.BlockSpec((1, d), lambda i: (0, 0)),
            pl.BlockSpec((1, d), lambda i: (0, 0)),
            pl.BlockSpec((d, LANES), lambda i: (0, 0)),
        ],
        out_specs=(pl.BlockSpec((tm * ROW_PITCH, LANES), lambda i: (i, 0)),
                   pl.BlockSpec((tm, LANES), lambda i: (i, 0))),
        compiler_params=_params(("parallel",)),
        name="outproj_ln1_router",
    )(o_diff, o_nsa, w_out, x2, g, b, wr)


MOE_TILE_ROWS = 256
COMBINE_TILE_TOKENS = 256


def _token_copy(src_hbm, dst, sem, src_row, dst_tok):
    return pltpu.make_async_copy(src_hbm.at[pl.ds(src_row, N_CHUNKS)],
                                 dst.at[pl.ds(dst_tok * ROW_PITCH, N_CHUNKS)], sem)


def _wait_tokens(src_hbm, dst, sem, count):
    pltpu.make_async_copy(src_hbm.at[pl.ds(0, count * N_CHUNKS)], dst.at[pl.ds(0, count * N_CHUNKS)], sem).wait()


def _moe_kernel(te_ref, nu_ref, rows_ref, rows_next_ref, x_hbm, wg_ref, wu_ref, wd_ref, o_ref,
                xbuf, sem, wg_sc, wu_sc, wd_sc, *, tm):
    i = pl.program_id(0)
    n_used = nu_ref[0]
    slot = lax.rem(i, 2)

    def issue(ids_ref, s):
        def body(r, carry):
            _token_copy(x_hbm, xbuf.at[s], sem.at[s], ids_ref[0, 0, r], r).start()
            return carry

        lax.fori_loop(0, tm, body, 0, unroll=8)

    @pl.when(i == 0)
    def _():
        issue(rows_ref, 0)

    @pl.when(i + 1 < n_used)
    def _():
        for r in range(tm):
            _token_copy(x_hbm, xbuf.at[1 - slot], sem.at[1 - slot], rows_next_ref[0, 0, r], r).start(priority=r % 2)

    prev = te_ref[jnp.maximum(i - 1, 0)]

    @pl.when((i == 0) | (te_ref[i] != prev))
    def _():
        wg_sc[...] = wg_ref[...].astype(BF16)
        wu_sc[...] = wu_ref[...].astype(BF16)
        wd_sc[...] = wd_ref[...].astype(BF16)

    @pl.when(i < n_used)
    def _():
        _wait_tokens(x_hbm, xbuf.at[slot], sem.at[slot], tm)
        xb = _load_chunk_rows(xbuf.at[slot], tm).astype(BF16)
        gate = _dot(xb, wg_sc[...])
        up = _dot(xb, wu_sc[...])
        h = (gate * jax.nn.sigmoid(gate) * up).astype(BF16)
        _store_chunk_rows(o_ref, _dot(h, wd_sc[...]))

    @pl.when(i >= n_used)
    def _():
        o_ref[...] = jnp.zeros(o_ref.shape, o_ref.dtype)


def _moe_call(tile_expert, n_used, rows, x1c, w_gate, w_up, w_down, tm):
    n_tiles = rows.shape[0]
    d, f = w_gate.shape[1], w_gate.shape[2]
    grid_spec = pltpu.PrefetchScalarGridSpec(
        num_scalar_prefetch=2,
        grid=(n_tiles,),
        in_specs=[
            pl.BlockSpec((1, 1, tm), lambda i, te, nu: (i, 0, 0), memory_space=pltpu.SMEM),
            pl.BlockSpec((1, 1, tm), lambda i, te, nu: (jnp.minimum(i + 1, n_tiles - 1), 0, 0),
                         memory_space=pltpu.SMEM),
            pl.BlockSpec(memory_space=pl.ANY),
            pl.BlockSpec((None, d, f), lambda i, te, nu: (te[i], 0, 0)),
            pl.BlockSpec((None, d, f), lambda i, te, nu: (te[i], 0, 0)),
            pl.BlockSpec((None, f, d), lambda i, te, nu: (te[i], 0, 0)),
        ],
        out_specs=pl.BlockSpec((tm * ROW_PITCH, LANES), lambda i, te, nu: (i, 0)),
        scratch_shapes=[
            pltpu.VMEM((2, tm * ROW_PITCH, LANES), F32),
            pltpu.SemaphoreType.DMA((2,)),
            pltpu.VMEM((d, f), BF16), pltpu.VMEM((d, f), BF16), pltpu.VMEM((f, d), BF16),
        ],
    )
    return pl.pallas_call(
        functools.partial(_moe_kernel, tm=tm),
        out_shape=jax.ShapeDtypeStruct((n_tiles * tm * ROW_PITCH, LANES), F32),
        grid_spec=grid_spec,
        compiler_params=_params(("arbitrary",)),
        name="moe_experts",
    )(tile_expert, n_used, rows, rows, x1c, w_gate, w_up, w_down)


def _combine_kernel(slots_ref, slots_next_ref, y_hbm, x1_ref, route_ref, g_ref, b_ref, o_ref, buf, sem, *, tm):
    i = pl.program_id(0)
    slot = lax.rem(i, 2)

    @pl.when(i == 0)
    def _():
        def body(r, carry):
            for k in range(2):
                _token_copy(y_hbm, buf.at[0, k], sem.at[0], slots_ref[0, 0, 2 * r + k], r).start()
            return carry

        lax.fori_loop(0, tm, body, 0, unroll=8)

    def tile(prefetch):
        if prefetch:
            for r in range(tm):
                for k in range(2):
                    _token_copy(y_hbm, buf.at[1 - slot, k], sem.at[1 - slot],
                                slots_next_ref[0, 0, 2 * r + k], r).start(priority=k)
        for k in range(2):
            _wait_tokens(y_hbm, buf.at[slot, k], sem.at[slot], tm)
        route = route_ref[...]
        y = (route[:, 2:3] * _load_chunk_rows(buf.at[slot, 0], tm)
             + route[:, 3:4] * _load_chunk_rows(buf.at[slot, 1], tm))
        x1 = _load_chunk_rows(x1_ref, tm)
        o_ref[...] = _layernorm(DN_ALPHA * x1 + y, g_ref[...], b_ref[...])

    last = pl.num_programs(0) - 1
    pl.when(i < last)(lambda: tile(True))
    pl.when(i == last)(lambda: tile(False))


def _combine_call(slots, yc, x1c, route, g, b, tm):
    n = route.shape[0]
    d = D_MODEL
    nt = n // tm
    return pl.pallas_call(
        functools.partial(_combine_kernel, tm=tm),
        out_shape=jax.ShapeDtypeStruct((n, d), F32),
        grid=(nt,),
        in_specs=[
            pl.BlockSpec((1, 1, 2 * tm), lambda i: (i, 0, 0), memory_space=pltpu.SMEM),
            pl.BlockSpec((1, 1, 2 * tm), lambda i: (jnp.minimum(i + 1, nt - 1), 0, 0), memory_space=pltpu.SMEM),
            pl.BlockSpec(memory_space=pl.ANY),
            pl.BlockSpec((tm * ROW_PITCH, LANES), lambda i: (i, 0)),
            pl.BlockSpec((tm, LANES), lambda i: (i, 0)),
            pl.BlockSpec((1, d), lambda i: (0, 0)),
            pl.BlockSpec((1, d), lambda i: (0, 0)),
        ],
        out_specs=pl.BlockSpec((tm, d), lambda i: (i, 0)),
        scratch_shapes=[pltpu.VMEM((2, 2, tm * ROW_PITCH, LANES), F32), pltpu.SemaphoreType.DMA((2,))],
        compiler_params=_params(("arbitrary",)),
        name="moe_combine_ln2",
    )(slots, slots, yc, x1c, route, g, b)


def _moe_plan(e_idx, tm, n_tiles):
    n = e_idx.shape[0]
    lanes = jnp.arange(N_EXPERTS, dtype=jnp.int32)[None, :]
    hit = ((e_idx[:, 0:1] == lanes) | (e_idx[:, 1:2] == lanes)).astype(jnp.int32)
    before = jnp.cumsum(hit, axis=0) - hit
    counts = jnp.sum(hit, axis=0)
    ptiles = (counts + tm - 1) // tm
    tile_end = jnp.cumsum(ptiles)
    base = (tile_end - ptiles) * tm
    slot = jnp.take_along_axis(before + base[None, :], e_idx, axis=1).reshape(-1)
    rows = jnp.zeros((n_tiles * tm,), jnp.int32).at[slot].set(jnp.arange(2 * n, dtype=jnp.int32) // 2)
    tile_ids = jnp.arange(n_tiles, dtype=jnp.int32)
    tile_expert = jnp.minimum(jnp.sum((tile_end[None, :] <= tile_ids[:, None]).astype(jnp.int32), axis=1),
                              N_EXPERTS - 1)
    return slot.astype(jnp.int32), rows, tile_expert, tile_end[-1:].astype(jnp.int32)


def _rope_tables(seq):
    inv_freq = ROPE_THETA ** (-np.arange(0, ROT_DIM, 2, dtype=np.float64) / ROT_DIM)
    ang = np.arange(seq, dtype=np.float64)[:, None] * inv_freq[None, :]
    cos, sin = np.cos(ang), np.sin(ang)
    pad1 = np.ones((seq, HEAD_DIM - ROT_DIM))
    pad0 = np.zeros((seq, HEAD_DIM - ROT_DIM))
    return (jnp.asarray(np.concatenate([cos, cos, pad1], axis=1), dtype=F32),
            jnp.asarray(np.concatenate([-sin, sin, pad0], axis=1), dtype=F32))


def _layer(x, w_in, diff_lambda, diff_subln_g, cmp_pos, cmp_w1, cmp_b1, cmp_w2, cmp_b2, w_out,
           ln1_g, ln1_b, router_group, router_expert, w_gate, w_up, w_down, ln2_g, ln2_b, lambda_init):
    batch, seq, d = x.shape
    n = batch * seq
    x2 = x.reshape(n, d)

    seg = lambda a, b: w_in[:, a:b]
    pad = lambda w: jnp.pad(w, ((0, 0), (0, LANES - w.shape[1])))
    gate_w = seg(5632, 5656)
    w_aug = jnp.concatenate(
        [seg(3072, 4096) * EXP2_SCALE, seg(0, 1024) * EXP2_SCALE, seg(1024, 2048), seg(2048, 3072), seg(4096, 5632),
         pad(gate_w[:, :12]), pad(gate_w[:, 12:])], axis=1).astype(BF16)
    flags = jnp.zeros((N_UNITS // 2,), jnp.int32).at[jnp.array(ROPE_BLOCKS)].set(1)
    cos_t, sin_t = _rope_tables(seq)
    proj = _proj_call(x2, w_aug, flags, cos_t, sin_t, seq, tm=min(PROJ_ROW_TILE, seq))

    nt = seq // CMP_STRIDE
    half_feat = CMP_STRIDE * HEAD_DIM
    w1cat = jnp.concatenate([cmp_w1[:, :half_feat], cmp_w1[:, half_feat:]], axis=2).astype(BF16)
    kvc = _compress_call(proj, w1cat, cmp_pos.reshape(2, 2, half_feat), cmp_b1[:, None, :],
                         cmp_w2.astype(BF16), cmp_b2[:, None, :], batch, seq)

    o_diff = _diff_call(proj, diff_lambda, diff_subln_g[None, :], batch, seq, min(DIFF_Q_TILE, seq), lambda_init)

    ns = seq // SLC_BLOCK
    ci = np.arange(nt)[None, :] * CMP_STRIDE
    sj = np.arange(ns)[:, None] * SLC_BLOCK
    ovt = jnp.asarray((ci < sj + SLC_BLOCK) & (ci + CMP_BLOCK > sj) & (np.arange(nt)[None, :] < nt - 1), dtype=BF16)
    o_nsa = _nsa_call(proj, cos_t, sin_t, kvc, ovt, batch, seq, tq=NSA_Q_TILE, tk=min(NSA_K_TILE, seq))

    wr = jnp.pad(jnp.concatenate([router_group, router_expert], axis=1),
                 ((0, 0), (0, LANES - N_GROUPS - N_EXPERTS)))
    x1, route = _outproj_call(o_diff, o_nsa, w_out.astype(BF16), x2, ln1_g[None, :], ln1_b[None, :], wr,
                              tm=OUT_ROW_TILE)

    tm = MOE_TILE_ROWS
    n_tiles = -(-2 * n // tm) + N_EXPERTS
    e_idx = route[:, 0:2].astype(jnp.int32)
    slot, rows, tile_expert, n_used = _moe_plan(e_idx, tm, n_tiles)
    ys = _moe_call(tile_expert, n_used, (rows * ROW_PITCH).reshape(n_tiles, 1, tm), x1, w_gate, w_up, w_down, tm)
    tc = COMBINE_TILE_TOKENS
    out = _combine_call((slot * ROW_PITCH).reshape(n // tc, 1, 2 * tc), ys, x1, route,
                        ln2_g[None, :], ln2_b[None, :], tc)
    return out.reshape(batch, seq, d)


def kernel(x, w_in, diff_lambda, diff_subln_g, cmp_pos, cmp_w1, cmp_b1, cmp_w2, cmp_b2, w_out, ln1_g, ln1_b,
           router_group, router_expert, expert_w_gate, expert_w_up, expert_w_down, ln2_g, ln2_b):
    for l in range(DEPTH):
        lambda_init = 0.8 - 0.6 * math.exp(-0.3 * l)
        x = _layer(x, w_in[l], diff_lambda[l], diff_subln_g[l], cmp_pos[l], cmp_w1[l], cmp_b1[l], cmp_w2[l],
                   cmp_b2[l], w_out[l], ln1_g[l], ln1_b[l], router_group[l], router_expert[l],
                   expert_w_gate[l], expert_w_up[l], expert_w_down[l], ln2_g[l], ln2_b[l], lambda_init)
    return x
```

```python
import functools
import math

import numpy as np
import jax
import jax.numpy as jnp
from jax import lax
from jax.experimental import pallas as pl
from jax.experimental.pallas import tpu as pltpu

F32 = jnp.float32
BF16 = jnp.bfloat16

D_MODEL = 2048
HEAD_DIM = 128
ROT_DIM = HEAD_DIM // 4
ROPE_THETA = 500000.0
NEG_INF = -1e30
BIG = 1e30
EPS = 1e-5

DIFF_HEADS = 4
DIFF_VDIM = 2 * HEAD_DIM

NSA_HEADS = 8
NSA_KV_GROUPS = 2
NSA_HPG = NSA_HEADS // NSA_KV_GROUPS
CMP_BLOCK = 32
CMP_STRIDE = 16
CMP_HIDDEN = 256
SLC_BLOCK = 64
SLC_TOPK = 16
WINDOW = 512

N_GROUPS = 4
EXPERTS_PER_GROUP = 8
N_EXPERTS = N_GROUPS * EXPERTS_PER_GROUP
EXPERT_HIDDEN = 512

DEPTH = 1
DN_ALPHA = (2.0 * DEPTH) ** 0.25

LANES = 128
VMEM_LIMIT = 56 * 1024 * 1024

PROJ_ROW_TILE = 2048
DIFF_Q_TILE = 1024
NSA_Q_TILE = 256
NSA_K_TILE = 512
OUT_ROW_TILE = 512

U_NQ, U_DQ, U_DK, U_DV = 0, 8, 16, 24
U_KC, U_VC, U_KS, U_VS, U_KW, U_VW, U_GATE = 32, 34, 36, 38, 40, 42, 44
N_UNITS = 46
ROPE_BLOCKS = tuple(range(U_DQ // 2, U_DV // 2)) + (U_KS // 2, U_KW // 2)


def _dot(a, b):
    return jnp.dot(a, b, preferred_element_type=F32)


def _dot_nt(a, b):
    return lax.dot_general(a, b, (((1,), (1,)), ((), ())), preferred_element_type=F32)


def _params(sem, vmem=VMEM_LIMIT):
    return pltpu.CompilerParams(dimension_semantics=sem, vmem_limit_bytes=vmem)


PROJ_ROW_CHUNK = 256


def _rope(a, c, s):
    lane = lax.broadcasted_iota(jnp.int32, a.shape, 1)
    half = ROT_DIM // 2
    partner = jnp.where(lane < half, pltpu.roll(a, LANES - half, 1), pltpu.roll(a, half, 1))
    return a * c + partner * s


def _proj_kernel(flags_ref, x_ref, w_ref, cos_ref, sin_ref, o_ref, xb_sc):
    j = pl.program_id(1)

    @pl.when(j == 0)
    def _():
        xb_sc[...] = x_ref[...].astype(BF16)

    def block(rotary):
        for r0 in range(0, xb_sc.shape[0], PROJ_ROW_CHUNK):
            rs = slice(r0, r0 + PROJ_ROW_CHUNK)
            acc = _dot(xb_sc[rs, :], w_ref[...])
            if rotary:
                for hh in range(2):
                    cs = slice(hh * LANES, (hh + 1) * LANES)
                    o_ref[rs, cs] = _rope(acc[:, cs], cos_ref[rs, :], sin_ref[rs, :]).astype(o_ref.dtype)
            else:
                o_ref[rs, :] = acc.astype(o_ref.dtype)

    pl.when(flags_ref[j] == 0)(lambda: block(False))
    pl.when(flags_ref[j] != 0)(lambda: block(True))


def _proj_call(xb, w_aug, flags, cos_t, sin_t, seq, tm):
    n, d = xb.shape
    nj = w_aug.shape[1] // 256
    tpb = seq // tm
    grid_spec = pltpu.PrefetchScalarGridSpec(
        num_scalar_prefetch=1,
        grid=(n // tm, nj),
        in_specs=[
            pl.BlockSpec((tm, d), lambda i, j, f: (i, 0)),
            pl.BlockSpec((d, 256), lambda i, j, f: (0, j)),
            pl.BlockSpec((tm, LANES), lambda i, j, f: (i % tpb, 0)),
            pl.BlockSpec((tm, LANES), lambda i, j, f: (i % tpb, 0)),
        ],
        out_specs=pl.BlockSpec((tm, 256), lambda i, j, f: (i, j)),
        scratch_shapes=[pltpu.VMEM((tm, d), BF16)],
    )
    return pl.pallas_call(
        _proj_kernel,
        out_shape=jax.ShapeDtypeStruct((n, w_aug.shape[1]), BF16),
        grid_spec=grid_spec,
        compiler_params=_params(("parallel", "arbitrary")),
        name="proj",
    )(flags, xb, w_aug, cos_t, sin_t)


def _compress_kernel(t_ref, w1_ref, pos_ref, b1_ref, w2_ref, b2_ref, o_ref, t_sc):
    nt = t_sc.shape[0] // CMP_STRIDE
    t_sc[...] = t_ref[...].astype(F32)
    r = jnp.concatenate([t_sc[pl.ds(k, nt, stride=CMP_STRIDE), :] for k in range(CMP_STRIDE)], axis=1).astype(BF16)
    ab = _dot(r, w1_ref[...])
    pos = pos_ref[...]
    pa = jnp.broadcast_to(pos[0:1], (8, pos.shape[1])).astype(BF16)
    pb = jnp.broadcast_to(pos[1:2], (8, pos.shape[1])).astype(BF16)
    const = _dot(pa, w1_ref[:, :CMP_HIDDEN])[0:1] + _dot(pb, w1_ref[:, CMP_HIDDEN:])[0:1]
    h = ab[:, :CMP_HIDDEN] + pltpu.roll(ab[:, CMP_HIDDEN:], nt - 1, 0) + const + b1_ref[...]
    h = jax.nn.gelu(h)
    o_ref[...] = (_dot(h.astype(BF16), w2_ref[...]) + b2_ref[...]).astype(o_ref.dtype)


def _compress_call(proj, w1cat, pos2, b1, w2, b2, b, seq):
    four = 2 * NSA_KV_GROUPS
    nt = seq // CMP_STRIDE
    k = CMP_STRIDE * HEAD_DIM
    return pl.pallas_call(
        _compress_kernel,
        out_shape=jax.ShapeDtypeStruct((b, four, nt, HEAD_DIM), BF16),
        grid=(b, four),
        in_specs=[
            pl.BlockSpec((seq, HEAD_DIM), lambda i, c: (i, U_KC + c)),
            pl.BlockSpec((None, k, 2 * CMP_HIDDEN), lambda i, c: (c // 2, 0, 0)),
            pl.BlockSpec((None, 2, k), lambda i, c: (c // 2, 0, 0)),
            pl.BlockSpec((None, 1, CMP_HIDDEN), lambda i, c: (c // 2, 0, 0)),
            pl.BlockSpec((None, CMP_HIDDEN, HEAD_DIM), lambda i, c: (c // 2, 0, 0)),
            pl.BlockSpec((None, 1, HEAD_DIM), lambda i, c: (c // 2, 0, 0)),
        ],
        out_specs=pl.BlockSpec((None, None, nt, HEAD_DIM), lambda i, c: (i, c, 0, 0)),
        scratch_shapes=[pltpu.VMEM((seq, HEAD_DIM), F32)],
        compiler_params=_params(("parallel", "parallel")),
        name="compress",
    )(proj, w1cat, pos2, b1, w2, b2)


EXP2_SCALE = HEAD_DIM ** -0.5 * math.log2(math.e)
ONES_ROWS = 16


def _transpose_into(src_ref, dst_ref):
    def body(c, carry):
        off = pl.multiple_of(c * LANES, LANES)
        dst_ref[:, pl.ds(off, LANES)] = src_ref[pl.ds(off, LANES), :].astype(F32).T.astype(dst_ref.dtype)
        return carry

    n_tiles = src_ref.shape[0] // LANES
    lax.fori_loop(0, n_tiles, body, 0, unroll=math.gcd(n_tiles, 8))


def _diff_kernel(dl_ref, q_ref, k_ref, v_ref, g_ref, tri_ref, o_ref, vt_sc, acc_sc, sa_sc, sb_sc, *, tq, lambda_init):
    qi = pl.program_id(2)

    @pl.when(qi == 0)
    def _():
        for c in range(2):
            _transpose_into(v_ref.at[:, c * LANES:(c + 1) * LANES], vt_sc.at[c * LANES:(c + 1) * LANES, :])

    acc_sc[...] = jnp.zeros(acc_sc.shape, F32)
    q = q_ref[...]
    tk = tq // 2

    def scores(j, dst, lo=0):
        kt = k_ref[pl.ds(pl.multiple_of(j * tk, tk), tk), :]
        for c in range(2):
            dst[c, :, lo:] = _dot_nt(kt[:, c * HEAD_DIM:(c + 1) * HEAD_DIM], q[lo:, c * HEAD_DIM:(c + 1) * HEAD_DIM])

    def absorb(src, j, masked, carry, lo=0):
        off = pl.multiple_of(j * tk, tk)
        vt = vt_sc[:, pl.ds(off, tk)]
        out = []
        for c in range(2):
            m_old, l_old = carry[2 * c][:, lo:], carry[2 * c + 1][:, lo:]
            s = src[c, :, lo:]
            if masked and lo:
                s = s + tri_ref[...]
            elif masked:
                s = jnp.concatenate([s[:, :tk] + tri_ref[...], s[:, tk:]], axis=1)
            m_new = jnp.maximum(m_old, jnp.max(s, axis=0, keepdims=True))
            p = jnp.exp2(s - m_new)
            alpha = jnp.exp2(m_old - m_new)
            l_new = alpha * l_old + jnp.sum(p, axis=0, keepdims=True)
            acc_sc[c, :, lo:] = alpha * acc_sc[c, :, lo:] + _dot(vt, p.astype(BF16))
            if lo:
                m_new = jnp.concatenate([carry[2 * c][:, :lo], m_new], axis=1)
                l_new = jnp.concatenate([carry[2 * c + 1][:, :lo], l_new], axis=1)
            out += [m_new, l_new]
        return tuple(out)

    def pair(p, carry):
        j = 2 * p
        scores(j + 1, sb_sc)
        carry = absorb(sa_sc, j, False, carry)
        scores(j + 2, sa_sc)
        return absorb(sb_sc, j + 1, False, carry)

    init = (jnp.full((1, tq), NEG_INF, F32), jnp.zeros((1, tq), F32)) * 2
    scores(0, sa_sc)
    carry = lax.fori_loop(0, qi, pair, init)
    j_tail = 2 * qi
    scores(j_tail + 1, sb_sc, lo=tk)
    carry = absorb(sa_sc, j_tail, True, carry)
    _, l0, _, l1 = absorb(sb_sc, j_tail + 1, True, carry, lo=tk)

    dl = dl_ref[...]
    lam = (jnp.exp(jnp.sum(dl[0:1] * dl[1:2], axis=1, keepdims=True))
           - jnp.exp(jnp.sum(dl[2:3] * dl[3:4], axis=1, keepdims=True)) + lambda_init)
    o = acc_sc[0] * (1.0 / l0) - lam * (acc_sc[1] * (1.0 / l1))
    o = o * (lax.rsqrt(jnp.mean(o * o, axis=0, keepdims=True) + EPS) * (1.0 - lambda_init))
    for c in range(DIFF_VDIM // LANES):
        for r in range(tq // LANES):
            blk = o[c * LANES:(c + 1) * LANES, r * LANES:(r + 1) * LANES].T
            o_ref[r * LANES:(r + 1) * LANES, c * LANES:(c + 1) * LANES] = (
                blk * g_ref[:, c * LANES:(c + 1) * LANES]).astype(o_ref.dtype)


def _diff_call(proj, dl, g, batch, seq, tq, lambda_init):
    nq = seq // tq
    kern = functools.partial(_diff_kernel, tq=tq, lambda_init=lambda_init)
    tk = tq // 2
    tri = jnp.asarray(np.where(np.arange(tk)[:, None] <= np.arange(tk)[None, :], 0.0, NEG_INF), dtype=F32)
    return pl.pallas_call(
        kern,
        out_shape=jax.ShapeDtypeStruct((batch * seq, DIFF_HEADS * DIFF_VDIM), BF16),
        grid=(batch, DIFF_HEADS, nq),
        in_specs=[
            pl.BlockSpec((4, HEAD_DIM), lambda b, h, i: (0, 0)),
            pl.BlockSpec((tq, 256), lambda b, h, i: (b * nq + i, U_DQ // 2 + h)),
            pl.BlockSpec((seq, 256), lambda b, h, i: (b, U_DK // 2 + h)),
            pl.BlockSpec((seq, 256), lambda b, h, i: (b, U_DV // 2 + h)),
            pl.BlockSpec((1, DIFF_VDIM), lambda b, h, i: (0, 0)),
            pl.BlockSpec((tk, tk), lambda b, h, i: (0, 0)),
        ],
        out_specs=pl.BlockSpec((tq, DIFF_VDIM), lambda b, h, i: (b * nq + i, h)),
        scratch_shapes=[
            pltpu.VMEM((DIFF_VDIM, seq), BF16),
            pltpu.VMEM((2, DIFF_VDIM, tq), F32),
            pltpu.VMEM((2, tq // 2, tq), F32),
            pltpu.VMEM((2, tq // 2, tq), F32),
        ],
        compiler_params=_params(("arbitrary", "arbitrary", "arbitrary")),
        name="diff_attn",
    )(dl, proj, proj, proj, g, tri)


def _stack_heads(x):
    return jnp.concatenate([x[:, h * HEAD_DIM:(h + 1) * HEAD_DIM] for h in range(NSA_HPG)], axis=0)


def _heads(x):
    return jnp.concatenate([x] * NSA_HPG, axis=1)


def _mask_tables(seq, tq, tk):
    nt = seq // CMP_STRIDE
    slab = min(WINDOW + tq, seq)
    q = np.arange(tq)[None, :]
    k = np.arange(slab)[:, None]
    win = []
    for i in range(WINDOW // tq + 1):
        newest = q + (i * tq if i < WINDOW // tq else WINDOW)
        win.append((k <= newest) & (k > newest - WINDOW))
    u = np.arange(2 * nt)[:, None] - nt
    cmp_ok = u <= ((q - (CMP_BLOCK - 1)) >> int(math.log2(CMP_STRIDE)))
    kk = np.arange(tk)[:, None]
    causal = [kk <= d * tq + q for d in range(tk // tq)] + [np.ones((tk, tq), bool)]
    to_bias = lambda m: jnp.asarray(np.where(np.asarray(m), 0.0, NEG_INF), dtype=F32)
    return to_bias(np.stack(win)), to_bias(cmp_ok), to_bias(np.stack(causal))


def _nsa_kernel(q_ref, cos_ref, sin_ref, gate_ref, kc_ref, vc_ref, ks_ref, vs_ref, kw_ref, vw_ref, ovt_ref,
                wtab_ref, ptab_ref, ctab_ref, o_ref, vct_sc, vst_sc, vwt_sc, bias_sc, acc_sc, sa_sc, sb_sc, m_sc, *, tq, tk, seq, top_k):
    qi = pl.program_id(2)
    q0 = qi * tq
    rows = NSA_HPG * tq
    nt = kc_ref.shape[0]
    ns = ovt_ref.shape[0]
    nb = tk // SLC_BLOCK

    @pl.when(qi == 0)
    def _():
        _transpose_into(vc_ref, vct_sc)
        for v_ref, vt_sc in ((vs_ref, vst_sc), (vw_ref, vwt_sc)):
            _transpose_into(v_ref, vt_sc.at[:HEAD_DIM, :])
            vt_sc[HEAD_DIM:, :] = jnp.ones((ONES_ROWS, seq), vt_sc.dtype)

    q = q_ref[...]
    qs = _stack_heads(q)
    qr = jnp.concatenate(
        [_rope(q[:, h * HEAD_DIM:(h + 1) * HEAD_DIM].astype(F32), cos_ref[...], sin_ref[...]).astype(BF16)
         for h in range(NSA_HPG)], axis=0)
    qpos = q0 + (lax.broadcasted_iota(jnp.int32, (1, rows), 1) & (tq - 1))

    def sel_scores(j, dst):
        dst[...] = _dot_nt(ks_ref[pl.ds(pl.multiple_of(j * tk, tk), tk), :], qr)

    sel_scores(0, sa_sc)

    slab = min(WINDOW + tq, seq)
    start = pl.multiple_of(jnp.maximum(q0 - WINDOW, 0), tq)
    s_w = _dot_nt(kw_ref[pl.ds(start, slab), :], qr)
    s_w = s_w + _heads(wtab_ref[jnp.minimum(qi, WINDOW // tq)])
    p_w = jnp.exp2(s_w - jnp.max(s_w, axis=0, keepdims=True))
    o_w = _dot(vwt_sc[:, pl.ds(start, slab)], p_w.astype(BF16))
    o_w = o_w[:HEAD_DIM] * (1.0 / o_w[HEAD_DIM:HEAD_DIM + 1])

    s_c = _dot_nt(kc_ref[...], qs)
    c0 = pl.multiple_of(nt - q0 // CMP_STRIDE, CMP_STRIDE)
    s_c = s_c + _heads(ptab_ref[pl.ds(c0, nt), :])
    m_c = jnp.max(s_c, axis=0, keepdims=True)
    e_c = jnp.exp2(s_c - m_c)
    l_c = jnp.sum(e_c, axis=0, keepdims=True)
    p_c = e_c * jnp.where(qpos >= CMP_BLOCK - 1, 1.0 / l_c, 0.0)
    o_c = _dot(vct_sc[...], p_c.astype(BF16))

    p_sum = p_c[:, 0:tq]
    for h in range(1, NSA_HPG):
        p_sum = p_sum + p_c[:, h * tq:(h + 1) * tq]
    p_hi = p_sum.astype(BF16)
    p_lo = (p_sum - p_hi.astype(F32)).astype(BF16)
    imp = _dot(ovt_ref[...], p_hi) + _dot(ovt_ref[...], p_lo)

    blk = lax.broadcasted_iota(jnp.int32, (ns, tq), 0)
    qpos_l = q0 + lax.broadcasted_iota(jnp.int32, (ns, tq), 1)
    cur = lax.shift_right_logical(qpos_l, int(math.log2(SLC_BLOCK)))
    valid_s = blk <= cur
    forced = (blk == 0) | (blk == cur) | (blk == cur - 1)
    work = jnp.where(forced, -jnp.inf, jnp.where(valid_s, imp, NEG_INF))
    blk_f = blk.astype(F32)
    for _ in range(top_k - 3):
        mx = jnp.max(work, axis=0, keepdims=True)
        idx = jnp.min(jnp.where(work == mx, blk_f, float(ns)), axis=0, keepdims=True)
        work = jnp.where(blk_f == idx, -jnp.inf, work)
    bias_sc[...] = jnp.where(work == -jnp.inf, 0.0, NEG_INF)

    acc_sc[...] = jnp.zeros(acc_sc.shape, F32)

    def sel_absorb(src, j, causal, m_old):
        off = pl.multiple_of(j * tk, tk)
        s = src[...]
        if causal:
            s = s + _heads(ctab_ref[jnp.clip((q0 - off) // tq, 0, tk // tq)])
        bias = bias_sc[pl.ds(pl.multiple_of(j * nb, nb), nb), :]
        bias = jnp.concatenate([bias] * NSA_HPG, axis=1)
        s3 = s.reshape(nb, SLC_BLOCK, rows) + bias[:, None, :]
        m_new = jnp.maximum(m_old, jnp.max(jnp.max(s3, axis=0), axis=0, keepdims=True))
        p3 = jnp.exp2(s3 - m_new)
        alpha = jnp.exp2(m_old - m_new)
        p = p3.reshape(tk, rows).astype(BF16)
        acc_sc[...] = alpha * acc_sc[...] + _dot(vst_sc[:, pl.ds(off, tk)], p)
        return m_new

    def sel_pair(p, carry):
        j = 2 * p
        sel_scores(j + 1, sb_sc)
        carry = sel_absorb(sa_sc, j, False, carry)
        sel_scores(j + 2, sa_sc)
        return sel_absorb(sb_sc, j + 1, False, carry)

    n_pairs = (q0 // tk + 2) // 2
    carry = lax.fori_loop(0, n_pairs - 1, sel_pair, jnp.full((1, rows), NEG_INF, F32))
    j_tail = 2 * (n_pairs - 1)
    m_sc[...] = sel_absorb(sa_sc, j_tail, True, carry)

    @pl.when(j_tail + 1 <= q0 // tk)
    def _():
        sel_scores(j_tail + 1, sb_sc)
        sel_absorb(sb_sc, j_tail + 1, True, m_sc[...])

    o_s = acc_sc[:HEAD_DIM, :] * (1.0 / acc_sc[HEAD_DIM:HEAD_DIM + 1, :])

    gates = jax.nn.sigmoid(gate_ref[...].astype(F32))
    gates = jnp.concatenate([gates[r * LANES:(r + 1) * LANES].T for r in range(tq // LANES)],
                            axis=1)
    for h in range(NSA_HPG):
        sl = slice(h * tq, (h + 1) * tq)
        o = (gates[3 * h:3 * h + 1] * o_c[:, sl] + gates[3 * h + 1:3 * h + 2] * o_s[:, sl]
             + gates[3 * h + 2:3 * h + 3] * o_w[:, sl])
        for r in range(tq // LANES):
            o_ref[r * LANES:(r + 1) * LANES, h * HEAD_DIM:(h + 1) * HEAD_DIM] = (
                o[:, r * LANES:(r + 1) * LANES].T.astype(o_ref.dtype))


def _nsa_call(proj, cos_t, sin_t, kvc, ovt, batch, seq, tq, tk):
    nq = seq // tq
    assert WINDOW % tq == 0 and tk % tq == 0 and tq % CMP_STRIDE == 0
    wtab, ptab, ctab = _mask_tables(seq, tq, tk)
    nt = kvc.shape[2]
    ns = ovt.shape[0]
    width = NSA_HPG * HEAD_DIM
    kern = functools.partial(_nsa_kernel, tq=tq, tk=tk, seq=seq, top_k=min(SLC_TOPK, ns))

    def col(unit):
        return pl.BlockSpec((seq, HEAD_DIM), lambda b, g, i: (b, unit + g))

    return pl.pallas_call(
        kern,
        out_shape=jax.ShapeDtypeStruct((batch * seq, NSA_HEADS * HEAD_DIM), BF16),
        grid=(batch, NSA_KV_GROUPS, nq),
        in_specs=[
            pl.BlockSpec((tq, width), lambda b, g, i: (b * nq + i, U_NQ // 4 + g)),
            pl.BlockSpec((tq, LANES), lambda b, g, i: (i, 0)),
            pl.BlockSpec((tq, LANES), lambda b, g, i: (i, 0)),
            pl.BlockSpec((tq, LANES), lambda b, g, i: (b * nq + i, U_GATE + g)),
            pl.BlockSpec((None, None, nt, HEAD_DIM), lambda b, g, i: (b, g, 0, 0)),
            pl.BlockSpec((None, None, nt, HEAD_DIM), lambda b, g, i: (b, 2 + g, 0, 0)),
            col(U_KS), col(U_VS), col(U_KW), col(U_VW),
            pl.BlockSpec((ns, nt), lambda b, g, i: (0, 0)),
            pl.BlockSpec(wtab.shape, lambda b, g, i: (0, 0, 0)),
            pl.BlockSpec(ptab.shape, lambda b, g, i: (0, 0)),
            pl.BlockSpec(ctab.shape, lambda b, g, i: (0, 0, 0)),
        ],
        out_specs=pl.BlockSpec((tq, width), lambda b, g, i: (b * nq + i, g)),
        scratch_shapes=[
            pltpu.VMEM((HEAD_DIM, nt), BF16),
            pltpu.VMEM((HEAD_DIM + ONES_ROWS, seq), BF16),
            pltpu.VMEM((HEAD_DIM + ONES_ROWS, seq), BF16),
            pltpu.VMEM((ns, tq), F32),
            pltpu.VMEM((HEAD_DIM + ONES_ROWS, NSA_HPG * tq), F32),
            pltpu.VMEM((tk, NSA_HPG * tq), F32),
            pltpu.VMEM((tk, NSA_HPG * tq), F32),
            pltpu.VMEM((1, NSA_HPG * tq), F32),
        ],
        compiler_params=_params(("arbitrary", "arbitrary", "arbitrary")),
        name="nsa_attn",
    )(proj, cos_t, sin_t, proj, kvc, kvc, proj, proj, proj, proj, ovt, wtab, ptab, ctab)


def _layernorm(y, g, b):
    mu = jnp.mean(y, axis=1, keepdims=True)
    yc = y - mu
    var = jnp.mean(yc * yc, axis=1, keepdims=True)
    return yc * lax.rsqrt(var + EPS) * g + b


N_CHUNKS = D_MODEL // LANES
ROW_PITCH = N_CHUNKS + 1


def _store_chunk_rows(ref, val):
    tm = val.shape[0]
    for k in range(N_CHUNKS):
        ref[pl.ds(k, tm, stride=ROW_PITCH), :] = val[:, k * LANES:(k + 1) * LANES]
    for k in range(N_CHUNKS, ROW_PITCH):
        ref[pl.ds(k, tm, stride=ROW_PITCH), :] = jnp.zeros((tm, LANES), ref.dtype)


def _load_chunk_rows(ref, tm):
    return jnp.concatenate([ref[pl.ds(k, tm, stride=ROW_PITCH), :] for k in range(N_CHUNKS)], axis=1)


OUT_ROW_CHUNK = 256


def _outproj_kernel(od_ref, on_ref, w_ref, x_ref, g_ref, b_ref, wr_ref, x1_ref, route_ref):
    wr = wr_ref[...]
    w_hi = wr.astype(BF16)
    w_lo = (wr - w_hi.astype(F32)).astype(BF16)
    for r0 in range(0, x_ref.shape[0], OUT_ROW_CHUNK):
        _outproj_rows(od_ref, on_ref, w_ref, x_ref, g_ref, b_ref, w_hi, w_lo, x1_ref, route_ref,
                      slice(r0, r0 + OUT_ROW_CHUNK))


def _outproj_rows(od_ref, on_ref, w_ref, x_ref, g_ref, b_ref, w_hi, w_lo, x1_ref, route_ref, rs):
    half = od_ref.shape[1]
    h = _dot(od_ref[rs, :], w_ref[:half, :]) + _dot(on_ref[rs, :], w_ref[half:, :])
    x1 = _layernorm(DN_ALPHA * x_ref[rs, :] + h, g_ref[...], b_ref[...])
    _store_chunk_rows(x1_ref.at[rs.start * ROW_PITCH:rs.stop * ROW_PITCH, :], x1)

    x_hi = x1.astype(BF16)
    x_lo = (x1 - x_hi.astype(F32)).astype(BF16)
    w2 = jnp.concatenate([w_hi, w_lo], axis=1)
    a = _dot(x_hi, w2)
    b = _dot(x_lo, w2)
    logits = (a[:, :LANES] + a[:, LANES:]) + (b[:, :LANES] + b[:, LANES:])

    lane = lax.broadcasted_iota(jnp.int32, logits.shape, 1).astype(F32)
    ninf = -jnp.inf
    gl = jnp.where(lane < N_GROUPS, logits, ninf)
    gmax = jnp.max(gl, axis=1, keepdims=True)
    g_w = 1.0 / jnp.sum(jnp.exp(gl - gmax), axis=1, keepdims=True)
    g_sel = jnp.min(jnp.where(gl == gmax, lane, float(LANES)), axis=1, keepdims=True)
    lo = N_GROUPS + EXPERTS_PER_GROUP * g_sel
    el = jnp.where((lane >= lo) & (lane < lo + EXPERTS_PER_GROUP), logits, ninf)
    e1 = jnp.max(el, axis=1, keepdims=True)
    i1 = jnp.min(jnp.where(el == e1, lane, float(LANES)), axis=1, keepdims=True)
    el2 = jnp.where(lane == i1, ninf, el)
    e2 = jnp.max(el2, axis=1, keepdims=True)
    i2 = jnp.min(jnp.where(el2 == e2, lane, float(LANES)), axis=1, keepdims=True)
    r = jnp.exp(e2 - e1)
    w1 = g_w / (1.0 + r)
    w2 = g_w * r / (1.0 + r)
    route = jnp.where(lane == 0, i1 - N_GROUPS,
                      jnp.where(lane == 1, i2 - N_GROUPS,
                                jnp.where(lane == 2, w1, jnp.where(lane == 3, w2, 0.0))))
    route_ref[rs, :] = route


def _outproj_call(o_diff, o_nsa, w_out, x2, g, b, wr, tm):
    n, d = x2.shape
    half = o_diff.shape[1]
    return pl.pallas_call(
        _outproj_kernel,
        out_shape=(jax.ShapeDtypeStruct((n * ROW_PITCH, LANES), F32), jax.ShapeDtypeStruct((n, LANES), F32)),
        grid=(n // tm,),
        in_specs=[
            pl.BlockSpec((tm, half), lambda i: (i, 0)),
            pl.BlockSpec((tm, half), lambda i: (i, 0)),
            pl.BlockSpec((2 * half, d), lambda i: (0, 0)),
            pl.BlockSpec((tm, d), lambda i: (i, 0)),
            pl.BlockSpec((1, d), lambda i: (0, 0)),
            pl.BlockSpec((1, d), lambda i: (0, 0)),
            pl.BlockSpec((d, LANES), lambda i: (0, 0)),
        ],
        out_specs=(pl.BlockSpec((tm * ROW_PITCH, LANES), lambda i: (i, 0)),
                   pl.BlockSpec((tm, LANES), lambda i: (i, 0))),
        compiler_params=_params(("parallel",)),
        name="outproj_ln1_router",
    )(o_diff, o_nsa, w_out, x2, g, b, wr)


MOE_TILE_ROWS = 256
COMBINE_TILE_TOKENS = 256


def _token_copy(src_hbm, dst, sem, src_row, dst_tok):
    return pltpu.make_async_copy(src_hbm.at[pl.ds(src_row, N_CHUNKS)],
                                 dst.at[pl.ds(dst_tok * ROW_PITCH, N_CHUNKS)], sem)


def _wait_tokens(src_hbm, dst, sem, count):
    pltpu.make_async_copy(src_hbm.at[pl.ds(0, count * N_CHUNKS)], dst.at[pl.ds(0, count * N_CHUNKS)], sem).wait()


def _moe_kernel(te_ref, nu_ref, rows_ref, rows_next_ref, x_hbm, wg_ref, wu_ref, wd_ref, o_ref,
                xbuf, sem, wg_sc, wu_sc, wd_sc, *, tm):
    i = pl.program_id(0)
    n_used = nu_ref[0]
    slot = lax.rem(i, 2)

    def issue(ids_ref, s):
        def body(r, carry):
            _token_copy(x_hbm, xbuf.at[s], sem.at[s], ids_ref[0, 0, r], r).start()
            return carry

        lax.fori_loop(0, tm, body, 0, unroll=8)

    @pl.when(i == 0)
    def _():
        issue(rows_ref, 0)

    @pl.when(i + 1 < n_used)
    def _():
        for r in range(tm):
            _token_copy(x_hbm, xbuf.at[1 - slot], sem.at[1 - slot], rows_next_ref[0, 0, r], r).start(priority=r % 2)

    prev = te_ref[jnp.maximum(i - 1, 0)]

    @pl.when((i == 0) | (te_ref[i] != prev))
    def _():
        wg_sc[...] = wg_ref[...].astype(BF16)
        wu_sc[...] = wu_ref[...].astype(BF16)
        wd_sc[...] = wd_ref[...].astype(BF16)

    @pl.when(i < n_used)
    def _():
        _wait_tokens(x_hbm, xbuf.at[slot], sem.at[slot], tm)
        xb = _load_chunk_rows(xbuf.at[slot], tm).astype(BF16)
        gate = _dot(xb, wg_sc[...])
        up = _dot(xb, wu_sc[...])
        h = (gate * jax.nn.sigmoid(gate) * up).astype(BF16)
        _store_chunk_rows(o_ref, _dot(h, wd_sc[...]))

    @pl.when(i >= n_used)
    def _():
        o_ref[...] = jnp.zeros(o_ref.shape, o_ref.dtype)


def _moe_call(tile_expert, n_used, rows, x1c, w_gate, w_up, w_down, tm):
    n_tiles = rows.shape[0]
    d, f = w_gate.shape[1], w_gate.shape[2]
    grid_spec = pltpu.PrefetchScalarGridSpec(
        num_scalar_prefetch=2,
        grid=(n_tiles,),
        in_specs=[
            pl.BlockSpec((1, 1, tm), lambda i, te, nu: (i, 0, 0), memory_space=pltpu.SMEM),
            pl.BlockSpec((1, 1, tm), lambda i, te, nu: (jnp.minimum(i + 1, n_tiles - 1), 0, 0),
                         memory_space=pltpu.SMEM),
            pl.BlockSpec(memory_space=pl.ANY),
            pl.BlockSpec((None, d, f), lambda i, te, nu: (te[i], 0, 0)),
            pl.BlockSpec((None, d, f), lambda i, te, nu: (te[i], 0, 0)),
            pl.BlockSpec((None, f, d), lambda i, te, nu: (te[i], 0, 0)),
        ],
        out_specs=pl.BlockSpec((tm * ROW_PITCH, LANES), lambda i, te, nu: (i, 0)),
        scratch_shapes=[
            pltpu.VMEM((2, tm * ROW_PITCH, LANES), F32),
            pltpu.SemaphoreType.DMA((2,)),
            pltpu.VMEM((d, f), BF16), pltpu.VMEM((d, f), BF16), pltpu.VMEM((f, d), BF16),
        ],
    )
    return pl.pallas_call(
        functools.partial(_moe_kernel, tm=tm),
        out_shape=jax.ShapeDtypeStruct((n_tiles * tm * ROW_PITCH, LANES), F32),
        grid_spec=grid_spec,
        compiler_params=_params(("arbitrary",)),
        name="moe_experts",
    )(tile_expert, n_used, rows, rows, x1c, w_gate, w_up, w_down)


def _combine_kernel(slots_ref, slots_next_ref, y_hbm, x1_ref, route_ref, g_ref, b_ref, o_ref, buf, sem, *, tm):
    i = pl.program_id(0)
    slot = lax.rem(i, 2)

    @pl.when(i == 0)
    def _():
        def body(r, carry):
            for k in range(2):
                _token_copy(y_hbm, buf.at[0, k], sem.at[0], slots_ref[0, 0, 2 * r + k], r).start()
            return carry

        lax.fori_loop(0, tm, body, 0, unroll=8)

    def tile(prefetch):
        if prefetch:
            for r in range(tm):
                for k in range(2):
                    _token_copy(y_hbm, buf.at[1 - slot, k], sem.at[1 - slot],
                                slots_next_ref[0, 0, 2 * r + k], r).start(priority=k)
        for k in range(2):
            _wait_tokens(y_hbm, buf.at[slot, k], sem.at[slot], tm)
        route = route_ref[...]
        y = (route[:, 2:3] * _load_chunk_rows(buf.at[slot, 0], tm)
             + route[:, 3:4] * _load_chunk_rows(buf.at[slot, 1], tm))
        x1 = _load_chunk_rows(x1_ref, tm)
        o_ref[...] = _layernorm(DN_ALPHA * x1 + y, g_ref[...], b_ref[...])

    last = pl.num_programs(0) - 1
    pl.when(i < last)(lambda: tile(True))
    pl.when(i == last)(lambda: tile(False))


def _combine_call(slots, yc, x1c, route, g, b, tm):
    n = route.shape[0]
    d = D_MODEL
    nt = n // tm
    return pl.pallas_call(
        functools.partial(_combine_kernel, tm=tm),
        out_shape=jax.ShapeDtypeStruct((n, d), F32),
        grid=(nt,),
        in_specs=[
            pl.BlockSpec((1, 1, 2 * tm), lambda i: (i, 0, 0), memory_space=pltpu.SMEM),
            pl.BlockSpec((1, 1, 2 * tm), lambda i: (jnp.minimum(i + 1, nt - 1), 0, 0), memory_space=pltpu.SMEM),
            pl.BlockSpec(memory_space=pl.ANY),
            pl.BlockSpec((tm * ROW_PITCH, LANES), lambda i: (i, 0)),
            pl.BlockSpec((tm, LANES), lambda i: (i, 0)),
            pl.BlockSpec((1, d), lambda i: (0, 0)),
            pl.BlockSpec((1, d), lambda i: (0, 0)),
        ],
        out_specs=pl.BlockSpec((tm, d), lambda i: (i, 0)),
        scratch_shapes=[pltpu.VMEM((2, 2, tm * ROW_PITCH, LANES), F32), pltpu.SemaphoreType.DMA((2,))],
        compiler_params=_params(("arbitrary",)),
        name="moe_combine_ln2",
    )(slots, slots, yc, x1c, route, g, b)


def _moe_plan(e_idx, tm, n_tiles):
    n = e_idx.shape[0]
    lanes = jnp.arange(N_EXPERTS, dtype=jnp.int32)[None, :]
    hit = ((e_idx[:, 0:1] == lanes) | (e_idx[:, 1:2] == lanes)).astype(jnp.int32)
    before = jnp.cumsum(hit, axis=0) - hit
    counts = jnp.sum(hit, axis=0)
    ptiles = (counts + tm - 1) // tm
    tile_end = jnp.cumsum(ptiles)
    base = (tile_end - ptiles) * tm
    slot = jnp.take_along_axis(before + base[None, :], e_idx, axis=1).reshape(-1)
    rows = jnp.zeros((n_tiles * tm,), jnp.int32).at[slot].set(
        jnp.arange(2 * n, dtype=jnp.int32) // 2, unique_indices=True, mode="promise_in_bounds")
    tile_ids = jnp.arange(n_tiles, dtype=jnp.int32)
    tile_expert = jnp.minimum(jnp.sum((tile_end[None, :] <= tile_ids[:, None]).astype(jnp.int32), axis=1),
                              N_EXPERTS - 1)
    return slot.astype(jnp.int32), rows, tile_expert, tile_end[-1:].astype(jnp.int32)


def _rope_tables(seq):
    inv_freq = ROPE_THETA ** (-np.arange(0, ROT_DIM, 2, dtype=np.float64) / ROT_DIM)
    ang = np.arange(seq, dtype=np.float64)[:, None] * inv_freq[None, :]
    cos, sin = np.cos(ang), np.sin(ang)
    pad1 = np.ones((seq, HEAD_DIM - ROT_DIM))
    pad0 = np.zeros((seq, HEAD_DIM - ROT_DIM))
    return (jnp.asarray(np.concatenate([cos, cos, pad1], axis=1), dtype=F32),
            jnp.asarray(np.concatenate([-sin, sin, pad0], axis=1), dtype=F32))


def _layer(x, w_in, diff_lambda, diff_subln_g, cmp_pos, cmp_w1, cmp_b1, cmp_w2, cmp_b2, w_out,
           ln1_g, ln1_b, router_group, router_expert, w_gate, w_up, w_down, ln2_g, ln2_b, lambda_init):
    batch, seq, d = x.shape
    n = batch * seq
    x2 = x.reshape(n, d)

    seg = lambda a, b: w_in[:, a:b]
    pad = lambda w: jnp.pad(w, ((0, 0), (0, LANES - w.shape[1])))
    gate_w = seg(5632, 5656)
    w_aug = jnp.concatenate(
        [seg(3072, 4096) * EXP2_SCALE, seg(0, 1024) * EXP2_SCALE, seg(1024, 2048), seg(2048, 3072), seg(4096, 5632),
         pad(gate_w[:, :12]), pad(gate_w[:, 12:])], axis=1).astype(BF16)
    flags = jnp.zeros((N_UNITS // 2,), jnp.int32).at[jnp.array(ROPE_BLOCKS)].set(1)
    cos_t, sin_t = _rope_tables(seq)
    proj = _proj_call(x2, w_aug, flags, cos_t, sin_t, seq, tm=min(PROJ_ROW_TILE, seq))

    nt = seq // CMP_STRIDE
    half_feat = CMP_STRIDE * HEAD_DIM
    w1cat = jnp.concatenate([cmp_w1[:, :half_feat], cmp_w1[:, half_feat:]], axis=2).astype(BF16)
    kvc = _compress_call(proj, w1cat, cmp_pos.reshape(2, 2, half_feat), cmp_b1[:, None, :],
                         cmp_w2.astype(BF16), cmp_b2[:, None, :], batch, seq)

    o_diff = _diff_call(proj, diff_lambda, diff_subln_g[None, :], batch, seq, min(DIFF_Q_TILE, seq), lambda_init)

    ns = seq // SLC_BLOCK
    ci = np.arange(nt)[None, :] * CMP_STRIDE
    sj = np.arange(ns)[:, None] * SLC_BLOCK
    ovt = jnp.asarray((ci < sj + SLC_BLOCK) & (ci + CMP_BLOCK > sj) & (np.arange(nt)[None, :] < nt - 1), dtype=BF16)
    o_nsa = _nsa_call(proj, cos_t, sin_t, kvc, ovt, batch, seq, tq=NSA_Q_TILE, tk=min(NSA_K_TILE, seq))

    wr = jnp.pad(jnp.concatenate([router_group, router_expert], axis=1),
                 ((0, 0), (0, LANES - N_GROUPS - N_EXPERTS)))
    x1, route = _outproj_call(o_diff, o_nsa, w_out.astype(BF16), x2, ln1_g[None, :], ln1_b[None, :], wr,
                              tm=OUT_ROW_TILE)

    tm = MOE_TILE_ROWS
    n_tiles = -(-2 * n // tm) + N_EXPERTS
    e_idx = route[:, 0:2].astype(jnp.int32)
    slot, rows, tile_expert, n_used = _moe_plan(e_idx, tm, n_tiles)
    ys = _moe_call(tile_expert, n_used, (rows * ROW_PITCH).reshape(n_tiles, 1, tm), x1, w_gate, w_up, w_down, tm)
    tc = COMBINE_TILE_TOKENS
    out = _combine_call((slot * ROW_PITCH).reshape(n // tc, 1, 2 * tc), ys, x1, route,
                        ln2_g[None, :], ln2_b[None, :], tc)
    return out.reshape(batch, seq, d)


def kernel(x, w_in, diff_lambda, diff_subln_g, cmp_pos, cmp_w1, cmp_b1, cmp_w2, cmp_b2, w_out, ln1_g, ln1_b,
           router_group, router_expert, expert_w_gate, expert_w_up, expert_w_down, ln2_g, ln2_b):
    for l in range(DEPTH):
        lambda_init = 0.8 - 0.6 * math.exp(-0.3 * l)
        x = _layer(x, w_in[l], diff_lambda[l], diff_subln_g[l], cmp_pos[l], cmp_w1[l], cmp_b1[l], cmp_w2[l],
                   cmp_b2[l], w_out[l], ln1_g[l], ln1_b[l], router_group[l], router_expert[l],
                   expert_w_gate[l], expert_w_up[l], expert_w_down[l], ln2_g[l], ln2_b[l], lambda_init)
    return x
```
